```python
import math
import jax
import jax.numpy as jnp
from jax import lax
import numpy as np

D_MODEL = 2048
BATCH = 4
SEQ = 4096
DEPTH = 2

SWA_HEADS = 8
SWA_KV_HEADS = 2
SWA_HEAD_DIM = 64
WINDOW = 128
HG_HEADS = 8
HG_DK = 128
HG_DV = 128
HG_CHUNK = 64
MLA_HEADS = 4
MLA_Q_RANK = 512
MLA_KV_RANK = 256
MLA_NOPE = 128
MLA_ROPE = 64
MLA_QK = MLA_NOPE + MLA_ROPE
MLA_V = 128
Q_BLOCK = 128
ROPE_THETA = 10000.0
REL_BUCKETS = 32
REL_MAX_DIST = 128
MEM_LEN = 256
MEM_HEADS = 4
MEM_HEAD_DIM = 128
MEM_WIDTH = MEM_HEADS * MEM_HEAD_DIM
N_GROUPS = 8
EXPERTS_PER_GROUP = 8
N_EXPERTS = N_GROUPS * EXPERTS_PER_GROUP
TOP_K = 2
D_EXPERT = 512
MOE_BLOCK = 128

EPS = 1e-6
NEG_INF = -1e30
F32 = jnp.float32

SWA_WIDTH = SWA_HEADS * SWA_HEAD_DIM
HG_WIDTH = HG_HEADS * HG_DV
MLA_WIDTH = MLA_HEADS * MLA_V
MIX_WIDTH = SWA_WIDTH + HG_WIDTH + MLA_WIDTH
IN_SIZES = (SWA_HEADS * SWA_HEAD_DIM, SWA_KV_HEADS * SWA_HEAD_DIM, SWA_KV_HEADS * SWA_HEAD_DIM,
            HG_HEADS * HG_DK, HG_HEADS * HG_DK, HG_HEADS * HG_DV, HG_HEADS * HG_DV,
            MLA_Q_RANK, MLA_KV_RANK, MLA_ROPE)
IN_WIDTH = sum(IN_SIZES)

kernel_name = 'hybrid_parallel_heads_hier_moe'


def rmsnorm(x, gain):
    xf = x.astype(F32)
    y = xf * lax.rsqrt(jnp.mean(xf * xf, axis=-1, keepdims=True) + EPS)
    return (y * gain.astype(F32)).astype(x.dtype)


def split_columns(t, sizes):
    outs, off = [], 0
    for n in sizes:
        outs.append(t[..., off:off + n])
        off += n
    return outs


def apply_rope(x, positions):
    half = x.shape[-1] // 2
    inv_freq = ROPE_THETA ** (-jnp.arange(half, dtype=F32) / half)
    ang = positions.astype(F32)[:, None] * inv_freq[None, :]
    cos = jnp.cos(ang)[None, :, None, :]
    sin = jnp.sin(ang)[None, :, None, :]
    xf = x.astype(F32)
    x1, x2 = xf[..., :half], xf[..., half:]
    return jnp.concatenate([x1 * cos - x2 * sin, x1 * sin + x2 * cos], axis=-1).astype(x.dtype)


def t5_bucket(n):
    max_exact = REL_BUCKETS // 2
    nf = jnp.maximum(n, 1).astype(F32)
    large = max_exact + (jnp.log(nf / max_exact) / math.log(REL_MAX_DIST / max_exact)
                         * (REL_BUCKETS - max_exact)).astype(jnp.int32)
    large = jnp.minimum(large, REL_BUCKETS - 1)
    return jnp.where(n < max_exact, n, large)


def band_relative_bias(table):
    qi = jnp.arange(WINDOW)[:, None]
    kj = jnp.arange(2 * WINDOW)[None, :]
    dist = jnp.maximum(qi + WINDOW - kj, 0)
    return jnp.transpose(table[t5_bucket(dist)], (2, 0, 1))


def swa_gqa_sinks(q, k, v, sinks, rel_bias):
    b, s = q.shape[:2]
    nb = s // WINDOW
    grp = SWA_HEADS // SWA_KV_HEADS
    qb = q.reshape(b, nb, WINDOW, SWA_KV_HEADS, grp, SWA_HEAD_DIM)

    def band(t):
        tp = jnp.pad(t, ((0, 0), (WINDOW, 0), (0, 0), (0, 0)))
        tp = tp.reshape(b, nb + 1, WINDOW, SWA_KV_HEADS, SWA_HEAD_DIM)
        return jnp.concatenate([tp[:, :-1], tp[:, 1:]], axis=2)

    kb, vb = band(k), band(v)
    scores = jnp.einsum('bnqhgd,bnkhd->bnhgqk', qb, kb, preferred_element_type=F32) * (SWA_HEAD_DIM ** -0.5)
    scores = scores + rel_bias.reshape(SWA_KV_HEADS, grp, WINDOW, 2 * WINDOW).astype(F32)
    qi = jnp.arange(WINDOW)[:, None]
    kj = jnp.arange(2 * WINDOW)[None, :]
    dist = qi + WINDOW - kj
    in_window = (dist >= 0) & (dist < WINDOW)
    key_pos = jnp.arange(nb)[:, None, None] * WINDOW + kj[None] - WINDOW
    valid = in_window[None] & (key_pos >= 0)
    scores = jnp.where(valid[None, :, None, None], scores, NEG_INF)
    sink = jnp.broadcast_to(sinks.reshape(SWA_KV_HEADS, grp, 1, 1).astype(F32), scores.shape[:-1] + (1,))
    probs = jax.nn.softmax(jnp.concatenate([scores, sink], axis=-1), axis=-1)[..., :-1]
    out = jnp.einsum('bnhgqk,bnkhd->bnqhgd', probs.astype(v.dtype), vb)
    return out.reshape(b, s, SWA_WIDTH)


def hgrn2_chunkwise(q, k, log_f, v):
    b, s = q.shape[:2]
    nc = s // HG_CHUNK

    def chunks(t):
        return t.reshape(b, nc, HG_CHUNK, HG_HEADS, t.shape[-1]).transpose(1, 0, 3, 2, 4)

    causal = jnp.tril(jnp.ones((HG_CHUNK, HG_CHUNK), dtype=bool))

    def step(state, inp):
        qc, kc, gc, vc = inp
        bcum = jnp.cumsum(gc, axis=2)
        btot = bcum[:, :, -1:]
        inter = jnp.einsum('bhtk,bhkv->bhtv', qc * jnp.exp(bcum), state)
        diff = jnp.where(causal[:, :, None], bcum[:, :, :, None, :] - bcum[:, :, None, :, :], -jnp.inf)
        attn = jnp.einsum('bhtk,bhsk,bhtsk->bhts', qc, kc, jnp.exp(diff))
        intra = jnp.einsum('bhts,bhsv->bhtv', attn, vc)
        new_state = state * jnp.exp(btot[:, :, 0])[..., None] + \
            jnp.einsum('bhsk,bhsv->bhkv', kc * jnp.exp(btot - bcum), vc)
        return new_state, inter + intra

    state0 = jnp.zeros((b, HG_HEADS, HG_DK, HG_DV), F32)
    _, out = lax.scan(step, state0, (chunks(q), chunks(k), chunks(log_f), chunks(v)))
    return out.transpose(1, 0, 3, 2, 4).reshape(b, s, HG_HEADS, HG_DV)


def mla_causal_attention(q, k, v):
    b, s = q.shape[:2]
    nb = s // Q_BLOCK
    qb = q.reshape(b, nb, Q_BLOCK, MLA_HEADS, MLA_QK).transpose(1, 0, 2, 3, 4)
    key_pos = jnp.arange(s)
    scale = MLA_QK ** -0.5

    def one_block(args):
        qblk, i = args
        sc = jnp.einsum('bqhd,bkhd->bhqk', qblk, k, preferred_element_type=F32) * scale
        qpos = i * Q_BLOCK + jnp.arange(Q_BLOCK)
        sc = jnp.where(key_pos[None, :] <= qpos[:, None], sc, NEG_INF)
        p = jax.nn.softmax(sc, axis=-1)
        return jnp.einsum('bhqk,bkhd->bqhd', p.astype(v.dtype), v)

    out = lax.map(one_block, (qb, jnp.arange(nb)))
    return out.transpose(1, 0, 2, 3, 4).reshape(b, s, MLA_WIDTH)


def memory_cross_attention(hn, mem, g_mem_kv, w_mq, w_mkv, mem_gq, mem_gk, w_mo):
    b, s, _ = hn.shape
    m = mem.shape[1]
    q = rmsnorm((hn @ w_mq).reshape(b, s, MEM_HEADS, MEM_HEAD_DIM), mem_gq)
    kv = rmsnorm(mem, g_mem_kv) @ w_mkv
    k, v = kv[..., :MEM_WIDTH], kv[..., MEM_WIDTH:]
    k = rmsnorm(k.reshape(b, m, MEM_HEADS, MEM_HEAD_DIM), mem_gk)
    v = v.reshape(b, m, MEM_HEADS, MEM_HEAD_DIM)
    sc = jnp.einsum('bshd,bmhd->bhsm', q, k, preferred_element_type=F32) * (MEM_HEAD_DIM ** -0.5)
    p = jax.nn.softmax(sc, axis=-1)
    o = jnp.einsum('bhsm,bmhd->bshd', p.astype(v.dtype), v).reshape(b, s, MEM_WIDTH)
    return o @ w_mo


def hierarchical_moe(hn, w_gr, b_gr, w_er, b_er, w_gate, w_up, w_down):
    b, s, d = hn.shape
    n = b * s
    ht = hn.reshape(n, d)
    hf = ht.astype(F32)
    g_prob = jax.nn.softmax(hf @ w_gr.astype(F32) + b_gr.astype(F32), axis=-1)
    grp = jnp.argmax(g_prob, axis=-1)
    p_grp = jnp.take_along_axis(g_prob, grp[:, None], axis=-1)
    e_logits = (hf @ w_er.astype(F32) + b_er.astype(F32)).reshape(n, N_GROUPS, EXPERTS_PER_GROUP)
    e_logits = jnp.take_along_axis(e_logits, grp[:, None, None], axis=1)[:, 0]
    top_p, top_i = lax.top_k(jax.nn.softmax(e_logits, axis=-1), TOP_K)
    gates = p_grp * top_p / jnp.sum(top_p, axis=-1, keepdims=True)
    expert_id = grp[:, None] * EXPERTS_PER_GROUP + top_i

    a = n * TOP_K
    flat_e = expert_id.reshape(a)
    order = jnp.argsort(flat_e)
    e_sorted = flat_e[order]
    tok_sorted = order // TOP_K
    gate_sorted = gates.reshape(a)[order]
    counts = jnp.bincount(flat_e, length=N_EXPERTS)
    padded = (counts + MOE_BLOCK - 1) // MOE_BLOCK * MOE_BLOCK
    padded_end = jnp.cumsum(padded)
    start = jnp.cumsum(counts) - counts
    dest = (padded_end - padded)[e_sorted] + jnp.arange(a) - start[e_sorted]
    n_blocks = -(-a // MOE_BLOCK) + N_EXPERTS
    slot_tok = jnp.full((n_blocks * MOE_BLOCK,), n, jnp.int32).at[dest].set(tok_sorted)
    h_pad = jnp.concatenate([ht, jnp.zeros((1, d), ht.dtype)], axis=0)
    xin = h_pad[slot_tok].reshape(n_blocks, MOE_BLOCK, d)
    block_expert = jnp.minimum(
        jnp.searchsorted(padded_end, jnp.arange(n_blocks) * MOE_BLOCK, side='right'), N_EXPERTS - 1)

    def run_block(args):
        xb, e = args
        return (jax.nn.silu(xb @ w_gate[e]) * (xb @ w_up[e])) @ w_down[e]

    y = lax.map(run_block, (xin, block_expert)).reshape(n_blocks * MOE_BLOCK, d)
    y_assign = y[dest] * gate_sorted[:, None].astype(y.dtype)
    out = jax.ops.segment_sum(y_assign, tok_sorted, num_segments=n)
    return out.reshape(b, s, d)


def setup_inputs(seed: int = 0) -> dict:
    key = jax.random.key(seed)
    keys = iter(jax.random.split(key, 40))

    def normal(shape, scale):
        return jax.random.normal(next(keys), shape, F32) * scale

    def gain(shape):
        return 1.0 + normal(shape, 0.02)

    L, D = DEPTH, D_MODEL
    return {
        'x': normal((BATCH, SEQ, D), 1.0),
        'mem': normal((BATCH, MEM_LEN, D), 1.0),
        'rel_bias_table': normal((REL_BUCKETS, SWA_HEADS), 0.5),
        'hg_lb_logits': normal((L, HG_HEADS * HG_DK), 0.5),
        'g_mix': gain((L, D)),
        'w_in': normal((L, D, IN_WIDTH), D ** -0.5),
        'swa_gq': gain((L, SWA_HEAD_DIM)),
        'swa_gk': gain((L, SWA_HEAD_DIM)),
        'swa_sinks': normal((L, SWA_HEADS), 0.5),
        'hg_g_out': gain((L, HG_DV)),
        'mla_g_cq': gain((L, MLA_Q_RANK)),
        'mla_g_ckv': gain((L, MLA_KV_RANK)),
        'mla_w_uq': normal((L, MLA_Q_RANK, MLA_HEADS * MLA_QK), MLA_Q_RANK ** -0.5),
        'mla_w_ukv': normal((L, MLA_KV_RANK, MLA_HEADS * (MLA_NOPE + MLA_V)), MLA_KV_RANK ** -0.5),
        'mla_gq': gain((L, MLA_QK)),
        'mla_gk': gain((L, MLA_QK)),
        'w_out': normal((L, MIX_WIDTH, D), MIX_WIDTH ** -0.5),
        'g_mem_q': gain((L, D)),
        'g_mem_kv': gain((L, D)),
        'w_mq': normal((L, D, MEM_WIDTH), D ** -0.5),
        'w_mkv': normal((L, D, 2 * MEM_WIDTH), D ** -0.5),
        'mem_gq': gain((L, MEM_HEAD_DIM)),
        'mem_gk': gain((L, MEM_HEAD_DIM)),
        'w_mo': normal((L, MEM_WIDTH, D), MEM_WIDTH ** -0.5),
        'g_ffn': gain((L, D)),
        'w_group_router': normal((L, D, N_GROUPS), D ** -0.5),
        'b_group_router': normal((L, N_GROUPS), 0.01),
        'w_expert_router': normal((L, D, N_EXPERTS), D ** -0.5),
        'b_expert_router': normal((L, N_EXPERTS), 0.01),
        'w_gate': normal((L, N_EXPERTS, D, D_EXPERT), D ** -0.5),
        'w_up': normal((L, N_EXPERTS, D, D_EXPERT), D ** -0.5),
        'w_down': normal((L, N_EXPERTS, D_EXPERT, D), D_EXPERT ** -0.5),
    }


def reference(x, mem, rel_bias_table, hg_lb_logits, g_mix, w_in, swa_gq, swa_gk, swa_sinks,
              hg_g_out, mla_g_cq, mla_g_ckv, mla_w_uq, mla_w_ukv, mla_gq, mla_gk, w_out,
              g_mem_q, g_mem_kv, w_mq, w_mkv, mem_gq, mem_gk, w_mo,
              g_ffn, w_group_router, b_group_router, w_expert_router, b_expert_router,
              w_gate, w_up, w_down):
    b, s, _ = x.shape
    positions = jnp.arange(s, dtype=jnp.int32)
    rel_bias = band_relative_bias(rel_bias_table)
    lb_all = jnp.cumsum(jax.nn.softmax(hg_lb_logits.astype(F32), axis=0), axis=0)
    lb_all = lb_all - lb_all[:1]

    for l in range(DEPTH):
        hn = rmsnorm(x, g_mix[l])
        u = hn @ w_in[l]
        sq, sk, sv, hq, hf, hi, hg, cq, ckv, kr = split_columns(u, IN_SIZES)

        sq = rmsnorm(sq.reshape(b, s, SWA_HEADS, SWA_HEAD_DIM), swa_gq[l])
        sk = rmsnorm(sk.reshape(b, s, SWA_KV_HEADS, SWA_HEAD_DIM), swa_gk[l])
        sv = sv.reshape(b, s, SWA_KV_HEADS, SWA_HEAD_DIM)
        out_a = swa_gqa_sinks(sq, sk, sv, swa_sinks[l], rel_bias)

        lb = lb_all[l].reshape(HG_HEADS, HG_DK)
        f_pre = hf.reshape(b, s, HG_HEADS, HG_DK).astype(F32)
        log_f = jnp.logaddexp(jnp.log(lb), jnp.log1p(-lb) + jax.nn.log_sigmoid(f_pre))
        k_in = (1.0 - lb) * jax.nn.sigmoid(-f_pre)
        o_b = hgrn2_chunkwise(hq.reshape(b, s, HG_HEADS, HG_DK).astype(F32), k_in, log_f,
                              hi.reshape(b, s, HG_HEADS, HG_DV).astype(F32))
        out_b = (rmsnorm(o_b, hg_g_out[l]) * jax.nn.silu(hg.reshape(b, s, HG_HEADS, HG_DV).astype(F32)))
        out_b = out_b.astype(x.dtype).reshape(b, s, HG_WIDTH)

        cq = rmsnorm(cq, mla_g_cq[l])
        ckv = rmsnorm(ckv, mla_g_ckv[l])
        qm = (cq @ mla_w_uq[l]).reshape(b, s, MLA_HEADS, MLA_QK)
        kvm = (ckv @ mla_w_ukv[l]).reshape(b, s, MLA_HEADS, MLA_NOPE + MLA_V)
        k_nope, vm = kvm[..., :MLA_NOPE], kvm[..., MLA_NOPE:]
        k_rope = jnp.broadcast_to(kr[:, :, None, :], (b, s, MLA_HEADS, MLA_ROPE))
        km = jnp.concatenate([k_nope, k_rope], axis=-1)
        qm = rmsnorm(qm, mla_gq[l])
        km = rmsnorm(km, mla_gk[l])
        qm = jnp.concatenate([qm[..., :MLA_NOPE], apply_rope(qm[..., MLA_NOPE:], positions)], axis=-1)
        km = jnp.concatenate([km[..., :MLA_NOPE], apply_rope(km[..., MLA_NOPE:], positions)], axis=-1)
        out_c = mla_causal_attention(qm, km, vm)

        x = x + jnp.concatenate([out_a, out_b, out_c], axis=-1) @ w_out[l]

        x = x + memory_cross_attention(rmsnorm(x, g_mem_q[l]), mem, g_mem_kv[l], w_mq[l], w_mkv[l],
                                       mem_gq[l], mem_gk[l], w_mo[l])

        x = x + hierarchical_moe(rmsnorm(x, g_ffn[l]), w_group_router[l], b_group_router[l],
                                 w_expert_router[l], b_expert_router[l], w_gate[l], w_up[l], w_down[l])
    return x
```

```python
import functools
import math

import jax
import jax.numpy as jnp
from jax import lax
from jax.experimental import pallas as pl
from jax.experimental.pallas import tpu as pltpu

F32 = jnp.float32
BF16 = jnp.bfloat16
I32 = jnp.int32

D_MODEL = 2048
DEPTH = 2
SWA_HEADS = 8
SWA_KV_HEADS = 2
SWA_HEAD_DIM = 64
WINDOW = 128
HG_HEADS = 8
HG_DK = 128
HG_DV = 128
MLA_HEADS = 4
MLA_Q_RANK = 512
MLA_KV_RANK = 256
MLA_NOPE = 128
MLA_ROPE = 64
MLA_QK = MLA_NOPE + MLA_ROPE
MLA_V = 128
ROPE_THETA = 10000.0
REL_BUCKETS = 32
REL_MAX_DIST = 128
MEM_HEADS = 4
MEM_HEAD_DIM = 128
MEM_WIDTH = MEM_HEADS * MEM_HEAD_DIM
N_GROUPS = 8
EXPERTS_PER_GROUP = 8
N_EXPERTS = N_GROUPS * EXPERTS_PER_GROUP
TOP_K = 2
D_EXPERT = 512
MOE_BLOCK = 128
EPS = 1e-6
NEG_INF = -1e30

SWA_WIDTH = SWA_HEADS * SWA_HEAD_DIM
SWA_KV_WIDTH = SWA_KV_HEADS * SWA_HEAD_DIM
HG_WIDTH = HG_HEADS * HG_DV
MLA_WIDTH = MLA_HEADS * MLA_V
IN_SIZES = (SWA_WIDTH, SWA_KV_WIDTH, SWA_KV_WIDTH,
            HG_HEADS * HG_DK, HG_HEADS * HG_DK, HG_WIDTH, HG_WIDTH,
            MLA_Q_RANK, MLA_KV_RANK, MLA_ROPE)

LANES = 128
MLA_QK_PAD = 2 * LANES
VMEM_LIMIT_BYTES = 56 * 1024 * 1024

HG_CHUNK = 256
NT_DIMS = (((1,), (1,)), ((), ()))
TN_DIMS = (((0,), (0,)), ((), ()))


def _params(*semantics):
    return pltpu.CompilerParams(dimension_semantics=semantics, vmem_limit_bytes=VMEM_LIMIT_BYTES)


def _rms(x, gain=None):
    y = x * lax.rsqrt(jnp.mean(x * x, axis=-1, keepdims=True) + EPS)
    return y if gain is None else y * gain


def _rms_matmul_kernel(x_ref, g_ref, w_ref, o_ref, hn_ref):
    @pl.when(pl.program_id(1) == 0)
    def _():
        hn_ref[...] = _rms(x_ref[...], g_ref[...]).astype(BF16)

    o_ref[...] = jnp.dot(hn_ref[...], w_ref[...], preferred_element_type=F32).astype(o_ref.dtype)


def _rms_matmul(x, gain, w, *, tm, tn, out_dtype=F32):
    n, d = x.shape
    nout = w.shape[1]
    return pl.pallas_call(
        _rms_matmul_kernel,
        grid=(n // tm, nout // tn),
        in_specs=[pl.BlockSpec((tm, d), lambda i, j: (i, 0)),
                  pl.BlockSpec((1, d), lambda i, j: (0, 0)),
                  pl.BlockSpec((d, tn), lambda i, j: (0, j))],
        out_specs=pl.BlockSpec((tm, tn), lambda i, j: (i, j)),
        out_shape=jax.ShapeDtypeStruct((n, nout), out_dtype),
        scratch_shapes=[pltpu.VMEM((tm, d), BF16)],
        compiler_params=_params("parallel", "arbitrary"),
        name="rms_matmul",
    )(x, gain.reshape(1, d), w)


def _swa_kernel(sink_ref, q_ref, kp_ref, kc_ref, vp_ref, vc_ref, bias_ref, gq_ref, gk_ref, o_ref):
    blk = pl.program_id(1)
    q = q_ref[0]
    k = jnp.concatenate([kp_ref[0], kc_ref[0]], axis=0)
    v = jnp.concatenate([vp_ref[0], vc_ref[0]], axis=0)
    qi = lax.broadcasted_iota(I32, (WINDOW, 2 * WINDOW), 0)
    kj = lax.broadcasted_iota(I32, (WINDOW, 2 * WINDOW), 1)
    dist = qi + WINDOW - kj
    has_prev = jnp.where(blk > 0, 0, WINDOW)
    valid = (dist >= 0) & (dist < WINDOW) & (kj >= has_prev)
    grp = SWA_HEADS // SWA_KV_HEADS
    outs = []
    for g in range(SWA_KV_HEADS):
        lo = g * SWA_HEAD_DIM
        kg = _rms(k[:, lo:lo + SWA_HEAD_DIM], gk_ref[...]).astype(BF16)
        vg = v[:, lo:lo + SWA_HEAD_DIM].astype(BF16)
        for hh in range(grp):
            h = g * grp + hh
            qh = _rms(q[:, h * SWA_HEAD_DIM:(h + 1) * SWA_HEAD_DIM], gq_ref[...])
            qh = (qh * (SWA_HEAD_DIM ** -0.5)).astype(BF16)
            s = lax.dot_general(qh, kg, NT_DIMS, preferred_element_type=F32) + bias_ref[h]
            s = jnp.where(valid, s, NEG_INF)
            sink = sink_ref[h]
            m = jnp.maximum(jnp.max(s, axis=-1, keepdims=True), sink)
            e = jnp.exp(s - m)
            denom = jnp.sum(e, axis=-1, keepdims=True) + jnp.exp(sink - m)
            p = (e / denom).astype(BF16)
            outs.append(jnp.dot(p, vg, preferred_element_type=F32))
    o_ref[0] = jnp.concatenate(outs, axis=-1).astype(o_ref.dtype)


def _swa(u_swa, rel_bias, gq, gk, sinks):
    b, s, _ = u_swa.shape
    nb = s // WINDOW
    kcol = SWA_WIDTH // LANES
    vcol = kcol + 1
    prev = lambda bi, n, *_: (bi, jnp.maximum(n - 1, 0))
    return pl.pallas_call(
        _swa_kernel,
        grid_spec=pltpu.PrefetchScalarGridSpec(
            num_scalar_prefetch=1,
            grid=(b, nb),
            in_specs=[
                pl.BlockSpec((1, WINDOW, SWA_WIDTH), lambda bi, n, *_: (bi, n, 0)),
                pl.BlockSpec((1, WINDOW, LANES), lambda bi, n, *_: prev(bi, n) + (kcol,)),
                pl.BlockSpec((1, WINDOW, LANES), lambda bi, n, *_: (bi, n, kcol)),
                pl.BlockSpec((1, WINDOW, LANES), lambda bi, n, *_: prev(bi, n) + (vcol,)),
                pl.BlockSpec((1, WINDOW, LANES), lambda bi, n, *_: (bi, n, vcol)),
                pl.BlockSpec((SWA_HEADS, WINDOW, 2 * WINDOW), lambda bi, n, *_: (0, 0, 0)),
                pl.BlockSpec((1, SWA_HEAD_DIM), lambda bi, n, *_: (0, 0)),
                pl.BlockSpec((1, SWA_HEAD_DIM), lambda bi, n, *_: (0, 0)),
            ],
            out_specs=pl.BlockSpec((1, WINDOW, SWA_WIDTH), lambda bi, n, *_: (bi, n, 0)),
        ),
        out_shape=jax.ShapeDtypeStruct((b, s, SWA_WIDTH), BF16),
        compiler_params=_params("parallel", "parallel"),
        name="swa_attention",
    )(sinks, u_swa, u_swa, u_swa, u_swa, u_swa, rel_bias,
      gq.reshape(1, SWA_HEAD_DIM), gk.reshape(1, SWA_HEAD_DIM))


def _hgrn_level_matrix(c):
    t = jnp.arange(c)[:, None]
    r = jnp.arange(c)[None, :]
    mats = [(r <= t)]
    half = c // 2
    while half >= 1:
        mid = (t // (2 * half)) * (2 * half) + half
        is_q = (t & half) != 0
        mats.append(jnp.where(is_q, (r >= mid) & (r <= t), (r > t) & (r < mid)))
        half //= 2
    return jnp.concatenate(mats, axis=0).astype(BF16)


def _hgrn_kernel(q_ref, f_ref, v_ref, gate_ref, par_ref, amat_ref, o_ref, st_ref, *, chunk):
    @pl.when(pl.program_id(2) == 0)
    def _():
        st_ref[...] = jnp.zeros_like(st_ref)

    c = chunk
    q = q_ref[0]
    f = f_ref[0]
    v = v_ref[0]
    gate = gate_ref[0]
    par = par_ref[0]
    log_lb, log1m_lb, one_m_lb, g_out = par[0:1], par[1:2], par[2:3], par[3:4]

    log_sig = jnp.minimum(f, 0.0) - jnp.log1p(jnp.exp(-jnp.abs(f)))
    bb = log1m_lb + log_sig
    log_f = jnp.maximum(log_lb, bb) + jnp.log1p(jnp.exp(-jnp.abs(log_lb - bb)))
    kk = one_m_lb / (1.0 + jnp.exp(f))

    g_hi = log_f.astype(BF16)
    g_lo = (log_f - g_hi.astype(F32)).astype(BF16)
    e2 = jnp.dot(amat_ref[...], jnp.concatenate([g_hi, g_lo], axis=1), preferred_element_type=F32)
    expo = e2[:, :HG_DK] + e2[:, HG_DK:]
    bcum = expo[0:c]
    btot = bcum[c - 1:c]

    row = lax.broadcasted_iota(I32, (c, 1), 0)
    txs = lax.broadcasted_iota(I32, (c, c), 0) ^ lax.broadcasted_iota(I32, (c, c), 1)
    attn = jnp.zeros((c, c), F32)
    half = c // 2
    lvl = 1
    while half >= 1:
        w = jnp.exp(expo[lvl * c:(lvl + 1) * c])
        is_q = (row & half) != 0
        qt = jnp.where(is_q, q * w, 0.0).astype(BF16)
        kt = jnp.where(is_q, 0.0, kk * w).astype(BF16)
        a = lax.dot_general(qt, kt, NT_DIMS, preferred_element_type=F32)
        attn = jnp.where((txs & (-half)) == half, a, attn)
        half //= 2
        lvl += 1

    vb = v.astype(BF16)
    diag = jnp.sum(q * kk, axis=-1, keepdims=True)
    intra = jnp.dot(attn.astype(BF16), vb, preferred_element_type=F32) + diag * v
    st = st_ref[...]
    qe = (q * jnp.exp(bcum)).astype(BF16)
    inter = lax.dot_general(qe, st.astype(BF16), NT_DIMS, preferred_element_type=F32)
    o = inter + intra
    kd = (kk * jnp.exp(btot - bcum)).astype(BF16)
    st_ref[...] = st * jnp.exp(btot) + lax.dot_general(vb, kd, TN_DIMS, preferred_element_type=F32)

    o = _rms(o, g_out)
    o_ref[0] = (o * gate / (1.0 + jnp.exp(-gate))).astype(o_ref.dtype)


def _hgrn(u_hg, par, *, chunk):
    b, s, _ = u_hg.shape
    amat = _hgrn_level_matrix(chunk)
    col = lambda off: (lambda bi, h, c: (bi, c, off + h))
    return pl.pallas_call(
        functools.partial(_hgrn_kernel, chunk=chunk),
        grid=(b, HG_HEADS, s // chunk),
        in_specs=[pl.BlockSpec((1, chunk, HG_DK), col(0)),
                  pl.BlockSpec((1, chunk, HG_DK), col(HG_HEADS)),
                  pl.BlockSpec((1, chunk, HG_DV), col(2 * HG_HEADS)),
                  pl.BlockSpec((1, chunk, HG_DV), col(3 * HG_HEADS)),
                  pl.BlockSpec((1, 8, HG_DK), lambda bi, h, c: (h, 0, 0)),
                  pl.BlockSpec(amat.shape, lambda bi, h, c: (0, 0))],
        out_specs=pl.BlockSpec((1, chunk, HG_DV), lambda bi, h, c: (bi, c, h)),
        out_shape=jax.ShapeDtypeStruct((b, s, HG_WIDTH), BF16),
        scratch_shapes=[pltpu.VMEM((HG_DV, HG_DK), F32)],
        compiler_params=_params("parallel", "parallel", "arbitrary"),
        name="hgrn2",
    )(u_hg, u_hg, u_hg, u_hg, par, amat)


def _mla_prep_kernel(u_ref, tab_ref, wq_ref, wkv_ref, gcq_ref, gckv_ref, gqn_ref, gqr_ref, gkn_ref, gkr_ref,
                     q_ref, k_ref, v_ref):
    u = u_ref[0]
    cq = _rms(u[:, :MLA_Q_RANK], gcq_ref[...]).astype(BF16)
    ckv = _rms(u[:, MLA_Q_RANK:MLA_Q_RANK + MLA_KV_RANK], gckv_ref[...]).astype(BF16)
    kr = u[:, MLA_Q_RANK + MLA_KV_RANK:]
    qf = jnp.dot(cq, wq_ref[...], preferred_element_type=F32)
    kvf = jnp.dot(ckv, wkv_ref[...], preferred_element_type=F32)
    tab = tab_ref[...]
    low = lax.broadcasted_iota(I32, kr.shape, 1) < MLA_ROPE
    kr_sq = jnp.sum(jnp.where(low, kr * kr, 0.0), axis=-1, keepdims=True)
    scale = MLA_QK ** -0.5
    for h in range(MLA_HEADS):
        qn = qf[:, h * MLA_QK_PAD:h * MLA_QK_PAD + MLA_NOPE]
        qr = qf[:, h * MLA_QK_PAD + MLA_NOPE:(h + 1) * MLA_QK_PAD]
        ss = jnp.sum(qn * qn, axis=-1, keepdims=True) + jnp.sum(jnp.where(low, qr * qr, 0.0), axis=-1, keepdims=True)
        rstd = lax.rsqrt(ss / MLA_QK + EPS) * scale
        t = qr * rstd * tab * gqr_ref[...]
        rope = jnp.where(low, t + pltpu.roll(t, MLA_ROPE, 1), 0.0)
        q_ref[0, h] = jnp.concatenate([qn * rstd * gqn_ref[...], rope], axis=-1).astype(q_ref.dtype)

        kn = kvf[:, h * MLA_NOPE:(h + 1) * MLA_NOPE]
        ss = jnp.sum(kn * kn, axis=-1, keepdims=True) + kr_sq
        rstd = lax.rsqrt(ss / MLA_QK + EPS)
        t = kr * rstd * tab * gkr_ref[...]
        rope = t + pltpu.roll(t, MLA_ROPE, 1)
        k_ref[0, h] = jnp.concatenate([kn * rstd * gkn_ref[...], rope], axis=-1).astype(k_ref.dtype)
        v_ref[0, h] = kvf[:, MLA_HEADS * MLA_NOPE + h * MLA_V:MLA_HEADS * MLA_NOPE + (h + 1) * MLA_V].astype(v_ref.dtype)


def _swap_halves(a):
    half = a.shape[-1] // 2
    return jnp.concatenate([a[..., half:], a[..., :half]], axis=-1)


def _mla_prep(u_mla, w_uq, w_ukv, g_cq, g_ckv, gq, gk, *, tm):
    b, s, wu = u_mla.shape
    wq = w_uq.reshape(MLA_Q_RANK, MLA_HEADS, MLA_QK)
    wq = jnp.concatenate([wq, _swap_halves(wq[..., MLA_NOPE:])], axis=-1).reshape(MLA_Q_RANK, MLA_HEADS * MLA_QK_PAD)
    wkv = w_ukv.reshape(MLA_KV_RANK, MLA_HEADS, MLA_NOPE + MLA_V)
    wkv = jnp.concatenate([wkv[..., :MLA_NOPE].reshape(MLA_KV_RANK, -1), wkv[..., MLA_NOPE:].reshape(MLA_KV_RANK, -1)], axis=1)
    half = MLA_ROPE // 2
    inv_freq = ROPE_THETA ** (-jnp.arange(half, dtype=F32) / half)
    ang = jnp.arange(s, dtype=F32)[:, None] * inv_freq[None, :]
    cos, sin = jnp.cos(ang), jnp.sin(ang)
    tab = jnp.concatenate([cos, cos, -sin, sin], axis=-1)
    rope_gain = lambda g: jnp.concatenate([g[MLA_NOPE:], _swap_halves(g[MLA_NOPE:])]).reshape(1, 2 * MLA_ROPE)
    vec = lambda i, j: (0, 0)
    head_out = lambda width: pl.BlockSpec((1, MLA_HEADS, tm, width), lambda i, j: (i, 0, j, 0))
    return pl.pallas_call(
        _mla_prep_kernel,
        grid=(b, s // tm),
        in_specs=[pl.BlockSpec((1, tm, wu), lambda i, j: (i, j, 0)),
                  pl.BlockSpec((tm, 2 * MLA_ROPE), lambda i, j: (j, 0)),
                  pl.BlockSpec(wq.shape, vec),
                  pl.BlockSpec(wkv.shape, vec),
                  pl.BlockSpec((1, MLA_Q_RANK), vec),
                  pl.BlockSpec((1, MLA_KV_RANK), vec),
                  pl.BlockSpec((1, MLA_NOPE), vec),
                  pl.BlockSpec((1, 2 * MLA_ROPE), vec),
                  pl.BlockSpec((1, MLA_NOPE), vec),
                  pl.BlockSpec((1, 2 * MLA_ROPE), vec)],
        out_specs=[head_out(MLA_QK_PAD), head_out(MLA_QK_PAD), head_out(MLA_V)],
        out_shape=[jax.ShapeDtypeStruct((b, MLA_HEADS, s, MLA_QK_PAD), BF16),
                   jax.ShapeDtypeStruct((b, MLA_HEADS, s, MLA_QK_PAD), BF16),
                   jax.ShapeDtypeStruct((b, MLA_HEADS, s, MLA_V), BF16)],
        compiler_params=_params("parallel", "parallel"),
        name="mla_prep",
    )(u_mla, tab, wq.astype(BF16), wkv.astype(BF16), g_cq.reshape(1, -1), g_ckv.reshape(1, -1),
      gq[:MLA_NOPE].reshape(1, -1), rope_gain(gq), gk[:MLA_NOPE].reshape(1, -1), rope_gain(gk))


def _mla_attn_kernel(q_ref, k_ref, v_ref, o_ref, m_ref, l_ref, acc_ref, *, tq):
    qi = pl.program_id(2)
    ki = pl.program_id(3)

    @pl.when(ki == 0)
    def _():
        m_ref[...] = jnp.full_like(m_ref, NEG_INF)
        l_ref[...] = jnp.zeros_like(l_ref)
        acc_ref[...] = jnp.zeros_like(acc_ref)

    @pl.when(ki <= qi)
    def _():
        s = lax.dot_general(q_ref[0, 0], k_ref[0, 0], NT_DIMS, preferred_element_type=F32)
        row = lax.broadcasted_iota(I32, s.shape, 0)
        col = lax.broadcasted_iota(I32, s.shape, 1)
        s = jnp.where((ki < qi) | (col <= row), s, NEG_INF)
        m_prev = m_ref[...]
        m_next = jnp.maximum(m_prev, jnp.max(s, axis=-1, keepdims=True))
        alpha = jnp.exp(m_prev - m_next)
        p = jnp.exp(s - m_next[:, :1])
        l_ref[...] = alpha * l_ref[...] + jnp.sum(p, axis=-1, keepdims=True)
        acc_ref[...] = alpha * acc_ref[...] + jnp.dot(p.astype(BF16), v_ref[0, 0], preferred_element_type=F32)
        m_ref[...] = m_next

    @pl.when(ki == pl.num_programs(3) - 1)
    def _():
        o_ref[0] = (acc_ref[...] / l_ref[...]).astype(o_ref.dtype)


def _mla_attention(q, k, v, *, tq):
    b, h, s, _ = q.shape
    nq = s // tq
    kv_idx = lambda bi, hi, qi, ki: (bi, hi, jnp.minimum(ki, qi), 0)
    return pl.pallas_call(
        functools.partial(_mla_attn_kernel, tq=tq),
        grid=(b, h, nq, nq),
        in_specs=[pl.BlockSpec((1, 1, tq, MLA_QK_PAD), lambda bi, hi, qi, ki: (bi, hi, qi, 0)),
                  pl.BlockSpec((1, 1, tq, MLA_QK_PAD), kv_idx),
                  pl.BlockSpec((1, 1, tq, MLA_V), kv_idx)],
        out_specs=pl.BlockSpec((1, tq, MLA_V), lambda bi, hi, qi, ki: (bi, qi, hi)),
        out_shape=jax.ShapeDtypeStruct((b, s, h * MLA_V), BF16),
        scratch_shapes=[pltpu.VMEM((tq, MLA_V), F32), pltpu.VMEM((tq, MLA_V), F32), pltpu.VMEM((tq, MLA_V), F32)],
        compiler_params=_params("parallel", "parallel", "parallel", "arbitrary"),
        name="mla_attention",
    )(q, k, v)


def _out_proj_kernel(x_ref, a_ref, b_ref, c_ref, wa_ref, wb_ref, wc_ref, o_ref):
    acc = jnp.dot(a_ref[...], wa_ref[...], preferred_element_type=F32)
    acc += jnp.dot(b_ref[...], wb_ref[...], preferred_element_type=F32)
    acc += jnp.dot(c_ref[...], wc_ref[...], preferred_element_type=F32)
    o_ref[...] = x_ref[...] + acc


def _out_proj(x, a, bmix, c, w_out, *, tm):
    n, d = x.shape
    wa = w_out[:SWA_WIDTH].astype(BF16)
    wb = w_out[SWA_WIDTH:SWA_WIDTH + HG_WIDTH].astype(BF16)
    wc = w_out[SWA_WIDTH + HG_WIDTH:].astype(BF16)
    row = lambda width: pl.BlockSpec((tm, width), lambda i: (i, 0))
    full = lambda w: pl.BlockSpec(w.shape, lambda i: (0, 0))
    return pl.pallas_call(
        _out_proj_kernel,
        grid=(n // tm,),
        in_specs=[row(d), row(SWA_WIDTH), row(HG_WIDTH), row(MLA_WIDTH), full(wa), full(wb), full(wc)],
        out_specs=row(d),
        out_shape=jax.ShapeDtypeStruct((n, d), F32),
        compiler_params=_params("parallel"),
        name="out_proj",
    )(x, a, bmix, c, wa, wb, wc)


def _mem_attn_kernel(x_ref, kv_ref, g_ref, wq_ref, wo_ref, gq_ref, gk_ref, o_ref):
    x = x_ref[0]
    hn = _rms(x, g_ref[...]).astype(BF16)
    qf = jnp.dot(hn, wq_ref[...], preferred_element_type=F32)
    kv = kv_ref[0]
    outs = []
    for h in range(MEM_HEADS):
        lo = h * MEM_HEAD_DIM
        qh = (_rms(qf[:, lo:lo + MEM_HEAD_DIM], gq_ref[...]) * (MEM_HEAD_DIM ** -0.5)).astype(BF16)
        kh = _rms(kv[:, lo:lo + MEM_HEAD_DIM], gk_ref[...]).astype(BF16)
        vh = kv[:, MEM_WIDTH + lo:MEM_WIDTH + lo + MEM_HEAD_DIM].astype(BF16)
        s = lax.dot_general(qh, kh, NT_DIMS, preferred_element_type=F32)
        e = jnp.exp(s - jnp.max(s, axis=-1, keepdims=True))
        p = (e / jnp.sum(e, axis=-1, keepdims=True)).astype(BF16)
        outs.append(jnp.dot(p, vh, preferred_element_type=F32))
    o = jnp.concatenate(outs, axis=-1).astype(BF16)
    o_ref[0] = x + jnp.dot(o, wo_ref[...], preferred_element_type=F32)


def _mem_attn(x, kv, g_q, w_mq, w_mo, mem_gq, mem_gk, *, tm):
    b, s, d = x.shape
    m = kv.shape[1]
    vec = lambda i, j: (0, 0)
    return pl.pallas_call(
        _mem_attn_kernel,
        grid=(b, s // tm),
        in_specs=[pl.BlockSpec((1, tm, d), lambda i, j: (i, j, 0)),
                  pl.BlockSpec((1, m, 2 * MEM_WIDTH), lambda i, j: (i, 0, 0)),
                  pl.BlockSpec((1, d), vec),
                  pl.BlockSpec((d, MEM_WIDTH), vec),
                  pl.BlockSpec((MEM_WIDTH, d), vec),
                  pl.BlockSpec((1, MEM_HEAD_DIM), vec),
                  pl.BlockSpec((1, MEM_HEAD_DIM), vec)],
        out_specs=pl.BlockSpec((1, tm, d), lambda i, j: (i, j, 0)),
        out_shape=jax.ShapeDtypeStruct((b, s, d), F32),
        compiler_params=_params("parallel", "parallel"),
        name="mem_attention",
    )(x, kv, g_q.reshape(1, d), w_mq.astype(BF16), w_mo.astype(BF16),
      mem_gq.reshape(1, -1), mem_gk.reshape(1, -1))


def _split_bf16(a):
    hi = a.astype(BF16)
    return hi, (a - hi.astype(F32)).astype(BF16)


def _router_kernel(x_ref, g_ref, whi_ref, wlo_ref, b_ref, hn_ref, ids_ref, gates_ref):
    hn = _rms(x_ref[...], g_ref[...])
    hn_ref[...] = hn
    hi, lo = _split_bf16(hn)
    logits = (jnp.dot(hi, whi_ref[...], preferred_element_type=F32)
              + jnp.dot(hi, wlo_ref[...], preferred_element_type=F32)
              + jnp.dot(lo, whi_ref[...], preferred_element_type=F32)) + b_ref[...]
    lane = lax.broadcasted_iota(I32, logits.shape, 1)
    gl = jnp.where(lane < N_GROUPS, logits, NEG_INF)
    gmax = jnp.max(gl, axis=-1, keepdims=True)
    p_grp = 1.0 / jnp.sum(jnp.exp(gl - gmax), axis=-1, keepdims=True)
    grp = jnp.min(jnp.where(gl == gmax, lane, LANES), axis=-1, keepdims=True)
    in_grp = (lane >= N_GROUPS) & (lane < N_GROUPS + N_EXPERTS) & (((lane - N_GROUPS) // EXPERTS_PER_GROUP) == grp)
    el = jnp.where(in_grp, logits, NEG_INF)
    m1 = jnp.max(el, axis=-1, keepdims=True)
    i1 = jnp.min(jnp.where(el == m1, lane, LANES), axis=-1, keepdims=True)
    el2 = jnp.where(lane == i1, NEG_INF, el)
    m2 = jnp.max(el2, axis=-1, keepdims=True)
    i2 = jnp.min(jnp.where(el2 == m2, lane, LANES), axis=-1, keepdims=True)
    r = jnp.exp(m2 - m1)
    g1 = p_grp / (1.0 + r)
    ids_ref[...] = jnp.where(lane == 0, i1 - N_GROUPS, jnp.where(lane == 1, i2 - N_GROUPS, 0))
    gates_ref[...] = jnp.where(lane == 0, g1, jnp.where(lane == 1, g1 * r, 0.0))


def _router(x, g_ffn, w_gr, b_gr, w_er, b_er, *, tm):
    n, d = x.shape
    pad = LANES - N_GROUPS - N_EXPERTS
    w = jnp.concatenate([w_gr, w_er, jnp.zeros((d, pad), F32)], axis=1)
    bias = jnp.concatenate([b_gr, b_er, jnp.zeros((pad,), F32)]).reshape(1, LANES)
    whi, wlo = _split_bf16(w)
    row = lambda width: pl.BlockSpec((tm, width), lambda i: (i, 0))
    vec = lambda i: (0, 0)
    return pl.pallas_call(
        _router_kernel,
        grid=(n // tm,),
        in_specs=[row(d), pl.BlockSpec((1, d), vec), pl.BlockSpec((d, LANES), vec),
                  pl.BlockSpec((d, LANES), vec), pl.BlockSpec((1, LANES), vec)],
        out_specs=[row(d), row(LANES), row(LANES)],
        out_shape=[jax.ShapeDtypeStruct((n, d), F32), jax.ShapeDtypeStruct((n, LANES), I32),
                   jax.ShapeDtypeStruct((n, LANES), F32)],
        compiler_params=_params("parallel"),
        name="moe_router",
    )(x, g_ffn.reshape(1, d), whi, wlo, bias)


def _row_copy(src_hbm, src_row, buf, slot, dst_row, sem):
    return pltpu.make_async_copy(src_hbm.at[pl.ds(src_row, 1)], buf.at[slot, pl.ds(dst_row, 1)], sem.at[slot])


def _expert_kernel(bexp_ref, tok_ref, nblk_ref, h_hbm, wg_ref, wu_ref, wd_ref, y_ref,
                   xbuf, sem, wg_bf, wu_bf, wd_bf):
    i = pl.program_id(0)
    n_used = nblk_ref[0]

    def start_gather(blk, slot):
        for r in range(MOE_BLOCK):
            _row_copy(h_hbm, tok_ref[blk * MOE_BLOCK + r], xbuf, slot, r, sem).start()

    def wait_gather(slot):
        for r in range(MOE_BLOCK):
            _row_copy(h_hbm, 0, xbuf, slot, r, sem).wait()

    @pl.when(i == 0)
    def _():
        start_gather(0, 0)

    @pl.when(i + 1 < n_used)
    def _():
        start_gather(i + 1, (i + 1) % 2)

    @pl.when((i == 0) | (bexp_ref[i] != bexp_ref[jnp.maximum(i - 1, 0)]))
    def _():
        wg_bf[...] = wg_ref[0].astype(BF16)
        wu_bf[...] = wu_ref[0].astype(BF16)
        wd_bf[...] = wd_ref[0].astype(BF16)

    @pl.when(i < n_used)
    def _():
        wait_gather(i % 2)
        xb = xbuf[i % 2].astype(BF16)
        gate = jnp.dot(xb, wg_bf[...], preferred_element_type=F32)
        up = jnp.dot(xb, wu_bf[...], preferred_element_type=F32)
        act = (gate / (1.0 + jnp.exp(-gate)) * up).astype(BF16)
        y_ref[...] = jnp.dot(act, wd_bf[...], preferred_element_type=F32)

    @pl.when(i >= n_used)
    def _():
        y_ref[...] = jnp.zeros_like(y_ref)


def _experts(hn, block_expert, slot_tok, n_used, w_gate, w_up, w_down):
    n, d = hn.shape
    n_blocks = block_expert.shape[0]
    wmap = lambda i, bexp, tok, nblk: (bexp[i], 0, 0)
    return pl.pallas_call(
        _expert_kernel,
        grid_spec=pltpu.PrefetchScalarGridSpec(
            num_scalar_prefetch=3,
            grid=(n_blocks,),
            in_specs=[pl.BlockSpec(memory_space=pl.ANY),
                      pl.BlockSpec((1, d, D_EXPERT), wmap),
                      pl.BlockSpec((1, d, D_EXPERT), wmap),
                      pl.BlockSpec((1, D_EXPERT, d), wmap)],
            out_specs=pl.BlockSpec((MOE_BLOCK, d), lambda i, *_: (i, 0)),
            scratch_shapes=[pltpu.VMEM((2, MOE_BLOCK, d), F32),
                            pltpu.SemaphoreType.DMA((2,)),
                            pltpu.VMEM((d, D_EXPERT), BF16),
                            pltpu.VMEM((d, D_EXPERT), BF16),
                            pltpu.VMEM((D_EXPERT, d), BF16)],
        ),
        out_shape=jax.ShapeDtypeStruct((n_blocks * MOE_BLOCK, d), F32),
        compiler_params=_params("arbitrary"),
        name="moe_experts",
    )(block_expert, slot_tok, n_used, hn, w_gate, w_up, w_down)


def _combine_kernel(pos_ref, x_ref, gates_ref, y_hbm, o_ref, ybuf, sem, *, tt):
    i = pl.program_id(0)
    nsteps = pl.num_programs(0)

    def start_gather(step, slot):
        for r in range(tt):
            for kk in range(TOP_K):
                _row_copy(y_hbm, pos_ref[(step * tt + r) * TOP_K + kk], ybuf, slot, kk * tt + r, sem).start()

    def wait_gather(slot):
        for r in range(TOP_K * tt):
            _row_copy(y_hbm, 0, ybuf, slot, r, sem).wait()

    @pl.when(i == 0)
    def _():
        start_gather(0, 0)

    @pl.when(i + 1 < nsteps)
    def _():
        start_gather(i + 1, (i + 1) % 2)

    wait_gather(i % 2)
    yb = ybuf[i % 2]
    g = gates_ref[...]
    o_ref[...] = x_ref[...] + g[:, 0:1] * yb[:tt] + g[:, 1:2] * yb[tt:]


def _combine(x, gates, pos, y, *, tt):
    n, d = x.shape
    return pl.pallas_call(
        functools.partial(_combine_kernel, tt=tt),
        grid_spec=pltpu.PrefetchScalarGridSpec(
            num_scalar_prefetch=1,
            grid=(n // tt,),
            in_specs=[pl.BlockSpec((tt, d), lambda i, *_: (i, 0)),
                      pl.BlockSpec((tt, LANES), lambda i, *_: (i, 0)),
                      pl.BlockSpec(memory_space=pl.ANY)],
            out_specs=pl.BlockSpec((tt, d), lambda i, *_: (i, 0)),
            scratch_shapes=[pltpu.VMEM((2, TOP_K * tt, d), F32), pltpu.SemaphoreType.DMA((2,))],
        ),
        out_shape=jax.ShapeDtypeStruct((n, d), F32),
        compiler_params=_params("arbitrary"),
        name="moe_combine",
    )(pos, x, gates, y)


def _moe_plan(expert_id):
    n = expert_id.shape[0]
    a = n * TOP_K
    flat_e = expert_id.reshape(a)
    order = jnp.argsort(flat_e)
    e_sorted = flat_e[order]
    counts = jnp.bincount(flat_e, length=N_EXPERTS)
    padded = (counts + MOE_BLOCK - 1) // MOE_BLOCK * MOE_BLOCK
    padded_end = jnp.cumsum(padded)
    start = jnp.cumsum(counts) - counts
    dest = ((padded_end - padded)[e_sorted] + jnp.arange(a) - start[e_sorted]).astype(I32)
    n_blocks = -(-a // MOE_BLOCK) + N_EXPERTS
    slot_tok = jnp.zeros((n_blocks * MOE_BLOCK,), I32).at[dest].set((order // TOP_K).astype(I32))
    block_expert = jnp.minimum(
        jnp.searchsorted(padded_end, jnp.arange(n_blocks) * MOE_BLOCK, side='right'), N_EXPERTS - 1).astype(I32)
    pos = jnp.zeros((a,), I32).at[order].set(dest)
    n_used = (padded_end[-1] // MOE_BLOCK).astype(I32).reshape(1)
    return block_expert, slot_tok, n_used, pos


def _band_relative_bias(table):
    def bucket(nd):
        max_exact = REL_BUCKETS // 2
        nf = jnp.maximum(nd, 1).astype(F32)
        large = max_exact + (jnp.log(nf / max_exact) / math.log(REL_MAX_DIST / max_exact)
                             * (REL_BUCKETS - max_exact)).astype(I32)
        return jnp.where(nd < max_exact, nd, jnp.minimum(large, REL_BUCKETS - 1))

    qi = jnp.arange(WINDOW)[:, None]
    kj = jnp.arange(2 * WINDOW)[None, :]
    dist = jnp.maximum(qi + WINDOW - kj, 0)
    return jnp.transpose(table[bucket(dist)], (2, 0, 1)).astype(F32)


def _hgrn_params(lb, g_out):
    lb = lb.reshape(HG_HEADS, 1, HG_DK)
    gain = jnp.broadcast_to(g_out.reshape(1, 1, HG_DV), (HG_HEADS, 1, HG_DV))
    rows = [jnp.log(lb), jnp.log1p(-lb), 1.0 - lb, gain, jnp.zeros((HG_HEADS, 4, HG_DK), F32)]
    return jnp.concatenate(rows, axis=1).astype(F32)


def kernel(x, mem, rel_bias_table, hg_lb_logits, g_mix, w_in, swa_gq, swa_gk, swa_sinks, hg_g_out, mla_g_cq, mla_g_ckv, mla_w_uq, mla_w_ukv, mla_gq, mla_gk, w_out, g_mem_q, g_mem_kv, w_mq, w_mkv, mem_gq, mem_gk, w_mo, g_ffn, w_group_router, b_group_router, w_expert_router, b_expert_router, w_gate, w_up, w_down):
    b, s, d = x.shape
    n = b * s
    m = mem.shape[1]
    rel_bias = _band_relative_bias(rel_bias_table)
    lb_all = jnp.cumsum(jax.nn.softmax(hg_lb_logits.astype(F32), axis=0), axis=0)
    lb_all = lb_all - lb_all[:1]
    offs = [0]
    for width in IN_SIZES:
        offs.append(offs[-1] + width)
    o_hg, o_cq, o_kr = offs[3], offs[7], offs[9]

    xf = x.reshape(n, d)
    mem2 = mem.reshape(b * m, d)
    for l in range(DEPTH):
        w = w_in[l]
        kr_cols = w[:, o_kr:]
        w_swa = w[:, :o_hg].astype(BF16)
        w_hg = w[:, o_hg:o_cq].astype(BF16)
        w_mla = jnp.concatenate([w[:, o_cq:], _swap_halves(kr_cols)], axis=1).astype(BF16)
        u_swa = _rms_matmul(xf, g_mix[l], w_swa, tm=1024, tn=w_swa.shape[1]).reshape(b, s, -1)
        u_hg = _rms_matmul(xf, g_mix[l], w_hg, tm=1024, tn=1024).reshape(b, s, -1)
        u_mla = _rms_matmul(xf, g_mix[l], w_mla, tm=1024, tn=w_mla.shape[1]).reshape(b, s, -1)

        out_a = _swa(u_swa, rel_bias, swa_gq[l], swa_gk[l], swa_sinks[l])
        out_b = _hgrn(u_hg, _hgrn_params(lb_all[l], hg_g_out[l]), chunk=HG_CHUNK)
        qm, km, vm = _mla_prep(u_mla, mla_w_uq[l], mla_w_ukv[l], mla_g_cq[l], mla_g_ckv[l],
                               mla_gq[l], mla_gk[l], tm=512)
        out_c = _mla_attention(qm, km, vm, tq=512)
        xf = _out_proj(xf, out_a.reshape(n, -1), out_b.reshape(n, -1), out_c.reshape(n, -1), w_out[l], tm=512)

        kv = _rms_matmul(mem2, g_mem_kv[l], w_mkv[l].astype(BF16), tm=b * m, tn=2 * MEM_WIDTH)
        xf = _mem_attn(xf.reshape(b, s, d), kv.reshape(b, m, -1), g_mem_q[l], w_mq[l], w_mo[l],
                       mem_gq[l], mem_gk[l], tm=512).reshape(n, d)

        hn, ids, gates = _router(xf, g_ffn[l], w_group_router[l], b_group_router[l],
                                 w_expert_router[l], b_expert_router[l], tm=512)
        block_expert, slot_tok, n_used, pos = _moe_plan(ids[:, :TOP_K])
        y = _experts(hn, block_expert, slot_tok, n_used, w_gate[l], w_up[l], w_down[l])
        xf = _combine(xf, gates, pos, y, tt=64)
    return xf.reshape(b, s, d)
```

```python
import functools
import math

import jax
import jax.numpy as jnp
from jax import lax
from jax.experimental import pallas as pl
from jax.experimental.pallas import tpu as pltpu

F32 = jnp.float32
BF16 = jnp.bfloat16
I32 = jnp.int32

D_MODEL = 2048
DEPTH = 2
SWA_HEADS = 8
SWA_KV_HEADS = 2
SWA_HEAD_DIM = 64
WINDOW = 128
HG_HEADS = 8
HG_DK = 128
HG_DV = 128
MLA_HEADS = 4
MLA_Q_RANK = 512
MLA_KV_RANK = 256
MLA_NOPE = 128
MLA_ROPE = 64
MLA_QK = MLA_NOPE + MLA_ROPE
MLA_V = 128
ROPE_THETA = 10000.0
REL_BUCKETS = 32
REL_MAX_DIST = 128
MEM_HEADS = 4
MEM_HEAD_DIM = 128
MEM_WIDTH = MEM_HEADS * MEM_HEAD_DIM
N_GROUPS = 8
EXPERTS_PER_GROUP = 8
N_EXPERTS = N_GROUPS * EXPERTS_PER_GROUP
TOP_K = 2
D_EXPERT = 512
MOE_ROWS = 256
EPS = 1e-6
NEG_INF = -1e30

SWA_WIDTH = SWA_HEADS * SWA_HEAD_DIM
SWA_KV_WIDTH = SWA_KV_HEADS * SWA_HEAD_DIM
HG_WIDTH = HG_HEADS * HG_DV
MLA_WIDTH = MLA_HEADS * MLA_V
IN_SIZES = (SWA_WIDTH, SWA_KV_WIDTH, SWA_KV_WIDTH,
            HG_HEADS * HG_DK, HG_HEADS * HG_DK, HG_WIDTH, HG_WIDTH,
            MLA_Q_RANK, MLA_KV_RANK, MLA_ROPE)

LANES = 128
MLA_QK_PAD = 2 * LANES
VMEM_LIMIT_BYTES = 56 * 1024 * 1024

HG_CHUNK = 256
NT_DIMS = (((1,), (1,)), ((), ()))
TN_DIMS = (((0,), (0,)), ((), ()))


def _params(*semantics):
    return pltpu.CompilerParams(dimension_semantics=semantics, vmem_limit_bytes=VMEM_LIMIT_BYTES)


def _rms(x, gain=None):
    y = x * lax.rsqrt(jnp.mean(x * x, axis=-1, keepdims=True) + EPS)
    return y if gain is None else y * gain


def _rms_matmul_kernel(x_ref, g_ref, w_ref, o_ref, hn_ref):
    @pl.when(pl.program_id(1) == 0)
    def _():
        hn_ref[...] = _rms(x_ref[...], g_ref[...]).astype(BF16)

    o_ref[...] = jnp.dot(hn_ref[...], w_ref[...], preferred_element_type=F32).astype(o_ref.dtype)


def _rms_matmul(x, gain, w, *, tm, tn, out_dtype=F32):
    n, d = x.shape
    nout = w.shape[1]
    return pl.pallas_call(
        _rms_matmul_kernel,
        grid=(n // tm, nout // tn),
        in_specs=[pl.BlockSpec((tm, d), lambda i, j: (i, 0)),
                  pl.BlockSpec((1, d), lambda i, j: (0, 0)),
                  pl.BlockSpec((d, tn), lambda i, j: (0, j))],
        out_specs=pl.BlockSpec((tm, tn), lambda i, j: (i, j)),
        out_shape=jax.ShapeDtypeStruct((n, nout), out_dtype),
        scratch_shapes=[pltpu.VMEM((tm, d), BF16)],
        compiler_params=_params("parallel", "arbitrary"),
        name="rms_matmul",
    )(x, gain.reshape(1, d), w)


def _swa_kernel(sink_ref, q_ref, kp_ref, kc_ref, vp_ref, vc_ref, bias_ref, gq_ref, gk_ref, o_ref):
    blk = pl.program_id(1)
    q = q_ref[0]
    k = jnp.concatenate([kp_ref[0], kc_ref[0]], axis=0)
    v = jnp.concatenate([vp_ref[0], vc_ref[0]], axis=0)
    qi = lax.broadcasted_iota(I32, (WINDOW, 2 * WINDOW), 0)
    kj = lax.broadcasted_iota(I32, (WINDOW, 2 * WINDOW), 1)
    dist = qi + WINDOW - kj
    has_prev = jnp.where(blk > 0, 0, WINDOW)
    valid = (dist >= 0) & (dist < WINDOW) & (kj >= has_prev)
    grp = SWA_HEADS // SWA_KV_HEADS
    outs = []
    for g in range(SWA_KV_HEADS):
        lo = g * SWA_HEAD_DIM
        kg = _rms(k[:, lo:lo + SWA_HEAD_DIM], gk_ref[...]).astype(BF16)
        vg = v[:, lo:lo + SWA_HEAD_DIM].astype(BF16)
        for hh in range(grp):
            h = g * grp + hh
            qh = _rms(q[:, h * SWA_HEAD_DIM:(h + 1) * SWA_HEAD_DIM], gq_ref[...])
            qh = (qh * (SWA_HEAD_DIM ** -0.5)).astype(BF16)
            s = lax.dot_general(qh, kg, NT_DIMS, preferred_element_type=F32) + bias_ref[h]
            s = jnp.where(valid, s, NEG_INF)
            sink = sink_ref[h]
            m = jnp.maximum(jnp.max(s, axis=-1, keepdims=True), sink)
            e = jnp.exp(s - m)
            denom = jnp.sum(e, axis=-1, keepdims=True) + jnp.exp(sink - m)
            p = (e / denom).astype(BF16)
            outs.append(jnp.dot(p, vg, preferred_element_type=F32))
    o_ref[0] = jnp.concatenate(outs, axis=-1).astype(o_ref.dtype)


def _swa(u_swa, rel_bias, gq, gk, sinks):
    b, s, _ = u_swa.shape
    nb = s // WINDOW
    kcol = SWA_WIDTH // LANES
    vcol = kcol + 1
    prev = lambda bi, n, *_: (bi, jnp.maximum(n - 1, 0))
    return pl.pallas_call(
        _swa_kernel,
        grid_spec=pltpu.PrefetchScalarGridSpec(
            num_scalar_prefetch=1,
            grid=(b, nb),
            in_specs=[
                pl.BlockSpec((1, WINDOW, SWA_WIDTH), lambda bi, n, *_: (bi, n, 0)),
                pl.BlockSpec((1, WINDOW, LANES), lambda bi, n, *_: prev(bi, n) + (kcol,)),
                pl.BlockSpec((1, WINDOW, LANES), lambda bi, n, *_: (bi, n, kcol)),
                pl.BlockSpec((1, WINDOW, LANES), lambda bi, n, *_: prev(bi, n) + (vcol,)),
                pl.BlockSpec((1, WINDOW, LANES), lambda bi, n, *_: (bi, n, vcol)),
                pl.BlockSpec((SWA_HEADS, WINDOW, 2 * WINDOW), lambda bi, n, *_: (0, 0, 0)),
                pl.BlockSpec((1, SWA_HEAD_DIM), lambda bi, n, *_: (0, 0)),
                pl.BlockSpec((1, SWA_HEAD_DIM), lambda bi, n, *_: (0, 0)),
            ],
            out_specs=pl.BlockSpec((1, WINDOW, SWA_WIDTH), lambda bi, n, *_: (bi, n, 0)),
        ),
        out_shape=jax.ShapeDtypeStruct((b, s, SWA_WIDTH), BF16),
        compiler_params=_params("parallel", "parallel"),
        name="swa_attention",
    )(sinks, u_swa, u_swa, u_swa, u_swa, u_swa, rel_bias,
      gq.reshape(1, SWA_HEAD_DIM), gk.reshape(1, SWA_HEAD_DIM))


def _hgrn_level_matrix(c):
    t = jnp.arange(c)[:, None]
    r = jnp.arange(c)[None, :]
    mats = [(r <= t)]
    half = c // 2
    while half >= 1:
        mid = (t // (2 * half)) * (2 * half) + half
        is_q = (t & half) != 0
        mats.append(jnp.where(is_q, (r >= mid) & (r <= t), (r > t) & (r < mid)))
        half //= 2
    return jnp.concatenate(mats, axis=0).astype(BF16)


def _hgrn_kernel(q_ref, f_ref, v_ref, gate_ref, par_ref, amat_ref, o_ref, st_ref, *, chunk):
    @pl.when(pl.program_id(2) == 0)
    def _():
        st_ref[...] = jnp.zeros_like(st_ref)

    c = chunk
    q = q_ref[0]
    f = f_ref[0]
    v = v_ref[0]
    gate = gate_ref[0]
    par = par_ref[0]
    log_lb, log1m_lb, one_m_lb, g_out = par[0:1], par[1:2], par[2:3], par[3:4]

    log_sig = jnp.minimum(f, 0.0) - jnp.log1p(jnp.exp(-jnp.abs(f)))
    bb = log1m_lb + log_sig
    log_f = jnp.maximum(log_lb, bb) + jnp.log1p(jnp.exp(-jnp.abs(log_lb - bb)))
    kk = one_m_lb / (1.0 + jnp.exp(f))

    g_hi = log_f.astype(BF16)
    g_lo = (log_f - g_hi.astype(F32)).astype(BF16)
    e2 = jnp.dot(amat_ref[...], jnp.concatenate([g_hi, g_lo], axis=1), preferred_element_type=F32)
    expo = e2[:, :HG_DK] + e2[:, HG_DK:]
    bcum = expo[0:c]
    btot = bcum[c - 1:c]

    row = lax.broadcasted_iota(I32, (c, 1), 0)
    txs = lax.broadcasted_iota(I32, (c, c), 0) ^ lax.broadcasted_iota(I32, (c, c), 1)
    attn = jnp.zeros((c, c), F32)
    half = c // 2
    lvl = 1
    while half >= 1:
        w = jnp.exp(expo[lvl * c:(lvl + 1) * c])
        is_q = (row & half) != 0
        qt = jnp.where(is_q, q * w, 0.0).astype(BF16)
        kt = jnp.where(is_q, 0.0, kk * w).astype(BF16)
        a = lax.dot_general(qt, kt, NT_DIMS, preferred_element_type=F32)
        attn = jnp.where((txs & (-half)) == half, a, attn)
        half //= 2
        lvl += 1

    vb = v.astype(BF16)
    diag = jnp.sum(q * kk, axis=-1, keepdims=True)
    intra = jnp.dot(attn.astype(BF16), vb, preferred_element_type=F32) + diag * v
    st = st_ref[...]
    qe = (q * jnp.exp(bcum)).astype(BF16)
    inter = lax.dot_general(qe, st.astype(BF16), NT_DIMS, preferred_element_type=F32)
    o = inter + intra
    kd = (kk * jnp.exp(btot - bcum)).astype(BF16)
    st_ref[...] = st * jnp.exp(btot) + lax.dot_general(vb, kd, TN_DIMS, preferred_element_type=F32)

    o = _rms(o, g_out)
    o_ref[0] = (o * gate / (1.0 + jnp.exp(-gate))).astype(o_ref.dtype)


def _hgrn(u_hg, par, *, chunk):
    b, s, _ = u_hg.shape
    amat = _hgrn_level_matrix(chunk)
    col = lambda off: (lambda bi, h, c: (bi, c, off + h))
    return pl.pallas_call(
        functools.partial(_hgrn_kernel, chunk=chunk),
        grid=(b, HG_HEADS, s // chunk),
        in_specs=[pl.BlockSpec((1, chunk, HG_DK), col(0)),
                  pl.BlockSpec((1, chunk, HG_DK), col(HG_HEADS)),
                  pl.BlockSpec((1, chunk, HG_DV), col(2 * HG_HEADS)),
                  pl.BlockSpec((1, chunk, HG_DV), col(3 * HG_HEADS)),
                  pl.BlockSpec((1, 8, HG_DK), lambda bi, h, c: (h, 0, 0)),
                  pl.BlockSpec(amat.shape, lambda bi, h, c: (0, 0))],
        out_specs=pl.BlockSpec((1, chunk, HG_DV), lambda bi, h, c: (bi, c, h)),
        out_shape=jax.ShapeDtypeStruct((b, s, HG_WIDTH), BF16),
        scratch_shapes=[pltpu.VMEM((HG_DV, HG_DK), F32)],
        compiler_params=_params("parallel", "parallel", "arbitrary"),
        name="hgrn2",
    )(u_hg, u_hg, u_hg, u_hg, par, amat)


def _mla_prep_kernel(u_ref, tab_ref, wq_ref, wkv_ref, gcq_ref, gckv_ref, gqn_ref, gqr_ref, gkn_ref, gkr_ref,
                     q_ref, k_ref, v_ref):
    u = u_ref[0]
    cq = _rms(u[:, :MLA_Q_RANK], gcq_ref[...]).astype(BF16)
    ckv = _rms(u[:, MLA_Q_RANK:MLA_Q_RANK + MLA_KV_RANK], gckv_ref[...]).astype(BF16)
    kr = u[:, MLA_Q_RANK + MLA_KV_RANK:]
    qf = jnp.dot(cq, wq_ref[...], preferred_element_type=F32)
    kvf = jnp.dot(ckv, wkv_ref[...], preferred_element_type=F32)
    tab = tab_ref[...]
    low = lax.broadcasted_iota(I32, kr.shape, 1) < MLA_ROPE
    kr_sq = jnp.sum(jnp.where(low, kr * kr, 0.0), axis=-1, keepdims=True)
    scale = MLA_QK ** -0.5
    for h in range(MLA_HEADS):
        qn = qf[:, h * MLA_QK_PAD:h * MLA_QK_PAD + MLA_NOPE]
        qr = qf[:, h * MLA_QK_PAD + MLA_NOPE:(h + 1) * MLA_QK_PAD]
        ss = jnp.sum(qn * qn, axis=-1, keepdims=True) + jnp.sum(jnp.where(low, qr * qr, 0.0), axis=-1, keepdims=True)
        rstd = lax.rsqrt(ss / MLA_QK + EPS) * scale
        t = qr * rstd * tab * gqr_ref[...]
        rope = jnp.where(low, t + pltpu.roll(t, MLA_ROPE, 1), 0.0)
        q_ref[0, h] = jnp.concatenate([qn * rstd * gqn_ref[...], rope], axis=-1).astype(q_ref.dtype)

        kn = kvf[:, h * MLA_NOPE:(h + 1) * MLA_NOPE]
        ss = jnp.sum(kn * kn, axis=-1, keepdims=True) + kr_sq
        rstd = lax.rsqrt(ss / MLA_QK + EPS)
        t = kr * rstd * tab * gkr_ref[...]
        rope = t + pltpu.roll(t, MLA_ROPE, 1)
        k_ref[0, h] = jnp.concatenate([kn * rstd * gkn_ref[...], rope], axis=-1).astype(k_ref.dtype)
        v_ref[0, h] = kvf[:, MLA_HEADS * MLA_NOPE + h * MLA_V:MLA_HEADS * MLA_NOPE + (h + 1) * MLA_V].astype(v_ref.dtype)


def _swap_halves(a):
    half = a.shape[-1] // 2
    return jnp.concatenate([a[..., half:], a[..., :half]], axis=-1)


def _mla_prep(u_mla, w_uq, w_ukv, g_cq, g_ckv, gq, gk, *, tm):
    b, s, wu = u_mla.shape
    wq = w_uq.reshape(MLA_Q_RANK, MLA_HEADS, MLA_QK)
    wq = jnp.concatenate([wq, _swap_halves(wq[..., MLA_NOPE:])], axis=-1).reshape(MLA_Q_RANK, MLA_HEADS * MLA_QK_PAD)
    wkv = w_ukv.reshape(MLA_KV_RANK, MLA_HEADS, MLA_NOPE + MLA_V)
    wkv = jnp.concatenate([wkv[..., :MLA_NOPE].reshape(MLA_KV_RANK, -1), wkv[..., MLA_NOPE:].reshape(MLA_KV_RANK, -1)], axis=1)
    half = MLA_ROPE // 2
    inv_freq = ROPE_THETA ** (-jnp.arange(half, dtype=F32) / half)
    ang = jnp.arange(s, dtype=F32)[:, None] * inv_freq[None, :]
    cos, sin = jnp.cos(ang), jnp.sin(ang)
    tab = jnp.concatenate([cos, cos, -sin, sin], axis=-1)
    rope_gain = lambda g: jnp.concatenate([g[MLA_NOPE:], _swap_halves(g[MLA_NOPE:])]).reshape(1, 2 * MLA_ROPE)
    vec = lambda i, j: (0, 0)
    head_out = lambda width: pl.BlockSpec((1, MLA_HEADS, tm, width), lambda i, j: (i, 0, j, 0))
    return pl.pallas_call(
        _mla_prep_kernel,
        grid=(b, s // tm),
        in_specs=[pl.BlockSpec((1, tm, wu), lambda i, j: (i, j, 0)),
                  pl.BlockSpec((tm, 2 * MLA_ROPE), lambda i, j: (j, 0)),
                  pl.BlockSpec(wq.shape, vec),
                  pl.BlockSpec(wkv.shape, vec),
                  pl.BlockSpec((1, MLA_Q_RANK), vec),
                  pl.BlockSpec((1, MLA_KV_RANK), vec),
                  pl.BlockSpec((1, MLA_NOPE), vec),
                  pl.BlockSpec((1, 2 * MLA_ROPE), vec),
                  pl.BlockSpec((1, MLA_NOPE), vec),
                  pl.BlockSpec((1, 2 * MLA_ROPE), vec)],
        out_specs=[head_out(MLA_QK_PAD), head_out(MLA_QK_PAD), head_out(MLA_V)],
        out_shape=[jax.ShapeDtypeStruct((b, MLA_HEADS, s, MLA_QK_PAD), BF16),
                   jax.ShapeDtypeStruct((b, MLA_HEADS, s, MLA_QK_PAD), BF16),
                   jax.ShapeDtypeStruct((b, MLA_HEADS, s, MLA_V), BF16)],
        compiler_params=_params("parallel", "parallel"),
        name="mla_prep",
    )(u_mla, tab, wq.astype(BF16), wkv.astype(BF16), g_cq.reshape(1, -1), g_ckv.reshape(1, -1),
      gq[:MLA_NOPE].reshape(1, -1), rope_gain(gq), gk[:MLA_NOPE].reshape(1, -1), rope_gain(gk))


def _mla_attn_kernel(q_ref, k_ref, v_ref, o_ref, m_ref, l_ref, acc_ref, *, tq):
    qi = pl.program_id(2)
    ki = pl.program_id(3)

    @pl.when(ki == 0)
    def _():
        m_ref[...] = jnp.full_like(m_ref, NEG_INF)
        l_ref[...] = jnp.zeros_like(l_ref)
        acc_ref[...] = jnp.zeros_like(acc_ref)

    @pl.when(ki <= qi)
    def _():
        s = lax.dot_general(q_ref[0, 0], k_ref[0, 0], NT_DIMS, preferred_element_type=F32)
        row = lax.broadcasted_iota(I32, s.shape, 0)
        col = lax.broadcasted_iota(I32, s.shape, 1)
        s = jnp.where((ki < qi) | (col <= row), s, NEG_INF)
        m_prev = m_ref[...]
        m_next = jnp.maximum(m_prev, jnp.max(s, axis=-1, keepdims=True))
        alpha = jnp.exp(m_prev - m_next)
        p = jnp.exp(s - m_next[:, :1])
        l_ref[...] = alpha * l_ref[...] + jnp.sum(p, axis=-1, keepdims=True)
        acc_ref[...] = alpha * acc_ref[...] + jnp.dot(p.astype(BF16), v_ref[0, 0], preferred_element_type=F32)
        m_ref[...] = m_next

    @pl.when(ki == pl.num_programs(3) - 1)
    def _():
        o_ref[0] = (acc_ref[...] / l_ref[...]).astype(o_ref.dtype)


def _mla_attention(q, k, v, *, tq):
    b, h, s, _ = q.shape
    nq = s // tq
    kv_idx = lambda bi, hi, qi, ki: (bi, hi, jnp.minimum(ki, qi), 0)
    return pl.pallas_call(
        functools.partial(_mla_attn_kernel, tq=tq),
        grid=(b, h, nq, nq),
        in_specs=[pl.BlockSpec((1, 1, tq, MLA_QK_PAD), lambda bi, hi, qi, ki: (bi, hi, qi, 0)),
                  pl.BlockSpec((1, 1, tq, MLA_QK_PAD), kv_idx),
                  pl.BlockSpec((1, 1, tq, MLA_V), kv_idx)],
        out_specs=pl.BlockSpec((1, tq, MLA_V), lambda bi, hi, qi, ki: (bi, qi, hi)),
        out_shape=jax.ShapeDtypeStruct((b, s, h * MLA_V), BF16),
        scratch_shapes=[pltpu.VMEM((tq, MLA_V), F32), pltpu.VMEM((tq, MLA_V), F32), pltpu.VMEM((tq, MLA_V), F32)],
        compiler_params=_params("parallel", "parallel", "parallel", "arbitrary"),
        name="mla_attention",
    )(q, k, v)


def _out_proj_kernel(x_ref, a_ref, b_ref, c_ref, wa_ref, wb_ref, wc_ref, o_ref):
    acc = jnp.dot(a_ref[...], wa_ref[...], preferred_element_type=F32)
    acc += jnp.dot(b_ref[...], wb_ref[...], preferred_element_type=F32)
    acc += jnp.dot(c_ref[...], wc_ref[...], preferred_element_type=F32)
    o_ref[...] = x_ref[...] + acc


def _out_proj(x, a, bmix, c, w_out, *, tm):
    n, d = x.shape
    wa = w_out[:SWA_WIDTH].astype(BF16)
    wb = w_out[SWA_WIDTH:SWA_WIDTH + HG_WIDTH].astype(BF16)
    wc = w_out[SWA_WIDTH + HG_WIDTH:].astype(BF16)
    row = lambda width: pl.BlockSpec((tm, width), lambda i: (i, 0))
    full = lambda w: pl.BlockSpec(w.shape, lambda i: (0, 0))
    return pl.pallas_call(
        _out_proj_kernel,
        grid=(n // tm,),
        in_specs=[row(d), row(SWA_WIDTH), row(HG_WIDTH), row(MLA_WIDTH), full(wa), full(wb), full(wc)],
        out_specs=row(d),
        out_shape=jax.ShapeDtypeStruct((n, d), F32),
        compiler_params=_params("parallel"),
        name="out_proj",
    )(x, a, bmix, c, wa, wb, wc)


def _mem_attn_kernel(x_ref, kv_ref, g_ref, wq_ref, wo_ref, gq_ref, gk_ref, o_ref):
    x = x_ref[0]
    hn = _rms(x, g_ref[...]).astype(BF16)
    qf = jnp.dot(hn, wq_ref[...], preferred_element_type=F32)
    kv = kv_ref[0]
    outs = []
    for h in range(MEM_HEADS):
        lo = h * MEM_HEAD_DIM
        qh = (_rms(qf[:, lo:lo + MEM_HEAD_DIM], gq_ref[...]) * (MEM_HEAD_DIM ** -0.5)).astype(BF16)
        kh = _rms(kv[:, lo:lo + MEM_HEAD_DIM], gk_ref[...]).astype(BF16)
        vh = kv[:, MEM_WIDTH + lo:MEM_WIDTH + lo + MEM_HEAD_DIM].astype(BF16)
        s = lax.dot_general(qh, kh, NT_DIMS, preferred_element_type=F32)
        e = jnp.exp(s - jnp.max(s, axis=-1, keepdims=True))
        p = (e / jnp.sum(e, axis=-1, keepdims=True)).astype(BF16)
        outs.append(jnp.dot(p, vh, preferred_element_type=F32))
    o = jnp.concatenate(outs, axis=-1).astype(BF16)
    o_ref[0] = x + jnp.dot(o, wo_ref[...], preferred_element_type=F32)


def _mem_attn(x, kv, g_q, w_mq, w_mo, mem_gq, mem_gk, *, tm):
    b, s, d = x.shape
    m = kv.shape[1]
    vec = lambda i, j: (0, 0)
    return pl.pallas_call(
        _mem_attn_kernel,
        grid=(b, s // tm),
        in_specs=[pl.BlockSpec((1, tm, d), lambda i, j: (i, j, 0)),
                  pl.BlockSpec((1, m, 2 * MEM_WIDTH), lambda i, j: (i, 0, 0)),
                  pl.BlockSpec((1, d), vec),
                  pl.BlockSpec((d, MEM_WIDTH), vec),
                  pl.BlockSpec((MEM_WIDTH, d), vec),
                  pl.BlockSpec((1, MEM_HEAD_DIM), vec),
                  pl.BlockSpec((1, MEM_HEAD_DIM), vec)],
        out_specs=pl.BlockSpec((1, tm, d), lambda i, j: (i, j, 0)),
        out_shape=jax.ShapeDtypeStruct((b, s, d), F32),
        compiler_params=_params("parallel", "parallel"),
        name="mem_attention",
    )(x, kv, g_q.reshape(1, d), w_mq.astype(BF16), w_mo.astype(BF16),
      mem_gq.reshape(1, -1), mem_gk.reshape(1, -1))


def _split_bf16(a):
    hi = a.astype(BF16)
    return hi, (a - hi.astype(F32)).astype(BF16)


def _router_kernel(x_ref, g_ref, whi_ref, wlo_ref, b_ref, ids_ref, gates_ref, cnt_ref, carry_ref):
    @pl.when(pl.program_id(0) == 0)
    def _():
        carry_ref[...] = jnp.zeros_like(carry_ref)

    hn = _rms(x_ref[...], g_ref[...])
    hi, lo = _split_bf16(hn)
    logits = (jnp.dot(hi, whi_ref[...], preferred_element_type=F32)
              + jnp.dot(hi, wlo_ref[...], preferred_element_type=F32)
              + jnp.dot(lo, whi_ref[...], preferred_element_type=F32)) + b_ref[...]
    lane = lax.broadcasted_iota(I32, logits.shape, 1)
    gl = jnp.where(lane < N_GROUPS, logits, NEG_INF)
    gmax = jnp.max(gl, axis=-1, keepdims=True)
    p_grp = 1.0 / jnp.sum(jnp.exp(gl - gmax), axis=-1, keepdims=True)
    grp = jnp.min(jnp.where(gl == gmax, lane, LANES), axis=-1, keepdims=True)
    in_grp = (lane >= N_GROUPS) & (lane < N_GROUPS + N_EXPERTS) & (((lane - N_GROUPS) // EXPERTS_PER_GROUP) == grp)
    el = jnp.where(in_grp, logits, NEG_INF)
    m1 = jnp.max(el, axis=-1, keepdims=True)
    i1 = jnp.min(jnp.where(el == m1, lane, LANES), axis=-1, keepdims=True)
    el2 = jnp.where(lane == i1, NEG_INF, el)
    m2 = jnp.max(el2, axis=-1, keepdims=True)
    i2 = jnp.min(jnp.where(el2 == m2, lane, LANES), axis=-1, keepdims=True)
    r = jnp.exp(m2 - m1)
    g1 = p_grp / (1.0 + r)
    gates_ref[...] = jnp.where(lane == 0, g1, jnp.where(lane == 1, g1 * r, 0.0))

    tm = logits.shape[0]
    used = jnp.where((lane == i1) | (lane == i2), 1.0, 0.0)
    earlier = lax.broadcasted_iota(I32, (tm, tm), 1) < lax.broadcasted_iota(I32, (tm, tm), 0)
    before = carry_ref[...] + jnp.dot(jnp.where(earlier, 1.0, 0.0).astype(BF16), used.astype(BF16),
                                      preferred_element_type=F32)
    r1 = jnp.sum(jnp.where(lane == i1, before, 0.0), axis=-1, keepdims=True).astype(I32)
    r2 = jnp.sum(jnp.where(lane == i2, before, 0.0), axis=-1, keepdims=True).astype(I32)
    carry_ref[...] += jnp.sum(used, axis=0, keepdims=True)
    cnt_ref[...] = carry_ref[...].astype(I32)
    ids_ref[...] = jnp.where(lane == 0, i1 - N_GROUPS, jnp.where(lane == 1, i2 - N_GROUPS,
                             jnp.where(lane == 2, r1, jnp.where(lane == 3, r2, 0))))


def _router(x, g_ffn, w_gr, b_gr, w_er, b_er, *, tm):
    n, d = x.shape
    pad = LANES - N_GROUPS - N_EXPERTS
    w = jnp.concatenate([w_gr, w_er, jnp.zeros((d, pad), F32)], axis=1)
    bias = jnp.concatenate([b_gr, b_er, jnp.zeros((pad,), F32)]).reshape(1, LANES)
    whi, wlo = _split_bf16(w)
    row = lambda width: pl.BlockSpec((tm, width), lambda i: (i, 0))
    vec = lambda i: (0, 0)
    return pl.pallas_call(
        _router_kernel,
        grid=(n // tm,),
        in_specs=[row(d), pl.BlockSpec((1, d), vec), pl.BlockSpec((d, LANES), vec),
                  pl.BlockSpec((d, LANES), vec), pl.BlockSpec((1, LANES), vec)],
        out_specs=[row(LANES), row(LANES), pl.BlockSpec((1, LANES), vec)],
        out_shape=[jax.ShapeDtypeStruct((n, LANES), I32), jax.ShapeDtypeStruct((n, LANES), F32),
                   jax.ShapeDtypeStruct((1, LANES), I32)],
        scratch_shapes=[pltpu.VMEM((1, LANES), F32)],
        compiler_params=_params("arbitrary"),
        name="moe_router",
    )(x, g_ffn.reshape(1, d), whi, wlo, bias)


HI_HALF_MASK = 0xFFFF0000


def _dispatch_kernel(dest_ref, x_ref, g_ref, xs_in_hbm, xs_hbm, buf, sem, *, ts):
    del xs_in_hbm
    i = pl.program_id(0)
    last = pl.num_programs(0) - 1
    slot = i % 2

    def row_copy(slot_, r, dst_row):
        return pltpu.make_async_copy(buf.at[slot_, pl.ds(r, 1)], xs_hbm.at[pl.ds(dst_row, 1)], sem.at[slot_])

    def wait_step(slot_):
        for _ in range(ts * TOP_K):
            row_copy(slot_, 0, 0).wait()

    @pl.when(i >= 2)
    def _():
        wait_step(slot)

    hn = _rms(x_ref[...], g_ref[...])
    half = hn.shape[1] // 2
    lo = pltpu.bitcast(hn[:, :half].astype(BF16).astype(F32), jnp.uint32) >> 16
    hi = pltpu.bitcast(hn[:, half:].astype(BF16).astype(F32), jnp.uint32) & jnp.uint32(HI_HALF_MASK)
    buf[slot] = lo | hi
    for r in range(ts):
        for kk in range(TOP_K):
            row_copy(slot, r, dest_ref[(i * ts + r) * TOP_K + kk]).start()

    @pl.when(i == last)
    def _():
        wait_step(slot)

    @pl.when((i == last) & (i >= 1))
    def _():
        wait_step(1 - slot)


def _dispatch(x, g_ffn, dest, n_rows, *, ts):
    n, d = x.shape
    xs0 = jnp.zeros((n_rows, d // 2), jnp.uint32)
    return pl.pallas_call(
        functools.partial(_dispatch_kernel, ts=ts),
        grid_spec=pltpu.PrefetchScalarGridSpec(
            num_scalar_prefetch=1,
            grid=(n // ts,),
            in_specs=[pl.BlockSpec((ts, d), lambda i, *_: (i, 0)),
                      pl.BlockSpec((1, d), lambda i, *_: (0, 0)),
                      pl.BlockSpec(memory_space=pl.ANY)],
            out_specs=pl.BlockSpec(memory_space=pl.ANY),
            scratch_shapes=[pltpu.VMEM((2, ts, d // 2), jnp.uint32), pltpu.SemaphoreType.DMA((2,))],
        ),
        out_shape=jax.ShapeDtypeStruct((n_rows, d // 2), jnp.uint32),
        input_output_aliases={3: 0},
        compiler_params=_params("arbitrary"),
        name="moe_dispatch",
    )(dest, x, g_ffn.reshape(1, d), xs0)


def _expert_kernel(bexp_ref, nblk_ref, xs_ref, wg_ref, wu_ref, wd_ref, y_ref, wg_bf, wu_bf, wd_bf):
    i = pl.program_id(0)

    @pl.when((i == 0) | (bexp_ref[i] != bexp_ref[jnp.maximum(i - 1, 0)]))
    def _():
        wg_bf[...] = wg_ref[0, 0].astype(BF16)
        wu_bf[...] = wu_ref[0, 0].astype(BF16)
        wd_bf[...] = wd_ref[0, 0].astype(BF16)

    @pl.when(i < nblk_ref[0])
    def _():
        w = xs_ref[...]
        half = w.shape[1]
        lo = pltpu.bitcast(w << 16, F32).astype(BF16)
        hi = pltpu.bitcast(w & jnp.uint32(HI_HALF_MASK), F32).astype(BF16)
        gate = (jnp.dot(lo, wg_bf[:half], preferred_element_type=F32)
                + jnp.dot(hi, wg_bf[half:], preferred_element_type=F32))
        up = (jnp.dot(lo, wu_bf[:half], preferred_element_type=F32)
              + jnp.dot(hi, wu_bf[half:], preferred_element_type=F32))
        act = (gate / (1.0 + jnp.exp(-gate)) * up).astype(BF16)
        y_ref[...] = jnp.dot(act, wd_bf[...], preferred_element_type=F32)

    @pl.when(i >= nblk_ref[0])
    def _():
        y_ref[...] = jnp.zeros_like(y_ref)


def _experts(xs, block_expert, n_used, w_gate, w_up, w_down, layer, *, rows):
    d = 2 * xs.shape[1]
    n_blocks = block_expert.shape[0]
    wmap = lambda i, bexp, nblk: (layer, bexp[i], 0, 0)
    return pl.pallas_call(
        _expert_kernel,
        grid_spec=pltpu.PrefetchScalarGridSpec(
            num_scalar_prefetch=2,
            grid=(n_blocks,),
            in_specs=[pl.BlockSpec((rows, d // 2), lambda i, *_: (i, 0)),
                      pl.BlockSpec((1, 1, d, D_EXPERT), wmap),
                      pl.BlockSpec((1, 1, d, D_EXPERT), wmap),
                      pl.BlockSpec((1, 1, D_EXPERT, d), wmap)],
            out_specs=pl.BlockSpec((rows, d), lambda i, *_: (i, 0)),
            scratch_shapes=[pltpu.VMEM((d, D_EXPERT), BF16),
                            pltpu.VMEM((d, D_EXPERT), BF16),
                            pltpu.VMEM((D_EXPERT, d), BF16)],
        ),
        out_shape=jax.ShapeDtypeStruct((n_blocks * rows, d), F32),
        compiler_params=_params("arbitrary"),
        name="moe_experts",
    )(block_expert, n_used, xs, w_gate, w_up, w_down)


def _combine_kernel(pos_ref, x_ref, gates_ref, y_hbm, o_ref, ybuf, sem, *, tt):
    i = pl.program_id(0)
    nsteps = pl.num_programs(0)

    def row_copy(src_row, slot, dst_row):
        return pltpu.make_async_copy(y_hbm.at[pl.ds(src_row, 1)], ybuf.at[slot, pl.ds(dst_row, 1)], sem.at[slot])

    def start_gather(step, slot):
        for r in range(tt):
            for kk in range(TOP_K):
                row_copy(pos_ref[(step * tt + r) * TOP_K + kk], slot, kk * tt + r).start()

    def wait_gather(slot):
        for r in range(TOP_K * tt):
            row_copy(0, slot, r).wait()

    @pl.when(i == 0)
    def _():
        start_gather(0, 0)

    @pl.when(i + 1 < nsteps)
    def _():
        start_gather(i + 1, (i + 1) % 2)

    wait_gather(i % 2)
    yb = ybuf[i % 2]
    g = gates_ref[...]
    o_ref[...] = x_ref[...] + g[:, 0:1] * yb[:tt] + g[:, 1:2] * yb[tt:]


def _combine(x, gates, pos, y, *, tt):
    n, d = x.shape
    return pl.pallas_call(
        functools.partial(_combine_kernel, tt=tt),
        grid_spec=pltpu.PrefetchScalarGridSpec(
            num_scalar_prefetch=1,
            grid=(n // tt,),
            in_specs=[pl.BlockSpec((tt, d), lambda i, *_: (i, 0)),
                      pl.BlockSpec((tt, LANES), lambda i, *_: (i, 0)),
                      pl.BlockSpec(memory_space=pl.ANY)],
            out_specs=pl.BlockSpec((tt, d), lambda i, *_: (i, 0)),
            scratch_shapes=[pltpu.VMEM((2, TOP_K * tt, d), F32), pltpu.SemaphoreType.DMA((2,))],
        ),
        out_shape=jax.ShapeDtypeStruct((n, d), F32),
        compiler_params=_params("arbitrary"),
        name="moe_combine",
    )(pos, x, gates, y)


def _moe_plan(ids, cnt, *, rows):
    n = ids.shape[0]
    expert, rank = ids[:, :TOP_K], ids[:, TOP_K:2 * TOP_K]
    counts = cnt[0, N_GROUPS:N_GROUPS + N_EXPERTS]
    padded = (counts + rows - 1) // rows * rows
    padded_end = jnp.cumsum(padded)
    seg_start = padded_end - padded
    onehot = expert[..., None] == jnp.arange(N_EXPERTS, dtype=I32)
    dest = (jnp.sum(jnp.where(onehot, seg_start, 0), axis=-1) + rank).astype(I32).reshape(n * TOP_K)
    n_blocks = -(-(n * TOP_K) // rows) + N_EXPERTS
    block_start = jnp.arange(n_blocks, dtype=I32) * rows
    block_expert = jnp.minimum(jnp.sum(padded_end[None, :] <= block_start[:, None], axis=1), N_EXPERTS - 1).astype(I32)
    n_used = (padded_end[-1] // rows).astype(I32).reshape(1)
    return dest, block_expert, n_used


def _band_relative_bias(table):
    def bucket(nd):
        max_exact = REL_BUCKETS // 2
        nf = jnp.maximum(nd, 1).astype(F32)
        large = max_exact + (jnp.log(nf / max_exact) / math.log(REL_MAX_DIST / max_exact)
                             * (REL_BUCKETS - max_exact)).astype(I32)
        return jnp.where(nd < max_exact, nd, jnp.minimum(large, REL_BUCKETS - 1))

    qi = jnp.arange(WINDOW)[:, None]
    kj = jnp.arange(2 * WINDOW)[None, :]
    dist = jnp.maximum(qi + WINDOW - kj, 0)
    return jnp.transpose(table[bucket(dist)], (2, 0, 1)).astype(F32)


def _hgrn_params(lb, g_out):
    lb = lb.reshape(HG_HEADS, 1, HG_DK)
    gain = jnp.broadcast_to(g_out.reshape(1, 1, HG_DV), (HG_HEADS, 1, HG_DV))
    rows = [jnp.log(lb), jnp.log1p(-lb), 1.0 - lb, gain, jnp.zeros((HG_HEADS, 4, HG_DK), F32)]
    return jnp.concatenate(rows, axis=1).astype(F32)


def kernel(x, mem, rel_bias_table, hg_lb_logits, g_mix, w_in, swa_gq, swa_gk, swa_sinks, hg_g_out, mla_g_cq, mla_g_ckv, mla_w_uq, mla_w_ukv, mla_gq, mla_gk, w_out, g_mem_q, g_mem_kv, w_mq, w_mkv, mem_gq, mem_gk, w_mo, g_ffn, w_group_router, b_group_router, w_expert_router, b_expert_router, w_gate, w_up, w_down):
    b, s, d = x.shape
    n = b * s
    m = mem.shape[1]
    rel_bias = _band_relative_bias(rel_bias_table)
    lb_all = jnp.cumsum(jax.nn.softmax(hg_lb_logits.astype(F32), axis=0), axis=0)
    lb_all = lb_all - lb_all[:1]
    offs = [0]
    for width in IN_SIZES:
        offs.append(offs[-1] + width)
    o_hg, o_cq, o_kr = offs[3], offs[7], offs[9]

    xf = x.reshape(n, d)
    mem2 = mem.reshape(b * m, d)
    for l in range(DEPTH):
        w = w_in[l]
        kr_cols = w[:, o_kr:]
        w_swa = w[:, :o_hg].astype(BF16)
        w_hg = w[:, o_hg:o_cq].astype(BF16)
        w_mla = jnp.concatenate([w[:, o_cq:], _swap_halves(kr_cols)], axis=1).astype(BF16)
        u_swa = _rms_matmul(xf, g_mix[l], w_swa, tm=1024, tn=w_swa.shape[1]).reshape(b, s, -1)
        u_hg = _rms_matmul(xf, g_mix[l], w_hg, tm=1024, tn=1024).reshape(b, s, -1)
        u_mla = _rms_matmul(xf, g_mix[l], w_mla, tm=1024, tn=w_mla.shape[1]).reshape(b, s, -1)

        out_a = _swa(u_swa, rel_bias, swa_gq[l], swa_gk[l], swa_sinks[l])
        out_b = _hgrn(u_hg, _hgrn_params(lb_all[l], hg_g_out[l]), chunk=HG_CHUNK)
        qm, km, vm = _mla_prep(u_mla, mla_w_uq[l], mla_w_ukv[l], mla_g_cq[l], mla_g_ckv[l],
                               mla_gq[l], mla_gk[l], tm=512)
        out_c = _mla_attention(qm, km, vm, tq=512)
        xf = _out_proj(xf, out_a.reshape(n, -1), out_b.reshape(n, -1), out_c.reshape(n, -1), w_out[l], tm=512)

        kv = _rms_matmul(mem2, g_mem_kv[l], w_mkv[l].astype(BF16), tm=b * m, tn=2 * MEM_WIDTH)
        xf = _mem_attn(xf.reshape(b, s, d), kv.reshape(b, m, -1), g_mem_q[l], w_mq[l], w_mo[l],
                       mem_gq[l], mem_gk[l], tm=512).reshape(n, d)

        ids, gates, cnt = _router(xf, g_ffn[l], w_group_router[l], b_group_router[l],
                                  w_expert_router[l], b_expert_router[l], tm=512)
        dest, block_expert, n_used = _moe_plan(ids, cnt, rows=MOE_ROWS)
        xs = _dispatch(xf, g_ffn[l], dest, block_expert.shape[0] * MOE_ROWS, ts=64)
        y = _experts(xs, block_expert, n_used, w_gate, w_up, w_down, l, rows=MOE_ROWS)
        xf = _combine(xf, gates, dest, y, tt=64)
    return xf.reshape(b, s, d)
```

```python
import functools
import math

import jax
import jax.numpy as jnp
from jax import lax
from jax.experimental import pallas as pl
from jax.experimental.pallas import tpu as pltpu

F32 = jnp.float32
BF16 = jnp.bfloat16
I32 = jnp.int32

D_MODEL = 2048
DEPTH = 2
SWA_HEADS = 8
SWA_KV_HEADS = 2
SWA_HEAD_DIM = 64
WINDOW = 128
HG_HEADS = 8
HG_DK = 128
HG_DV = 128
MLA_HEADS = 4
MLA_Q_RANK = 512
MLA_KV_RANK = 256
MLA_NOPE = 128
MLA_ROPE = 64
MLA_QK = MLA_NOPE + MLA_ROPE
MLA_V = 128
ROPE_THETA = 10000.0
REL_BUCKETS = 32
REL_MAX_DIST = 128
MEM_HEADS = 4
MEM_HEAD_DIM = 128
MEM_WIDTH = MEM_HEADS * MEM_HEAD_DIM
N_GROUPS = 8
EXPERTS_PER_GROUP = 8
N_EXPERTS = N_GROUPS * EXPERTS_PER_GROUP
TOP_K = 2
D_EXPERT = 512
MOE_ROWS = 256
EPS = 1e-6
NEG_INF = -1e30

SWA_WIDTH = SWA_HEADS * SWA_HEAD_DIM
SWA_KV_WIDTH = SWA_KV_HEADS * SWA_HEAD_DIM
HG_WIDTH = HG_HEADS * HG_DV
MLA_WIDTH = MLA_HEADS * MLA_V
IN_SIZES = (SWA_WIDTH, SWA_KV_WIDTH, SWA_KV_WIDTH,
            HG_HEADS * HG_DK, HG_HEADS * HG_DK, HG_WIDTH, HG_WIDTH,
            MLA_Q_RANK, MLA_KV_RANK, MLA_ROPE)

LANES = 128
MLA_QK_PAD = 2 * LANES
VMEM_LIMIT_BYTES = 56 * 1024 * 1024

HG_CHUNK = 128
HG_SUB = 4
MLA_TQ = 1024
MLA_TK = 512
MLA_ROW_GROUP = 256
NT_DIMS = (((1,), (1,)), ((), ()))
TN_DIMS = (((0,), (0,)), ((), ()))


def _params(*semantics):
    return pltpu.CompilerParams(dimension_semantics=semantics, vmem_limit_bytes=VMEM_LIMIT_BYTES)


def _rms(x, gain=None):
    y = x * lax.rsqrt(jnp.mean(x * x, axis=-1, keepdims=True) + EPS)
    return y if gain is None else y * gain


def _rms_matmul_kernel(x_ref, g_ref, w_ref, o_ref, hn_ref):
    @pl.when(pl.program_id(1) == 0)
    def _():
        hn_ref[...] = _rms(x_ref[...], g_ref[...]).astype(BF16)

    o_ref[...] = jnp.dot(hn_ref[...], w_ref[...], preferred_element_type=F32).astype(o_ref.dtype)


def _rms_matmul(x, gain, w, *, tm, tn, out_dtype=F32):
    n, d = x.shape
    nout = w.shape[1]
    return pl.pallas_call(
        _rms_matmul_kernel,
        grid=(n // tm, nout // tn),
        in_specs=[pl.BlockSpec((tm, d), lambda i, j: (i, 0)),
                  pl.BlockSpec((1, d), lambda i, j: (0, 0)),
                  pl.BlockSpec((d, tn), lambda i, j: (0, j))],
        out_specs=pl.BlockSpec((tm, tn), lambda i, j: (i, j)),
        out_shape=jax.ShapeDtypeStruct((n, nout), out_dtype),
        scratch_shapes=[pltpu.VMEM((tm, d), BF16)],
        compiler_params=_params("parallel", "arbitrary"),
        name="rms_matmul",
    )(x, gain.reshape(1, d), w)


def _swa_kernel(sink_ref, q_ref, kp_ref, kc_ref, vp_ref, vc_ref, bias_ref, gq_ref, gk_ref, o_ref):
    blk = pl.program_id(1)
    q = q_ref[0]
    k = jnp.concatenate([kp_ref[0], kc_ref[0]], axis=0)
    v = jnp.concatenate([vp_ref[0], vc_ref[0]], axis=0)
    qi = lax.broadcasted_iota(I32, (WINDOW, 2 * WINDOW), 0)
    kj = lax.broadcasted_iota(I32, (WINDOW, 2 * WINDOW), 1)
    dist = qi + WINDOW - kj
    has_prev = jnp.where(blk > 0, 0, WINDOW)
    valid = (dist >= 0) & (dist < WINDOW) & (kj >= has_prev)
    grp = SWA_HEADS // SWA_KV_HEADS
    outs = []
    for g in range(SWA_KV_HEADS):
        lo = g * SWA_HEAD_DIM
        kg = _rms(k[:, lo:lo + SWA_HEAD_DIM], gk_ref[...]).astype(BF16)
        vg = v[:, lo:lo + SWA_HEAD_DIM].astype(BF16)
        for hh in range(grp):
            h = g * grp + hh
            qh = _rms(q[:, h * SWA_HEAD_DIM:(h + 1) * SWA_HEAD_DIM], gq_ref[...])
            qh = (qh * (SWA_HEAD_DIM ** -0.5)).astype(BF16)
            s = lax.dot_general(qh, kg, NT_DIMS, preferred_element_type=F32) + bias_ref[h]
            s = jnp.where(valid, s, NEG_INF)
            sink = sink_ref[h]
            m = jnp.maximum(jnp.max(s, axis=-1, keepdims=True), sink)
            e = jnp.exp(s - m)
            denom = jnp.sum(e, axis=-1, keepdims=True) + jnp.exp(sink - m)
            p = (e / denom).astype(BF16)
            outs.append(jnp.dot(p, vg, preferred_element_type=F32))
    o_ref[0] = jnp.concatenate(outs, axis=-1).astype(o_ref.dtype)


def _swa(u_swa, rel_bias, gq, gk, sinks):
    b, s, _ = u_swa.shape
    nb = s // WINDOW
    kcol = SWA_WIDTH // LANES
    vcol = kcol + 1
    prev = lambda bi, n, *_: (bi, jnp.maximum(n - 1, 0))
    return pl.pallas_call(
        _swa_kernel,
        grid_spec=pltpu.PrefetchScalarGridSpec(
            num_scalar_prefetch=1,
            grid=(b, nb),
            in_specs=[
                pl.BlockSpec((1, WINDOW, SWA_WIDTH), lambda bi, n, *_: (bi, n, 0)),
                pl.BlockSpec((1, WINDOW, LANES), lambda bi, n, *_: prev(bi, n) + (kcol,)),
                pl.BlockSpec((1, WINDOW, LANES), lambda bi, n, *_: (bi, n, kcol)),
                pl.BlockSpec((1, WINDOW, LANES), lambda bi, n, *_: prev(bi, n) + (vcol,)),
                pl.BlockSpec((1, WINDOW, LANES), lambda bi, n, *_: (bi, n, vcol)),
                pl.BlockSpec((SWA_HEADS, WINDOW, 2 * WINDOW), lambda bi, n, *_: (0, 0, 0)),
                pl.BlockSpec((1, SWA_HEAD_DIM), lambda bi, n, *_: (0, 0)),
                pl.BlockSpec((1, SWA_HEAD_DIM), lambda bi, n, *_: (0, 0)),
            ],
            out_specs=pl.BlockSpec((1, WINDOW, SWA_WIDTH), lambda bi, n, *_: (bi, n, 0)),
        ),
        out_shape=jax.ShapeDtypeStruct((b, s, SWA_WIDTH), BF16),
        compiler_params=_params("parallel", "parallel"),
        name="swa_attention",
    )(sinks, u_swa, u_swa, u_swa, u_swa, u_swa, rel_bias,
      gq.reshape(1, SWA_HEAD_DIM), gk.reshape(1, SWA_HEAD_DIM))


def _hgrn_level_matrix(c):
    t = jnp.arange(c)[:, None]
    r = jnp.arange(c)[None, :]
    mats = [(r <= t)]
    half = c // 2
    while half >= 1:
        mid = (t // (2 * half)) * (2 * half) + half
        is_q = (t & half) != 0
        mats.append(jnp.where(is_q, (r >= mid) & (r <= t), (r > t) & (r < mid)))
        half //= 2
    return jnp.concatenate(mats, axis=0).astype(BF16)


def _hgrn_pair_level(c):
    t = jnp.arange(c, dtype=I32)[:, None]
    s = jnp.arange(c, dtype=I32)[None, :]
    x = t ^ s
    lvl = jnp.zeros((c, c), I32)
    half = c // 2
    while half >= 1:
        lvl = jnp.where((x & (-half)) == half, half, lvl)
        half //= 2
    return jnp.where(t > s, lvl, 0)


def _hgrn_kernel(q_ref, f_ref, v_ref, gate_ref, par_ref, amat_ref, lvl_ref, o_ref, st_ref, *, chunk, n_sub):
    @pl.when(pl.program_id(2) == 0)
    def _():
        st_ref[...] = jnp.zeros_like(st_ref)

    c = chunk
    par = par_ref[0]
    log_lb, log1m_lb, one_m_lb, g_out = par[0:1], par[1:2], par[2:3], par[3:4]
    row = lax.broadcasted_iota(I32, (c, 1), 0)
    pair_level = lvl_ref[...]
    st = st_ref[...]
    for sub in range(n_sub):
        rows = pl.ds(sub * c, c)
        q = q_ref[0, rows]
        f = f_ref[0, rows]
        v = v_ref[0, rows]
        gate = gate_ref[0, rows]

        e = jnp.exp(-jnp.abs(f))
        log_sig = jnp.minimum(f, 0.0) - jnp.log(1.0 + e)
        bb = log1m_lb + log_sig
        log_f = jnp.maximum(log_lb, bb) + jnp.log(1.0 + jnp.exp(-jnp.abs(log_lb - bb)))
        kk = one_m_lb * jnp.where(f < 0.0, 1.0, e) / (1.0 + e)

        g_hi = log_f.astype(BF16)
        g_lo = (log_f - g_hi.astype(F32)).astype(BF16)
        e2 = jnp.dot(amat_ref[...], jnp.concatenate([g_hi, g_lo], axis=1), preferred_element_type=F32)
        expo = e2[:, :HG_DK] + e2[:, HG_DK:]
        bcum = expo[0:c]
        btot = bcum[c - 1:c]

        attn = jnp.zeros((c, c), F32)
        half = c // 2
        lvl = 1
        while half >= 1:
            w = jnp.exp(expo[lvl * c:(lvl + 1) * c])
            z = (jnp.where((row & half) != 0, q, kk) * w).astype(BF16)
            a = lax.dot_general(z, z, NT_DIMS, preferred_element_type=F32)
            attn = jnp.where(pair_level == half, a, attn)
            half //= 2
            lvl += 1

        vb = v.astype(BF16)
        diag = jnp.sum(q * kk, axis=-1, keepdims=True)
        intra = jnp.dot(attn.astype(BF16), vb, preferred_element_type=F32) + diag * v
        qe = (q * jnp.exp(bcum)).astype(BF16)
        inter = lax.dot_general(qe, st.astype(BF16), NT_DIMS, preferred_element_type=F32)
        kd = (kk * jnp.exp(btot - bcum)).astype(BF16)
        st = st * jnp.exp(btot) + lax.dot_general(vb, kd, TN_DIMS, preferred_element_type=F32)

        o = _rms(inter + intra, g_out)
        o_ref[0, rows] = (o * gate / (1.0 + jnp.exp(-gate))).astype(o_ref.dtype)
    st_ref[...] = st


def _hgrn(u_hg, par, *, chunk, n_sub):
    b, s, _ = u_hg.shape
    amat = _hgrn_level_matrix(chunk)
    pair_level = _hgrn_pair_level(chunk)
    step = chunk * n_sub
    col = lambda off: (lambda bi, h, c: (bi, c, off + h))
    return pl.pallas_call(
        functools.partial(_hgrn_kernel, chunk=chunk, n_sub=n_sub),
        grid=(b, HG_HEADS, s // step),
        in_specs=[pl.BlockSpec((1, step, HG_DK), col(0)),
                  pl.BlockSpec((1, step, HG_DK), col(HG_HEADS)),
                  pl.BlockSpec((1, step, HG_DV), col(2 * HG_HEADS)),
                  pl.BlockSpec((1, step, HG_DV), col(3 * HG_HEADS)),
                  pl.BlockSpec((1, 8, HG_DK), lambda bi, h, c: (h, 0, 0)),
                  pl.BlockSpec(amat.shape, lambda bi, h, c: (0, 0)),
                  pl.BlockSpec(pair_level.shape, lambda bi, h, c: (0, 0))],
        out_specs=pl.BlockSpec((1, step, HG_DV), lambda bi, h, c: (bi, c, h)),
        out_shape=jax.ShapeDtypeStruct((b, s, HG_WIDTH), BF16),
        scratch_shapes=[pltpu.VMEM((HG_DV, HG_DK), F32)],
        compiler_params=_params("parallel", "parallel", "arbitrary"),
        name="hgrn2",
    )(u_hg, u_hg, u_hg, u_hg, par, amat, pair_level)


def _mla_prep_kernel(u_ref, tab_ref, wq_ref, wkv_ref, gcq_ref, gckv_ref, gqn_ref, gqr_ref, gkn_ref, gkr_ref,
                     q_ref, k_ref, v_ref):
    u = u_ref[0]
    cq = _rms(u[:, :MLA_Q_RANK], gcq_ref[...]).astype(BF16)
    ckv = _rms(u[:, MLA_Q_RANK:MLA_Q_RANK + MLA_KV_RANK], gckv_ref[...]).astype(BF16)
    kr = u[:, MLA_Q_RANK + MLA_KV_RANK:]
    qf = jnp.dot(cq, wq_ref[...], preferred_element_type=F32)
    kvf = jnp.dot(ckv, wkv_ref[...], preferred_element_type=F32)
    tab = tab_ref[...]
    low = lax.broadcasted_iota(I32, kr.shape, 1) < MLA_ROPE
    kr_sq = jnp.sum(jnp.where(low, kr * kr, 0.0), axis=-1, keepdims=True)
    scale = MLA_QK ** -0.5 * math.log2(math.e)
    ones = jnp.ones((u.shape[0], MLA_V), F32)
    for h in range(MLA_HEADS):
        qn = qf[:, h * MLA_QK_PAD:h * MLA_QK_PAD + MLA_NOPE]
        qr = qf[:, h * MLA_QK_PAD + MLA_NOPE:(h + 1) * MLA_QK_PAD]
        ss = jnp.sum(qn * qn, axis=-1, keepdims=True) + jnp.sum(jnp.where(low, qr * qr, 0.0), axis=-1, keepdims=True)
        rstd = lax.rsqrt(ss / MLA_QK + EPS) * scale
        t = qr * rstd * tab * gqr_ref[...]
        rope = jnp.where(low, t + pltpu.roll(t, MLA_ROPE, 1), 0.0)
        q_ref[0, h] = jnp.concatenate([qn * rstd * gqn_ref[...], rope], axis=-1).astype(q_ref.dtype)

        kn = kvf[:, h * MLA_NOPE:(h + 1) * MLA_NOPE]
        ss = jnp.sum(kn * kn, axis=-1, keepdims=True) + kr_sq
        rstd = lax.rsqrt(ss / MLA_QK + EPS)
        t = kr * rstd * tab * gkr_ref[...]
        rope = t + pltpu.roll(t, MLA_ROPE, 1)
        k_ref[0, h] = jnp.concatenate([kn * rstd * gkn_ref[...], rope], axis=-1).astype(k_ref.dtype)
        vh = kvf[:, MLA_HEADS * MLA_NOPE + h * MLA_V:MLA_HEADS * MLA_NOPE + (h + 1) * MLA_V]
        v_ref[0, h] = jnp.concatenate([vh, ones], axis=-1).astype(v_ref.dtype)


def _swap_halves(a):
    half = a.shape[-1] // 2
    return jnp.concatenate([a[..., half:], a[..., :half]], axis=-1)


def _mla_prep(u_mla, w_uq, w_ukv, g_cq, g_ckv, gq, gk, *, tm):
    b, s, wu = u_mla.shape
    wq = w_uq.reshape(MLA_Q_RANK, MLA_HEADS, MLA_QK)
    wq = jnp.concatenate([wq, _swap_halves(wq[..., MLA_NOPE:])], axis=-1).reshape(MLA_Q_RANK, MLA_HEADS * MLA_QK_PAD)
    wkv = w_ukv.reshape(MLA_KV_RANK, MLA_HEADS, MLA_NOPE + MLA_V)
    wkv = jnp.concatenate([wkv[..., :MLA_NOPE].reshape(MLA_KV_RANK, -1), wkv[..., MLA_NOPE:].reshape(MLA_KV_RANK, -1)], axis=1)
    half = MLA_ROPE // 2
    inv_freq = ROPE_THETA ** (-jnp.arange(half, dtype=F32) / half)
    ang = jnp.arange(s, dtype=F32)[:, None] * inv_freq[None, :]
    cos, sin = jnp.cos(ang), jnp.sin(ang)
    tab = jnp.concatenate([cos, cos, -sin, sin], axis=-1)
    rope_gain = lambda g: jnp.concatenate([g[MLA_NOPE:], _swap_halves(g[MLA_NOPE:])]).reshape(1, 2 * MLA_ROPE)
    vec = lambda i, j: (0, 0)
    head_out = lambda width: pl.BlockSpec((1, MLA_HEADS, tm, width), lambda i, j: (i, 0, j, 0))
    return pl.pallas_call(
        _mla_prep_kernel,
        grid=(b, s // tm),
        in_specs=[pl.BlockSpec((1, tm, wu), lambda i, j: (i, j, 0)),
                  pl.BlockSpec((tm, 2 * MLA_ROPE), lambda i, j: (j, 0)),
                  pl.BlockSpec(wq.shape, vec),
                  pl.BlockSpec(wkv.shape, vec),
                  pl.BlockSpec((1, MLA_Q_RANK), vec),
                  pl.BlockSpec((1, MLA_KV_RANK), vec),
                  pl.BlockSpec((1, MLA_NOPE), vec),
                  pl.BlockSpec((1, 2 * MLA_ROPE), vec),
                  pl.BlockSpec((1, MLA_NOPE), vec),
                  pl.BlockSpec((1, 2 * MLA_ROPE), vec)],
        out_specs=[head_out(MLA_QK_PAD), head_out(MLA_QK_PAD), head_out(2 * MLA_V)],
        out_shape=[jax.ShapeDtypeStruct((b, MLA_HEADS, s, MLA_QK_PAD), BF16),
                   jax.ShapeDtypeStruct((b, MLA_HEADS, s, MLA_QK_PAD), BF16),
                   jax.ShapeDtypeStruct((b, MLA_HEADS, s, 2 * MLA_V), BF16)],
        compiler_params=_params("parallel", "parallel"),
        name="mla_prep",
    )(u_mla, tab, wq.astype(BF16), wkv.astype(BF16), g_cq.reshape(1, -1), g_ckv.reshape(1, -1),
      gq[:MLA_NOPE].reshape(1, -1), rope_gain(gq), gk[:MLA_NOPE].reshape(1, -1), rope_gain(gk))


def _mla_attn_kernel(qi_ref, ki_ref, q_ref, k_ref, v_ref, o_ref, m_ref, acc_ref, *, tq, tk):
    step = pl.program_id(2)
    qi = qi_ref[step]
    ki = ki_ref[step]

    @pl.when(ki == 0)
    def _():
        m_ref[...] = jnp.full_like(m_ref, NEG_INF)
        acc_ref[...] = jnp.zeros_like(acc_ref)

    def update(masked):
        for r0 in range(0, tq, MLA_ROW_GROUP):
            rows = pl.ds(r0, MLA_ROW_GROUP)
            s = lax.dot_general(q_ref[0, 0, rows], k_ref[0, 0], NT_DIMS, preferred_element_type=F32)
            if masked:
                row = qi * tq + r0 + lax.broadcasted_iota(I32, s.shape, 0)
                col = ki * tk + lax.broadcasted_iota(I32, s.shape, 1)
                s = jnp.where(col <= row, s, NEG_INF)
            m_prev = m_ref[rows]
            m_next = jnp.maximum(m_prev, jnp.max(s, axis=-1, keepdims=True))
            alpha = jnp.exp2(m_prev - m_next)
            p = jnp.exp2(s - m_next[:, :1]).astype(BF16)
            pv = jnp.dot(p, v_ref[0, 0], preferred_element_type=F32)
            acc_ref[rows, :MLA_V] = alpha * acc_ref[rows, :MLA_V] + pv[:, :MLA_V]
            acc_ref[rows, MLA_V:] = alpha * acc_ref[rows, MLA_V:] + pv[:, MLA_V:]
            m_ref[rows] = m_next

    crosses_diagonal = (ki + 1) * tk - 1 > qi * tq
    pl.when(crosses_diagonal)(lambda: update(True))
    pl.when(jnp.logical_not(crosses_diagonal))(lambda: update(False))

    @pl.when((ki + 1) * tk == (qi + 1) * tq)
    def _():
        o_ref[0] = (acc_ref[:, :MLA_V] / acc_ref[:, MLA_V:]).astype(o_ref.dtype)


def _mla_attention(q, k, v, *, tq, tk):
    b, h, s, _ = q.shape
    pairs = [(qi, ki) for qi in range(s // tq) for ki in range((qi + 1) * tq // tk)]
    qi_of = jnp.array([p[0] for p in pairs], I32)
    ki_of = jnp.array([p[1] for p in pairs], I32)
    kv_idx = lambda bi, hi, t, qi_ref, ki_ref: (bi, hi, ki_ref[t], 0)
    return pl.pallas_call(
        functools.partial(_mla_attn_kernel, tq=tq, tk=tk),
        grid_spec=pltpu.PrefetchScalarGridSpec(
            num_scalar_prefetch=2,
            grid=(b, h, len(pairs)),
            in_specs=[pl.BlockSpec((1, 1, tq, MLA_QK_PAD), lambda bi, hi, t, qi_ref, ki_ref: (bi, hi, qi_ref[t], 0)),
                      pl.BlockSpec((1, 1, tk, MLA_QK_PAD), kv_idx),
                      pl.BlockSpec((1, 1, tk, 2 * MLA_V), kv_idx)],
            out_specs=pl.BlockSpec((1, tq, MLA_V), lambda bi, hi, t, qi_ref, ki_ref: (bi, qi_ref[t], hi)),
            scratch_shapes=[pltpu.VMEM((tq, MLA_V), F32), pltpu.VMEM((tq, 2 * MLA_V), F32)],
        ),
        out_shape=jax.ShapeDtypeStruct((b, s, h * MLA_V), BF16),
        compiler_params=_params("parallel", "parallel", "arbitrary"),
        name="mla_attention",
    )(qi_of, ki_of, q, k, v)


def _out_proj_kernel(x_ref, a_ref, b_ref, c_ref, wa_ref, wb_ref, wc_ref, o_ref):
    acc = jnp.dot(a_ref[...], wa_ref[...], preferred_element_type=F32)
    acc += jnp.dot(b_ref[...], wb_ref[...], preferred_element_type=F32)
    acc += jnp.dot(c_ref[...], wc_ref[...], preferred_element_type=F32)
    o_ref[...] = x_ref[...] + acc


def _out_proj(x, a, bmix, c, w_out, *, tm):
    n, d = x.shape
    wa = w_out[:SWA_WIDTH].astype(BF16)
    wb = w_out[SWA_WIDTH:SWA_WIDTH + HG_WIDTH].astype(BF16)
    wc = w_out[SWA_WIDTH + HG_WIDTH:].astype(BF16)
    row = lambda width: pl.BlockSpec((tm, width), lambda i: (i, 0))
    full = lambda w: pl.BlockSpec(w.shape, lambda i: (0, 0))
    return pl.pallas_call(
        _out_proj_kernel,
        grid=(n // tm,),
        in_specs=[row(d), row(SWA_WIDTH), row(HG_WIDTH), row(MLA_WIDTH), full(wa), full(wb), full(wc)],
        out_specs=row(d),
        out_shape=jax.ShapeDtypeStruct((n, d), F32),
        compiler_params=_params("parallel"),
        name="out_proj",
    )(x, a, bmix, c, wa, wb, wc)


def _mem_attn_kernel(x_ref, kv_ref, g_ref, wq_ref, wo_ref, gq_ref, gk_ref, o_ref):
    x = x_ref[0]
    hn = _rms(x, g_ref[...]).astype(BF16)
    qf = jnp.dot(hn, wq_ref[...], preferred_element_type=F32)
    kv = kv_ref[0]
    outs = []
    for h in range(MEM_HEADS):
        lo = h * MEM_HEAD_DIM
        qh = (_rms(qf[:, lo:lo + MEM_HEAD_DIM], gq_ref[...]) * (MEM_HEAD_DIM ** -0.5)).astype(BF16)
        kh = _rms(kv[:, lo:lo + MEM_HEAD_DIM], gk_ref[...]).astype(BF16)
        vh = kv[:, MEM_WIDTH + lo:MEM_WIDTH + lo + MEM_HEAD_DIM].astype(BF16)
        s = lax.dot_general(qh, kh, NT_DIMS, preferred_element_type=F32)
        e = jnp.exp(s - jnp.max(s, axis=-1, keepdims=True))
        p = (e / jnp.sum(e, axis=-1, keepdims=True)).astype(BF16)
        outs.append(jnp.dot(p, vh, preferred_element_type=F32))
    o = jnp.concatenate(outs, axis=-1).astype(BF16)
    o_ref[0] = x + jnp.dot(o, wo_ref[...], preferred_element_type=F32)


def _mem_attn(x, kv, g_q, w_mq, w_mo, mem_gq, mem_gk, *, tm):
    b, s, d = x.shape
    m = kv.shape[1]
    vec = lambda i, j: (0, 0)
    return pl.pallas_call(
        _mem_attn_kernel,
        grid=(b, s // tm),
        in_specs=[pl.BlockSpec((1, tm, d), lambda i, j: (i, j, 0)),
                  pl.BlockSpec((1, m, 2 * MEM_WIDTH), lambda i, j: (i, 0, 0)),
                  pl.BlockSpec((1, d), vec),
                  pl.BlockSpec((d, MEM_WIDTH), vec),
                  pl.BlockSpec((MEM_WIDTH, d), vec),
                  pl.BlockSpec((1, MEM_HEAD_DIM), vec),
                  pl.BlockSpec((1, MEM_HEAD_DIM), vec)],
        out_specs=pl.BlockSpec((1, tm, d), lambda i, j: (i, j, 0)),
        out_shape=jax.ShapeDtypeStruct((b, s, d), F32),
        compiler_params=_params("parallel", "parallel"),
        name="mem_attention",
    )(x, kv, g_q.reshape(1, d), w_mq.astype(BF16), w_mo.astype(BF16),
      mem_gq.reshape(1, -1), mem_gk.reshape(1, -1))


def _split_bf16(a):
    hi = a.astype(BF16)
    return hi, (a - hi.astype(F32)).astype(BF16)


def _router_kernel(x_ref, g_ref, whi_ref, wlo_ref, b_ref, ids_ref, gates_ref, cnt_ref, carry_ref):
    @pl.when(pl.program_id(0) == 0)
    def _():
        carry_ref[...] = jnp.zeros_like(carry_ref)

    hn = _rms(x_ref[...], g_ref[...])
    hi, lo = _split_bf16(hn)
    logits = (jnp.dot(hi, whi_ref[...], preferred_element_type=F32)
              + jnp.dot(hi, wlo_ref[...], preferred_element_type=F32)
              + jnp.dot(lo, whi_ref[...], preferred_element_type=F32)) + b_ref[...]
    lane = lax.broadcasted_iota(I32, logits.shape, 1)
    gl = jnp.where(lane < N_GROUPS, logits, NEG_INF)
    gmax = jnp.max(gl, axis=-1, keepdims=True)
    p_grp = 1.0 / jnp.sum(jnp.exp(gl - gmax), axis=-1, keepdims=True)
    grp = jnp.min(jnp.where(gl == gmax, lane, LANES), axis=-1, keepdims=True)
    in_grp = (lane >= N_GROUPS) & (lane < N_GROUPS + N_EXPERTS) & (((lane - N_GROUPS) // EXPERTS_PER_GROUP) == grp)
    el = jnp.where(in_grp, logits, NEG_INF)
    m1 = jnp.max(el, axis=-1, keepdims=True)
    i1 = jnp.min(jnp.where(el == m1, lane, LANES), axis=-1, keepdims=True)
    el2 = jnp.where(lane == i1, NEG_INF, el)
    m2 = jnp.max(el2, axis=-1, keepdims=True)
    i2 = jnp.min(jnp.where(el2 == m2, lane, LANES), axis=-1, keepdims=True)
    r = jnp.exp(m2 - m1)
    g1 = p_grp / (1.0 + r)
    gates_ref[...] = jnp.where(lane == 0, g1, jnp.where(lane == 1, g1 * r, 0.0))

    tm = logits.shape[0]
    used = jnp.where((lane == i1) | (lane == i2), 1.0, 0.0)
    earlier = lax.broadcasted_iota(I32, (tm, tm), 1) < lax.broadcasted_iota(I32, (tm, tm), 0)
    before = carry_ref[...] + jnp.dot(jnp.where(earlier, 1.0, 0.0).astype(BF16), used.astype(BF16),
                                      preferred_element_type=F32)
    r1 = jnp.sum(jnp.where(lane == i1, before, 0.0), axis=-1, keepdims=True).astype(I32)
    r2 = jnp.sum(jnp.where(lane == i2, before, 0.0), axis=-1, keepdims=True).astype(I32)
    carry_ref[...] += jnp.sum(used, axis=0, keepdims=True)
    cnt_ref[...] = carry_ref[...].astype(I32)
    ids_ref[...] = jnp.where(lane == 0, i1 - N_GROUPS, jnp.where(lane == 1, i2 - N_GROUPS,
                             jnp.where(lane == 2, r1, jnp.where(lane == 3, r2, 0))))


def _router(x, g_ffn, w_gr, b_gr, w_er, b_er, *, tm):
    n, d = x.shape
    pad = LANES - N_GROUPS - N_EXPERTS
    w = jnp.concatenate([w_gr, w_er, jnp.zeros((d, pad), F32)], axis=1)
    bias = jnp.concatenate([b_gr, b_er, jnp.zeros((pad,), F32)]).reshape(1, LANES)
    whi, wlo = _split_bf16(w)
    row = lambda width: pl.BlockSpec((tm, width), lambda i: (i, 0))
    vec = lambda i: (0, 0)
    return pl.pallas_call(
        _router_kernel,
        grid=(n // tm,),
        in_specs=[row(d), pl.BlockSpec((1, d), vec), pl.BlockSpec((d, LANES), vec),
                  pl.BlockSpec((d, LANES), vec), pl.BlockSpec((1, LANES), vec)],
        out_specs=[row(LANES), row(LANES), pl.BlockSpec((1, LANES), vec)],
        out_shape=[jax.ShapeDtypeStruct((n, LANES), I32), jax.ShapeDtypeStruct((n, LANES), F32),
                   jax.ShapeDtypeStruct((1, LANES), I32)],
        scratch_shapes=[pltpu.VMEM((1, LANES), F32)],
        compiler_params=_params("arbitrary"),
        name="moe_router",
    )(x, g_ffn.reshape(1, d), whi, wlo, bias)


HI_HALF_MASK = 0xFFFF0000


def _dispatch_kernel(dest_ref, x_ref, g_ref, xs_in_hbm, xs_hbm, buf, sem, *, ts):
    del xs_in_hbm
    i = pl.program_id(0)
    last = pl.num_programs(0) - 1
    slot = i % 2

    def row_copy(slot_, r, dst_row):
        return pltpu.make_async_copy(buf.at[slot_, pl.ds(r, 1)], xs_hbm.at[pl.ds(dst_row, 1)], sem.at[slot_])

    def wait_step(slot_):
        for _ in range(ts * TOP_K):
            row_copy(slot_, 0, 0).wait()

    @pl.when(i >= 2)
    def _():
        wait_step(slot)

    hn = _rms(x_ref[...], g_ref[...])
    half = hn.shape[1] // 2
    lo = pltpu.bitcast(hn[:, :half].astype(BF16).astype(F32), jnp.uint32) >> 16
    hi = pltpu.bitcast(hn[:, half:].astype(BF16).astype(F32), jnp.uint32) & jnp.uint32(HI_HALF_MASK)
    buf[slot] = lo | hi
    for r in range(ts):
        for kk in range(TOP_K):
            row_copy(slot, r, dest_ref[(i * ts + r) * TOP_K + kk]).start()

    @pl.when(i == last)
    def _():
        wait_step(slot)

    @pl.when((i == last) & (i >= 1))
    def _():
        wait_step(1 - slot)


def _dispatch(x, g_ffn, dest, n_rows, *, ts):
    n, d = x.shape
    xs0 = jnp.zeros((n_rows, d // 2), jnp.uint32)
    return pl.pallas_call(
        functools.partial(_dispatch_kernel, ts=ts),
        grid_spec=pltpu.PrefetchScalarGridSpec(
            num_scalar_prefetch=1,
            grid=(n // ts,),
            in_specs=[pl.BlockSpec((ts, d), lambda i, *_: (i, 0)),
                      pl.BlockSpec((1, d), lambda i, *_: (0, 0)),
                      pl.BlockSpec(memory_space=pl.ANY)],
            out_specs=pl.BlockSpec(memory_space=pl.ANY),
            scratch_shapes=[pltpu.VMEM((2, ts, d // 2), jnp.uint32), pltpu.SemaphoreType.DMA((2,))],
        ),
        out_shape=jax.ShapeDtypeStruct((n_rows, d // 2), jnp.uint32),
        input_output_aliases={3: 0},
        compiler_params=_params("arbitrary"),
        name="moe_dispatch",
    )(dest, x, g_ffn.reshape(1, d), xs0)


def _expert_kernel(bexp_ref, nblk_ref, xs_ref, wg_ref, wu_ref, wd_ref, y_ref, wg_bf, wu_bf, wd_bf):
    i = pl.program_id(0)

    @pl.when((i == 0) | (bexp_ref[i] != bexp_ref[jnp.maximum(i - 1, 0)]))
    def _():
        wg_bf[...] = wg_ref[0, 0].astype(BF16)
        wu_bf[...] = wu_ref[0, 0].astype(BF16)
        wd_bf[...] = wd_ref[0, 0].astype(BF16)

    @pl.when(i < nblk_ref[0])
    def _():
        w = xs_ref[...]
        half = w.shape[1]
        lo = pltpu.bitcast(w << 16, F32).astype(BF16)
        hi = pltpu.bitcast(w & jnp.uint32(HI_HALF_MASK), F32).astype(BF16)
        gate = (jnp.dot(lo, wg_bf[:half], preferred_element_type=F32)
                + jnp.dot(hi, wg_bf[half:], preferred_element_type=F32))
        up = (jnp.dot(lo, wu_bf[:half], preferred_element_type=F32)
              + jnp.dot(hi, wu_bf[half:], preferred_element_type=F32))
        act = (gate / (1.0 + jnp.exp(-gate)) * up).astype(BF16)
        y_ref[...] = jnp.dot(act, wd_bf[...], preferred_element_type=F32)

    @pl.when(i >= nblk_ref[0])
    def _():
        y_ref[...] = jnp.zeros_like(y_ref)


def _experts(xs, block_expert, n_used, w_gate, w_up, w_down, layer, *, rows):
    d = 2 * xs.shape[1]
    n_blocks = block_expert.shape[0]
    wmap = lambda i, bexp, nblk: (layer, bexp[i], 0, 0)
    return pl.pallas_call(
        _expert_kernel,
        grid_spec=pltpu.PrefetchScalarGridSpec(
            num_scalar_prefetch=2,
            grid=(n_blocks,),
            in_specs=[pl.BlockSpec((rows, d // 2), lambda i, *_: (i, 0)),
                      pl.BlockSpec((1, 1, d, D_EXPERT), wmap),
                      pl.BlockSpec((1, 1, d, D_EXPERT), wmap),
                      pl.BlockSpec((1, 1, D_EXPERT, d), wmap)],
            out_specs=pl.BlockSpec((rows, d), lambda i, *_: (i, 0)),
            scratch_shapes=[pltpu.VMEM((d, D_EXPERT), BF16),
                            pltpu.VMEM((d, D_EXPERT), BF16),
                            pltpu.VMEM((D_EXPERT, d), BF16)],
        ),
        out_shape=jax.ShapeDtypeStruct((n_blocks * rows, d), F32),
        compiler_params=_params("arbitrary"),
        name="moe_experts",
    )(block_expert, n_used, xs, w_gate, w_up, w_down)


def _combine_kernel(pos_ref, x_ref, gates_ref, y_hbm, o_ref, ybuf, sem, *, tt):
    i = pl.program_id(0)
    nsteps = pl.num_programs(0)

    def row_copy(src_row, slot, dst_row):
        return pltpu.make_async_copy(y_hbm.at[pl.ds(src_row, 1)], ybuf.at[slot, pl.ds(dst_row, 1)], sem.at[slot])

    def start_gather(step, slot):
        for r in range(tt):
            for kk in range(TOP_K):
                row_copy(pos_ref[(step * tt + r) * TOP_K + kk], slot, kk * tt + r).start()

    def wait_gather(slot):
        for r in range(TOP_K * tt):
            row_copy(0, slot, r).wait()

    @pl.when(i == 0)
    def _():
        start_gather(0, 0)

    @pl.when(i + 1 < nsteps)
    def _():
        start_gather(i + 1, (i + 1) % 2)

    wait_gather(i % 2)
    yb = ybuf[i % 2]
    g = gates_ref[...]
    o_ref[...] = x_ref[...] + g[:, 0:1] * yb[:tt] + g[:, 1:2] * yb[tt:]


def _combine(x, gates, pos, y, *, tt):
    n, d = x.shape
    return pl.pallas_call(
        functools.partial(_combine_kernel, tt=tt),
        grid_spec=pltpu.PrefetchScalarGridSpec(
            num_scalar_prefetch=1,
            grid=(n // tt,),
            in_specs=[pl.BlockSpec((tt, d), lambda i, *_: (i, 0)),
                      pl.BlockSpec((tt, LANES), lambda i, *_: (i, 0)),
                      pl.BlockSpec(memory_space=pl.ANY)],
            out_specs=pl.BlockSpec((tt, d), lambda i, *_: (i, 0)),
            scratch_shapes=[pltpu.VMEM((2, TOP_K * tt, d), F32), pltpu.SemaphoreType.DMA((2,))],
        ),
        out_shape=jax.ShapeDtypeStruct((n, d), F32),
        compiler_params=_params("arbitrary"),
        name="moe_combine",
    )(pos, x, gates, y)


def _moe_plan(ids, cnt, *, rows):
    n = ids.shape[0]
    expert, rank = ids[:, :TOP_K], ids[:, TOP_K:2 * TOP_K]
    counts = cnt[0, N_GROUPS:N_GROUPS + N_EXPERTS]
    padded = (counts + rows - 1) // rows * rows
    padded_end = jnp.cumsum(padded)
    seg_start = padded_end - padded
    onehot = expert[..., None] == jnp.arange(N_EXPERTS, dtype=I32)
    dest = (jnp.sum(jnp.where(onehot, seg_start, 0), axis=-1) + rank).astype(I32).reshape(n * TOP_K)
    n_blocks = -(-(n * TOP_K) // rows) + N_EXPERTS
    block_start = jnp.arange(n_blocks, dtype=I32) * rows
    block_expert = jnp.minimum(jnp.sum(padded_end[None, :] <= block_start[:, None], axis=1), N_EXPERTS - 1).astype(I32)
    n_used = (padded_end[-1] // rows).astype(I32).reshape(1)
    return dest, block_expert, n_used


def _band_relative_bias(table):
    def bucket(nd):
        max_exact = REL_BUCKETS // 2
        nf = jnp.maximum(nd, 1).astype(F32)
        large = max_exact + (jnp.log(nf / max_exact) / math.log(REL_MAX_DIST / max_exact)
                             * (REL_BUCKETS - max_exact)).astype(I32)
        return jnp.where(nd < max_exact, nd, jnp.minimum(large, REL_BUCKETS - 1))

    qi = jnp.arange(WINDOW)[:, None]
    kj = jnp.arange(2 * WINDOW)[None, :]
    dist = jnp.maximum(qi + WINDOW - kj, 0)
    return jnp.transpose(table[bucket(dist)], (2, 0, 1)).astype(F32)


def _hgrn_params(lb, g_out):
    lb = lb.reshape(HG_HEADS, 1, HG_DK)
    gain = jnp.broadcast_to(g_out.reshape(1, 1, HG_DV), (HG_HEADS, 1, HG_DV))
    rows = [jnp.log(lb), jnp.log1p(-lb), 1.0 - lb, gain, jnp.zeros((HG_HEADS, 4, HG_DK), F32)]
    return jnp.concatenate(rows, axis=1).astype(F32)


def kernel(x, mem, rel_bias_table, hg_lb_logits, g_mix, w_in, swa_gq, swa_gk, swa_sinks, hg_g_out, mla_g_cq, mla_g_ckv, mla_w_uq, mla_w_ukv, mla_gq, mla_gk, w_out, g_mem_q, g_mem_kv, w_mq, w_mkv, mem_gq, mem_gk, w_mo, g_ffn, w_group_router, b_group_router, w_expert_router, b_expert_router, w_gate, w_up, w_down):
    b, s, d = x.shape
    n = b * s
    m = mem.shape[1]
    rel_bias = _band_relative_bias(rel_bias_table)
    lb_all = jnp.cumsum(jax.nn.softmax(hg_lb_logits.astype(F32), axis=0), axis=0)
    lb_all = lb_all - lb_all[:1]
    offs = [0]
    for width in IN_SIZES:
        offs.append(offs[-1] + width)
    o_hg, o_cq, o_kr = offs[3], offs[7], offs[9]

    xf = x.reshape(n, d)
    mem2 = mem.reshape(b * m, d)
    for l in range(DEPTH):
        w = w_in[l]
        kr_cols = w[:, o_kr:]
        w_swa = w[:, :o_hg].astype(BF16)
        w_hg = w[:, o_hg:o_cq].astype(BF16)
        w_mla = jnp.concatenate([w[:, o_cq:], _swap_halves(kr_cols)], axis=1).astype(BF16)
        u_swa = _rms_matmul(xf, g_mix[l], w_swa, tm=1024, tn=w_swa.shape[1]).reshape(b, s, -1)
        u_hg = _rms_matmul(xf, g_mix[l], w_hg, tm=1024, tn=1024).reshape(b, s, -1)
        u_mla = _rms_matmul(xf, g_mix[l], w_mla, tm=1024, tn=w_mla.shape[1]).reshape(b, s, -1)

        out_a = _swa(u_swa, rel_bias, swa_gq[l], swa_gk[l], swa_sinks[l])
        out_b = _hgrn(u_hg, _hgrn_params(lb_all[l], hg_g_out[l]), chunk=HG_CHUNK, n_sub=HG_SUB)
        qm, km, vm = _mla_prep(u_mla, mla_w_uq[l], mla_w_ukv[l], mla_g_cq[l], mla_g_ckv[l],
                               mla_gq[l], mla_gk[l], tm=512)
        out_c = _mla_attention(qm, km, vm, tq=MLA_TQ, tk=MLA_TK)
        xf = _out_proj(xf, out_a.reshape(n, -1), out_b.reshape(n, -1), out_c.reshape(n, -1), w_out[l], tm=512)

        kv = _rms_matmul(mem2, g_mem_kv[l], w_mkv[l].astype(BF16), tm=b * m, tn=2 * MEM_WIDTH)
        xf = _mem_attn(xf.reshape(b, s, d), kv.reshape(b, m, -1), g_mem_q[l], w_mq[l], w_mo[l],
                       mem_gq[l], mem_gk[l], tm=512).reshape(n, d)

        ids, gates, cnt = _router(xf, g_ffn[l], w_group_router[l], b_group_router[l],
                                  w_expert_router[l], b_expert_router[l], tm=512)
        dest, block_expert, n_used = _moe_plan(ids, cnt, rows=MOE_ROWS)
        xs = _dispatch(xf, g_ffn[l], dest, block_expert.shape[0] * MOE_ROWS, ts=64)
        y = _experts(xs, block_expert, n_used, w_gate, w_up, w_down, l, rows=MOE_ROWS)
        xf = _combine(xf, gates, dest, y, tt=64)
    return xf.reshape(b, s, d)
```

```python
import functools
import math

import jax
import jax.numpy as jnp
from jax import lax
from jax.experimental import pallas as pl
from jax.experimental.pallas import tpu as pltpu

F32 = jnp.float32
BF16 = jnp.bfloat16
I32 = jnp.int32

D_MODEL = 2048
DEPTH = 2
SWA_HEADS = 8
SWA_KV_HEADS = 2
SWA_HEAD_DIM = 64
WINDOW = 128
HG_HEADS = 8
HG_DK = 128
HG_DV = 128
MLA_HEADS = 4
MLA_Q_RANK = 512
MLA_KV_RANK = 256
MLA_NOPE = 128
MLA_ROPE = 64
MLA_QK = MLA_NOPE + MLA_ROPE
MLA_V = 128
ROPE_THETA = 10000.0
REL_BUCKETS = 32
REL_MAX_DIST = 128
MEM_HEADS = 4
MEM_HEAD_DIM = 128
MEM_WIDTH = MEM_HEADS * MEM_HEAD_DIM
N_GROUPS = 8
EXPERTS_PER_GROUP = 8
N_EXPERTS = N_GROUPS * EXPERTS_PER_GROUP
TOP_K = 2
D_EXPERT = 512
MOE_ROWS = 256
EPS = 1e-6
NEG_INF = -1e30

SWA_WIDTH = SWA_HEADS * SWA_HEAD_DIM
SWA_KV_WIDTH = SWA_KV_HEADS * SWA_HEAD_DIM
HG_WIDTH = HG_HEADS * HG_DV
MLA_WIDTH = MLA_HEADS * MLA_V
IN_SIZES = (SWA_WIDTH, SWA_KV_WIDTH, SWA_KV_WIDTH,
            HG_HEADS * HG_DK, HG_HEADS * HG_DK, HG_WIDTH, HG_WIDTH,
            MLA_Q_RANK, MLA_KV_RANK, MLA_ROPE)

LANES = 128
DMA_PRIORITIES = 2
MLA_QK_PAD = 2 * LANES
VMEM_LIMIT_BYTES = 56 * 1024 * 1024

HG_CHUNK = 128
HG_SUB = 4
MLA_TQ = 1024
MLA_TK = 512
MLA_ROW_GROUP = 256
NT_DIMS = (((1,), (1,)), ((), ()))
TN_DIMS = (((0,), (0,)), ((), ()))


def _params(*semantics):
    return pltpu.CompilerParams(dimension_semantics=semantics, vmem_limit_bytes=VMEM_LIMIT_BYTES)


def _rms(x, gain=None):
    y = x * lax.rsqrt(jnp.mean(x * x, axis=-1, keepdims=True) + EPS)
    return y if gain is None else y * gain


def _rms_matmul_kernel(x_ref, g_ref, w_ref, o_ref, hn_ref):
    @pl.when(pl.program_id(1) == 0)
    def _():
        hn_ref[...] = _rms(x_ref[...], g_ref[...]).astype(BF16)

    o_ref[...] = jnp.dot(hn_ref[...], w_ref[...], preferred_element_type=F32).astype(o_ref.dtype)


def _rms_matmul(x, gain, w, *, tm, tn, out_dtype=F32):
    n, d = x.shape
    nout = w.shape[1]
    return pl.pallas_call(
        _rms_matmul_kernel,
        grid=(n // tm, nout // tn),
        in_specs=[pl.BlockSpec((tm, d), lambda i, j: (i, 0)),
                  pl.BlockSpec((1, d), lambda i, j: (0, 0)),
                  pl.BlockSpec((d, tn), lambda i, j: (0, j))],
        out_specs=pl.BlockSpec((tm, tn), lambda i, j: (i, j)),
        out_shape=jax.ShapeDtypeStruct((n, nout), out_dtype),
        scratch_shapes=[pltpu.VMEM((tm, d), BF16)],
        compiler_params=_params("parallel", "arbitrary"),
        name="rms_matmul",
    )(x, gain.reshape(1, d), w)


def _swa_kernel(sink_ref, q_ref, kp_ref, kc_ref, vp_ref, vc_ref, bias_ref, gq_ref, gk_ref, o_ref):
    blk = pl.program_id(1)
    q = q_ref[0]
    k = jnp.concatenate([kp_ref[0], kc_ref[0]], axis=0)
    v = jnp.concatenate([vp_ref[0], vc_ref[0]], axis=0)
    qi = lax.broadcasted_iota(I32, (WINDOW, 2 * WINDOW), 0)
    kj = lax.broadcasted_iota(I32, (WINDOW, 2 * WINDOW), 1)
    dist = qi + WINDOW - kj
    has_prev = jnp.where(blk > 0, 0, WINDOW)
    valid = (dist >= 0) & (dist < WINDOW) & (kj >= has_prev)
    grp = SWA_HEADS // SWA_KV_HEADS
    outs = []
    for g in range(SWA_KV_HEADS):
        lo = g * SWA_HEAD_DIM
        kg = _rms(k[:, lo:lo + SWA_HEAD_DIM], gk_ref[...]).astype(BF16)
        vg = v[:, lo:lo + SWA_HEAD_DIM].astype(BF16)
        for hh in range(grp):
            h = g * grp + hh
            qh = _rms(q[:, h * SWA_HEAD_DIM:(h + 1) * SWA_HEAD_DIM], gq_ref[...])
            qh = (qh * (SWA_HEAD_DIM ** -0.5)).astype(BF16)
            s = lax.dot_general(qh, kg, NT_DIMS, preferred_element_type=F32) + bias_ref[h]
            s = jnp.where(valid, s, NEG_INF)
            sink = sink_ref[h]
            m = jnp.maximum(jnp.max(s, axis=-1, keepdims=True), sink)
            e = jnp.exp(s - m)
            denom = jnp.sum(e, axis=-1, keepdims=True) + jnp.exp(sink - m)
            p = (e / denom).astype(BF16)
            outs.append(jnp.dot(p, vg, preferred_element_type=F32))
    o_ref[0] = jnp.concatenate(outs, axis=-1).astype(o_ref.dtype)


def _swa(u_swa, rel_bias, gq, gk, sinks):
    b, s, _ = u_swa.shape
    nb = s // WINDOW
    kcol = SWA_WIDTH // LANES
    vcol = kcol + 1
    prev = lambda bi, n, *_: (bi, jnp.maximum(n - 1, 0))
    return pl.pallas_call(
        _swa_kernel,
        grid_spec=pltpu.PrefetchScalarGridSpec(
            num_scalar_prefetch=1,
            grid=(b, nb),
            in_specs=[
                pl.BlockSpec((1, WINDOW, SWA_WIDTH), lambda bi, n, *_: (bi, n, 0)),
                pl.BlockSpec((1, WINDOW, LANES), lambda bi, n, *_: prev(bi, n) + (kcol,)),
                pl.BlockSpec((1, WINDOW, LANES), lambda bi, n, *_: (bi, n, kcol)),
                pl.BlockSpec((1, WINDOW, LANES), lambda bi, n, *_: prev(bi, n) + (vcol,)),
                pl.BlockSpec((1, WINDOW, LANES), lambda bi, n, *_: (bi, n, vcol)),
                pl.BlockSpec((SWA_HEADS, WINDOW, 2 * WINDOW), lambda bi, n, *_: (0, 0, 0)),
                pl.BlockSpec((1, SWA_HEAD_DIM), lambda bi, n, *_: (0, 0)),
                pl.BlockSpec((1, SWA_HEAD_DIM), lambda bi, n, *_: (0, 0)),
            ],
            out_specs=pl.BlockSpec((1, WINDOW, SWA_WIDTH), lambda bi, n, *_: (bi, n, 0)),
        ),
        out_shape=jax.ShapeDtypeStruct((b, s, SWA_WIDTH), BF16),
        compiler_params=_params("parallel", "parallel"),
        name="swa_attention",
    )(sinks, u_swa, u_swa, u_swa, u_swa, u_swa, rel_bias,
      gq.reshape(1, SWA_HEAD_DIM), gk.reshape(1, SWA_HEAD_DIM))


def _hgrn_level_matrix(c):
    t = jnp.arange(c)[:, None]
    r = jnp.arange(c)[None, :]
    mats = [(r <= t)]
    half = c // 2
    while half >= 1:
        mid = (t // (2 * half)) * (2 * half) + half
        is_q = (t & half) != 0
        mats.append(jnp.where(is_q, (r >= mid) & (r <= t), (r > t) & (r < mid)))
        half //= 2
    return jnp.concatenate(mats, axis=0).astype(BF16)


def _hgrn_pair_level(c):
    t = jnp.arange(c, dtype=I32)[:, None]
    s = jnp.arange(c, dtype=I32)[None, :]
    x = t ^ s
    lvl = jnp.zeros((c, c), I32)
    half = c // 2
    while half >= 1:
        lvl = jnp.where((x & (-half)) == half, half, lvl)
        half //= 2
    return jnp.where(t > s, lvl, 0)


def _hgrn_kernel(q_ref, f_ref, v_ref, gate_ref, par_ref, amat_ref, lvl_ref, o_ref, st_ref, *, chunk, n_sub):
    @pl.when(pl.program_id(2) == 0)
    def _():
        st_ref[...] = jnp.zeros_like(st_ref)

    c = chunk
    par = par_ref[0]
    log_lb, log1m_lb, one_m_lb, g_out = par[0:1], par[1:2], par[2:3], par[3:4]
    row = lax.broadcasted_iota(I32, (c, 1), 0)
    pair_level = lvl_ref[...]
    st = st_ref[...]
    for sub in range(n_sub):
        rows = pl.ds(sub * c, c)
        q = q_ref[0, rows]
        f = f_ref[0, rows]
        v = v_ref[0, rows]
        gate = gate_ref[0, rows]

        e = jnp.exp(-jnp.abs(f))
        log_sig = jnp.minimum(f, 0.0) - jnp.log(1.0 + e)
        bb = log1m_lb + log_sig
        log_f = jnp.maximum(log_lb, bb) + jnp.log(1.0 + jnp.exp(-jnp.abs(log_lb - bb)))
        kk = one_m_lb * jnp.where(f < 0.0, 1.0, e) / (1.0 + e)

        g_hi = log_f.astype(BF16)
        g_lo = (log_f - g_hi.astype(F32)).astype(BF16)
        e2 = jnp.dot(amat_ref[...], jnp.concatenate([g_hi, g_lo], axis=1), preferred_element_type=F32)
        expo = e2[:, :HG_DK] + e2[:, HG_DK:]
        bcum = expo[0:c]
        btot = bcum[c - 1:c]

        attn = jnp.zeros((c, c), F32)
        half = c // 2
        lvl = 1
        while half >= 1:
            w = jnp.exp(expo[lvl * c:(lvl + 1) * c])
            z = (jnp.where((row & half) != 0, q, kk) * w).astype(BF16)
            a = lax.dot_general(z, z, NT_DIMS, preferred_element_type=F32)
            attn = jnp.where(pair_level == half, a, attn)
            half //= 2
            lvl += 1

        vb = v.astype(BF16)
        diag = jnp.sum(q * kk, axis=-1, keepdims=True)
        intra = jnp.dot(attn.astype(BF16), vb, preferred_element_type=F32) + diag * v
        qe = (q * jnp.exp(bcum)).astype(BF16)
        inter = lax.dot_general(qe, st.astype(BF16), NT_DIMS, preferred_element_type=F32)
        kd = (kk * jnp.exp(btot - bcum)).astype(BF16)
        st = st * jnp.exp(btot) + lax.dot_general(vb, kd, TN_DIMS, preferred_element_type=F32)

        o = _rms(inter + intra, g_out)
        o_ref[0, rows] = (o * gate / (1.0 + jnp.exp(-gate))).astype(o_ref.dtype)
    st_ref[...] = st


def _hgrn(u_hg, par, *, chunk, n_sub):
    b, s, _ = u_hg.shape
    amat = _hgrn_level_matrix(chunk)
    pair_level = _hgrn_pair_level(chunk)
    step = chunk * n_sub
    col = lambda off: (lambda bi, h, c: (bi, c, off + h))
    return pl.pallas_call(
        functools.partial(_hgrn_kernel, chunk=chunk, n_sub=n_sub),
        grid=(b, HG_HEADS, s // step),
        in_specs=[pl.BlockSpec((1, step, HG_DK), col(0)),
                  pl.BlockSpec((1, step, HG_DK), col(HG_HEADS)),
                  pl.BlockSpec((1, step, HG_DV), col(2 * HG_HEADS)),
                  pl.BlockSpec((1, step, HG_DV), col(3 * HG_HEADS)),
                  pl.BlockSpec((1, 8, HG_DK), lambda bi, h, c: (h, 0, 0)),
                  pl.BlockSpec(amat.shape, lambda bi, h, c: (0, 0)),
                  pl.BlockSpec(pair_level.shape, lambda bi, h, c: (0, 0))],
        out_specs=pl.BlockSpec((1, step, HG_DV), lambda bi, h, c: (bi, c, h)),
        out_shape=jax.ShapeDtypeStruct((b, s, HG_WIDTH), BF16),
        scratch_shapes=[pltpu.VMEM((HG_DV, HG_DK), F32)],
        compiler_params=_params("parallel", "parallel", "arbitrary"),
        name="hgrn2",
    )(u_hg, u_hg, u_hg, u_hg, par, amat, pair_level)


def _mla_prep_kernel(u_ref, tab_ref, wq_ref, wkv_ref, gcq_ref, gckv_ref, gqn_ref, gqr_ref, gkn_ref, gkr_ref,
                     q_ref, k_ref, v_ref):
    u = u_ref[0]
    cq = _rms(u[:, :MLA_Q_RANK], gcq_ref[...]).astype(BF16)
    ckv = _rms(u[:, MLA_Q_RANK:MLA_Q_RANK + MLA_KV_RANK], gckv_ref[...]).astype(BF16)
    kr = u[:, MLA_Q_RANK + MLA_KV_RANK:]
    qf = jnp.dot(cq, wq_ref[...], preferred_element_type=F32)
    kvf = jnp.dot(ckv, wkv_ref[...], preferred_element_type=F32)
    tab = tab_ref[...]
    low = lax.broadcasted_iota(I32, kr.shape, 1) < MLA_ROPE
    kr_sq = jnp.sum(jnp.where(low, kr * kr, 0.0), axis=-1, keepdims=True)
    scale = MLA_QK ** -0.5 * math.log2(math.e)
    ones = jnp.ones((u.shape[0], MLA_V), F32)
    for h in range(MLA_HEADS):
        qn = qf[:, h * MLA_QK_PAD:h * MLA_QK_PAD + MLA_NOPE]
        qr = qf[:, h * MLA_QK_PAD + MLA_NOPE:(h + 1) * MLA_QK_PAD]
        ss = jnp.sum(qn * qn, axis=-1, keepdims=True) + jnp.sum(jnp.where(low, qr * qr, 0.0), axis=-1, keepdims=True)
        rstd = lax.rsqrt(ss / MLA_QK + EPS) * scale
        t = qr * rstd * tab * gqr_ref[...]
        rope = jnp.where(low, t + pltpu.roll(t, MLA_ROPE, 1), 0.0)
        q_ref[0, h] = jnp.concatenate([qn * rstd * gqn_ref[...], rope], axis=-1).astype(q_ref.dtype)

        kn = kvf[:, h * MLA_NOPE:(h + 1) * MLA_NOPE]
        ss = jnp.sum(kn * kn, axis=-1, keepdims=True) + kr_sq
        rstd = lax.rsqrt(ss / MLA_QK + EPS)
        t = kr * rstd * tab * gkr_ref[...]
        rope = t + pltpu.roll(t, MLA_ROPE, 1)
        k_ref[0, h] = jnp.concatenate([kn * rstd * gkn_ref[...], rope], axis=-1).astype(k_ref.dtype)
        vh = kvf[:, MLA_HEADS * MLA_NOPE + h * MLA_V:MLA_HEADS * MLA_NOPE + (h + 1) * MLA_V]
        v_ref[0, h] = jnp.concatenate([vh, ones], axis=-1).astype(v_ref.dtype)


def _swap_halves(a):
    half = a.shape[-1] // 2
    return jnp.concatenate([a[..., half:], a[..., :half]], axis=-1)


def _mla_prep(u_mla, w_uq, w_ukv, g_cq, g_ckv, gq, gk, *, tm):
    b, s, wu = u_mla.shape
    wq = w_uq.reshape(MLA_Q_RANK, MLA_HEADS, MLA_QK)
    wq = jnp.concatenate([wq, _swap_halves(wq[..., MLA_NOPE:])], axis=-1).reshape(MLA_Q_RANK, MLA_HEADS * MLA_QK_PAD)
    wkv = w_ukv.reshape(MLA_KV_RANK, MLA_HEADS, MLA_NOPE + MLA_V)
    wkv = jnp.concatenate([wkv[..., :MLA_NOPE].reshape(MLA_KV_RANK, -1), wkv[..., MLA_NOPE:].reshape(MLA_KV_RANK, -1)], axis=1)
    half = MLA_ROPE // 2
    inv_freq = ROPE_THETA ** (-jnp.arange(half, dtype=F32) / half)
    ang = jnp.arange(s, dtype=F32)[:, None] * inv_freq[None, :]
    cos, sin = jnp.cos(ang), jnp.sin(ang)
    tab = jnp.concatenate([cos, cos, -sin, sin], axis=-1)
    rope_gain = lambda g: jnp.concatenate([g[MLA_NOPE:], _swap_halves(g[MLA_NOPE:])]).reshape(1, 2 * MLA_ROPE)
    vec = lambda i, j: (0, 0)
    head_out = lambda width: pl.BlockSpec((1, MLA_HEADS, tm, width), lambda i, j: (i, 0, j, 0))
    return pl.pallas_call(
        _mla_prep_kernel,
        grid=(b, s // tm),
        in_specs=[pl.BlockSpec((1, tm, wu), lambda i, j: (i, j, 0)),
                  pl.BlockSpec((tm, 2 * MLA_ROPE), lambda i, j: (j, 0)),
                  pl.BlockSpec(wq.shape, vec),
                  pl.BlockSpec(wkv.shape, vec),
                  pl.BlockSpec((1, MLA_Q_RANK), vec),
                  pl.BlockSpec((1, MLA_KV_RANK), vec),
                  pl.BlockSpec((1, MLA_NOPE), vec),
                  pl.BlockSpec((1, 2 * MLA_ROPE), vec),
                  pl.BlockSpec((1, MLA_NOPE), vec),
                  pl.BlockSpec((1, 2 * MLA_ROPE), vec)],
        out_specs=[head_out(MLA_QK_PAD), head_out(MLA_QK_PAD), head_out(2 * MLA_V)],
        out_shape=[jax.ShapeDtypeStruct((b, MLA_HEADS, s, MLA_QK_PAD), BF16),
                   jax.ShapeDtypeStruct((b, MLA_HEADS, s, MLA_QK_PAD), BF16),
                   jax.ShapeDtypeStruct((b, MLA_HEADS, s, 2 * MLA_V), BF16)],
        compiler_params=_params("parallel", "parallel"),
        name="mla_prep",
    )(u_mla, tab, wq.astype(BF16), wkv.astype(BF16), g_cq.reshape(1, -1), g_ckv.reshape(1, -1),
      gq[:MLA_NOPE].reshape(1, -1), rope_gain(gq), gk[:MLA_NOPE].reshape(1, -1), rope_gain(gk))


def _mla_attn_kernel(qi_ref, ki_ref, q_ref, k_ref, v_ref, o_ref, m_ref, acc_ref, *, tq, tk):
    step = pl.program_id(2)
    qi = qi_ref[step]
    ki = ki_ref[step]

    @pl.when(ki == 0)
    def _():
        m_ref[...] = jnp.full_like(m_ref, NEG_INF)
        acc_ref[...] = jnp.zeros_like(acc_ref)

    def update(masked):
        for r0 in range(0, tq, MLA_ROW_GROUP):
            rows = pl.ds(r0, MLA_ROW_GROUP)
            s = lax.dot_general(q_ref[0, 0, rows], k_ref[0, 0], NT_DIMS, preferred_element_type=F32)
            if masked:
                row = qi * tq + r0 + lax.broadcasted_iota(I32, s.shape, 0)
                col = ki * tk + lax.broadcasted_iota(I32, s.shape, 1)
                s = jnp.where(col <= row, s, NEG_INF)
            m_prev = m_ref[rows]
            m_next = jnp.maximum(m_prev, jnp.max(s, axis=-1, keepdims=True))
            alpha = jnp.exp2(m_prev - m_next)
            p = jnp.exp2(s - m_next[:, :1]).astype(BF16)
            pv = jnp.dot(p, v_ref[0, 0], preferred_element_type=F32)
            acc_ref[rows, :MLA_V] = alpha * acc_ref[rows, :MLA_V] + pv[:, :MLA_V]
            acc_ref[rows, MLA_V:] = alpha * acc_ref[rows, MLA_V:] + pv[:, MLA_V:]
            m_ref[rows] = m_next

    crosses_diagonal = (ki + 1) * tk - 1 > qi * tq
    pl.when(crosses_diagonal)(lambda: update(True))
    pl.when(jnp.logical_not(crosses_diagonal))(lambda: update(False))

    @pl.when((ki + 1) * tk == (qi + 1) * tq)
    def _():
        o_ref[0] = (acc_ref[:, :MLA_V] / acc_ref[:, MLA_V:]).astype(o_ref.dtype)


def _mla_attention(q, k, v, *, tq, tk):
    b, h, s, _ = q.shape
    pairs = [(qi, ki) for qi in range(s // tq) for ki in range((qi + 1) * tq // tk)]
    qi_of = jnp.array([p[0] for p in pairs], I32)
    ki_of = jnp.array([p[1] for p in pairs], I32)
    kv_idx = lambda bi, hi, t, qi_ref, ki_ref: (bi, hi, ki_ref[t], 0)
    return pl.pallas_call(
        functools.partial(_mla_attn_kernel, tq=tq, tk=tk),
        grid_spec=pltpu.PrefetchScalarGridSpec(
            num_scalar_prefetch=2,
            grid=(b, h, len(pairs)),
            in_specs=[pl.BlockSpec((1, 1, tq, MLA_QK_PAD), lambda bi, hi, t, qi_ref, ki_ref: (bi, hi, qi_ref[t], 0)),
                      pl.BlockSpec((1, 1, tk, MLA_QK_PAD), kv_idx),
                      pl.BlockSpec((1, 1, tk, 2 * MLA_V), kv_idx)],
            out_specs=pl.BlockSpec((1, tq, MLA_V), lambda bi, hi, t, qi_ref, ki_ref: (bi, qi_ref[t], hi)),
            scratch_shapes=[pltpu.VMEM((tq, MLA_V), F32), pltpu.VMEM((tq, 2 * MLA_V), F32)],
        ),
        out_shape=jax.ShapeDtypeStruct((b, s, h * MLA_V), BF16),
        compiler_params=_params("parallel", "parallel", "arbitrary"),
        name="mla_attention",
    )(qi_of, ki_of, q, k, v)


def _out_proj_kernel(x_ref, a_ref, b_ref, c_ref, wa_ref, wb_ref, wc_ref, o_ref):
    acc = jnp.dot(a_ref[...], wa_ref[...], preferred_element_type=F32)
    acc += jnp.dot(b_ref[...], wb_ref[...], preferred_element_type=F32)
    acc += jnp.dot(c_ref[...], wc_ref[...], preferred_element_type=F32)
    o_ref[...] = x_ref[...] + acc


def _out_proj(x, a, bmix, c, w_out, *, tm):
    n, d = x.shape
    wa = w_out[:SWA_WIDTH].astype(BF16)
    wb = w_out[SWA_WIDTH:SWA_WIDTH + HG_WIDTH].astype(BF16)
    wc = w_out[SWA_WIDTH + HG_WIDTH:].astype(BF16)
    row = lambda width: pl.BlockSpec((tm, width), lambda i: (i, 0))
    full = lambda w: pl.BlockSpec(w.shape, lambda i: (0, 0))
    return pl.pallas_call(
        _out_proj_kernel,
        grid=(n // tm,),
        in_specs=[row(d), row(SWA_WIDTH), row(HG_WIDTH), row(MLA_WIDTH), full(wa), full(wb), full(wc)],
        out_specs=row(d),
        out_shape=jax.ShapeDtypeStruct((n, d), F32),
        compiler_params=_params("parallel"),
        name="out_proj",
    )(x, a, bmix, c, wa, wb, wc)


def _mem_attn_kernel(x_ref, kv_ref, g_ref, wq_ref, wo_ref, gq_ref, gk_ref, o_ref):
    x = x_ref[0]
    hn = _rms(x, g_ref[...]).astype(BF16)
    qf = jnp.dot(hn, wq_ref[...], preferred_element_type=F32)
    kv = kv_ref[0]
    outs = []
    for h in range(MEM_HEADS):
        lo = h * MEM_HEAD_DIM
        qh = (_rms(qf[:, lo:lo + MEM_HEAD_DIM], gq_ref[...]) * (MEM_HEAD_DIM ** -0.5)).astype(BF16)
        kh = _rms(kv[:, lo:lo + MEM_HEAD_DIM], gk_ref[...]).astype(BF16)
        vh = kv[:, MEM_WIDTH + lo:MEM_WIDTH + lo + MEM_HEAD_DIM].astype(BF16)
        s = lax.dot_general(qh, kh, NT_DIMS, preferred_element_type=F32)
        e = jnp.exp(s - jnp.max(s, axis=-1, keepdims=True))
        p = (e / jnp.sum(e, axis=-1, keepdims=True)).astype(BF16)
        outs.append(jnp.dot(p, vh, preferred_element_type=F32))
    o = jnp.concatenate(outs, axis=-1).astype(BF16)
    o_ref[0] = x + jnp.dot(o, wo_ref[...], preferred_element_type=F32)


def _mem_attn(x, kv, g_q, w_mq, w_mo, mem_gq, mem_gk, *, tm):
    b, s, d = x.shape
    m = kv.shape[1]
    vec = lambda i, j: (0, 0)
    return pl.pallas_call(
        _mem_attn_kernel,
        grid=(b, s // tm),
        in_specs=[pl.BlockSpec((1, tm, d), lambda i, j: (i, j, 0)),
                  pl.BlockSpec((1, m, 2 * MEM_WIDTH), lambda i, j: (i, 0, 0)),
                  pl.BlockSpec((1, d), vec),
                  pl.BlockSpec((d, MEM_WIDTH), vec),
                  pl.BlockSpec((MEM_WIDTH, d), vec),
                  pl.BlockSpec((1, MEM_HEAD_DIM), vec),
                  pl.BlockSpec((1, MEM_HEAD_DIM), vec)],
        out_specs=pl.BlockSpec((1, tm, d), lambda i, j: (i, j, 0)),
        out_shape=jax.ShapeDtypeStruct((b, s, d), F32),
        compiler_params=_params("parallel", "parallel"),
        name="mem_attention",
    )(x, kv, g_q.reshape(1, d), w_mq.astype(BF16), w_mo.astype(BF16),
      mem_gq.reshape(1, -1), mem_gk.reshape(1, -1))


def _split_bf16(a):
    hi = a.astype(BF16)
    return hi, (a - hi.astype(F32)).astype(BF16)


def _router_kernel(x_ref, g_ref, whi_ref, wlo_ref, b_ref, ids_ref, gates_ref, cnt_ref, carry_ref):
    @pl.when(pl.program_id(0) == 0)
    def _():
        carry_ref[...] = jnp.zeros_like(carry_ref)

    hn = _rms(x_ref[...], g_ref[...])
    hi, lo = _split_bf16(hn)
    logits = (jnp.dot(hi, whi_ref[...], preferred_element_type=F32)
              + jnp.dot(hi, wlo_ref[...], preferred_element_type=F32)
              + jnp.dot(lo, whi_ref[...], preferred_element_type=F32)) + b_ref[...]
    lane = lax.broadcasted_iota(I32, logits.shape, 1)
    gl = jnp.where(lane < N_GROUPS, logits, NEG_INF)
    gmax = jnp.max(gl, axis=-1, keepdims=True)
    p_grp = 1.0 / jnp.sum(jnp.exp(gl - gmax), axis=-1, keepdims=True)
    grp = jnp.min(jnp.where(gl == gmax, lane, LANES), axis=-1, keepdims=True)
    in_grp = (lane >= N_GROUPS) & (lane < N_GROUPS + N_EXPERTS) & (((lane - N_GROUPS) // EXPERTS_PER_GROUP) == grp)
    el = jnp.where(in_grp, logits, NEG_INF)
    m1 = jnp.max(el, axis=-1, keepdims=True)
    i1 = jnp.min(jnp.where(el == m1, lane, LANES), axis=-1, keepdims=True)
    el2 = jnp.where(lane == i1, NEG_INF, el)
    m2 = jnp.max(el2, axis=-1, keepdims=True)
    i2 = jnp.min(jnp.where(el2 == m2, lane, LANES), axis=-1, keepdims=True)
    r = jnp.exp(m2 - m1)
    g1 = p_grp / (1.0 + r)
    gates_ref[...] = jnp.where(lane == 0, g1, jnp.where(lane == 1, g1 * r, 0.0))

    tm = logits.shape[0]
    used = jnp.where((lane == i1) | (lane == i2), 1.0, 0.0)
    earlier = lax.broadcasted_iota(I32, (tm, tm), 1) < lax.broadcasted_iota(I32, (tm, tm), 0)
    before = carry_ref[...] + jnp.dot(jnp.where(earlier, 1.0, 0.0).astype(BF16), used.astype(BF16),
                                      preferred_element_type=F32)
    r1 = jnp.sum(jnp.where(lane == i1, before, 0.0), axis=-1, keepdims=True).astype(I32)
    r2 = jnp.sum(jnp.where(lane == i2, before, 0.0), axis=-1, keepdims=True).astype(I32)
    carry_ref[...] += jnp.sum(used, axis=0, keepdims=True)
    cnt_ref[...] = carry_ref[...].astype(I32)
    ids_ref[...] = jnp.where(lane == 0, i1 - N_GROUPS, jnp.where(lane == 1, i2 - N_GROUPS,
                             jnp.where(lane == 2, r1, jnp.where(lane == 3, r2, 0))))


def _router(x, g_ffn, w_gr, b_gr, w_er, b_er, *, tm):
    n, d = x.shape
    pad = LANES - N_GROUPS - N_EXPERTS
    w = jnp.concatenate([w_gr, w_er, jnp.zeros((d, pad), F32)], axis=1)
    bias = jnp.concatenate([b_gr, b_er, jnp.zeros((pad,), F32)]).reshape(1, LANES)
    whi, wlo = _split_bf16(w)
    row = lambda width: pl.BlockSpec((tm, width), lambda i: (i, 0))
    vec = lambda i: (0, 0)
    return pl.pallas_call(
        _router_kernel,
        grid=(n // tm,),
        in_specs=[row(d), pl.BlockSpec((1, d), vec), pl.BlockSpec((d, LANES), vec),
                  pl.BlockSpec((d, LANES), vec), pl.BlockSpec((1, LANES), vec)],
        out_specs=[row(LANES), row(LANES), pl.BlockSpec((1, LANES), vec)],
        out_shape=[jax.ShapeDtypeStruct((n, LANES), I32), jax.ShapeDtypeStruct((n, LANES), F32),
                   jax.ShapeDtypeStruct((1, LANES), I32)],
        scratch_shapes=[pltpu.VMEM((1, LANES), F32)],
        compiler_params=_params("arbitrary"),
        name="moe_router",
    )(x, g_ffn.reshape(1, d), whi, wlo, bias)


HI_HALF_MASK = 0xFFFF0000


def _dispatch_kernel(dest_ref, x_ref, g_ref, xs_in_hbm, xs_hbm, buf, sem, *, ts):
    del xs_in_hbm
    i = pl.program_id(0)
    last = pl.num_programs(0) - 1
    slot = i % 2

    def row_copy(slot_, r, dst_row):
        return pltpu.make_async_copy(buf.at[slot_, pl.ds(r, 1)], xs_hbm.at[pl.ds(dst_row, 1)], sem.at[slot_])

    def wait_step(slot_):
        for _ in range(ts * TOP_K):
            row_copy(slot_, 0, 0).wait()

    @pl.when(i >= 2)
    def _():
        wait_step(slot)

    hn = _rms(x_ref[...], g_ref[...])
    half = hn.shape[1] // 2
    lo = pltpu.bitcast(hn[:, :half].astype(BF16).astype(F32), jnp.uint32) >> 16
    hi = pltpu.bitcast(hn[:, half:].astype(BF16).astype(F32), jnp.uint32) & jnp.uint32(HI_HALF_MASK)
    buf[slot] = lo | hi
    for r in range(ts):
        for kk in range(TOP_K):
            row_copy(slot, r, dest_ref[(i * ts + r) * TOP_K + kk]).start(priority=kk % DMA_PRIORITIES)

    @pl.when(i == last)
    def _():
        wait_step(slot)

    @pl.when((i == last) & (i >= 1))
    def _():
        wait_step(1 - slot)


def _dispatch(x, g_ffn, dest, n_rows, *, ts):
    n, d = x.shape
    xs0 = jnp.zeros((n_rows, d // 2), jnp.uint32)
    return pl.pallas_call(
        functools.partial(_dispatch_kernel, ts=ts),
        grid_spec=pltpu.PrefetchScalarGridSpec(
            num_scalar_prefetch=1,
            grid=(n // ts,),
            in_specs=[pl.BlockSpec((ts, d), lambda i, *_: (i, 0)),
                      pl.BlockSpec((1, d), lambda i, *_: (0, 0)),
                      pl.BlockSpec(memory_space=pl.ANY)],
            out_specs=pl.BlockSpec(memory_space=pl.ANY),
            scratch_shapes=[pltpu.VMEM((2, ts, d // 2), jnp.uint32), pltpu.SemaphoreType.DMA((2,))],
        ),
        out_shape=jax.ShapeDtypeStruct((n_rows, d // 2), jnp.uint32),
        input_output_aliases={3: 0},
        compiler_params=_params("arbitrary"),
        name="moe_dispatch",
    )(dest, x, g_ffn.reshape(1, d), xs0)


def _expert_kernel(bexp_ref, next_ref, nblk_ref, xs_ref, wg_hbm, wu_hbm, wd_hbm, y_ref,
                   wg_f32, wu_f32, wd_f32, sem, wg_bf, wu_bf, wd_bf, *, layer):
    i = pl.program_id(0)
    n_used = nblk_ref[0]
    expert = bexp_ref[i]
    first_of_run = (i == 0) | (expert != bexp_ref[jnp.maximum(i - 1, 0)])
    slot = next_ref[2 * i + 1]

    def weight_copies(e, slot_):
        return [pltpu.make_async_copy(hbm.at[layer, e], buf.at[slot_], sem.at[slot_, j])
                for j, (hbm, buf) in enumerate(((wg_hbm, wg_f32), (wu_hbm, wu_f32), (wd_hbm, wd_f32)))]

    @pl.when(i == 0)
    def _():
        for cp in weight_copies(expert, 0):
            cp.start()

    @pl.when(first_of_run & (i < n_used))
    def _():
        for cp in weight_copies(expert, slot):
            cp.wait()
        next_expert = next_ref[2 * i]

        @pl.when(next_expert >= 0)
        def _():
            for cp in weight_copies(next_expert, 1 - slot):
                cp.start()

        wg_bf[...] = wg_f32[slot].astype(BF16)
        wu_bf[...] = wu_f32[slot].astype(BF16)
        wd_bf[...] = wd_f32[slot].astype(BF16)

    @pl.when(i < n_used)
    def _():
        w = xs_ref[...]
        half = w.shape[1]
        lo = pltpu.bitcast(w << 16, F32).astype(BF16)
        hi = pltpu.bitcast(w & jnp.uint32(HI_HALF_MASK), F32).astype(BF16)
        gate = (jnp.dot(lo, wg_bf[:half], preferred_element_type=F32)
                + jnp.dot(hi, wg_bf[half:], preferred_element_type=F32))
        up = (jnp.dot(lo, wu_bf[:half], preferred_element_type=F32)
              + jnp.dot(hi, wu_bf[half:], preferred_element_type=F32))
        act = (gate / (1.0 + jnp.exp(-gate)) * up).astype(BF16)
        y_ref[...] = jnp.dot(act, wd_bf[...], preferred_element_type=F32)

    @pl.when(i >= nblk_ref[0])
    def _():
        y_ref[...] = jnp.zeros_like(y_ref)


def _experts(xs, block_expert, run_next, n_used, w_gate, w_up, w_down, layer, *, rows):
    d = 2 * xs.shape[1]
    n_blocks = block_expert.shape[0]
    hbm = pl.BlockSpec(memory_space=pl.ANY)
    return pl.pallas_call(
        functools.partial(_expert_kernel, layer=layer),
        grid_spec=pltpu.PrefetchScalarGridSpec(
            num_scalar_prefetch=3,
            grid=(n_blocks,),
            in_specs=[pl.BlockSpec((rows, d // 2), lambda i, *_: (i, 0)), hbm, hbm, hbm],
            out_specs=pl.BlockSpec((rows, d), lambda i, *_: (i, 0)),
            scratch_shapes=[pltpu.VMEM((2, d, D_EXPERT), F32),
                            pltpu.VMEM((2, d, D_EXPERT), F32),
                            pltpu.VMEM((2, D_EXPERT, d), F32),
                            pltpu.SemaphoreType.DMA((2, 3)),
                            pltpu.VMEM((d, D_EXPERT), BF16),
                            pltpu.VMEM((d, D_EXPERT), BF16),
                            pltpu.VMEM((D_EXPERT, d), BF16)],
        ),
        out_shape=jax.ShapeDtypeStruct((n_blocks * rows, d), F32),
        compiler_params=_params("arbitrary"),
        name="moe_experts",
    )(block_expert, run_next, n_used, xs, w_gate, w_up, w_down)


def _combine_kernel(pos_ref, x_ref, gates_ref, y_hbm, o_ref, ybuf, sem, *, tt):
    i = pl.program_id(0)
    nsteps = pl.num_programs(0)

    def row_copy(src_row, slot, dst_row):
        return pltpu.make_async_copy(y_hbm.at[pl.ds(src_row, 1)], ybuf.at[slot, pl.ds(dst_row, 1)], sem.at[slot])

    def start_gather(step, slot):
        for r in range(tt):
            for kk in range(TOP_K):
                row_copy(pos_ref[(step * tt + r) * TOP_K + kk], slot, kk * tt + r).start(priority=kk % DMA_PRIORITIES)

    def wait_gather(slot):
        for r in range(TOP_K * tt):
            row_copy(0, slot, r).wait()

    @pl.when(i == 0)
    def _():
        start_gather(0, 0)

    @pl.when(i + 1 < nsteps)
    def _():
        start_gather(i + 1, (i + 1) % 2)

    wait_gather(i % 2)
    yb = ybuf[i % 2]
    g = gates_ref[...]
    o_ref[...] = x_ref[...] + g[:, 0:1] * yb[:tt] + g[:, 1:2] * yb[tt:]


def _combine(x, gates, pos, y, *, tt):
    n, d = x.shape
    return pl.pallas_call(
        functools.partial(_combine_kernel, tt=tt),
        grid_spec=pltpu.PrefetchScalarGridSpec(
            num_scalar_prefetch=1,
            grid=(n // tt,),
            in_specs=[pl.BlockSpec((tt, d), lambda i, *_: (i, 0)),
                      pl.BlockSpec((tt, LANES), lambda i, *_: (i, 0)),
                      pl.BlockSpec(memory_space=pl.ANY)],
            out_specs=pl.BlockSpec((tt, d), lambda i, *_: (i, 0)),
            scratch_shapes=[pltpu.VMEM((2, TOP_K * tt, d), F32), pltpu.SemaphoreType.DMA((2,))],
        ),
        out_shape=jax.ShapeDtypeStruct((n, d), F32),
        compiler_params=_params("arbitrary"),
        name="moe_combine",
    )(pos, x, gates, y)


def _moe_plan(ids, cnt, *, rows):
    n = ids.shape[0]
    expert, rank = ids[:, :TOP_K], ids[:, TOP_K:2 * TOP_K]
    counts = cnt[0, N_GROUPS:N_GROUPS + N_EXPERTS]
    padded = (counts + rows - 1) // rows * rows
    padded_end = jnp.cumsum(padded)
    seg_start = padded_end - padded
    onehot = expert[..., None] == jnp.arange(N_EXPERTS, dtype=I32)
    dest = (jnp.sum(jnp.where(onehot, seg_start, 0), axis=-1) + rank).astype(I32).reshape(n * TOP_K)
    n_blocks = -(-(n * TOP_K) // rows) + N_EXPERTS
    block_start = jnp.arange(n_blocks, dtype=I32) * rows
    block_expert = jnp.minimum(jnp.sum(padded_end[None, :] <= block_start[:, None], axis=1), N_EXPERTS - 1).astype(I32)
    n_used = (padded_end[-1] // rows).astype(I32)
    block_onehot = block_expert[:, None] == jnp.arange(N_EXPERTS, dtype=I32)
    pick = lambda per_expert: jnp.sum(jnp.where(block_onehot, per_expert, 0), axis=-1)
    next_block = pick(padded_end // rows)
    next_onehot = jnp.minimum(next_block, n_blocks - 1)[:, None] == jnp.arange(n_blocks, dtype=I32)
    next_expert = jnp.where(next_block < n_used, jnp.sum(jnp.where(next_onehot, block_expert, 0), axis=-1), -1)
    nonempty = (counts > 0).astype(I32)
    run_parity = pick(jnp.cumsum(nonempty) - nonempty) % 2
    run_next = jnp.stack([next_expert, run_parity], axis=-1).astype(I32).reshape(2 * n_blocks)
    return dest, block_expert, run_next, n_used.reshape(1)


def _band_relative_bias(table):
    def bucket(nd):
        max_exact = REL_BUCKETS // 2
        nf = jnp.maximum(nd, 1).astype(F32)
        large = max_exact + (jnp.log(nf / max_exact) / math.log(REL_MAX_DIST / max_exact)
                             * (REL_BUCKETS - max_exact)).astype(I32)
        return jnp.where(nd < max_exact, nd, jnp.minimum(large, REL_BUCKETS - 1))

    qi = jnp.arange(WINDOW)[:, None]
    kj = jnp.arange(2 * WINDOW)[None, :]
    dist = jnp.maximum(qi + WINDOW - kj, 0)
    onehot = (bucket(dist)[..., None] == jnp.arange(REL_BUCKETS)).astype(F32)
    return jnp.einsum('qkb,bh->hqk', onehot, table.astype(F32), precision=lax.Precision.HIGHEST)


def _hgrn_params(lb, g_out):
    lb = lb.reshape(HG_HEADS, 1, HG_DK)
    gain = jnp.broadcast_to(g_out.reshape(1, 1, HG_DV), (HG_HEADS, 1, HG_DV))
    rows = [jnp.log(lb), jnp.log1p(-lb), 1.0 - lb, gain, jnp.zeros((HG_HEADS, 4, HG_DK), F32)]
    return jnp.concatenate(rows, axis=1).astype(F32)


def kernel(x, mem, rel_bias_table, hg_lb_logits, g_mix, w_in, swa_gq, swa_gk, swa_sinks, hg_g_out, mla_g_cq, mla_g_ckv, mla_w_uq, mla_w_ukv, mla_gq, mla_gk, w_out, g_mem_q, g_mem_kv, w_mq, w_mkv, mem_gq, mem_gk, w_mo, g_ffn, w_group_router, b_group_router, w_expert_router, b_expert_router, w_gate, w_up, w_down):
    b, s, d = x.shape
    n = b * s
    m = mem.shape[1]
    rel_bias = _band_relative_bias(rel_bias_table)
    lb_all = jnp.cumsum(jax.nn.softmax(hg_lb_logits.astype(F32), axis=0), axis=0)
    lb_all = lb_all - lb_all[:1]
    offs = [0]
    for width in IN_SIZES:
        offs.append(offs[-1] + width)
    o_hg, o_cq, o_kr = offs[3], offs[7], offs[9]

    xf = x.reshape(n, d)
    mem2 = mem.reshape(b * m, d)
    for l in range(DEPTH):
        w = w_in[l]
        kr_cols = w[:, o_kr:]
        w_swa = w[:, :o_hg].astype(BF16)
        w_hg = w[:, o_hg:o_cq].astype(BF16)
        w_mla = jnp.concatenate([w[:, o_cq:], _swap_halves(kr_cols)], axis=1).astype(BF16)
        u_swa = _rms_matmul(xf, g_mix[l], w_swa, tm=1024, tn=w_swa.shape[1]).reshape(b, s, -1)
        u_hg = _rms_matmul(xf, g_mix[l], w_hg, tm=1024, tn=1024).reshape(b, s, -1)
        u_mla = _rms_matmul(xf, g_mix[l], w_mla, tm=1024, tn=w_mla.shape[1]).reshape(b, s, -1)

        out_a = _swa(u_swa, rel_bias, swa_gq[l], swa_gk[l], swa_sinks[l])
        out_b = _hgrn(u_hg, _hgrn_params(lb_all[l], hg_g_out[l]), chunk=HG_CHUNK, n_sub=HG_SUB)
        qm, km, vm = _mla_prep(u_mla, mla_w_uq[l], mla_w_ukv[l], mla_g_cq[l], mla_g_ckv[l],
                               mla_gq[l], mla_gk[l], tm=512)
        out_c = _mla_attention(qm, km, vm, tq=MLA_TQ, tk=MLA_TK)
        xf = _out_proj(xf, out_a.reshape(n, -1), out_b.reshape(n, -1), out_c.reshape(n, -1), w_out[l], tm=512)

        kv = _rms_matmul(mem2, g_mem_kv[l], w_mkv[l].astype(BF16), tm=b * m, tn=2 * MEM_WIDTH)
        xf = _mem_attn(xf.reshape(b, s, d), kv.reshape(b, m, -1), g_mem_q[l], w_mq[l], w_mo[l],
                       mem_gq[l], mem_gk[l], tm=512).reshape(n, d)

        ids, gates, cnt = _router(xf, g_ffn[l], w_group_router[l], b_group_router[l],
                                  w_expert_router[l], b_expert_router[l], tm=512)
        dest, block_expert, run_next, n_used = _moe_plan(ids, cnt, rows=MOE_ROWS)
        xs = _dispatch(xf, g_ffn[l], dest, block_expert.shape[0] * MOE_ROWS, ts=64)
        y = _experts(xs, block_expert, run_next, n_used, w_gate, w_up, w_down, l, rows=MOE_ROWS)
        xf = _combine(xf, gates, dest, y, tt=64)
    return xf.reshape(b, s, d)
```

```python
import functools
import math

import jax
import jax.numpy as jnp
from jax import lax
from jax.experimental import pallas as pl
from jax.experimental.pallas import tpu as pltpu

F32 = jnp.float32
BF16 = jnp.bfloat16
I32 = jnp.int32

D_MODEL = 2048
DEPTH = 2
SWA_HEADS = 8
SWA_KV_HEADS = 2
SWA_HEAD_DIM = 64
WINDOW = 128
HG_HEADS = 8
HG_DK = 128
HG_DV = 128
MLA_HEADS = 4
MLA_Q_RANK = 512
MLA_KV_RANK = 256
MLA_NOPE = 128
MLA_ROPE = 64
MLA_QK = MLA_NOPE + MLA_ROPE
MLA_V = 128
ROPE_THETA = 10000.0
REL_BUCKETS = 32
REL_MAX_DIST = 128
MEM_HEADS = 4
MEM_HEAD_DIM = 128
MEM_WIDTH = MEM_HEADS * MEM_HEAD_DIM
N_GROUPS = 8
EXPERTS_PER_GROUP = 8
N_EXPERTS = N_GROUPS * EXPERTS_PER_GROUP
TOP_K = 2
D_EXPERT = 512
MOE_ROWS = 256
EPS = 1e-6
NEG_INF = -1e30

SWA_WIDTH = SWA_HEADS * SWA_HEAD_DIM
SWA_KV_WIDTH = SWA_KV_HEADS * SWA_HEAD_DIM
HG_WIDTH = HG_HEADS * HG_DV
MLA_WIDTH = MLA_HEADS * MLA_V
IN_SIZES = (SWA_WIDTH, SWA_KV_WIDTH, SWA_KV_WIDTH,
            HG_HEADS * HG_DK, HG_HEADS * HG_DK, HG_WIDTH, HG_WIDTH,
            MLA_Q_RANK, MLA_KV_RANK, MLA_ROPE)

LANES = 128
DMA_PRIORITIES = 2
MLA_QK_PAD = 2 * LANES
VMEM_LIMIT_BYTES = 56 * 1024 * 1024

HG_CHUNK = 128
HG_SUB = 4
MLA_TQ = 1024
MLA_TK = 512
MLA_ROW_GROUP = 256
NT_DIMS = (((1,), (1,)), ((), ()))
TN_DIMS = (((0,), (0,)), ((), ()))


def _params(*semantics):
    return pltpu.CompilerParams(dimension_semantics=semantics, vmem_limit_bytes=VMEM_LIMIT_BYTES)


def _rms(x, gain=None):
    y = x * lax.rsqrt(jnp.mean(x * x, axis=-1, keepdims=True) + EPS)
    return y if gain is None else y * gain


def _rms_matmul_kernel(x_ref, g_ref, w_ref, o_ref, hn_ref):
    @pl.when(pl.program_id(1) == 0)
    def _():
        hn_ref[...] = _rms(x_ref[...], g_ref[...]).astype(BF16)

    o_ref[...] = jnp.dot(hn_ref[...], w_ref[...], preferred_element_type=F32).astype(o_ref.dtype)


def _rms_matmul(x, gain, w, *, tm, tn, out_dtype=F32):
    n, d = x.shape
    nout = w.shape[1]
    return pl.pallas_call(
        _rms_matmul_kernel,
        grid=(n // tm, nout // tn),
        in_specs=[pl.BlockSpec((tm, d), lambda i, j: (i, 0)),
                  pl.BlockSpec((1, d), lambda i, j: (0, 0)),
                  pl.BlockSpec((d, tn), lambda i, j: (0, j))],
        out_specs=pl.BlockSpec((tm, tn), lambda i, j: (i, j)),
        out_shape=jax.ShapeDtypeStruct((n, nout), out_dtype),
        scratch_shapes=[pltpu.VMEM((tm, d), BF16)],
        compiler_params=_params("parallel", "arbitrary"),
        name="rms_matmul",
    )(x, gain.reshape(1, d), w)


def _swa_kernel(sink_ref, q_ref, kp_ref, kc_ref, vp_ref, vc_ref, bias_ref, gq_ref, gk_ref, o_ref):
    blk = pl.program_id(1)
    q = q_ref[0].astype(F32)
    k = jnp.concatenate([kp_ref[0], kc_ref[0]], axis=0).astype(F32)
    v = jnp.concatenate([vp_ref[0], vc_ref[0]], axis=0)
    qi = lax.broadcasted_iota(I32, (WINDOW, 2 * WINDOW), 0)
    kj = lax.broadcasted_iota(I32, (WINDOW, 2 * WINDOW), 1)
    dist = qi + WINDOW - kj
    has_prev = jnp.where(blk > 0, 0, WINDOW)
    valid = (dist >= 0) & (dist < WINDOW) & (kj >= has_prev)
    grp = SWA_HEADS // SWA_KV_HEADS
    outs = []
    for g in range(SWA_KV_HEADS):
        lo = g * SWA_HEAD_DIM
        kg = _rms(k[:, lo:lo + SWA_HEAD_DIM], gk_ref[...]).astype(BF16)
        vg = v[:, lo:lo + SWA_HEAD_DIM].astype(BF16)
        for hh in range(grp):
            h = g * grp + hh
            qh = _rms(q[:, h * SWA_HEAD_DIM:(h + 1) * SWA_HEAD_DIM], gq_ref[...])
            qh = (qh * (SWA_HEAD_DIM ** -0.5)).astype(BF16)
            s = lax.dot_general(qh, kg, NT_DIMS, preferred_element_type=F32) + bias_ref[h]
            s = jnp.where(valid, s, NEG_INF)
            sink = sink_ref[h]
            m = jnp.maximum(jnp.max(s, axis=-1, keepdims=True), sink)
            e = jnp.exp(s - m)
            denom = jnp.sum(e, axis=-1, keepdims=True) + jnp.exp(sink - m)
            p = (e / denom).astype(BF16)
            outs.append(jnp.dot(p, vg, preferred_element_type=F32))
    o_ref[0] = jnp.concatenate(outs, axis=-1).astype(o_ref.dtype)


def _swa(u_swa, rel_bias, gq, gk, sinks):
    b, s, _ = u_swa.shape
    nb = s // WINDOW
    kcol = SWA_WIDTH // LANES
    vcol = kcol + 1
    prev = lambda bi, n, *_: (bi, jnp.maximum(n - 1, 0))
    return pl.pallas_call(
        _swa_kernel,
        grid_spec=pltpu.PrefetchScalarGridSpec(
            num_scalar_prefetch=1,
            grid=(b, nb),
            in_specs=[
                pl.BlockSpec((1, WINDOW, SWA_WIDTH), lambda bi, n, *_: (bi, n, 0)),
                pl.BlockSpec((1, WINDOW, LANES), lambda bi, n, *_: prev(bi, n) + (kcol,)),
                pl.BlockSpec((1, WINDOW, LANES), lambda bi, n, *_: (bi, n, kcol)),
                pl.BlockSpec((1, WINDOW, LANES), lambda bi, n, *_: prev(bi, n) + (vcol,)),
                pl.BlockSpec((1, WINDOW, LANES), lambda bi, n, *_: (bi, n, vcol)),
                pl.BlockSpec((SWA_HEADS, WINDOW, 2 * WINDOW), lambda bi, n, *_: (0, 0, 0)),
                pl.BlockSpec((1, SWA_HEAD_DIM), lambda bi, n, *_: (0, 0)),
                pl.BlockSpec((1, SWA_HEAD_DIM), lambda bi, n, *_: (0, 0)),
            ],
            out_specs=pl.BlockSpec((1, WINDOW, SWA_WIDTH), lambda bi, n, *_: (bi, n, 0)),
        ),
        out_shape=jax.ShapeDtypeStruct((b, s, SWA_WIDTH), BF16),
        compiler_params=_params("parallel", "parallel"),
        name="swa_attention",
    )(sinks, u_swa, u_swa, u_swa, u_swa, u_swa, rel_bias,
      gq.reshape(1, SWA_HEAD_DIM), gk.reshape(1, SWA_HEAD_DIM))


def _hgrn_level_matrix(c):
    t = jnp.arange(c)[:, None]
    r = jnp.arange(c)[None, :]
    mats = [(r <= t)]
    half = c // 2
    while half >= 1:
        mid = (t // (2 * half)) * (2 * half) + half
        is_q = (t & half) != 0
        mats.append(jnp.where(is_q, (r >= mid) & (r <= t), (r > t) & (r < mid)))
        half //= 2
    return jnp.concatenate(mats, axis=0).astype(BF16)


def _hgrn_pair_level(c):
    t = jnp.arange(c, dtype=I32)[:, None]
    s = jnp.arange(c, dtype=I32)[None, :]
    x = t ^ s
    lvl = jnp.zeros((c, c), I32)
    half = c // 2
    while half >= 1:
        lvl = jnp.where((x & (-half)) == half, half, lvl)
        half //= 2
    return jnp.where(t > s, lvl, 0)


def _hgrn_kernel(q_ref, f_ref, v_ref, gate_ref, par_ref, amat_ref, lvl_ref, o_ref, st_ref, *, chunk, n_sub):
    @pl.when(pl.program_id(2) == 0)
    def _():
        st_ref[...] = jnp.zeros_like(st_ref)

    c = chunk
    par = par_ref[0]
    log_lb, log1m_lb, one_m_lb, g_out = par[0:1], par[1:2], par[2:3], par[3:4]
    row = lax.broadcasted_iota(I32, (c, 1), 0)
    pair_level = lvl_ref[...]
    st = st_ref[...]
    for sub in range(n_sub):
        rows = pl.ds(sub * c, c)
        q = q_ref[0, rows].astype(F32)
        f = f_ref[0, rows].astype(F32)
        v = v_ref[0, rows].astype(F32)
        gate = gate_ref[0, rows].astype(F32)

        e = jnp.exp(-jnp.abs(f))
        log_sig = jnp.minimum(f, 0.0) - jnp.log(1.0 + e)
        bb = log1m_lb + log_sig
        log_f = jnp.maximum(log_lb, bb) + jnp.log(1.0 + jnp.exp(-jnp.abs(log_lb - bb)))
        kk = one_m_lb * jnp.where(f < 0.0, 1.0, e) / (1.0 + e)

        g_hi = log_f.astype(BF16)
        g_lo = (log_f - g_hi.astype(F32)).astype(BF16)
        e2 = jnp.dot(amat_ref[...], jnp.concatenate([g_hi, g_lo], axis=1), preferred_element_type=F32)
        expo = e2[:, :HG_DK] + e2[:, HG_DK:]
        bcum = expo[0:c]
        btot = bcum[c - 1:c]

        attn = jnp.zeros((c, c), F32)
        half = c // 2
        lvl = 1
        while half >= 1:
            w = jnp.exp(expo[lvl * c:(lvl + 1) * c])
            z = (jnp.where((row & half) != 0, q, kk) * w).astype(BF16)
            a = lax.dot_general(z, z, NT_DIMS, preferred_element_type=F32)
            attn = jnp.where(pair_level == half, a, attn)
            half //= 2
            lvl += 1

        vb = v.astype(BF16)
        diag = jnp.sum(q * kk, axis=-1, keepdims=True)
        intra = jnp.dot(attn.astype(BF16), vb, preferred_element_type=F32) + diag * v
        qe = (q * jnp.exp(bcum)).astype(BF16)
        inter = lax.dot_general(qe, st.astype(BF16), NT_DIMS, preferred_element_type=F32)
        kd = (kk * jnp.exp(btot - bcum)).astype(BF16)
        st = st * jnp.exp(btot) + lax.dot_general(vb, kd, TN_DIMS, preferred_element_type=F32)

        o = _rms(inter + intra, g_out)
        o_ref[0, rows] = (o * gate / (1.0 + jnp.exp(-gate))).astype(o_ref.dtype)
    st_ref[...] = st


def _hgrn(u_hg, par, *, chunk, n_sub):
    b, s, _ = u_hg.shape
    amat = _hgrn_level_matrix(chunk)
    pair_level = _hgrn_pair_level(chunk)
    step = chunk * n_sub
    col = lambda off: (lambda bi, h, c: (bi, c, off + h))
    return pl.pallas_call(
        functools.partial(_hgrn_kernel, chunk=chunk, n_sub=n_sub),
        grid=(b, HG_HEADS, s // step),
        in_specs=[pl.BlockSpec((1, step, HG_DK), col(0)),
                  pl.BlockSpec((1, step, HG_DK), col(HG_HEADS)),
                  pl.BlockSpec((1, step, HG_DV), col(2 * HG_HEADS)),
                  pl.BlockSpec((1, step, HG_DV), col(3 * HG_HEADS)),
                  pl.BlockSpec((1, 8, HG_DK), lambda bi, h, c: (h, 0, 0)),
                  pl.BlockSpec(amat.shape, lambda bi, h, c: (0, 0)),
                  pl.BlockSpec(pair_level.shape, lambda bi, h, c: (0, 0))],
        out_specs=pl.BlockSpec((1, step, HG_DV), lambda bi, h, c: (bi, c, h)),
        out_shape=jax.ShapeDtypeStruct((b, s, HG_WIDTH), BF16),
        scratch_shapes=[pltpu.VMEM((HG_DV, HG_DK), F32)],
        compiler_params=_params("parallel", "parallel", "arbitrary"),
        name="hgrn2",
    )(u_hg, u_hg, u_hg, u_hg, par, amat, pair_level)


def _mla_prep_kernel(u_ref, tab_ref, wq_ref, wkv_ref, gcq_ref, gckv_ref, gqn_ref, gqr_ref, gkn_ref, gkr_ref,
                     q_ref, k_ref, v_ref):
    u = u_ref[0].astype(F32)
    cq = _rms(u[:, :MLA_Q_RANK], gcq_ref[...]).astype(BF16)
    ckv = _rms(u[:, MLA_Q_RANK:MLA_Q_RANK + MLA_KV_RANK], gckv_ref[...]).astype(BF16)
    kr = u[:, MLA_Q_RANK + MLA_KV_RANK:]
    qf = jnp.dot(cq, wq_ref[...], preferred_element_type=F32)
    kvf = jnp.dot(ckv, wkv_ref[...], preferred_element_type=F32)
    tab = tab_ref[...]
    low = lax.broadcasted_iota(I32, kr.shape, 1) < MLA_ROPE
    kr_sq = jnp.sum(jnp.where(low, kr * kr, 0.0), axis=-1, keepdims=True)
    scale = MLA_QK ** -0.5 * math.log2(math.e)
    ones = jnp.ones((u.shape[0], MLA_V), F32)
    for h in range(MLA_HEADS):
        qn = qf[:, h * MLA_QK_PAD:h * MLA_QK_PAD + MLA_NOPE]
        qr = qf[:, h * MLA_QK_PAD + MLA_NOPE:(h + 1) * MLA_QK_PAD]
        ss = jnp.sum(qn * qn, axis=-1, keepdims=True) + jnp.sum(jnp.where(low, qr * qr, 0.0), axis=-1, keepdims=True)
        rstd = lax.rsqrt(ss / MLA_QK + EPS) * scale
        t = qr * rstd * tab * gqr_ref[...]
        rope = jnp.where(low, t + pltpu.roll(t, MLA_ROPE, 1), 0.0)
        q_ref[0, h] = jnp.concatenate([qn * rstd * gqn_ref[...], rope], axis=-1).astype(q_ref.dtype)

        kn = kvf[:, h * MLA_NOPE:(h + 1) * MLA_NOPE]
        ss = jnp.sum(kn * kn, axis=-1, keepdims=True) + kr_sq
        rstd = lax.rsqrt(ss / MLA_QK + EPS)
        t = kr * rstd * tab * gkr_ref[...]
        rope = t + pltpu.roll(t, MLA_ROPE, 1)
        k_ref[0, h] = jnp.concatenate([kn * rstd * gkn_ref[...], rope], axis=-1).astype(k_ref.dtype)
        vh = kvf[:, MLA_HEADS * MLA_NOPE + h * MLA_V:MLA_HEADS * MLA_NOPE + (h + 1) * MLA_V]
        v_ref[0, h] = jnp.concatenate([vh, ones], axis=-1).astype(v_ref.dtype)


def _swap_halves(a):
    half = a.shape[-1] // 2
    return jnp.concatenate([a[..., half:], a[..., :half]], axis=-1)


def _mla_prep(u_mla, w_uq, w_ukv, g_cq, g_ckv, gq, gk, *, tm):
    b, s, wu = u_mla.shape
    wq = w_uq.reshape(MLA_Q_RANK, MLA_HEADS, MLA_QK)
    wq = jnp.concatenate([wq, _swap_halves(wq[..., MLA_NOPE:])], axis=-1).reshape(MLA_Q_RANK, MLA_HEADS * MLA_QK_PAD)
    wkv = w_ukv.reshape(MLA_KV_RANK, MLA_HEADS, MLA_NOPE + MLA_V)
    wkv = jnp.concatenate([wkv[..., :MLA_NOPE].reshape(MLA_KV_RANK, -1), wkv[..., MLA_NOPE:].reshape(MLA_KV_RANK, -1)], axis=1)
    half = MLA_ROPE // 2
    inv_freq = ROPE_THETA ** (-jnp.arange(half, dtype=F32) / half)
    ang = jnp.arange(s, dtype=F32)[:, None] * inv_freq[None, :]
    cos, sin = jnp.cos(ang), jnp.sin(ang)
    tab = jnp.concatenate([cos, cos, -sin, sin], axis=-1)
    rope_gain = lambda g: jnp.concatenate([g[MLA_NOPE:], _swap_halves(g[MLA_NOPE:])]).reshape(1, 2 * MLA_ROPE)
    vec = lambda i, j: (0, 0)
    head_out = lambda width: pl.BlockSpec((1, MLA_HEADS, tm, width), lambda i, j: (i, 0, j, 0))
    return pl.pallas_call(
        _mla_prep_kernel,
        grid=(b, s // tm),
        in_specs=[pl.BlockSpec((1, tm, wu), lambda i, j: (i, j, 0)),
                  pl.BlockSpec((tm, 2 * MLA_ROPE), lambda i, j: (j, 0)),
                  pl.BlockSpec(wq.shape, vec),
                  pl.BlockSpec(wkv.shape, vec),
                  pl.BlockSpec((1, MLA_Q_RANK), vec),
                  pl.BlockSpec((1, MLA_KV_RANK), vec),
                  pl.BlockSpec((1, MLA_NOPE), vec),
                  pl.BlockSpec((1, 2 * MLA_ROPE), vec),
                  pl.BlockSpec((1, MLA_NOPE), vec),
                  pl.BlockSpec((1, 2 * MLA_ROPE), vec)],
        out_specs=[head_out(MLA_QK_PAD), head_out(MLA_QK_PAD), head_out(2 * MLA_V)],
        out_shape=[jax.ShapeDtypeStruct((b, MLA_HEADS, s, MLA_QK_PAD), BF16),
                   jax.ShapeDtypeStruct((b, MLA_HEADS, s, MLA_QK_PAD), BF16),
                   jax.ShapeDtypeStruct((b, MLA_HEADS, s, 2 * MLA_V), BF16)],
        compiler_params=_params("parallel", "parallel"),
        name="mla_prep",
    )(u_mla, tab, wq.astype(BF16), wkv.astype(BF16), g_cq.reshape(1, -1), g_ckv.reshape(1, -1),
      gq[:MLA_NOPE].reshape(1, -1), rope_gain(gq), gk[:MLA_NOPE].reshape(1, -1), rope_gain(gk))


def _mla_attn_kernel(qi_ref, ki_ref, q_ref, k_ref, v_ref, o_ref, m_ref, acc_ref, *, tq, tk):
    step = pl.program_id(2)
    qi = qi_ref[step]
    ki = ki_ref[step]

    @pl.when(ki == 0)
    def _():
        m_ref[...] = jnp.full_like(m_ref, NEG_INF)
        acc_ref[...] = jnp.zeros_like(acc_ref)

    def update(masked):
        for r0 in range(0, tq, MLA_ROW_GROUP):
            rows = pl.ds(r0, MLA_ROW_GROUP)
            s = lax.dot_general(q_ref[0, 0, rows], k_ref[0, 0], NT_DIMS, preferred_element_type=F32)
            if masked:
                row = qi * tq + r0 + lax.broadcasted_iota(I32, s.shape, 0)
                col = ki * tk + lax.broadcasted_iota(I32, s.shape, 1)
                s = jnp.where(col <= row, s, NEG_INF)
            m_prev = m_ref[rows]
            m_next = jnp.maximum(m_prev, jnp.max(s, axis=-1, keepdims=True))
            alpha = jnp.exp2(m_prev - m_next)
            p = jnp.exp2(s - m_next[:, :1]).astype(BF16)
            pv = jnp.dot(p, v_ref[0, 0], preferred_element_type=F32)
            acc_ref[rows, :MLA_V] = alpha * acc_ref[rows, :MLA_V] + pv[:, :MLA_V]
            acc_ref[rows, MLA_V:] = alpha * acc_ref[rows, MLA_V:] + pv[:, MLA_V:]
            m_ref[rows] = m_next

    crosses_diagonal = (ki + 1) * tk - 1 > qi * tq
    pl.when(crosses_diagonal)(lambda: update(True))
    pl.when(jnp.logical_not(crosses_diagonal))(lambda: update(False))

    @pl.when((ki + 1) * tk == (qi + 1) * tq)
    def _():
        o_ref[0] = (acc_ref[:, :MLA_V] / acc_ref[:, MLA_V:]).astype(o_ref.dtype)


def _mla_attention(q, k, v, *, tq, tk):
    b, h, s, _ = q.shape
    pairs = [(qi, ki) for qi in range(s // tq) for ki in range((qi + 1) * tq // tk)]
    qi_of = jnp.array([p[0] for p in pairs], I32)
    ki_of = jnp.array([p[1] for p in pairs], I32)
    kv_idx = lambda bi, hi, t, qi_ref, ki_ref: (bi, hi, ki_ref[t], 0)
    return pl.pallas_call(
        functools.partial(_mla_attn_kernel, tq=tq, tk=tk),
        grid_spec=pltpu.PrefetchScalarGridSpec(
            num_scalar_prefetch=2,
            grid=(b, h, len(pairs)),
            in_specs=[pl.BlockSpec((1, 1, tq, MLA_QK_PAD), lambda bi, hi, t, qi_ref, ki_ref: (bi, hi, qi_ref[t], 0)),
                      pl.BlockSpec((1, 1, tk, MLA_QK_PAD), kv_idx),
                      pl.BlockSpec((1, 1, tk, 2 * MLA_V), kv_idx)],
            out_specs=pl.BlockSpec((1, tq, MLA_V), lambda bi, hi, t, qi_ref, ki_ref: (bi, qi_ref[t], hi)),
            scratch_shapes=[pltpu.VMEM((tq, MLA_V), F32), pltpu.VMEM((tq, 2 * MLA_V), F32)],
        ),
        out_shape=jax.ShapeDtypeStruct((b, s, h * MLA_V), BF16),
        compiler_params=_params("parallel", "parallel", "arbitrary"),
        name="mla_attention",
    )(qi_of, ki_of, q, k, v)


def _mem_attn_body(x, kv, g_ref, wq_ref, wo_ref, gq_ref, gk_ref):
    hn = _rms(x, g_ref[...]).astype(BF16)
    qf = jnp.dot(hn, wq_ref[...], preferred_element_type=F32)
    outs = []
    for h in range(MEM_HEADS):
        lo = h * MEM_HEAD_DIM
        qh = (_rms(qf[:, lo:lo + MEM_HEAD_DIM], gq_ref[...]) * (MEM_HEAD_DIM ** -0.5)).astype(BF16)
        kh = _rms(kv[:, lo:lo + MEM_HEAD_DIM], gk_ref[...]).astype(BF16)
        vh = kv[:, MEM_WIDTH + lo:MEM_WIDTH + lo + MEM_HEAD_DIM].astype(BF16)
        s = lax.dot_general(qh, kh, NT_DIMS, preferred_element_type=F32)
        e = jnp.exp(s - jnp.max(s, axis=-1, keepdims=True))
        p = (e / jnp.sum(e, axis=-1, keepdims=True)).astype(BF16)
        outs.append(jnp.dot(p, vh, preferred_element_type=F32))
    o = jnp.concatenate(outs, axis=-1).astype(BF16)
    return x + jnp.dot(o, wo_ref[...], preferred_element_type=F32)


def _split_bf16(a):
    hi = a.astype(BF16)
    return hi, (a - hi.astype(F32)).astype(BF16)


def _router_body(x, g_ref, whi_ref, wlo_ref, b_ref, ids_ref, gates_ref, cnt_ref, carry_ref):
    hn = _rms(x, g_ref[...])
    hi, lo = _split_bf16(hn)
    logits = (jnp.dot(hi, whi_ref[...], preferred_element_type=F32)
              + jnp.dot(hi, wlo_ref[...], preferred_element_type=F32)
              + jnp.dot(lo, whi_ref[...], preferred_element_type=F32)) + b_ref[...]
    lane = lax.broadcasted_iota(I32, logits.shape, 1)
    gl = jnp.where(lane < N_GROUPS, logits, NEG_INF)
    gmax = jnp.max(gl, axis=-1, keepdims=True)
    p_grp = 1.0 / jnp.sum(jnp.exp(gl - gmax), axis=-1, keepdims=True)
    grp = jnp.min(jnp.where(gl == gmax, lane, LANES), axis=-1, keepdims=True)
    in_grp = (lane >= N_GROUPS) & (lane < N_GROUPS + N_EXPERTS) & (((lane - N_GROUPS) // EXPERTS_PER_GROUP) == grp)
    el = jnp.where(in_grp, logits, NEG_INF)
    m1 = jnp.max(el, axis=-1, keepdims=True)
    i1 = jnp.min(jnp.where(el == m1, lane, LANES), axis=-1, keepdims=True)
    el2 = jnp.where(lane == i1, NEG_INF, el)
    m2 = jnp.max(el2, axis=-1, keepdims=True)
    i2 = jnp.min(jnp.where(el2 == m2, lane, LANES), axis=-1, keepdims=True)
    r = jnp.exp(m2 - m1)
    g1 = p_grp / (1.0 + r)
    gates_ref[...] = jnp.where(lane == 0, g1, jnp.where(lane == 1, g1 * r, 0.0))

    tm = logits.shape[0]
    used = jnp.where((lane == i1) | (lane == i2), 1.0, 0.0)
    earlier = lax.broadcasted_iota(I32, (tm, tm), 1) < lax.broadcasted_iota(I32, (tm, tm), 0)
    before = carry_ref[...] + jnp.dot(jnp.where(earlier, 1.0, 0.0).astype(BF16), used.astype(BF16),
                                      preferred_element_type=F32)
    r1 = jnp.sum(jnp.where(lane == i1, before, 0.0), axis=-1, keepdims=True).astype(I32)
    r2 = jnp.sum(jnp.where(lane == i2, before, 0.0), axis=-1, keepdims=True).astype(I32)
    carry_ref[...] += jnp.sum(used, axis=0, keepdims=True)
    cnt_ref[...] = carry_ref[...].astype(I32)
    ids_ref[...] = jnp.where(lane == 0, i1 - N_GROUPS, jnp.where(lane == 1, i2 - N_GROUPS,
                             jnp.where(lane == 2, r1, jnp.where(lane == 3, r2, 0))))


def _post_mixer_kernel(x_ref, a_ref, b_ref, c_ref, wa_ref, wb_ref, wc_ref,
                       kv_ref, gmq_ref, wmq_ref, wmo_ref, mgq_ref, mgk_ref,
                       gffn_ref, whi_ref, wlo_ref, bias_ref,
                       o_ref, ids_ref, gates_ref, cnt_ref, carry_ref):
    @pl.when((pl.program_id(0) == 0) & (pl.program_id(1) == 0))
    def _():
        carry_ref[...] = jnp.zeros_like(carry_ref)

    mix = jnp.dot(a_ref[0], wa_ref[...], preferred_element_type=F32)
    mix += jnp.dot(b_ref[0], wb_ref[...], preferred_element_type=F32)
    mix += jnp.dot(c_ref[0], wc_ref[...], preferred_element_type=F32)
    x = x_ref[0] + mix
    x = _mem_attn_body(x, kv_ref[0], gmq_ref, wmq_ref, wmo_ref, mgq_ref, mgk_ref)
    o_ref[0] = x
    _router_body(x, gffn_ref, whi_ref, wlo_ref, bias_ref, ids_ref, gates_ref, cnt_ref, carry_ref)


def _post_mixer(x, a, bmix, c, w_out, kv, g_mem_q, w_mq, w_mo, mem_gq, mem_gk,
                g_ffn, w_gr, b_gr, w_er, b_er, *, tm):
    b, s, d = x.shape
    m = kv.shape[1]
    nst = s // tm
    wa = w_out[:SWA_WIDTH].astype(BF16)
    wb = w_out[SWA_WIDTH:SWA_WIDTH + HG_WIDTH].astype(BF16)
    wc = w_out[SWA_WIDTH + HG_WIDTH:].astype(BF16)
    pad = LANES - N_GROUPS - N_EXPERTS
    wr = jnp.concatenate([w_gr, w_er, jnp.zeros((d, pad), F32)], axis=1)
    bias = jnp.concatenate([b_gr, b_er, jnp.zeros((pad,), F32)]).reshape(1, LANES)
    whi, wlo = _split_bf16(wr)
    tile = lambda width: pl.BlockSpec((1, tm, width), lambda i, j: (i, j, 0))
    const = lambda shape: pl.BlockSpec(shape, lambda i, j: (0,) * len(shape), pipeline_mode=pl.Buffered(1))
    flat = lambda width: pl.BlockSpec((tm, width), lambda i, j: (i * nst + j, 0))
    return pl.pallas_call(
        _post_mixer_kernel,
        grid=(b, nst),
        in_specs=[tile(d), tile(SWA_WIDTH), tile(HG_WIDTH), tile(MLA_WIDTH),
                  const(wa.shape), const(wb.shape), const(wc.shape),
                  pl.BlockSpec((1, m, 2 * MEM_WIDTH), lambda i, j: (i, 0, 0)),
                  const((1, d)), const((d, MEM_WIDTH)), const((MEM_WIDTH, d)),
                  const((1, MEM_HEAD_DIM)), const((1, MEM_HEAD_DIM)),
                  const((1, d)), const((d, LANES)), const((d, LANES)), const((1, LANES))],
        out_specs=[tile(d), flat(LANES), flat(LANES), pl.BlockSpec((1, LANES), lambda i, j: (0, 0))],
        out_shape=[jax.ShapeDtypeStruct((b, s, d), F32), jax.ShapeDtypeStruct((b * s, LANES), I32),
                   jax.ShapeDtypeStruct((b * s, LANES), F32), jax.ShapeDtypeStruct((1, LANES), I32)],
        scratch_shapes=[pltpu.VMEM((1, LANES), F32)],
        compiler_params=_params("arbitrary", "arbitrary"),
        name="post_mixer",
    )(x, a, bmix, c, wa, wb, wc, kv, g_mem_q.reshape(1, d), w_mq.astype(BF16), w_mo.astype(BF16),
      mem_gq.reshape(1, -1), mem_gk.reshape(1, -1), g_ffn.reshape(1, d), whi, wlo, bias)


HI_HALF_MASK = 0xFFFF0000


def _dispatch_kernel(dest_ref, x_ref, g_ref, xs_in_hbm, xs_hbm, buf, sem, *, ts):
    del xs_in_hbm
    i = pl.program_id(0)
    last = pl.num_programs(0) - 1
    slot = i % 2

    def row_copy(slot_, r, dst_row):
        return pltpu.make_async_copy(buf.at[slot_, pl.ds(r, 1)], xs_hbm.at[pl.ds(dst_row, 1)], sem.at[slot_])

    def wait_step(slot_):
        for _ in range(ts * TOP_K):
            row_copy(slot_, 0, 0).wait()

    @pl.when(i >= 2)
    def _():
        wait_step(slot)

    hn = _rms(x_ref[...], g_ref[...])
    half = hn.shape[1] // 2
    lo = pltpu.bitcast(hn[:, :half].astype(BF16).astype(F32), jnp.uint32) >> 16
    hi = pltpu.bitcast(hn[:, half:].astype(BF16).astype(F32), jnp.uint32) & jnp.uint32(HI_HALF_MASK)
    buf[slot] = lo | hi
    for r in range(ts):
        for kk in range(TOP_K):
            row_copy(slot, r, dest_ref[(i * ts + r) * TOP_K + kk]).start(priority=kk % DMA_PRIORITIES)

    @pl.when(i == last)
    def _():
        wait_step(slot)

    @pl.when((i == last) & (i >= 1))
    def _():
        wait_step(1 - slot)


def _dispatch(x, g_ffn, dest, n_rows, *, ts):
    n, d = x.shape
    xs0 = jnp.zeros((n_rows, d // 2), jnp.uint32)
    return pl.pallas_call(
        functools.partial(_dispatch_kernel, ts=ts),
        grid_spec=pltpu.PrefetchScalarGridSpec(
            num_scalar_prefetch=1,
            grid=(n // ts,),
            in_specs=[pl.BlockSpec((ts, d), lambda i, *_: (i, 0)),
                      pl.BlockSpec((1, d), lambda i, *_: (0, 0)),
                      pl.BlockSpec(memory_space=pl.ANY)],
            out_specs=pl.BlockSpec(memory_space=pl.ANY),
            scratch_shapes=[pltpu.VMEM((2, ts, d // 2), jnp.uint32), pltpu.SemaphoreType.DMA((2,))],
        ),
        out_shape=jax.ShapeDtypeStruct((n_rows, d // 2), jnp.uint32),
        input_output_aliases={3: 0},
        compiler_params=_params("arbitrary"),
        name="moe_dispatch",
    )(dest, x, g_ffn.reshape(1, d), xs0)


def _expert_kernel(bexp_ref, next_ref, nblk_ref, xs_ref, wg_hbm, wu_hbm, wd_hbm, y_ref,
                   wg_f32, wu_f32, wd_f32, sem, wg_bf, wu_bf, wd_bf, *, layer):
    i = pl.program_id(0)
    n_used = nblk_ref[0]
    expert = bexp_ref[i]
    first_of_run = (i == 0) | (expert != bexp_ref[jnp.maximum(i - 1, 0)])
    slot = next_ref[2 * i + 1]

    def weight_copies(e, slot_):
        return [pltpu.make_async_copy(hbm.at[layer, e], buf.at[slot_], sem.at[slot_, j])
                for j, (hbm, buf) in enumerate(((wg_hbm, wg_f32), (wu_hbm, wu_f32), (wd_hbm, wd_f32)))]

    @pl.when(i == 0)
    def _():
        for cp in weight_copies(expert, 0):
            cp.start()

    @pl.when(first_of_run & (i < n_used))
    def _():
        for cp in weight_copies(expert, slot):
            cp.wait()
        next_expert = next_ref[2 * i]

        @pl.when(next_expert >= 0)
        def _():
            for cp in weight_copies(next_expert, 1 - slot):
                cp.start()

        wg_bf[...] = wg_f32[slot].astype(BF16)
        wu_bf[...] = wu_f32[slot].astype(BF16)
        wd_bf[...] = wd_f32[slot].astype(BF16)

    @pl.when(i < n_used)
    def _():
        w = xs_ref[...]
        half = w.shape[1]
        lo = pltpu.bitcast(w << 16, F32).astype(BF16)
        hi = pltpu.bitcast(w & jnp.uint32(HI_HALF_MASK), F32).astype(BF16)
        gate = (jnp.dot(lo, wg_bf[:half], preferred_element_type=F32)
                + jnp.dot(hi, wg_bf[half:], preferred_element_type=F32))
        up = (jnp.dot(lo, wu_bf[:half], preferred_element_type=F32)
              + jnp.dot(hi, wu_bf[half:], preferred_element_type=F32))
        act = (gate / (1.0 + jnp.exp(-gate)) * up).astype(BF16)
        y_ref[...] = jnp.dot(act, wd_bf[...], preferred_element_type=F32)

    @pl.when(i >= nblk_ref[0])
    def _():
        y_ref[...] = jnp.zeros_like(y_ref)


def _experts(xs, block_expert, run_next, n_used, w_gate, w_up, w_down, layer, *, rows):
    d = 2 * xs.shape[1]
    n_blocks = block_expert.shape[0]
    hbm = pl.BlockSpec(memory_space=pl.ANY)
    return pl.pallas_call(
        functools.partial(_expert_kernel, layer=layer),
        grid_spec=pltpu.PrefetchScalarGridSpec(
            num_scalar_prefetch=3,
            grid=(n_blocks,),
            in_specs=[pl.BlockSpec((rows, d // 2), lambda i, *_: (i, 0)), hbm, hbm, hbm],
            out_specs=pl.BlockSpec((rows, d), lambda i, *_: (i, 0)),
            scratch_shapes=[pltpu.VMEM((2, d, D_EXPERT), F32),
                            pltpu.VMEM((2, d, D_EXPERT), F32),
                            pltpu.VMEM((2, D_EXPERT, d), F32),
                            pltpu.SemaphoreType.DMA((2, 3)),
                            pltpu.VMEM((d, D_EXPERT), BF16),
                            pltpu.VMEM((d, D_EXPERT), BF16),
                            pltpu.VMEM((D_EXPERT, d), BF16)],
        ),
        out_shape=jax.ShapeDtypeStruct((n_blocks * rows, d), F32),
        compiler_params=_params("arbitrary"),
        name="moe_experts",
    )(block_expert, run_next, n_used, xs, w_gate, w_up, w_down)


def _combine_kernel(pos_ref, x_ref, gates_ref, y_hbm, o_ref, ybuf, sem, *, tt):
    i = pl.program_id(0)
    nsteps = pl.num_programs(0)

    def row_copy(src_row, slot, dst_row):
        return pltpu.make_async_copy(y_hbm.at[pl.ds(src_row, 1)], ybuf.at[slot, pl.ds(dst_row, 1)], sem.at[slot])

    def start_gather(step, slot):
        for r in range(tt):
            for kk in range(TOP_K):
                row_copy(pos_ref[(step * tt + r) * TOP_K + kk], slot, kk * tt + r).start(priority=kk % DMA_PRIORITIES)

    def wait_gather(slot):
        for r in range(TOP_K * tt):
            row_copy(0, slot, r).wait()

    @pl.when(i == 0)
    def _():
        start_gather(0, 0)

    @pl.when(i + 1 < nsteps)
    def _():
        start_gather(i + 1, (i + 1) % 2)

    wait_gather(i % 2)
    yb = ybuf[i % 2]
    g = gates_ref[...]
    o_ref[...] = x_ref[...] + g[:, 0:1] * yb[:tt] + g[:, 1:2] * yb[tt:]


def _combine(x, gates, pos, y, *, tt):
    n, d = x.shape
    return pl.pallas_call(
        functools.partial(_combine_kernel, tt=tt),
        grid_spec=pltpu.PrefetchScalarGridSpec(
            num_scalar_prefetch=1,
            grid=(n // tt,),
            in_specs=[pl.BlockSpec((tt, d), lambda i, *_: (i, 0)),
                      pl.BlockSpec((tt, LANES), lambda i, *_: (i, 0)),
                      pl.BlockSpec(memory_space=pl.ANY)],
            out_specs=pl.BlockSpec((tt, d), lambda i, *_: (i, 0)),
            scratch_shapes=[pltpu.VMEM((2, TOP_K * tt, d), F32), pltpu.SemaphoreType.DMA((2,))],
        ),
        out_shape=jax.ShapeDtypeStruct((n, d), F32),
        compiler_params=_params("arbitrary"),
        name="moe_combine",
    )(pos, x, gates, y)


def _moe_plan(ids, cnt, *, rows):
    n = ids.shape[0]
    expert, rank = ids[:, :TOP_K], ids[:, TOP_K:2 * TOP_K]
    counts = cnt[0, N_GROUPS:N_GROUPS + N_EXPERTS]
    padded = (counts + rows - 1) // rows * rows
    padded_end = jnp.cumsum(padded)
    seg_start = padded_end - padded
    onehot = expert[..., None] == jnp.arange(N_EXPERTS, dtype=I32)
    dest = (jnp.sum(jnp.where(onehot, seg_start, 0), axis=-1) + rank).astype(I32).reshape(n * TOP_K)
    n_blocks = -(-(n * TOP_K) // rows) + N_EXPERTS
    block_start = jnp.arange(n_blocks, dtype=I32) * rows
    block_expert = jnp.minimum(jnp.sum(padded_end[None, :] <= block_start[:, None], axis=1), N_EXPERTS - 1).astype(I32)
    n_used = (padded_end[-1] // rows).astype(I32)
    block_onehot = block_expert[:, None] == jnp.arange(N_EXPERTS, dtype=I32)
    pick = lambda per_expert: jnp.sum(jnp.where(block_onehot, per_expert, 0), axis=-1)
    next_block = pick(padded_end // rows)
    next_onehot = jnp.minimum(next_block, n_blocks - 1)[:, None] == jnp.arange(n_blocks, dtype=I32)
    next_expert = jnp.where(next_block < n_used, jnp.sum(jnp.where(next_onehot, block_expert, 0), axis=-1), -1)
    nonempty = (counts > 0).astype(I32)
    run_parity = pick(jnp.cumsum(nonempty) - nonempty) % 2
    run_next = jnp.stack([next_expert, run_parity], axis=-1).astype(I32).reshape(2 * n_blocks)
    return dest, block_expert, run_next, n_used.reshape(1)


def _band_relative_bias(table):
    def bucket(nd):
        max_exact = REL_BUCKETS // 2
        nf = jnp.maximum(nd, 1).astype(F32)
        large = max_exact + (jnp.log(nf / max_exact) / math.log(REL_MAX_DIST / max_exact)
                             * (REL_BUCKETS - max_exact)).astype(I32)
        return jnp.where(nd < max_exact, nd, jnp.minimum(large, REL_BUCKETS - 1))

    qi = jnp.arange(WINDOW)[:, None]
    kj = jnp.arange(2 * WINDOW)[None, :]
    dist = jnp.maximum(qi + WINDOW - kj, 0)
    onehot = (bucket(dist)[..., None] == jnp.arange(REL_BUCKETS)).astype(F32)
    return jnp.einsum('qkb,bh->hqk', onehot, table.astype(F32), precision=lax.Precision.HIGHEST)


def _hgrn_params(lb, g_out):
    lb = lb.reshape(HG_HEADS, 1, HG_DK)
    gain = jnp.broadcast_to(g_out.reshape(1, 1, HG_DV), (HG_HEADS, 1, HG_DV))
    rows = [jnp.log(lb), jnp.log1p(-lb), 1.0 - lb, gain, jnp.zeros((HG_HEADS, 4, HG_DK), F32)]
    return jnp.concatenate(rows, axis=1).astype(F32)


def kernel(x, mem, rel_bias_table, hg_lb_logits, g_mix, w_in, swa_gq, swa_gk, swa_sinks, hg_g_out, mla_g_cq, mla_g_ckv, mla_w_uq, mla_w_ukv, mla_gq, mla_gk, w_out, g_mem_q, g_mem_kv, w_mq, w_mkv, mem_gq, mem_gk, w_mo, g_ffn, w_group_router, b_group_router, w_expert_router, b_expert_router, w_gate, w_up, w_down):
    b, s, d = x.shape
    n = b * s
    m = mem.shape[1]
    rel_bias = _band_relative_bias(rel_bias_table)
    lb_all = jnp.cumsum(jax.nn.softmax(hg_lb_logits.astype(F32), axis=0), axis=0)
    lb_all = lb_all - lb_all[:1]
    offs = [0]
    for width in IN_SIZES:
        offs.append(offs[-1] + width)
    o_hg, o_cq, o_kr = offs[3], offs[7], offs[9]

    xf = x.reshape(n, d)
    mem2 = mem.reshape(b * m, d)
    for l in range(DEPTH):
        w = w_in[l]
        kr_cols = w[:, o_kr:]
        w_swa = w[:, :o_hg].astype(BF16)
        w_hg = w[:, o_hg:o_cq].astype(BF16)
        w_mla = jnp.concatenate([w[:, o_cq:], _swap_halves(kr_cols)], axis=1).astype(BF16)
        u_swa = _rms_matmul(xf, g_mix[l], w_swa, tm=1024, tn=w_swa.shape[1], out_dtype=BF16).reshape(b, s, -1)
        u_hg = _rms_matmul(xf, g_mix[l], w_hg, tm=1024, tn=1024, out_dtype=BF16).reshape(b, s, -1)
        u_mla = _rms_matmul(xf, g_mix[l], w_mla, tm=1024, tn=w_mla.shape[1], out_dtype=BF16).reshape(b, s, -1)

        out_a = _swa(u_swa, rel_bias, swa_gq[l], swa_gk[l], swa_sinks[l])
        out_b = _hgrn(u_hg, _hgrn_params(lb_all[l], hg_g_out[l]), chunk=HG_CHUNK, n_sub=HG_SUB)
        qm, km, vm = _mla_prep(u_mla, mla_w_uq[l], mla_w_ukv[l], mla_g_cq[l], mla_g_ckv[l],
                               mla_gq[l], mla_gk[l], tm=512)
        out_c = _mla_attention(qm, km, vm, tq=MLA_TQ, tk=MLA_TK)

        kv = _rms_matmul(mem2, g_mem_kv[l], w_mkv[l].astype(BF16), tm=b * m, tn=2 * MEM_WIDTH)
        x3, ids, gates, cnt = _post_mixer(
            xf.reshape(b, s, d), out_a, out_b, out_c, w_out[l], kv.reshape(b, m, -1),
            g_mem_q[l], w_mq[l], w_mo[l], mem_gq[l], mem_gk[l],
            g_ffn[l], w_group_router[l], b_group_router[l], w_expert_router[l], b_expert_router[l], tm=512)
        xf = x3.reshape(n, d)
        dest, block_expert, run_next, n_used = _moe_plan(ids, cnt, rows=MOE_ROWS)
        xs = _dispatch(xf, g_ffn[l], dest, block_expert.shape[0] * MOE_ROWS, ts=64)
        y = _experts(xs, block_expert, run_next, n_used, w_gate, w_up, w_down, l, rows=MOE_ROWS)
        xf = _combine(xf, gates, dest, y, tt=64)
    return xf.reshape(b, s, d)
```

```python
import functools
import math

import jax
import jax.numpy as jnp
from jax import lax
from jax.experimental import pallas as pl
from jax.experimental.pallas import tpu as pltpu

F32 = jnp.float32
BF16 = jnp.bfloat16
I32 = jnp.int32

D_MODEL = 2048
DEPTH = 2
SWA_HEADS = 8
SWA_KV_HEADS = 2
SWA_HEAD_DIM = 64
WINDOW = 128
HG_HEADS = 8
HG_DK = 128
HG_DV = 128
MLA_HEADS = 4
MLA_Q_RANK = 512
MLA_KV_RANK = 256
MLA_NOPE = 128
MLA_ROPE = 64
MLA_QK = MLA_NOPE + MLA_ROPE
MLA_V = 128
ROPE_THETA = 10000.0
REL_BUCKETS = 32
REL_MAX_DIST = 128
MEM_HEADS = 4
MEM_HEAD_DIM = 128
MEM_WIDTH = MEM_HEADS * MEM_HEAD_DIM
N_GROUPS = 8
EXPERTS_PER_GROUP = 8
N_EXPERTS = N_GROUPS * EXPERTS_PER_GROUP
TOP_K = 2
D_EXPERT = 512
MOE_ROWS = 256
EPS = 1e-6
NEG_INF = -1e30
LOG2_E = math.log2(math.e)

SWA_WIDTH = SWA_HEADS * SWA_HEAD_DIM
SWA_KV_WIDTH = SWA_KV_HEADS * SWA_HEAD_DIM
HG_WIDTH = HG_HEADS * HG_DV
MLA_WIDTH = MLA_HEADS * MLA_V
IN_SIZES = (SWA_WIDTH, SWA_KV_WIDTH, SWA_KV_WIDTH,
            HG_HEADS * HG_DK, HG_HEADS * HG_DK, HG_WIDTH, HG_WIDTH,
            MLA_Q_RANK, MLA_KV_RANK, MLA_ROPE)

LANES = 128
SUBLANES = 8
DMA_PRIORITIES = 2
MLA_QK_PAD = 2 * LANES
VMEM_LIMIT_BYTES = 56 * 1024 * 1024

HG_CHUNK = 128
HG_SUB = 4
MLA_TQ = 1024
MLA_TK = 512
MLA_ROW_GROUP = 256
NT_DIMS = (((1,), (1,)), ((), ()))
TN_DIMS = (((0,), (0,)), ((), ()))


def _params(*semantics):
    return pltpu.CompilerParams(dimension_semantics=semantics, vmem_limit_bytes=VMEM_LIMIT_BYTES)


def _rms(x, gain=None):
    y = x * lax.rsqrt(jnp.mean(x * x, axis=-1, keepdims=True) + EPS)
    return y if gain is None else y * gain


def _rms_matmul_kernel(x_ref, g_ref, w_ref, o_ref, hn_ref):
    @pl.when(pl.program_id(1) == 0)
    def _():
        hn_ref[...] = _rms(x_ref[...], g_ref[...]).astype(BF16)

    o_ref[...] = jnp.dot(hn_ref[...], w_ref[...], preferred_element_type=F32).astype(o_ref.dtype)


def _rms_matmul(x, gain, w, *, tm, tn, out_dtype=F32):
    n, d = x.shape
    nout = w.shape[1]
    return pl.pallas_call(
        _rms_matmul_kernel,
        grid=(n // tm, nout // tn),
        in_specs=[pl.BlockSpec((tm, d), lambda i, j: (i, 0)),
                  pl.BlockSpec((1, d), lambda i, j: (0, 0)),
                  pl.BlockSpec((d, tn), lambda i, j: (0, j))],
        out_specs=pl.BlockSpec((tm, tn), lambda i, j: (i, j)),
        out_shape=jax.ShapeDtypeStruct((n, nout), out_dtype),
        scratch_shapes=[pltpu.VMEM((tm, d), BF16)],
        compiler_params=_params("parallel", "arbitrary"),
        name="rms_matmul",
    )(x, gain.reshape(1, d), w)


def _swa_kernel(sink_ref, q_ref, kp_ref, kc_ref, vp_ref, vc_ref, bias_ref, gq_ref, gk_ref, o_ref):
    blk = pl.program_id(1)
    q = q_ref[0].astype(F32)
    k = jnp.concatenate([kp_ref[0], kc_ref[0]], axis=0).astype(F32)
    v = jnp.concatenate([vp_ref[0], vc_ref[0]], axis=0)
    qi = lax.broadcasted_iota(I32, (WINDOW, 2 * WINDOW), 0)
    kj = lax.broadcasted_iota(I32, (WINDOW, 2 * WINDOW), 1)
    dist = qi + WINDOW - kj
    has_prev = jnp.where(blk > 0, 0, WINDOW)
    valid = (dist >= 0) & (dist < WINDOW) & (kj >= has_prev)
    grp = SWA_HEADS // SWA_KV_HEADS
    outs = []
    for g in range(SWA_KV_HEADS):
        lo = g * SWA_HEAD_DIM
        kg = _rms(k[:, lo:lo + SWA_HEAD_DIM], gk_ref[...]).astype(BF16)
        vg = v[:, lo:lo + SWA_HEAD_DIM].astype(BF16)
        for hh in range(grp):
            h = g * grp + hh
            qh = _rms(q[:, h * SWA_HEAD_DIM:(h + 1) * SWA_HEAD_DIM], gq_ref[...])
            qh = (qh * (SWA_HEAD_DIM ** -0.5)).astype(BF16)
            s = lax.dot_general(qh, kg, NT_DIMS, preferred_element_type=F32) + bias_ref[h]
            s = jnp.where(valid, s, NEG_INF)
            sink = sink_ref[h]
            m = jnp.maximum(jnp.max(s, axis=-1, keepdims=True), sink)
            e = jnp.exp(s - m)
            denom = jnp.sum(e, axis=-1, keepdims=True) + jnp.exp(sink - m)
            p = (e / denom).astype(BF16)
            outs.append(jnp.dot(p, vg, preferred_element_type=F32))
    o_ref[0] = jnp.concatenate(outs, axis=-1).astype(o_ref.dtype)


def _swa(u_swa, rel_bias, gq, gk, sinks):
    b, s, _ = u_swa.shape
    nb = s // WINDOW
    kcol = SWA_WIDTH // LANES
    vcol = kcol + 1
    prev = lambda bi, n, *_: (bi, jnp.maximum(n - 1, 0))
    return pl.pallas_call(
        _swa_kernel,
        grid_spec=pltpu.PrefetchScalarGridSpec(
            num_scalar_prefetch=1,
            grid=(b, nb),
            in_specs=[
                pl.BlockSpec((1, WINDOW, SWA_WIDTH), lambda bi, n, *_: (bi, n, 0)),
                pl.BlockSpec((1, WINDOW, LANES), lambda bi, n, *_: prev(bi, n) + (kcol,)),
                pl.BlockSpec((1, WINDOW, LANES), lambda bi, n, *_: (bi, n, kcol)),
                pl.BlockSpec((1, WINDOW, LANES), lambda bi, n, *_: prev(bi, n) + (vcol,)),
                pl.BlockSpec((1, WINDOW, LANES), lambda bi, n, *_: (bi, n, vcol)),
                pl.BlockSpec((SWA_HEADS, WINDOW, 2 * WINDOW), lambda bi, n, *_: (0, 0, 0)),
                pl.BlockSpec((1, SWA_HEAD_DIM), lambda bi, n, *_: (0, 0)),
                pl.BlockSpec((1, SWA_HEAD_DIM), lambda bi, n, *_: (0, 0)),
            ],
            out_specs=pl.BlockSpec((1, WINDOW, SWA_WIDTH), lambda bi, n, *_: (bi, n, 0)),
        ),
        out_shape=jax.ShapeDtypeStruct((b, s, SWA_WIDTH), BF16),
        compiler_params=_params("parallel", "parallel"),
        name="swa_attention",
    )(sinks, u_swa, u_swa, u_swa, u_swa, u_swa, rel_bias,
      gq.reshape(1, SWA_HEAD_DIM), gk.reshape(1, SWA_HEAD_DIM))


def _hgrn_level_matrix(c):
    t = jnp.arange(c)[:, None]
    r = jnp.arange(c)[None, :]
    mats = [(r <= t)]
    half = c // 2
    while half >= 1:
        mid = (t // (2 * half)) * (2 * half) + half
        is_q = (t & half) != 0
        if half < SUBLANES:
            mats.append(jnp.where(is_q, (r >= mid) & (r <= t), (r > t) & (r < mid)))
        half //= 2
    return jnp.concatenate(mats, axis=0).astype(BF16)


def _hgrn_pair_level(c):
    t = jnp.arange(c, dtype=I32)[:, None]
    s = jnp.arange(c, dtype=I32)[None, :]
    x = t ^ s
    lvl = jnp.zeros((c, c), I32)
    half = c // 2
    while half >= 1:
        lvl = jnp.where((x & (-half)) == half, half, lvl)
        half //= 2
    return jnp.where(t > s, lvl, 0)


def _hgrn_kernel(q_ref, f_ref, v_ref, gate_ref, par_ref, amat_ref, lvl_ref, o_ref, st_ref, *, chunk, n_sub):
    @pl.when(pl.program_id(2) == 0)
    def _():
        st_ref[...] = jnp.zeros_like(st_ref)

    c = chunk
    par = par_ref[0]
    log_lb, log1m_lb, one_m_lb, g_out = par[0:1], par[1:2], par[2:3], par[3:4]
    row = lax.broadcasted_iota(I32, (c, 1), 0)
    pair_level = lvl_ref[...]
    st = st_ref[...]
    for sub in range(n_sub):
        rows = pl.ds(sub * c, c)
        q = q_ref[0, rows].astype(F32)
        f = f_ref[0, rows].astype(F32)
        v = v_ref[0, rows].astype(F32)
        gate = gate_ref[0, rows].astype(F32)

        e = jnp.exp(-jnp.abs(f))
        log_sig = jnp.minimum(f, 0.0) - jnp.log(1.0 + e)
        bb = log1m_lb + log_sig
        log_f = jnp.maximum(log_lb, bb) + jnp.log(1.0 + jnp.exp(-jnp.abs(log_lb - bb)))
        log_f = log_f * LOG2_E
        kk = one_m_lb * jnp.where(f < 0.0, 1.0, e) / (1.0 + e)

        g_hi = log_f.astype(BF16)
        g_lo = (log_f - g_hi.astype(F32)).astype(BF16)
        e2 = jnp.dot(amat_ref[...], jnp.concatenate([g_hi, g_lo], axis=1), preferred_element_type=F32)
        expo = e2[:, :HG_DK] + e2[:, HG_DK:]
        bcum = expo[0:c]
        btot = bcum[c - 1:c]

        attn = jnp.zeros((c, c), F32)
        half = c // 2
        lvl = 1
        while half >= 1:
            is_q = (row & half) != 0
            seg = 2 * half
            if half >= SUBLANES:
                ref = jnp.concatenate([jnp.broadcast_to(bcum[a + half - 1:a + half], (seg, HG_DK))
                                       for a in range(0, c, seg)], axis=0)
                diff = bcum - ref
                w = jnp.exp2(jnp.where(is_q, diff, -diff))
            else:
                w = jnp.exp2(expo[lvl * c:(lvl + 1) * c])
                lvl += 1
            zf = jnp.where(is_q, q, kk) * w
            z = zf.astype(BF16)
            if half >= SUBLANES:
                zq = jnp.concatenate([zf[a + half:a + seg] for a in range(0, c, seg)], axis=0).astype(BF16)
                aq = lax.dot_general(zq, z, NT_DIMS, preferred_element_type=F32)
                blank = jnp.zeros((half, c), F32)
                a = jnp.concatenate([blk for j in range(c // seg)
                                     for blk in (blank, aq[j * half:(j + 1) * half])], axis=0)
            else:
                a = lax.dot_general(z, z, NT_DIMS, preferred_element_type=F32)
            attn = jnp.where(pair_level == half, a, attn)
            half //= 2

        vb = v.astype(BF16)
        diag = jnp.sum(q * kk, axis=-1, keepdims=True)
        intra = jnp.dot(attn.astype(BF16), vb, preferred_element_type=F32) + diag * v
        qe = (q * jnp.exp2(bcum)).astype(BF16)
        inter = lax.dot_general(qe, st.astype(BF16), NT_DIMS, preferred_element_type=F32)
        kd = (kk * jnp.exp2(btot - bcum)).astype(BF16)
        st = st * jnp.exp2(btot) +lax.dot_general(vb, kd, TN_DIMS, preferred_element_type=F32)

        o = _rms(inter + intra, g_out)
        o_ref[0, rows] = (o * gate / (1.0 + jnp.exp(-gate))).astype(o_ref.dtype)
    st_ref[...] = st


def _hgrn(u_hg, par, *, chunk, n_sub):
    b, s, _ = u_hg.shape
    amat = _hgrn_level_matrix(chunk)
    pair_level = _hgrn_pair_level(chunk)
    step = chunk * n_sub
    col = lambda off: (lambda bi, h, c: (bi, c, off + h))
    return pl.pallas_call(
        functools.partial(_hgrn_kernel, chunk=chunk, n_sub=n_sub),
        grid=(b, HG_HEADS, s // step),
        in_specs=[pl.BlockSpec((1, step, HG_DK), col(0)),
                  pl.BlockSpec((1, step, HG_DK), col(HG_HEADS)),
                  pl.BlockSpec((1, step, HG_DV), col(2 * HG_HEADS)),
                  pl.BlockSpec((1, step, HG_DV), col(3 * HG_HEADS)),
                  pl.BlockSpec((1, 8, HG_DK), lambda bi, h, c: (h, 0, 0)),
                  pl.BlockSpec(amat.shape, lambda bi, h, c: (0, 0)),
                  pl.BlockSpec(pair_level.shape, lambda bi, h, c: (0, 0))],
        out_specs=pl.BlockSpec((1, step, HG_DV), lambda bi, h, c: (bi, c, h)),
        out_shape=jax.ShapeDtypeStruct((b, s, HG_WIDTH), BF16),
        scratch_shapes=[pltpu.VMEM((HG_DV, HG_DK), F32)],
        compiler_params=_params("parallel", "parallel", "arbitrary"),
        name="hgrn2",
    )(u_hg, u_hg, u_hg, u_hg, par, amat, pair_level)


def _mla_prep_kernel(u_ref, tab_ref, wq_ref, wkv_ref, gcq_ref, gckv_ref, gqn_ref, gqr_ref, gkn_ref, gkr_ref,
                     q_ref, k_ref, v_ref):
    u = u_ref[0].astype(F32)
    cq = _rms(u[:, :MLA_Q_RANK], gcq_ref[...]).astype(BF16)
    ckv = _rms(u[:, MLA_Q_RANK:MLA_Q_RANK + MLA_KV_RANK], gckv_ref[...]).astype(BF16)
    kr = u[:, MLA_Q_RANK + MLA_KV_RANK:]
    qf = jnp.dot(cq, wq_ref[...], preferred_element_type=F32)
    kvf = jnp.dot(ckv, wkv_ref[...], preferred_element_type=F32)
    tab = tab_ref[...]
    low = lax.broadcasted_iota(I32, kr.shape, 1) < MLA_ROPE
    kr_sq = jnp.sum(jnp.where(low, kr * kr, 0.0), axis=-1, keepdims=True)
    scale = MLA_QK ** -0.5 * LOG2_E
    ones = jnp.ones((u.shape[0], MLA_V), F32)
    for h in range(MLA_HEADS):
        qn = qf[:, h * MLA_QK_PAD:h * MLA_QK_PAD + MLA_NOPE]
        qr = qf[:, h * MLA_QK_PAD + MLA_NOPE:(h + 1) * MLA_QK_PAD]
        ss = jnp.sum(qn * qn, axis=-1, keepdims=True) + jnp.sum(jnp.where(low, qr * qr, 0.0), axis=-1, keepdims=True)
        rstd = lax.rsqrt(ss / MLA_QK + EPS) * scale
        t = qr * rstd * tab * gqr_ref[...]
        rope = jnp.where(low, t + pltpu.roll(t, MLA_ROPE, 1), 0.0)
        q_ref[0, h] = jnp.concatenate([qn * rstd * gqn_ref[...], rope], axis=-1).astype(q_ref.dtype)

        kn = kvf[:, h * MLA_NOPE:(h + 1) * MLA_NOPE]
        ss = jnp.sum(kn * kn, axis=-1, keepdims=True) + kr_sq
        rstd = lax.rsqrt(ss / MLA_QK + EPS)
        t = kr * rstd * tab * gkr_ref[...]
        rope = t + pltpu.roll(t, MLA_ROPE, 1)
        k_ref[0, h] = jnp.concatenate([kn * rstd * gkn_ref[...], rope], axis=-1).astype(k_ref.dtype)
        vh = kvf[:, MLA_HEADS * MLA_NOPE + h * MLA_V:MLA_HEADS * MLA_NOPE + (h + 1) * MLA_V]
        v_ref[0, h] = jnp.concatenate([vh, ones], axis=-1).astype(v_ref.dtype)


def _swap_halves(a):
    half = a.shape[-1] // 2
    return jnp.concatenate([a[..., half:], a[..., :half]], axis=-1)


def _mla_prep(u_mla, w_uq, w_ukv, g_cq, g_ckv, gq, gk, *, tm):
    b, s, wu = u_mla.shape
    wq = w_uq.reshape(MLA_Q_RANK, MLA_HEADS, MLA_QK)
    wq = jnp.concatenate([wq, _swap_halves(wq[..., MLA_NOPE:])], axis=-1).reshape(MLA_Q_RANK, MLA_HEADS * MLA_QK_PAD)
    wkv = w_ukv.reshape(MLA_KV_RANK, MLA_HEADS, MLA_NOPE + MLA_V)
    wkv = jnp.concatenate([wkv[..., :MLA_NOPE].reshape(MLA_KV_RANK, -1), wkv[..., MLA_NOPE:].reshape(MLA_KV_RANK, -1)], axis=1)
    half = MLA_ROPE // 2
    inv_freq = ROPE_THETA ** (-jnp.arange(half, dtype=F32) / half)
    ang = jnp.arange(s, dtype=F32)[:, None] * inv_freq[None, :]
    cos, sin = jnp.cos(ang), jnp.sin(ang)
    tab = jnp.concatenate([cos, cos, -sin, sin], axis=-1)
    rope_gain = lambda g: jnp.concatenate([g[MLA_NOPE:], _swap_halves(g[MLA_NOPE:])]).reshape(1, 2 * MLA_ROPE)
    vec = lambda i, j: (0, 0)
    head_out = lambda width: pl.BlockSpec((1, MLA_HEADS, tm, width), lambda i, j: (i, 0, j, 0))
    return pl.pallas_call(
        _mla_prep_kernel,
        grid=(b, s // tm),
        in_specs=[pl.BlockSpec((1, tm, wu), lambda i, j: (i, j, 0)),
                  pl.BlockSpec((tm, 2 * MLA_ROPE), lambda i, j: (j, 0)),
                  pl.BlockSpec(wq.shape, vec),
                  pl.BlockSpec(wkv.shape, vec),
                  pl.BlockSpec((1, MLA_Q_RANK), vec),
                  pl.BlockSpec((1, MLA_KV_RANK), vec),
                  pl.BlockSpec((1, MLA_NOPE), vec),
                  pl.BlockSpec((1, 2 * MLA_ROPE), vec),
                  pl.BlockSpec((1, MLA_NOPE), vec),
                  pl.BlockSpec((1, 2 * MLA_ROPE), vec)],
        out_specs=[head_out(MLA_QK_PAD), head_out(MLA_QK_PAD), head_out(2 * MLA_V)],
        out_shape=[jax.ShapeDtypeStruct((b, MLA_HEADS, s, MLA_QK_PAD), BF16),
                   jax.ShapeDtypeStruct((b, MLA_HEADS, s, MLA_QK_PAD), BF16),
                   jax.ShapeDtypeStruct((b, MLA_HEADS, s, 2 * MLA_V), BF16)],
        compiler_params=_params("parallel", "parallel"),
        name="mla_prep",
    )(u_mla, tab, wq.astype(BF16), wkv.astype(BF16), g_cq.reshape(1, -1), g_ckv.reshape(1, -1),
      gq[:MLA_NOPE].reshape(1, -1), rope_gain(gq), gk[:MLA_NOPE].reshape(1, -1), rope_gain(gk))


def _mla_attn_kernel(qi_ref, ki_ref, q_ref, k_ref, v_ref, o_ref, m_ref, acc_ref, *, tq, tk):
    step = pl.program_id(2)
    qi = qi_ref[step]
    ki = ki_ref[step]

    @pl.when(ki == 0)
    def _():
        m_ref[...] = jnp.full_like(m_ref, NEG_INF)
        acc_ref[...] = jnp.zeros_like(acc_ref)

    def update(rel):
        for r0 in range(0, tq, MLA_ROW_GROUP):
            if rel is not None and rel > r0 + MLA_ROW_GROUP - 1:
                continue
            rows = pl.ds(r0, MLA_ROW_GROUP)
            s = lax.dot_general(q_ref[0, 0, rows], k_ref[0, 0], NT_DIMS, preferred_element_type=F32)
            if rel is not None and rel + tk - 1 > r0:
                row = r0 + lax.broadcasted_iota(I32, s.shape, 0)
                col = rel + lax.broadcasted_iota(I32, s.shape, 1)
                s = jnp.where(col <= row, s, NEG_INF)
            m_prev = m_ref[rows]
            m_next = jnp.maximum(m_prev, jnp.max(s, axis=-1, keepdims=True))
            alpha = jnp.exp2(m_prev - m_next)
            p = jnp.exp2(s - m_next[:, :1]).astype(BF16)
            pv = jnp.dot(p, v_ref[0, 0], preferred_element_type=F32)
            acc_ref[rows, :MLA_V] = alpha * acc_ref[rows, :MLA_V] + pv[:, :MLA_V]
            acc_ref[rows, MLA_V:] = alpha * acc_ref[rows, MLA_V:] + pv[:, MLA_V:]
            m_ref[rows] = m_next

    key_offset = ki * tk - qi * tq
    pl.when(key_offset < 0)(lambda: update(None))
    for rel in range(0, tq, tk):
        pl.when(key_offset == rel)(functools.partial(update, rel))

    @pl.when((ki + 1) * tk == (qi + 1) * tq)
    def _():
        o_ref[0] = (acc_ref[:, :MLA_V] / acc_ref[:, MLA_V:]).astype(o_ref.dtype)


def _mla_attention(q, k, v, *, tq, tk):
    b, h, s, _ = q.shape
    pairs = [(qi, ki) for qi in range(s // tq) for ki in range((qi + 1) * tq // tk)]
    qi_of = jnp.array([p[0] for p in pairs], I32)
    ki_of = jnp.array([p[1] for p in pairs], I32)
    kv_idx = lambda bi, hi, t, qi_ref, ki_ref: (bi, hi, ki_ref[t], 0)
    return pl.pallas_call(
        functools.partial(_mla_attn_kernel, tq=tq, tk=tk),
        grid_spec=pltpu.PrefetchScalarGridSpec(
            num_scalar_prefetch=2,
            grid=(b, h, len(pairs)),
            in_specs=[pl.BlockSpec((1, 1, tq, MLA_QK_PAD), lambda bi, hi, t, qi_ref, ki_ref: (bi, hi, qi_ref[t], 0)),
                      pl.BlockSpec((1, 1, tk, MLA_QK_PAD), kv_idx),
                      pl.BlockSpec((1, 1, tk, 2 * MLA_V), kv_idx)],
            out_specs=pl.BlockSpec((1, tq, MLA_V), lambda bi, hi, t, qi_ref, ki_ref: (bi, qi_ref[t], hi)),
            scratch_shapes=[pltpu.VMEM((tq, MLA_V), F32), pltpu.VMEM((tq, 2 * MLA_V), F32)],
        ),
        out_shape=jax.ShapeDtypeStruct((b, s, h * MLA_V), BF16),
        compiler_params=_params("parallel", "parallel", "arbitrary"),
        name="mla_attention",
    )(qi_of, ki_of, q, k, v)


def _mem_attn_body(x, kv, g_ref, wq_ref, wo_ref, gq_ref, gk_ref):
    hn = _rms(x, g_ref[...]).astype(BF16)
    qf = jnp.dot(hn, wq_ref[...], preferred_element_type=F32)
    outs = []
    for h in range(MEM_HEADS):
        lo = h * MEM_HEAD_DIM
        qh = (_rms(qf[:, lo:lo + MEM_HEAD_DIM], gq_ref[...]) * (MEM_HEAD_DIM ** -0.5)).astype(BF16)
        kh = _rms(kv[:, lo:lo + MEM_HEAD_DIM], gk_ref[...]).astype(BF16)
        vh = kv[:, MEM_WIDTH + lo:MEM_WIDTH + lo + MEM_HEAD_DIM].astype(BF16)
        s = lax.dot_general(qh, kh, NT_DIMS, preferred_element_type=F32)
        e = jnp.exp(s - jnp.max(s, axis=-1, keepdims=True))
        p = (e / jnp.sum(e, axis=-1, keepdims=True)).astype(BF16)
        outs.append(jnp.dot(p, vh, preferred_element_type=F32))
    o = jnp.concatenate(outs, axis=-1).astype(BF16)
    return x + jnp.dot(o, wo_ref[...], preferred_element_type=F32)


def _split_bf16(a):
    hi = a.astype(BF16)
    return hi, (a - hi.astype(F32)).astype(BF16)


def _router_body(x, g_ref, whi_ref, wlo_ref, b_ref, ids_ref, gates_ref, cnt_ref, carry_ref):
    hn = _rms(x, g_ref[...])
    hi, lo = _split_bf16(hn)
    logits = (jnp.dot(hi, whi_ref[...], preferred_element_type=F32)
              + jnp.dot(hi, wlo_ref[...], preferred_element_type=F32)
              + jnp.dot(lo, whi_ref[...], preferred_element_type=F32)) + b_ref[...]
    lane = lax.broadcasted_iota(I32, logits.shape, 1)
    gl = jnp.where(lane < N_GROUPS, logits, NEG_INF)
    gmax = jnp.max(gl, axis=-1, keepdims=True)
    p_grp = 1.0 / jnp.sum(jnp.exp(gl - gmax), axis=-1, keepdims=True)
    grp = jnp.min(jnp.where(gl == gmax, lane, LANES), axis=-1, keepdims=True)
    in_grp = (lane >= N_GROUPS) & (lane < N_GROUPS + N_EXPERTS) & (((lane - N_GROUPS) // EXPERTS_PER_GROUP) == grp)
    el = jnp.where(in_grp, logits, NEG_INF)
    m1 = jnp.max(el, axis=-1, keepdims=True)
    i1 = jnp.min(jnp.where(el == m1, lane, LANES), axis=-1, keepdims=True)
    el2 = jnp.where(lane == i1, NEG_INF, el)
    m2 = jnp.max(el2, axis=-1, keepdims=True)
    i2 = jnp.min(jnp.where(el2 == m2, lane, LANES), axis=-1, keepdims=True)
    r = jnp.exp(m2 - m1)
    g1 = p_grp / (1.0 + r)
    gates_ref[...] = jnp.where(lane == 0, g1, jnp.where(lane == 1, g1 * r, 0.0))

    tm = logits.shape[0]
    used = jnp.where((lane == i1) | (lane == i2), 1.0, 0.0)
    earlier = lax.broadcasted_iota(I32, (tm, tm), 1) < lax.broadcasted_iota(I32, (tm, tm), 0)
    before = carry_ref[...] + jnp.dot(jnp.where(earlier, 1.0, 0.0).astype(BF16), used.astype(BF16),
                                      preferred_element_type=F32)
    r1 = jnp.sum(jnp.where(lane == i1, before, 0.0), axis=-1, keepdims=True).astype(I32)
    r2 = jnp.sum(jnp.where(lane == i2, before, 0.0), axis=-1, keepdims=True).astype(I32)
    carry_ref[...] += jnp.sum(used, axis=0, keepdims=True)
    cnt_ref[...] = carry_ref[...].astype(I32)
    ids_ref[...] = jnp.where(lane == 0, i1 - N_GROUPS, jnp.where(lane == 1, i2 - N_GROUPS,
                             jnp.where(lane == 2, r1, jnp.where(lane == 3, r2, 0))))


def _post_mixer_kernel(x_ref, a_ref, b_ref, c_ref, wa_ref, wb_ref, wc_ref,
                       kv_ref, gmq_ref, wmq_ref, wmo_ref, mgq_ref, mgk_ref,
                       gffn_ref, whi_ref, wlo_ref, bias_ref,
                       o_ref, ids_ref, gates_ref, cnt_ref, carry_ref):
    @pl.when((pl.program_id(0) == 0) & (pl.program_id(1) == 0))
    def _():
        carry_ref[...] = jnp.zeros_like(carry_ref)

    mix = jnp.dot(a_ref[0], wa_ref[...], preferred_element_type=F32)
    mix += jnp.dot(b_ref[0], wb_ref[...], preferred_element_type=F32)
    mix += jnp.dot(c_ref[0], wc_ref[...], preferred_element_type=F32)
    x = x_ref[0] + mix
    x = _mem_attn_body(x, kv_ref[0], gmq_ref, wmq_ref, wmo_ref, mgq_ref, mgk_ref)
    o_ref[0] = x
    _router_body(x, gffn_ref, whi_ref, wlo_ref, bias_ref, ids_ref, gates_ref, cnt_ref, carry_ref)


def _post_mixer(x, a, bmix, c, w_out, kv, g_mem_q, w_mq, w_mo, mem_gq, mem_gk,
                g_ffn, w_gr, b_gr, w_er, b_er, *, tm):
    b, s, d = x.shape
    m = kv.shape[1]
    nst = s // tm
    wa = w_out[:SWA_WIDTH].astype(BF16)
    wb = w_out[SWA_WIDTH:SWA_WIDTH + HG_WIDTH].astype(BF16)
    wc = w_out[SWA_WIDTH + HG_WIDTH:].astype(BF16)
    pad = LANES - N_GROUPS - N_EXPERTS
    wr = jnp.concatenate([w_gr, w_er, jnp.zeros((d, pad), F32)], axis=1)
    bias = jnp.concatenate([b_gr, b_er, jnp.zeros((pad,), F32)]).reshape(1, LANES)
    whi, wlo = _split_bf16(wr)
    tile = lambda width: pl.BlockSpec((1, tm, width), lambda i, j: (i, j, 0))
    const = lambda shape: pl.BlockSpec(shape, lambda i, j: (0,) * len(shape), pipeline_mode=pl.Buffered(1))
    flat = lambda width: pl.BlockSpec((tm, width), lambda i, j: (i * nst + j, 0))
    return pl.pallas_call(
        _post_mixer_kernel,
        grid=(b, nst),
        in_specs=[tile(d), tile(SWA_WIDTH), tile(HG_WIDTH), tile(MLA_WIDTH),
                  const(wa.shape), const(wb.shape), const(wc.shape),
                  pl.BlockSpec((1, m, 2 * MEM_WIDTH), lambda i, j: (i, 0, 0)),
                  const((1, d)), const((d, MEM_WIDTH)), const((MEM_WIDTH, d)),
                  const((1, MEM_HEAD_DIM)), const((1, MEM_HEAD_DIM)),
                  const((1, d)), const((d, LANES)), const((d, LANES)), const((1, LANES))],
        out_specs=[tile(d), flat(LANES), flat(LANES), pl.BlockSpec((1, LANES), lambda i, j: (0, 0))],
        out_shape=[jax.ShapeDtypeStruct((b, s, d), F32), jax.ShapeDtypeStruct((b * s, LANES), I32),
                   jax.ShapeDtypeStruct((b * s, LANES), F32), jax.ShapeDtypeStruct((1, LANES), I32)],
        scratch_shapes=[pltpu.VMEM((1, LANES), F32)],
        compiler_params=_params("arbitrary", "arbitrary"),
        name="post_mixer",
    )(x, a, bmix, c, wa, wb, wc, kv, g_mem_q.reshape(1, d), w_mq.astype(BF16), w_mo.astype(BF16),
      mem_gq.reshape(1, -1), mem_gk.reshape(1, -1), g_ffn.reshape(1, d), whi, wlo, bias)


HI_HALF_MASK = 0xFFFF0000


def _dispatch_kernel(dest_ref, x_ref, g_ref, xs_in_hbm, xs_hbm, buf, sem, *, ts):
    del xs_in_hbm
    i = pl.program_id(0)
    last = pl.num_programs(0) - 1
    slot = i % 2

    def row_copy(slot_, r, dst_row):
        return pltpu.make_async_copy(buf.at[slot_, pl.ds(r, 1)], xs_hbm.at[pl.ds(dst_row, 1)], sem.at[slot_])

    def wait_step(slot_):
        for _ in range(ts * TOP_K):
            row_copy(slot_, 0, 0).wait()

    @pl.when(i >= 2)
    def _():
        wait_step(slot)

    hn = _rms(x_ref[...], g_ref[...])
    half = hn.shape[1] // 2
    lo = pltpu.bitcast(hn[:, :half].astype(BF16).astype(F32), jnp.uint32) >> 16
    hi = pltpu.bitcast(hn[:, half:].astype(BF16).astype(F32), jnp.uint32) & jnp.uint32(HI_HALF_MASK)
    buf[slot] = lo | hi
    for r in range(ts):
        for kk in range(TOP_K):
            row_copy(slot, r, dest_ref[(i * ts + r) * TOP_K + kk]).start(priority=kk % DMA_PRIORITIES)

    @pl.when(i == last)
    def _():
        wait_step(slot)

    @pl.when((i == last) & (i >= 1))
    def _():
        wait_step(1 - slot)


def _dispatch(x, g_ffn, dest, n_rows, *, ts):
    n, d = x.shape
    xs0 = jnp.zeros((n_rows, d // 2), jnp.uint32)
    return pl.pallas_call(
        functools.partial(_dispatch_kernel, ts=ts),
        grid_spec=pltpu.PrefetchScalarGridSpec(
            num_scalar_prefetch=1,
            grid=(n // ts,),
            in_specs=[pl.BlockSpec((ts, d), lambda i, *_: (i, 0)),
                      pl.BlockSpec((1, d), lambda i, *_: (0, 0)),
                      pl.BlockSpec(memory_space=pl.ANY)],
            out_specs=pl.BlockSpec(memory_space=pl.ANY),
            scratch_shapes=[pltpu.VMEM((2, ts, d // 2), jnp.uint32), pltpu.SemaphoreType.DMA((2,))],
        ),
        out_shape=jax.ShapeDtypeStruct((n_rows, d // 2), jnp.uint32),
        input_output_aliases={3: 0},
        compiler_params=_params("arbitrary"),
        name="moe_dispatch",
    )(dest, x, g_ffn.reshape(1, d), xs0)


def _expert_kernel(bexp_ref, next_ref, nblk_ref, xs_ref, wg_hbm, wu_hbm, wd_hbm, y_ref,
                   wg_f32, wu_f32, wd_f32, sem, wg_bf, wu_bf, wd_bf, *, layer):
    i = pl.program_id(0)
    n_used = nblk_ref[0]
    expert = bexp_ref[i]
    first_of_run = (i == 0) | (expert != bexp_ref[jnp.maximum(i - 1, 0)])
    slot = next_ref[2 * i + 1]

    def weight_copies(e, slot_):
        return [pltpu.make_async_copy(hbm.at[layer, e], buf.at[slot_], sem.at[slot_, j])
                for j, (hbm, buf) in enumerate(((wg_hbm, wg_f32), (wu_hbm, wu_f32), (wd_hbm, wd_f32)))]

    @pl.when(i == 0)
    def _():
        for cp in weight_copies(expert, 0):
            cp.start()

    @pl.when(first_of_run & (i < n_used))
    def _():
        for cp in weight_copies(expert, slot):
            cp.wait()
        next_expert = next_ref[2 * i]

        @pl.when(next_expert >= 0)
        def _():
            for cp in weight_copies(next_expert, 1 - slot):
                cp.start()

        wg_bf[...] = wg_f32[slot].astype(BF16)
        wu_bf[...] = wu_f32[slot].astype(BF16)
        wd_bf[...] = wd_f32[slot].astype(BF16)

    @pl.when(i < n_used)
    def _():
        w = xs_ref[...]
        half = w.shape[1]
        lo = pltpu.bitcast(w << 16, F32).astype(BF16)
        hi = pltpu.bitcast(w & jnp.uint32(HI_HALF_MASK), F32).astype(BF16)
        gate = (jnp.dot(lo, wg_bf[:half], preferred_element_type=F32)
                + jnp.dot(hi, wg_bf[half:], preferred_element_type=F32))
        up = (jnp.dot(lo, wu_bf[:half], preferred_element_type=F32)
              + jnp.dot(hi, wu_bf[half:], preferred_element_type=F32))
        act = (gate / (1.0 + jnp.exp(-gate)) * up).astype(BF16)
        y_ref[...] = jnp.dot(act, wd_bf[...], preferred_element_type=F32)

    @pl.when(i >= nblk_ref[0])
    def _():
        y_ref[...] = jnp.zeros_like(y_ref)


def _experts(xs, block_expert, run_next, n_used, w_gate, w_up, w_down, layer, *, rows):
    d = 2 * xs.shape[1]
    n_blocks = block_expert.shape[0]
    hbm = pl.BlockSpec(memory_space=pl.ANY)
    return pl.pallas_call(
        functools.partial(_expert_kernel, layer=layer),
        grid_spec=pltpu.PrefetchScalarGridSpec(
            num_scalar_prefetch=3,
            grid=(n_blocks,),
            in_specs=[pl.BlockSpec((rows, d // 2), lambda i, *_: (i, 0)), hbm, hbm, hbm],
            out_specs=pl.BlockSpec((rows, d), lambda i, *_: (i, 0)),
            scratch_shapes=[pltpu.VMEM((2, d, D_EXPERT), F32),
                            pltpu.VMEM((2, d, D_EXPERT), F32),
                            pltpu.VMEM((2, D_EXPERT, d), F32),
                            pltpu.SemaphoreType.DMA((2, 3)),
                            pltpu.VMEM((d, D_EXPERT), BF16),
                            pltpu.VMEM((d, D_EXPERT), BF16),
                            pltpu.VMEM((D_EXPERT, d), BF16)],
        ),
        out_shape=jax.ShapeDtypeStruct((n_blocks * rows, d), F32),
        compiler_params=_params("arbitrary"),
        name="moe_experts",
    )(block_expert, run_next, n_used, xs, w_gate, w_up, w_down)


def _combine_kernel(pos_ref, x_ref, gates_ref, y_hbm, o_ref, ybuf, sem, *, tt):
    i = pl.program_id(0)
    nsteps = pl.num_programs(0)

    def row_copy(src_row, slot, dst_row):
        return pltpu.make_async_copy(y_hbm.at[pl.ds(src_row, 1)], ybuf.at[slot, pl.ds(dst_row, 1)], sem.at[slot])

    def start_gather(step, slot):
        for r in range(tt):
            for kk in range(TOP_K):
                row_copy(pos_ref[(step * tt + r) * TOP_K + kk], slot, kk * tt + r).start(priority=kk % DMA_PRIORITIES)

    def wait_gather(slot):
        for r in range(TOP_K * tt):
            row_copy(0, slot, r).wait()

    @pl.when(i == 0)
    def _():
        start_gather(0, 0)

    @pl.when(i + 1 < nsteps)
    def _():
        start_gather(i + 1, (i + 1) % 2)

    wait_gather(i % 2)
    yb = ybuf[i % 2]
    g = gates_ref[...]
    o_ref[...] = x_ref[...] + g[:, 0:1] * yb[:tt] + g[:, 1:2] * yb[tt:]


def _combine(x, gates, pos, y, *, tt):
    n, d = x.shape
    return pl.pallas_call(
        functools.partial(_combine_kernel, tt=tt),
        grid_spec=pltpu.PrefetchScalarGridSpec(
            num_scalar_prefetch=1,
            grid=(n // tt,),
            in_specs=[pl.BlockSpec((tt, d), lambda i, *_: (i, 0)),
                      pl.BlockSpec((tt, LANES), lambda i, *_: (i, 0)),
                      pl.BlockSpec(memory_space=pl.ANY)],
            out_specs=pl.BlockSpec((tt, d), lambda i, *_: (i, 0)),
            scratch_shapes=[pltpu.VMEM((2, TOP_K * tt, d), F32), pltpu.SemaphoreType.DMA((2,))],
        ),
        out_shape=jax.ShapeDtypeStruct((n, d), F32),
        compiler_params=_params("arbitrary"),
        name="moe_combine",
    )(pos, x, gates, y)


def _moe_plan(ids, cnt, *, rows):
    n = ids.shape[0]
    expert, rank = ids[:, :TOP_K], ids[:, TOP_K:2 * TOP_K]
    counts = cnt[0, N_GROUPS:N_GROUPS + N_EXPERTS]
    padded = (counts + rows - 1) // rows * rows
    padded_end = jnp.cumsum(padded)
    seg_start = padded_end - padded
    onehot = expert[..., None] == jnp.arange(N_EXPERTS, dtype=I32)
    dest = (jnp.sum(jnp.where(onehot, seg_start, 0), axis=-1) + rank).astype(I32).reshape(n * TOP_K)
    n_blocks = -(-(n * TOP_K) // rows) + N_EXPERTS
    block_start = jnp.arange(n_blocks, dtype=I32) * rows
    block_expert = jnp.minimum(jnp.sum(padded_end[None, :] <= block_start[:, None], axis=1), N_EXPERTS - 1).astype(I32)
    n_used = (padded_end[-1] // rows).astype(I32)
    block_onehot = block_expert[:, None] == jnp.arange(N_EXPERTS, dtype=I32)
    pick = lambda per_expert: jnp.sum(jnp.where(block_onehot, per_expert, 0), axis=-1)
    next_block = pick(padded_end // rows)
    next_onehot = jnp.minimum(next_block, n_blocks - 1)[:, None] == jnp.arange(n_blocks, dtype=I32)
    next_expert = jnp.where(next_block < n_used, jnp.sum(jnp.where(next_onehot, block_expert, 0), axis=-1), -1)
    nonempty = (counts > 0).astype(I32)
    run_parity = pick(jnp.cumsum(nonempty) - nonempty) % 2
    run_next = jnp.stack([next_expert, run_parity], axis=-1).astype(I32).reshape(2 * n_blocks)
    return dest, block_expert, run_next, n_used.reshape(1)


def _band_relative_bias(table):
    def bucket(nd):
        max_exact = REL_BUCKETS // 2
        nf = jnp.maximum(nd, 1).astype(F32)
        large = max_exact + (jnp.log(nf / max_exact) / math.log(REL_MAX_DIST / max_exact)
                             * (REL_BUCKETS - max_exact)).astype(I32)
        return jnp.where(nd < max_exact, nd, jnp.minimum(large, REL_BUCKETS - 1))

    qi = jnp.arange(WINDOW)[:, None]
    kj = jnp.arange(2 * WINDOW)[None, :]
    dist = jnp.maximum(qi + WINDOW - kj, 0)
    onehot = (bucket(dist)[..., None] == jnp.arange(REL_BUCKETS)).astype(F32)
    return jnp.einsum('qkb,bh->hqk', onehot, table.astype(F32), precision=lax.Precision.HIGHEST)


def _hgrn_params(lb, g_out):
    lb = lb.reshape(HG_HEADS, 1, HG_DK)
    gain = jnp.broadcast_to(g_out.reshape(1, 1, HG_DV), (HG_HEADS, 1, HG_DV))
    rows = [jnp.log(lb), jnp.log1p(-lb), 1.0 - lb, gain, jnp.zeros((HG_HEADS, 4, HG_DK), F32)]
    return jnp.concatenate(rows, axis=1).astype(F32)


def kernel(x, mem, rel_bias_table, hg_lb_logits, g_mix, w_in, swa_gq, swa_gk, swa_sinks, hg_g_out, mla_g_cq, mla_g_ckv, mla_w_uq, mla_w_ukv, mla_gq, mla_gk, w_out, g_mem_q, g_mem_kv, w_mq, w_mkv, mem_gq, mem_gk, w_mo, g_ffn, w_group_router, b_group_router, w_expert_router, b_expert_router, w_gate, w_up, w_down):
    b, s, d = x.shape
    n = b * s
    m = mem.shape[1]
    rel_bias = _band_relative_bias(rel_bias_table)
    lb_all = jnp.cumsum(jax.nn.softmax(hg_lb_logits.astype(F32), axis=0), axis=0)
    lb_all = lb_all - lb_all[:1]
    offs = [0]
    for width in IN_SIZES:
        offs.append(offs[-1] + width)
    o_hg, o_cq, o_kr = offs[3], offs[7], offs[9]

    xf = x.reshape(n, d)
    mem2 = mem.reshape(b * m, d)
    for l in range(DEPTH):
        w = w_in[l]
        kr_cols = w[:, o_kr:]
        w_swa = w[:, :o_hg].astype(BF16)
        w_hg = w[:, o_hg:o_cq].astype(BF16)
        w_mla = jnp.concatenate([w[:, o_cq:], _swap_halves(kr_cols)], axis=1).astype(BF16)
        u_swa = _rms_matmul(xf, g_mix[l], w_swa, tm=1024, tn=w_swa.shape[1], out_dtype=BF16).reshape(b, s, -1)
        u_hg = _rms_matmul(xf, g_mix[l], w_hg, tm=1024, tn=1024, out_dtype=BF16).reshape(b, s, -1)
        u_mla = _rms_matmul(xf, g_mix[l], w_mla, tm=1024, tn=w_mla.shape[1], out_dtype=BF16).reshape(b, s, -1)

        out_a = _swa(u_swa, rel_bias, swa_gq[l], swa_gk[l], swa_sinks[l])
        out_b = _hgrn(u_hg, _hgrn_params(lb_all[l], hg_g_out[l]), chunk=HG_CHUNK, n_sub=HG_SUB)
        qm, km, vm = _mla_prep(u_mla, mla_w_uq[l], mla_w_ukv[l], mla_g_cq[l], mla_g_ckv[l],
                               mla_gq[l], mla_gk[l], tm=512)
        out_c = _mla_attention(qm, km, vm, tq=MLA_TQ, tk=MLA_TK)

        kv = _rms_matmul(mem2, g_mem_kv[l], w_mkv[l].astype(BF16), tm=b * m, tn=2 * MEM_WIDTH)
        x3, ids, gates, cnt = _post_mixer(
            xf.reshape(b, s, d), out_a, out_b, out_c, w_out[l], kv.reshape(b, m, -1),
            g_mem_q[l], w_mq[l], w_mo[l], mem_gq[l], mem_gk[l],
            g_ffn[l], w_group_router[l], b_group_router[l], w_expert_router[l], b_expert_router[l], tm=512)
        xf = x3.reshape(n, d)
        dest, block_expert, run_next, n_used = _moe_plan(ids, cnt, rows=MOE_ROWS)
        xs = _dispatch(xf, g_ffn[l], dest, block_expert.shape[0] * MOE_ROWS, ts=64)
        y = _experts(xs, block_expert, run_next, n_used, w_gate, w_up, w_down, l, rows=MOE_ROWS)
        xf = _combine(xf, gates, dest, y, tt=64)
    return xf.reshape(b, s, d)
```

```python
import functools
import math

import jax
import jax.numpy as jnp
from jax import lax
from jax.experimental import pallas as pl
from jax.experimental.pallas import tpu as pltpu

F32 = jnp.float32
BF16 = jnp.bfloat16
I32 = jnp.int32

D_MODEL = 2048
DEPTH = 2
SWA_HEADS = 8
SWA_KV_HEADS = 2
SWA_HEAD_DIM = 64
WINDOW = 128
HG_HEADS = 8
HG_DK = 128
HG_DV = 128
MLA_HEADS = 4
MLA_Q_RANK = 512
MLA_KV_RANK = 256
MLA_NOPE = 128
MLA_ROPE = 64
MLA_QK = MLA_NOPE + MLA_ROPE
MLA_V = 128
ROPE_THETA = 10000.0
REL_BUCKETS = 32
REL_MAX_DIST = 128
MEM_HEADS = 4
MEM_HEAD_DIM = 128
MEM_WIDTH = MEM_HEADS * MEM_HEAD_DIM
N_GROUPS = 8
EXPERTS_PER_GROUP = 8
N_EXPERTS = N_GROUPS * EXPERTS_PER_GROUP
TOP_K = 2
D_EXPERT = 512
MOE_ROWS = 256
EPS = 1e-6
NEG_INF = -1e30
LOG2_E = math.log2(math.e)

SWA_WIDTH = SWA_HEADS * SWA_HEAD_DIM
SWA_KV_WIDTH = SWA_KV_HEADS * SWA_HEAD_DIM
HG_WIDTH = HG_HEADS * HG_DV
MLA_WIDTH = MLA_HEADS * MLA_V
IN_SIZES = (SWA_WIDTH, SWA_KV_WIDTH, SWA_KV_WIDTH,
            HG_HEADS * HG_DK, HG_HEADS * HG_DK, HG_WIDTH, HG_WIDTH,
            MLA_Q_RANK, MLA_KV_RANK, MLA_ROPE)

LANES = 128
SUBLANES = 8
DMA_PRIORITIES = 2
MLA_QK_PAD = 2 * LANES
VMEM_LIMIT_BYTES = 56 * 1024 * 1024

HG_CHUNK = 128
HG_SUB = 8
MLA_TQ = 1024
MLA_TK = 1024
MLA_KEY_GROUP = 512
MLA_ROW_GROUP = 256
NT_DIMS = (((1,), (1,)), ((), ()))
TN_DIMS = (((0,), (0,)), ((), ()))


def _params(*semantics):
    return pltpu.CompilerParams(dimension_semantics=semantics, vmem_limit_bytes=VMEM_LIMIT_BYTES)


def _rms(x, gain=None):
    y = x * lax.rsqrt(jnp.mean(x * x, axis=-1, keepdims=True) + EPS)
    return y if gain is None else y * gain


def _rms_matmul_kernel(x_ref, g_ref, w_ref, o_ref, hn_ref):
    @pl.when(pl.program_id(1) == 0)
    def _():
        hn_ref[...] = _rms(x_ref[...], g_ref[...]).astype(BF16)

    o_ref[...] = jnp.dot(hn_ref[...], w_ref[...], preferred_element_type=F32).astype(o_ref.dtype)


def _rms_matmul(x, gain, w, *, tm, tn, out_dtype=F32):
    n, d = x.shape
    nout = w.shape[1]
    return pl.pallas_call(
        _rms_matmul_kernel,
        grid=(n // tm, nout // tn),
        in_specs=[pl.BlockSpec((tm, d), lambda i, j: (i, 0)),
                  pl.BlockSpec((1, d), lambda i, j: (0, 0)),
                  pl.BlockSpec((d, tn), lambda i, j: (0, j))],
        out_specs=pl.BlockSpec((tm, tn), lambda i, j: (i, j)),
        out_shape=jax.ShapeDtypeStruct((n, nout), out_dtype),
        scratch_shapes=[pltpu.VMEM((tm, d), BF16)],
        compiler_params=_params("parallel", "arbitrary"),
        name="rms_matmul",
    )(x, gain.reshape(1, d), w)


def _head_rms(x, seg_ref, gain_ref):
    sq = x * x
    hi, lo = _split_bf16(sq)
    ss = (jnp.dot(hi, seg_ref[...], preferred_element_type=F32)
          + jnp.dot(lo, seg_ref[...], preferred_element_type=F32))
    return x * lax.rsqrt(ss * (1.0 / SWA_HEAD_DIM) + EPS) * gain_ref[...]


def _swa_kernel(sink_ref, q_ref, kp_ref, kc_ref, vp_ref, vc_ref, bias_ref, gq_ref, gk_ref, segq_ref, segk_ref, o_ref):
    blk = pl.program_id(1)
    grp = SWA_HEADS // SWA_KV_HEADS
    q = _head_rms(q_ref[0].astype(F32), segq_ref, gq_ref)
    k = _head_rms(jnp.concatenate([kp_ref[0], kc_ref[0]], axis=0).astype(F32), segk_ref, gk_ref)
    v = jnp.concatenate([vp_ref[0], vc_ref[0]], axis=0)
    qi = lax.broadcasted_iota(I32, (WINDOW, 2 * WINDOW), 0)
    kj = lax.broadcasted_iota(I32, (WINDOW, 2 * WINDOW), 1)
    dist = qi + WINDOW - kj
    has_prev = jnp.where(blk > 0, 0, WINDOW)
    valid = (dist >= 0) & (dist < WINDOW) & (kj >= has_prev)
    outs = []
    for g in range(SWA_KV_HEADS):
        lo = g * SWA_HEAD_DIM
        kg = k[:, lo:lo + SWA_HEAD_DIM].astype(BF16)
        vg = v[:, lo:lo + SWA_HEAD_DIM].astype(BF16)
        for h in range(g * grp, (g + 1) * grp):
            qh = q[:, h * SWA_HEAD_DIM:(h + 1) * SWA_HEAD_DIM].astype(BF16)
            s = lax.dot_general(qh, kg, NT_DIMS, preferred_element_type=F32) + bias_ref[h]
            s = jnp.where(valid, s, NEG_INF)
            sink = sink_ref[h]
            m = jnp.maximum(jnp.max(s, axis=-1, keepdims=True), sink)
            e = jnp.exp(s - m)
            denom = jnp.sum(e, axis=-1, keepdims=True) + jnp.exp(sink - m)
            outs.append(jnp.dot((e / denom).astype(BF16), vg, preferred_element_type=F32))
    o_ref[0] = jnp.concatenate(outs, axis=-1).astype(o_ref.dtype)


def _swa(u_swa, rel_bias, gq, gk, sinks):
    b, s, _ = u_swa.shape
    nb = s // WINDOW
    kcol = SWA_WIDTH // LANES
    vcol = kcol + 1
    prev = lambda bi, n, *_: (bi, jnp.maximum(n - 1, 0))
    grp = SWA_HEADS // SWA_KV_HEADS

    def head_blocks(width):
        head = jnp.arange(width) // SWA_HEAD_DIM
        return (head[:, None] == head[None, :]).astype(BF16)

    return pl.pallas_call(
        _swa_kernel,
        grid_spec=pltpu.PrefetchScalarGridSpec(
            num_scalar_prefetch=1,
            grid=(b, nb),
            in_specs=[
                pl.BlockSpec((1, WINDOW, SWA_WIDTH), lambda bi, n, *_: (bi, n, 0)),
                pl.BlockSpec((1, WINDOW, LANES), lambda bi, n, *_: prev(bi, n) + (kcol,)),
                pl.BlockSpec((1, WINDOW, LANES), lambda bi, n, *_: (bi, n, kcol)),
                pl.BlockSpec((1, WINDOW, LANES), lambda bi, n, *_: prev(bi, n) + (vcol,)),
                pl.BlockSpec((1, WINDOW, LANES), lambda bi, n, *_: (bi, n, vcol)),
                pl.BlockSpec((SWA_HEADS, WINDOW, 2 * WINDOW), lambda bi, n, *_: (0, 0, 0)),
                pl.BlockSpec((1, SWA_WIDTH), lambda bi, n, *_: (0, 0)),
                pl.BlockSpec((1, SWA_KV_WIDTH), lambda bi, n, *_: (0, 0)),
                pl.BlockSpec((SWA_WIDTH, SWA_WIDTH), lambda bi, n, *_: (0, 0)),
                pl.BlockSpec((SWA_KV_WIDTH, SWA_KV_WIDTH), lambda bi, n, *_: (0, 0)),
            ],
            out_specs=pl.BlockSpec((1, WINDOW, SWA_WIDTH), lambda bi, n, *_: (bi, n, 0)),
        ),
        out_shape=jax.ShapeDtypeStruct((b, s, SWA_WIDTH), BF16),
        compiler_params=_params("parallel", "parallel"),
        name="swa_attention",
    )(sinks, u_swa, u_swa, u_swa, u_swa, u_swa, rel_bias,
      jnp.tile(gq * (SWA_HEAD_DIM ** -0.5), SWA_HEADS).reshape(1, SWA_WIDTH),
      jnp.tile(gk, SWA_KV_HEADS).reshape(1, SWA_KV_WIDTH), head_blocks(SWA_WIDTH), head_blocks(SWA_KV_WIDTH))


def _hgrn_level_matrix(c):
    t = jnp.arange(c)[:, None]
    r = jnp.arange(c)[None, :]
    mats = [(r <= t)]
    half = c // 2
    while half >= 1:
        mid = (t // (2 * half)) * (2 * half) + half
        is_q = (t & half) != 0
        if half < SUBLANES:
            mats.append(jnp.where(is_q, (r >= mid) & (r <= t), (r > t) & (r < mid)))
        half //= 2
    return jnp.concatenate(mats, axis=0).astype(BF16)


def _hgrn_pair_level(c):
    t = jnp.arange(c, dtype=I32)[:, None]
    s = jnp.arange(c, dtype=I32)[None, :]
    x = t ^ s
    lvl = jnp.zeros((c, c), I32)
    half = c // 2
    while half >= 1:
        lvl = jnp.where((x & (-half)) == half, half, lvl)
        half //= 2
    return jnp.where(t > s, lvl, 0)


def _hgrn_kernel(q_ref, f_ref, v_ref, gate_ref, par_ref, amat_ref, lvl_ref, o_ref, st_ref, *, chunk, n_sub):
    @pl.when(pl.program_id(2) == 0)
    def _():
        st_ref[...] = jnp.zeros_like(st_ref)

    c = chunk
    par = par_ref[0]
    log_lb, log1m_lb, one_m_lb, g_out = par[0:1], par[1:2], par[2:3], par[3:4]
    row = lax.broadcasted_iota(I32, (c, 1), 0)
    pair_level = lvl_ref[...]
    st = st_ref[...]
    for sub in range(n_sub):
        rows = pl.ds(sub * c, c)
        q = q_ref[0, rows].astype(F32)
        f = f_ref[0, rows].astype(F32)
        v = v_ref[0, rows].astype(F32)
        gate = gate_ref[0, rows].astype(F32)

        e = jnp.exp(-jnp.abs(f))
        log_sig = jnp.minimum(f, 0.0) - jnp.log(1.0 + e)
        bb = log1m_lb + log_sig
        log_f = jnp.maximum(log_lb, bb) + jnp.log(1.0 + jnp.exp(-jnp.abs(log_lb - bb)))
        log_f = log_f * LOG2_E
        kk = one_m_lb * jnp.where(f < 0.0, 1.0, e) / (1.0 + e)

        g_hi = log_f.astype(BF16)
        g_lo = (log_f - g_hi.astype(F32)).astype(BF16)
        e2 = jnp.dot(amat_ref[...], jnp.concatenate([g_hi, g_lo], axis=1), preferred_element_type=F32)
        expo = e2[:, :HG_DK] + e2[:, HG_DK:]
        bcum = expo[0:c]
        btot = bcum[c - 1:c]

        attn = jnp.zeros((c, c), F32)
        half = c // 2
        lvl = 1
        while half >= 1:
            is_q = (row & half) != 0
            seg = 2 * half
            if half >= SUBLANES:
                ref = jnp.concatenate([jnp.broadcast_to(bcum[a + half - 1:a + half], (seg, HG_DK))
                                       for a in range(0, c, seg)], axis=0)
                diff = bcum - ref
                w = jnp.exp2(jnp.where(is_q, diff, -diff))
            else:
                w = jnp.exp2(expo[lvl * c:(lvl + 1) * c])
                lvl += 1
            zf = jnp.where(is_q, q, kk) * w
            z = zf.astype(BF16)
            if half >= SUBLANES:
                zq = jnp.concatenate([zf[a + half:a + seg] for a in range(0, c, seg)], axis=0).astype(BF16)
                aq = lax.dot_general(zq, z, NT_DIMS, preferred_element_type=F32)
                blank = jnp.zeros((half, c), F32)
                a = jnp.concatenate([blk for j in range(c // seg)
                                     for blk in (blank, aq[j * half:(j + 1) * half])], axis=0)
            else:
                a = lax.dot_general(z, z, NT_DIMS, preferred_element_type=F32)
            attn = jnp.where(pair_level == half, a, attn)
            half //= 2

        vb = v.astype(BF16)
        diag = jnp.sum(q * kk, axis=-1, keepdims=True)
        intra = jnp.dot(attn.astype(BF16), vb, preferred_element_type=F32) + diag * v
        qe = (q * jnp.exp2(bcum)).astype(BF16)
        inter = lax.dot_general(qe, st.astype(BF16), NT_DIMS, preferred_element_type=F32)
        kd = (kk * jnp.exp2(btot - bcum)).astype(BF16)
        st = st * jnp.exp2(btot) +lax.dot_general(vb, kd, TN_DIMS, preferred_element_type=F32)

        o = _rms(inter + intra, g_out)
        o_ref[0, rows] = (o * gate / (1.0 + jnp.exp(-gate))).astype(o_ref.dtype)
    st_ref[...] = st


def _hgrn(u_hg, par, *, chunk, n_sub):
    b, s, _ = u_hg.shape
    amat = _hgrn_level_matrix(chunk)
    pair_level = _hgrn_pair_level(chunk)
    step = chunk * n_sub
    col = lambda off: (lambda bi, h, c: (bi, c, off + h))
    return pl.pallas_call(
        functools.partial(_hgrn_kernel, chunk=chunk, n_sub=n_sub),
        grid=(b, HG_HEADS, s // step),
        in_specs=[pl.BlockSpec((1, step, HG_DK), col(0)),
                  pl.BlockSpec((1, step, HG_DK), col(HG_HEADS)),
                  pl.BlockSpec((1, step, HG_DV), col(2 * HG_HEADS)),
                  pl.BlockSpec((1, step, HG_DV), col(3 * HG_HEADS)),
                  pl.BlockSpec((1, 8, HG_DK), lambda bi, h, c: (h, 0, 0)),
                  pl.BlockSpec(amat.shape, lambda bi, h, c: (0, 0)),
                  pl.BlockSpec(pair_level.shape, lambda bi, h, c: (0, 0))],
        out_specs=pl.BlockSpec((1, step, HG_DV), lambda bi, h, c: (bi, c, h)),
        out_shape=jax.ShapeDtypeStruct((b, s, HG_WIDTH), BF16),
        scratch_shapes=[pltpu.VMEM((HG_DV, HG_DK), F32)],
        compiler_params=_params("parallel", "parallel", "arbitrary"),
        name="hgrn2",
    )(u_hg, u_hg, u_hg, u_hg, par, amat, pair_level)


def _mla_prep_kernel(u_ref, tab_ref, wq_ref, wkv_ref, gcq_ref, gckv_ref, gqn_ref, gqr_ref, gkn_ref, gkr_ref,
                     q_ref, k_ref, v_ref):
    u = u_ref[0].astype(F32)
    cq = _rms(u[:, :MLA_Q_RANK], gcq_ref[...]).astype(BF16)
    ckv = _rms(u[:, MLA_Q_RANK:MLA_Q_RANK + MLA_KV_RANK], gckv_ref[...]).astype(BF16)
    kr = u[:, MLA_Q_RANK + MLA_KV_RANK:]
    qf = jnp.dot(cq, wq_ref[...], preferred_element_type=F32)
    kvf = jnp.dot(ckv, wkv_ref[...], preferred_element_type=F32)
    tab = tab_ref[...]
    low = lax.broadcasted_iota(I32, kr.shape, 1) < MLA_ROPE
    kr_sq = jnp.sum(jnp.where(low, kr * kr, 0.0), axis=-1, keepdims=True)
    scale = MLA_QK ** -0.5 * LOG2_E
    ones = jnp.ones((u.shape[0], MLA_V), F32)
    for h in range(MLA_HEADS):
        qn = qf[:, h * MLA_QK_PAD:h * MLA_QK_PAD + MLA_NOPE]
        qr = qf[:, h * MLA_QK_PAD + MLA_NOPE:(h + 1) * MLA_QK_PAD]
        ss = jnp.sum(qn * qn, axis=-1, keepdims=True) + jnp.sum(jnp.where(low, qr * qr, 0.0), axis=-1, keepdims=True)
        rstd = lax.rsqrt(ss / MLA_QK + EPS) * scale
        t = qr * rstd * tab * gqr_ref[...]
        rope = jnp.where(low, t + pltpu.roll(t, MLA_ROPE, 1), 0.0)
        q_ref[0, h] = jnp.concatenate([qn * rstd * gqn_ref[...], rope], axis=-1).astype(q_ref.dtype)

        kn = kvf[:, h * MLA_NOPE:(h + 1) * MLA_NOPE]
        ss = jnp.sum(kn * kn, axis=-1, keepdims=True) + kr_sq
        rstd = lax.rsqrt(ss / MLA_QK + EPS)
        t = kr * rstd * tab * gkr_ref[...]
        rope = t + pltpu.roll(t, MLA_ROPE, 1)
        k_ref[0, h] = jnp.concatenate([kn * rstd * gkn_ref[...], rope], axis=-1).astype(k_ref.dtype)
        vh = kvf[:, MLA_HEADS * MLA_NOPE + h * MLA_V:MLA_HEADS * MLA_NOPE + (h + 1) * MLA_V]
        v_ref[0, h] = jnp.concatenate([vh, ones], axis=-1).astype(v_ref.dtype)


def _swap_halves(a):
    half = a.shape[-1] // 2
    return jnp.concatenate([a[..., half:], a[..., :half]], axis=-1)


def _mla_prep(u_mla, w_uq, w_ukv, g_cq, g_ckv, gq, gk, *, tm):
    b, s, wu = u_mla.shape
    wq = w_uq.reshape(MLA_Q_RANK, MLA_HEADS, MLA_QK)
    wq = jnp.concatenate([wq, _swap_halves(wq[..., MLA_NOPE:])], axis=-1).reshape(MLA_Q_RANK, MLA_HEADS * MLA_QK_PAD)
    wkv = w_ukv.reshape(MLA_KV_RANK, MLA_HEADS, MLA_NOPE + MLA_V)
    wkv = jnp.concatenate([wkv[..., :MLA_NOPE].reshape(MLA_KV_RANK, -1), wkv[..., MLA_NOPE:].reshape(MLA_KV_RANK, -1)], axis=1)
    half = MLA_ROPE // 2
    inv_freq = ROPE_THETA ** (-jnp.arange(half, dtype=F32) / half)
    ang = jnp.arange(s, dtype=F32)[:, None] * inv_freq[None, :]
    cos, sin = jnp.cos(ang), jnp.sin(ang)
    tab = jnp.concatenate([cos, cos, -sin, sin], axis=-1)
    rope_gain = lambda g: jnp.concatenate([g[MLA_NOPE:], _swap_halves(g[MLA_NOPE:])]).reshape(1, 2 * MLA_ROPE)
    vec = lambda i, j: (0, 0)
    head_out = lambda width: pl.BlockSpec((1, MLA_HEADS, tm, width), lambda i, j: (i, 0, j, 0))
    return pl.pallas_call(
        _mla_prep_kernel,
        grid=(b, s // tm),
        in_specs=[pl.BlockSpec((1, tm, wu), lambda i, j: (i, j, 0)),
                  pl.BlockSpec((tm, 2 * MLA_ROPE), lambda i, j: (j, 0)),
                  pl.BlockSpec(wq.shape, vec),
                  pl.BlockSpec(wkv.shape, vec),
                  pl.BlockSpec((1, MLA_Q_RANK), vec),
                  pl.BlockSpec((1, MLA_KV_RANK), vec),
                  pl.BlockSpec((1, MLA_NOPE), vec),
                  pl.BlockSpec((1, 2 * MLA_ROPE), vec),
                  pl.BlockSpec((1, MLA_NOPE), vec),
                  pl.BlockSpec((1, 2 * MLA_ROPE), vec)],
        out_specs=[head_out(MLA_QK_PAD), head_out(MLA_QK_PAD), head_out(2 * MLA_V)],
        out_shape=[jax.ShapeDtypeStruct((b, MLA_HEADS, s, MLA_QK_PAD), BF16),
                   jax.ShapeDtypeStruct((b, MLA_HEADS, s, MLA_QK_PAD), BF16),
                   jax.ShapeDtypeStruct((b, MLA_HEADS, s, 2 * MLA_V), BF16)],
        compiler_params=_params("parallel", "parallel"),
        name="mla_prep",
    )(u_mla, tab, wq.astype(BF16), wkv.astype(BF16), g_cq.reshape(1, -1), g_ckv.reshape(1, -1),
      gq[:MLA_NOPE].reshape(1, -1), rope_gain(gq), gk[:MLA_NOPE].reshape(1, -1), rope_gain(gk))


def _mla_attn_kernel(qi_ref, ki_ref, q_ref, k_ref, v_ref, o_ref, m_ref, acc_ref, *, tq, tk):
    step = pl.program_id(2)
    qi = qi_ref[step]
    ki = ki_ref[step]

    @pl.when(ki == 0)
    def _():
        m_ref[...] = jnp.full_like(m_ref, NEG_INF)
        acc_ref[...] = jnp.zeros_like(acc_ref)

    def update(rel):
        for r0, k0 in [(r, k) for r in range(0, tq, MLA_ROW_GROUP) for k in range(0, tk, MLA_KEY_GROUP)]:
            first_key = None if rel is None else rel + k0
            if first_key is not None and first_key > r0 + MLA_ROW_GROUP - 1:
                continue
            rows = pl.ds(r0, MLA_ROW_GROUP)
            keys = pl.ds(k0, MLA_KEY_GROUP)
            s = lax.dot_general(q_ref[0, 0, rows], k_ref[0, 0, keys], NT_DIMS, preferred_element_type=F32)
            if first_key is not None and first_key + MLA_KEY_GROUP - 1 > r0:
                row = r0 + lax.broadcasted_iota(I32, s.shape, 0)
                col = first_key + lax.broadcasted_iota(I32, s.shape, 1)
                s = jnp.where(col <= row, s, NEG_INF)
            m_prev = m_ref[rows]
            m_next = jnp.maximum(m_prev, jnp.max(s, axis=-1, keepdims=True))
            alpha = jnp.exp2(m_prev - m_next)
            p = jnp.exp2(s - m_next[:, :1]).astype(BF16)
            pv = jnp.dot(p, v_ref[0, 0, keys], preferred_element_type=F32)
            acc_ref[rows, :MLA_V] = alpha * acc_ref[rows, :MLA_V] + pv[:, :MLA_V]
            acc_ref[rows, MLA_V:] = alpha * acc_ref[rows, MLA_V:] + pv[:, MLA_V:]
            m_ref[rows] = m_next

    key_offset = ki * tk - qi * tq
    pl.when(key_offset < 0)(lambda: update(None))
    for rel in range(0, tq, tk):
        pl.when(key_offset == rel)(functools.partial(update, rel))

    @pl.when((ki + 1) * tk == (qi + 1) * tq)
    def _():
        o_ref[0] = (acc_ref[:, :MLA_V] / acc_ref[:, MLA_V:]).astype(o_ref.dtype)


def _mla_attention(q, k, v, *, tq, tk):
    b, h, s, _ = q.shape
    pairs = [(qi, ki) for qi in range(s // tq) for ki in range((qi + 1) * tq // tk)]
    qi_of = jnp.array([p[0] for p in pairs], I32)
    ki_of = jnp.array([p[1] for p in pairs], I32)
    kv_idx = lambda bi, hi, t, qi_ref, ki_ref: (bi, hi, ki_ref[t], 0)
    return pl.pallas_call(
        functools.partial(_mla_attn_kernel, tq=tq, tk=tk),
        grid_spec=pltpu.PrefetchScalarGridSpec(
            num_scalar_prefetch=2,
            grid=(b, h, len(pairs)),
            in_specs=[pl.BlockSpec((1, 1, tq, MLA_QK_PAD), lambda bi, hi, t, qi_ref, ki_ref: (bi, hi, qi_ref[t], 0)),
                      pl.BlockSpec((1, 1, tk, MLA_QK_PAD), kv_idx),
                      pl.BlockSpec((1, 1, tk, 2 * MLA_V), kv_idx)],
            out_specs=pl.BlockSpec((1, tq, MLA_V), lambda bi, hi, t, qi_ref, ki_ref: (bi, qi_ref[t], hi)),
            scratch_shapes=[pltpu.VMEM((tq, MLA_V), F32), pltpu.VMEM((tq, 2 * MLA_V), F32)],
        ),
        out_shape=jax.ShapeDtypeStruct((b, s, h * MLA_V), BF16),
        compiler_params=_params("parallel", "parallel", "arbitrary"),
        name="mla_attention",
    )(qi_of, ki_of, q, k, v)


def _mem_attn_body(x, kv, g_ref, wq_ref, wo_ref, gq_ref, gk_ref):
    hn = _rms(x, g_ref[...]).astype(BF16)
    qf = jnp.dot(hn, wq_ref[...], preferred_element_type=F32)
    outs = []
    for h in range(MEM_HEADS):
        lo = h * MEM_HEAD_DIM
        qh = (_rms(qf[:, lo:lo + MEM_HEAD_DIM], gq_ref[...]) * (MEM_HEAD_DIM ** -0.5)).astype(BF16)
        kh = _rms(kv[:, lo:lo + MEM_HEAD_DIM], gk_ref[...]).astype(BF16)
        vh = kv[:, MEM_WIDTH + lo:MEM_WIDTH + lo + MEM_HEAD_DIM].astype(BF16)
        s = lax.dot_general(qh, kh, NT_DIMS, preferred_element_type=F32)
        e = jnp.exp(s - jnp.max(s, axis=-1, keepdims=True))
        p = (e / jnp.sum(e, axis=-1, keepdims=True)).astype(BF16)
        outs.append(jnp.dot(p, vh, preferred_element_type=F32))
    o = jnp.concatenate(outs, axis=-1).astype(BF16)
    return x + jnp.dot(o, wo_ref[...], preferred_element_type=F32)


def _split_bf16(a):
    hi = a.astype(BF16)
    return hi, (a - hi.astype(F32)).astype(BF16)


def _router_body(x, g_ref, whi_ref, wlo_ref, b_ref, ids_ref, gates_ref, cnt_ref, carry_ref):
    hn = _rms(x, g_ref[...])
    hi, lo = _split_bf16(hn)
    logits = (jnp.dot(hi, whi_ref[...], preferred_element_type=F32)
              + jnp.dot(hi, wlo_ref[...], preferred_element_type=F32)
              + jnp.dot(lo, whi_ref[...], preferred_element_type=F32)) + b_ref[...]
    lane = lax.broadcasted_iota(I32, logits.shape, 1)
    gl = jnp.where(lane < N_GROUPS, logits, NEG_INF)
    gmax = jnp.max(gl, axis=-1, keepdims=True)
    p_grp = 1.0 / jnp.sum(jnp.exp(gl - gmax), axis=-1, keepdims=True)
    grp = jnp.min(jnp.where(gl == gmax, lane, LANES), axis=-1, keepdims=True)
    in_grp = (lane >= N_GROUPS) & (lane < N_GROUPS + N_EXPERTS) & (((lane - N_GROUPS) // EXPERTS_PER_GROUP) == grp)
    el = jnp.where(in_grp, logits, NEG_INF)
    m1 = jnp.max(el, axis=-1, keepdims=True)
    i1 = jnp.min(jnp.where(el == m1, lane, LANES), axis=-1, keepdims=True)
    el2 = jnp.where(lane == i1, NEG_INF, el)
    m2 = jnp.max(el2, axis=-1, keepdims=True)
    i2 = jnp.min(jnp.where(el2 == m2, lane, LANES), axis=-1, keepdims=True)
    r = jnp.exp(m2 - m1)
    g1 = p_grp / (1.0 + r)
    gates_ref[...] = jnp.where(lane == 0, g1, jnp.where(lane == 1, g1 * r, 0.0))

    tm = logits.shape[0]
    used = jnp.where((lane == i1) | (lane == i2), 1.0, 0.0)
    earlier = lax.broadcasted_iota(I32, (tm, tm), 1) < lax.broadcasted_iota(I32, (tm, tm), 0)
    before = carry_ref[...] + jnp.dot(jnp.where(earlier, 1.0, 0.0).astype(BF16), used.astype(BF16),
                                      preferred_element_type=F32)
    r1 = jnp.sum(jnp.where(lane == i1, before, 0.0), axis=-1, keepdims=True).astype(I32)
    r2 = jnp.sum(jnp.where(lane == i2, before, 0.0), axis=-1, keepdims=True).astype(I32)
    carry_ref[...] += jnp.sum(used, axis=0, keepdims=True)
    cnt_ref[...] = carry_ref[...].astype(I32)
    ids_ref[...] = jnp.where(lane == 0, i1 - N_GROUPS, jnp.where(lane == 1, i2 - N_GROUPS,
                             jnp.where(lane == 2, r1, jnp.where(lane == 3, r2, 0))))


def _post_mixer_kernel(x_ref, a_ref, b_ref, c_ref, wa_ref, wb_ref, wc_ref,
                       kv_ref, gmq_ref, wmq_ref, wmo_ref, mgq_ref, mgk_ref,
                       gffn_ref, whi_ref, wlo_ref, bias_ref,
                       o_ref, ids_ref, gates_ref, cnt_ref, carry_ref):
    @pl.when((pl.program_id(0) == 0) & (pl.program_id(1) == 0))
    def _():
        carry_ref[...] = jnp.zeros_like(carry_ref)

    mix = jnp.dot(a_ref[0], wa_ref[...], preferred_element_type=F32)
    mix += jnp.dot(b_ref[0], wb_ref[...], preferred_element_type=F32)
    mix += jnp.dot(c_ref[0], wc_ref[...], preferred_element_type=F32)
    x = x_ref[0] + mix
    x = _mem_attn_body(x, kv_ref[0], gmq_ref, wmq_ref, wmo_ref, mgq_ref, mgk_ref)
    o_ref[0] = x
    _router_body(x, gffn_ref, whi_ref, wlo_ref, bias_ref, ids_ref, gates_ref, cnt_ref, carry_ref)


def _post_mixer(x, a, bmix, c, w_out, kv, g_mem_q, w_mq, w_mo, mem_gq, mem_gk,
                g_ffn, w_gr, b_gr, w_er, b_er, *, tm):
    b, s, d = x.shape
    m = kv.shape[1]
    nst = s // tm
    wa = w_out[:SWA_WIDTH].astype(BF16)
    wb = w_out[SWA_WIDTH:SWA_WIDTH + HG_WIDTH].astype(BF16)
    wc = w_out[SWA_WIDTH + HG_WIDTH:].astype(BF16)
    pad = LANES - N_GROUPS - N_EXPERTS
    wr = jnp.concatenate([w_gr, w_er, jnp.zeros((d, pad), F32)], axis=1)
    bias = jnp.concatenate([b_gr, b_er, jnp.zeros((pad,), F32)]).reshape(1, LANES)
    whi, wlo = _split_bf16(wr)
    tile = lambda width: pl.BlockSpec((1, tm, width), lambda i, j: (i, j, 0))
    const = lambda shape: pl.BlockSpec(shape, lambda i, j: (0,) * len(shape), pipeline_mode=pl.Buffered(1))
    flat = lambda width: pl.BlockSpec((tm, width), lambda i, j: (i * nst + j, 0))
    return pl.pallas_call(
        _post_mixer_kernel,
        grid=(b, nst),
        in_specs=[tile(d), tile(SWA_WIDTH), tile(HG_WIDTH), tile(MLA_WIDTH),
                  const(wa.shape), const(wb.shape), const(wc.shape),
                  pl.BlockSpec((1, m, 2 * MEM_WIDTH), lambda i, j: (i, 0, 0)),
                  const((1, d)), const((d, MEM_WIDTH)), const((MEM_WIDTH, d)),
                  const((1, MEM_HEAD_DIM)), const((1, MEM_HEAD_DIM)),
                  const((1, d)), const((d, LANES)), const((d, LANES)), const((1, LANES))],
        out_specs=[tile(d), flat(LANES), flat(LANES), pl.BlockSpec((1, LANES), lambda i, j: (0, 0))],
        out_shape=[jax.ShapeDtypeStruct((b, s, d), F32), jax.ShapeDtypeStruct((b * s, LANES), I32),
                   jax.ShapeDtypeStruct((b * s, LANES), F32), jax.ShapeDtypeStruct((1, LANES), I32)],
        scratch_shapes=[pltpu.VMEM((1, LANES), F32)],
        compiler_params=_params("arbitrary", "arbitrary"),
        name="post_mixer",
    )(x, a, bmix, c, wa, wb, wc, kv, g_mem_q.reshape(1, d), w_mq.astype(BF16), w_mo.astype(BF16),
      mem_gq.reshape(1, -1), mem_gk.reshape(1, -1), g_ffn.reshape(1, d), whi, wlo, bias)


HI_HALF_MASK = 0xFFFF0000


def _dispatch_kernel(dest_ref, x_ref, g_ref, xs_in_hbm, xs_hbm, buf, sem, *, ts):
    del xs_in_hbm
    i = pl.program_id(0)
    last = pl.num_programs(0) - 1
    slot = i % 2

    def row_copy(slot_, r, dst_row):
        return pltpu.make_async_copy(buf.at[slot_, pl.ds(r, 1)], xs_hbm.at[pl.ds(dst_row, 1)], sem.at[slot_])

    def wait_step(slot_):
        for _ in range(ts * TOP_K):
            row_copy(slot_, 0, 0).wait()

    @pl.when(i >= 2)
    def _():
        wait_step(slot)

    hn = _rms(x_ref[...], g_ref[...])
    half = hn.shape[1] // 2
    lo = pltpu.bitcast(hn[:, :half].astype(BF16).astype(F32), jnp.uint32) >> 16
    hi = pltpu.bitcast(hn[:, half:].astype(BF16).astype(F32), jnp.uint32) & jnp.uint32(HI_HALF_MASK)
    buf[slot] = lo | hi
    for r in range(ts):
        for kk in range(TOP_K):
            row_copy(slot, r, dest_ref[(i * ts + r) * TOP_K + kk]).start(priority=kk % DMA_PRIORITIES)

    @pl.when(i == last)
    def _():
        wait_step(slot)

    @pl.when((i == last) & (i >= 1))
    def _():
        wait_step(1 - slot)


def _dispatch(x, g_ffn, dest, n_rows, *, ts):
    n, d = x.shape
    xs0 = jnp.zeros((n_rows, d // 2), jnp.uint32)
    return pl.pallas_call(
        functools.partial(_dispatch_kernel, ts=ts),
        grid_spec=pltpu.PrefetchScalarGridSpec(
            num_scalar_prefetch=1,
            grid=(n // ts,),
            in_specs=[pl.BlockSpec((ts, d), lambda i, *_: (i, 0)),
                      pl.BlockSpec((1, d), lambda i, *_: (0, 0)),
                      pl.BlockSpec(memory_space=pl.ANY)],
            out_specs=pl.BlockSpec(memory_space=pl.ANY),
            scratch_shapes=[pltpu.VMEM((2, ts, d // 2), jnp.uint32), pltpu.SemaphoreType.DMA((2,))],
        ),
        out_shape=jax.ShapeDtypeStruct((n_rows, d // 2), jnp.uint32),
        input_output_aliases={3: 0},
        compiler_params=_params("arbitrary"),
        name="moe_dispatch",
    )(dest, x, g_ffn.reshape(1, d), xs0)


def _expert_kernel(bexp_ref, next_ref, nblk_ref, xs_ref, wg_hbm, wu_hbm, wd_hbm, y_ref,
                   wg_f32, wu_f32, wd_f32, sem, wg_bf, wu_bf, wd_bf, *, layer):
    i = pl.program_id(0)
    n_used = nblk_ref[0]
    expert = bexp_ref[i]
    first_of_run = (i == 0) | (expert != bexp_ref[jnp.maximum(i - 1, 0)])
    slot = next_ref[2 * i + 1]

    def weight_copies(e, slot_):
        return [pltpu.make_async_copy(hbm.at[layer, e], buf.at[slot_], sem.at[slot_, j])
                for j, (hbm, buf) in enumerate(((wg_hbm, wg_f32), (wu_hbm, wu_f32), (wd_hbm, wd_f32)))]

    @pl.when(i == 0)
    def _():
        for cp in weight_copies(expert, 0):
            cp.start()

    @pl.when(first_of_run & (i < n_used))
    def _():
        for cp in weight_copies(expert, slot):
            cp.wait()
        next_expert = next_ref[2 * i]

        @pl.when(next_expert >= 0)
        def _():
            for cp in weight_copies(next_expert, 1 - slot):
                cp.start()

        wg_bf[...] = wg_f32[slot].astype(BF16)
        wu_bf[...] = wu_f32[slot].astype(BF16)
        wd_bf[...] = wd_f32[slot].astype(BF16)

    @pl.when(i < n_used)
    def _():
        w = xs_ref[...]
        half = w.shape[1]
        lo = pltpu.bitcast(w << 16, F32).astype(BF16)
        hi = pltpu.bitcast(w & jnp.uint32(HI_HALF_MASK), F32).astype(BF16)
        gate = (jnp.dot(lo, wg_bf[:half], preferred_element_type=F32)
                + jnp.dot(hi, wg_bf[half:], preferred_element_type=F32))
        up = (jnp.dot(lo, wu_bf[:half], preferred_element_type=F32)
              + jnp.dot(hi, wu_bf[half:], preferred_element_type=F32))
        act = (gate / (1.0 + jnp.exp(-gate)) * up).astype(BF16)
        y_ref[...] = jnp.dot(act, wd_bf[...], preferred_element_type=F32)

    @pl.when(i >= nblk_ref[0])
    def _():
        y_ref[...] = jnp.zeros_like(y_ref)


def _experts(xs, block_expert, run_next, n_used, w_gate, w_up, w_down, layer, *, rows):
    d = 2 * xs.shape[1]
    n_blocks = block_expert.shape[0]
    hbm = pl.BlockSpec(memory_space=pl.ANY)
    return pl.pallas_call(
        functools.partial(_expert_kernel, layer=layer),
        grid_spec=pltpu.PrefetchScalarGridSpec(
            num_scalar_prefetch=3,
            grid=(n_blocks,),
            in_specs=[pl.BlockSpec((rows, d // 2), lambda i, *_: (i, 0)), hbm, hbm, hbm],
            out_specs=pl.BlockSpec((rows, d), lambda i, *_: (i, 0)),
            scratch_shapes=[pltpu.VMEM((2, d, D_EXPERT), F32),
                            pltpu.VMEM((2, d, D_EXPERT), F32),
                            pltpu.VMEM((2, D_EXPERT, d), F32),
                            pltpu.SemaphoreType.DMA((2, 3)),
                            pltpu.VMEM((d, D_EXPERT), BF16),
                            pltpu.VMEM((d, D_EXPERT), BF16),
                            pltpu.VMEM((D_EXPERT, d), BF16)],
        ),
        out_shape=jax.ShapeDtypeStruct((n_blocks * rows, d), F32),
        compiler_params=_params("arbitrary"),
        name="moe_experts",
    )(block_expert, run_next, n_used, xs, w_gate, w_up, w_down)


def _combine_kernel(pos_ref, x_ref, gates_ref, y_hbm, o_ref, ybuf, sem, *, tt):
    i = pl.program_id(0)
    nsteps = pl.num_programs(0)

    def row_copy(src_row, slot, dst_row):
        return pltpu.make_async_copy(y_hbm.at[pl.ds(src_row, 1)], ybuf.at[slot, pl.ds(dst_row, 1)], sem.at[slot])

    def start_gather(step, slot):
        for r in range(tt):
            for kk in range(TOP_K):
                row_copy(pos_ref[(step * tt + r) * TOP_K + kk], slot, kk * tt + r).start(priority=kk % DMA_PRIORITIES)

    def wait_gather(slot):
        for r in range(TOP_K * tt):
            row_copy(0, slot, r).wait()

    @pl.when(i == 0)
    def _():
        start_gather(0, 0)

    @pl.when(i + 1 < nsteps)
    def _():
        start_gather(i + 1, (i + 1) % 2)

    wait_gather(i % 2)
    yb = ybuf[i % 2]
    g = gates_ref[...]
    o_ref[...] = x_ref[...] + g[:, 0:1] * yb[:tt] + g[:, 1:2] * yb[tt:]


def _combine(x, gates, pos, y, *, tt):
    n, d = x.shape
    return pl.pallas_call(
        functools.partial(_combine_kernel, tt=tt),
        grid_spec=pltpu.PrefetchScalarGridSpec(
            num_scalar_prefetch=1,
            grid=(n // tt,),
            in_specs=[pl.BlockSpec((tt, d), lambda i, *_: (i, 0)),
                      pl.BlockSpec((tt, LANES), lambda i, *_: (i, 0)),
                      pl.BlockSpec(memory_space=pl.ANY)],
            out_specs=pl.BlockSpec((tt, d), lambda i, *_: (i, 0)),
            scratch_shapes=[pltpu.VMEM((2, TOP_K * tt, d), F32), pltpu.SemaphoreType.DMA((2,))],
        ),
        out_shape=jax.ShapeDtypeStruct((n, d), F32),
        compiler_params=_params("arbitrary"),
        name="moe_combine",
    )(pos, x, gates, y)


def _moe_plan(ids, cnt, *, rows):
    n = ids.shape[0]
    expert, rank = ids[:, :TOP_K], ids[:, TOP_K:2 * TOP_K]
    counts = cnt[0, N_GROUPS:N_GROUPS + N_EXPERTS]
    padded = (counts + rows - 1) // rows * rows
    padded_end = jnp.cumsum(padded)
    seg_start = padded_end - padded
    onehot = expert[..., None] == jnp.arange(N_EXPERTS, dtype=I32)
    dest = (jnp.sum(jnp.where(onehot, seg_start, 0), axis=-1) + rank).astype(I32).reshape(n * TOP_K)
    n_blocks = -(-(n * TOP_K) // rows) + N_EXPERTS
    block_start = jnp.arange(n_blocks, dtype=I32) * rows
    block_expert = jnp.minimum(jnp.sum(padded_end[None, :] <= block_start[:, None], axis=1), N_EXPERTS - 1).astype(I32)
    n_used = (padded_end[-1] // rows).astype(I32)
    block_onehot = block_expert[:, None] == jnp.arange(N_EXPERTS, dtype=I32)
    pick = lambda per_expert: jnp.sum(jnp.where(block_onehot, per_expert, 0), axis=-1)
    next_block = pick(padded_end // rows)
    next_onehot = jnp.minimum(next_block, n_blocks - 1)[:, None] == jnp.arange(n_blocks, dtype=I32)
    next_expert = jnp.where(next_block < n_used, jnp.sum(jnp.where(next_onehot, block_expert, 0), axis=-1), -1)
    nonempty = (counts > 0).astype(I32)
    run_parity = pick(jnp.cumsum(nonempty) - nonempty) % 2
    run_next = jnp.stack([next_expert, run_parity], axis=-1).astype(I32).reshape(2 * n_blocks)
    return dest, block_expert, run_next, n_used.reshape(1)


def _band_relative_bias(table):
    def bucket(nd):
        max_exact = REL_BUCKETS // 2
        nf = jnp.maximum(nd, 1).astype(F32)
        large = max_exact + (jnp.log(nf / max_exact) / math.log(REL_MAX_DIST / max_exact)
                             * (REL_BUCKETS - max_exact)).astype(I32)
        return jnp.where(nd < max_exact, nd, jnp.minimum(large, REL_BUCKETS - 1))

    qi = jnp.arange(WINDOW)[:, None]
    kj = jnp.arange(2 * WINDOW)[None, :]
    dist = jnp.maximum(qi + WINDOW - kj, 0)
    onehot = (bucket(dist)[..., None] == jnp.arange(REL_BUCKETS)).astype(F32)
    return jnp.einsum('qkb,bh->hqk', onehot, table.astype(F32), precision=lax.Precision.HIGHEST)


def _hgrn_params(lb, g_out):
    lb = lb.reshape(HG_HEADS, 1, HG_DK)
    gain = jnp.broadcast_to(g_out.reshape(1, 1, HG_DV), (HG_HEADS, 1, HG_DV))
    rows = [jnp.log(lb), jnp.log1p(-lb), 1.0 - lb, gain, jnp.zeros((HG_HEADS, 4, HG_DK), F32)]
    return jnp.concatenate(rows, axis=1).astype(F32)


def kernel(x, mem, rel_bias_table, hg_lb_logits, g_mix, w_in, swa_gq, swa_gk, swa_sinks, hg_g_out, mla_g_cq, mla_g_ckv, mla_w_uq, mla_w_ukv, mla_gq, mla_gk, w_out, g_mem_q, g_mem_kv, w_mq, w_mkv, mem_gq, mem_gk, w_mo, g_ffn, w_group_router, b_group_router, w_expert_router, b_expert_router, w_gate, w_up, w_down):
    b, s, d = x.shape
    n = b * s
    m = mem.shape[1]
    rel_bias = _band_relative_bias(rel_bias_table)
    lb_all = jnp.cumsum(jax.nn.softmax(hg_lb_logits.astype(F32), axis=0), axis=0)
    lb_all = lb_all - lb_all[:1]
    offs = [0]
    for width in IN_SIZES:
        offs.append(offs[-1] + width)
    o_hg, o_cq, o_kr = offs[3], offs[7], offs[9]

    xf = x.reshape(n, d)
    mem2 = mem.reshape(b * m, d)
    for l in range(DEPTH):
        w = w_in[l]
        kr_cols = w[:, o_kr:]
        w_swa = w[:, :o_hg].astype(BF16)
        w_hg = w[:, o_hg:o_cq].astype(BF16)
        w_mla = jnp.concatenate([w[:, o_cq:], _swap_halves(kr_cols)], axis=1).astype(BF16)
        u_swa = _rms_matmul(xf, g_mix[l], w_swa, tm=1024, tn=w_swa.shape[1], out_dtype=BF16).reshape(b, s, -1)
        u_hg = _rms_matmul(xf, g_mix[l], w_hg, tm=1024, tn=1024, out_dtype=BF16).reshape(b, s, -1)
        u_mla = _rms_matmul(xf, g_mix[l], w_mla, tm=1024, tn=w_mla.shape[1], out_dtype=BF16).reshape(b, s, -1)

        out_a = _swa(u_swa, rel_bias, swa_gq[l], swa_gk[l], swa_sinks[l])
        out_b = _hgrn(u_hg, _hgrn_params(lb_all[l], hg_g_out[l]), chunk=HG_CHUNK, n_sub=HG_SUB)
        qm, km, vm = _mla_prep(u_mla, mla_w_uq[l], mla_w_ukv[l], mla_g_cq[l], mla_g_ckv[l],
                               mla_gq[l], mla_gk[l], tm=512)
        out_c = _mla_attention(qm, km, vm, tq=MLA_TQ, tk=MLA_TK)

        kv = _rms_matmul(mem2, g_mem_kv[l], w_mkv[l].astype(BF16), tm=b * m, tn=2 * MEM_WIDTH)
        x3, ids, gates, cnt = _post_mixer(
            xf.reshape(b, s, d), out_a, out_b, out_c, w_out[l], kv.reshape(b, m, -1),
            g_mem_q[l], w_mq[l], w_mo[l], mem_gq[l], mem_gk[l],
            g_ffn[l], w_group_router[l], b_group_router[l], w_expert_router[l], b_expert_router[l], tm=512)
        xf = x3.reshape(n, d)
        dest, block_expert, run_next, n_used = _moe_plan(ids, cnt, rows=MOE_ROWS)
        xs = _dispatch(xf, g_ffn[l], dest, block_expert.shape[0] * MOE_ROWS, ts=64)
        y = _experts(xs, block_expert, run_next, n_used, w_gate, w_up, w_down, l, rows=MOE_ROWS)
        xf = _combine(xf, gates, dest, y, tt=64)
    return xf.reshape(b, s, d)
```

```python
import functools
import math

import jax
import jax.numpy as jnp
from jax import lax
from jax.experimental import pallas as pl
from jax.experimental.pallas import tpu as pltpu

F32 = jnp.float32
BF16 = jnp.bfloat16
I32 = jnp.int32

D_MODEL = 2048
DEPTH = 2
SWA_HEADS = 8
SWA_KV_HEADS = 2
SWA_HEAD_DIM = 64
WINDOW = 128
HG_HEADS = 8
HG_DK = 128
HG_DV = 128
MLA_HEADS = 4
MLA_Q_RANK = 512
MLA_KV_RANK = 256
MLA_NOPE = 128
MLA_ROPE = 64
MLA_QK = MLA_NOPE + MLA_ROPE
MLA_V = 128
ROPE_THETA = 10000.0
REL_BUCKETS = 32
REL_MAX_DIST = 128
MEM_HEADS = 4
MEM_HEAD_DIM = 128
MEM_WIDTH = MEM_HEADS * MEM_HEAD_DIM
N_GROUPS = 8
EXPERTS_PER_GROUP = 8
N_EXPERTS = N_GROUPS * EXPERTS_PER_GROUP
TOP_K = 2
D_EXPERT = 512
MOE_ROWS = 256
EPS = 1e-6
NEG_INF = -1e30
LOG2_E = math.log2(math.e)

SWA_WIDTH = SWA_HEADS * SWA_HEAD_DIM
SWA_KV_WIDTH = SWA_KV_HEADS * SWA_HEAD_DIM
HG_WIDTH = HG_HEADS * HG_DV
MLA_WIDTH = MLA_HEADS * MLA_V
IN_SIZES = (SWA_WIDTH, SWA_KV_WIDTH, SWA_KV_WIDTH,
            HG_HEADS * HG_DK, HG_HEADS * HG_DK, HG_WIDTH, HG_WIDTH,
            MLA_Q_RANK, MLA_KV_RANK, MLA_ROPE)

LANES = 128
SUBLANES = 8
DMA_PRIORITIES = 2
MLA_QK_PAD = 2 * LANES
VMEM_LIMIT_BYTES = 56 * 1024 * 1024

HG_CHUNK = 128
SWA_BLOCKS_PER_STEP = 4
HG_SUB = 16
MLA_TQ = 2048
MLA_TK = 2048
MLA_KEY_GROUP = 512
MLA_ROW_GROUP = 256
NT_DIMS = (((1,), (1,)), ((), ()))
TN_DIMS = (((0,), (0,)), ((), ()))


def _params(*semantics):
    return pltpu.CompilerParams(dimension_semantics=semantics, vmem_limit_bytes=VMEM_LIMIT_BYTES)


def _rms(x, gain=None):
    y = x * lax.rsqrt(jnp.mean(x * x, axis=-1, keepdims=True) + EPS)
    return y if gain is None else y * gain


def _rms_matmul_kernel(x_ref, g_ref, w_ref, o_ref, hn_ref):
    @pl.when(pl.program_id(1) == 0)
    def _():
        hn_ref[...] = _rms(x_ref[...], g_ref[...]).astype(BF16)

    o_ref[...] = jnp.dot(hn_ref[...], w_ref[...], preferred_element_type=F32).astype(o_ref.dtype)


def _rms_matmul(x, gain, w, *, tm, tn, out_dtype=F32):
    n, d = x.shape
    nout = w.shape[1]
    return pl.pallas_call(
        _rms_matmul_kernel,
        grid=(n // tm, nout // tn),
        in_specs=[pl.BlockSpec((tm, d), lambda i, j: (i, 0)),
                  pl.BlockSpec((1, d), lambda i, j: (0, 0)),
                  pl.BlockSpec((d, tn), lambda i, j: (0, j))],
        out_specs=pl.BlockSpec((tm, tn), lambda i, j: (i, j)),
        out_shape=jax.ShapeDtypeStruct((n, nout), out_dtype),
        scratch_shapes=[pltpu.VMEM((tm, d), BF16)],
        compiler_params=_params("parallel", "arbitrary"),
        name="rms_matmul",
    )(x, gain.reshape(1, d), w)


def _head_rms(x, seg_ref, gain_ref):
    sq = x * x
    hi, lo = _split_bf16(sq)
    ss = (jnp.dot(hi, seg_ref[...], preferred_element_type=F32)
          + jnp.dot(lo, seg_ref[...], preferred_element_type=F32))
    return x * lax.rsqrt(ss * (1.0 / SWA_HEAD_DIM) + EPS) * gain_ref[...]


def _swa_kernel(sink_ref, q_ref, kp_ref, kc_ref, vp_ref, vc_ref, bias_ref, gq_ref, gk_ref, segq_ref, segk_ref, o_ref):
    blk = pl.program_id(1)
    grp = SWA_HEADS // SWA_KV_HEADS
    q_all = _head_rms(q_ref[0].astype(F32), segq_ref, gq_ref)
    k_all = _head_rms(jnp.concatenate([kp_ref[0], kc_ref[0]], axis=0).astype(F32), segk_ref, gk_ref)
    v_all = jnp.concatenate([vp_ref[0], vc_ref[0]], axis=0)
    qi = lax.broadcasted_iota(I32, (WINDOW, 2 * WINDOW), 0)
    kj = lax.broadcasted_iota(I32, (WINDOW, 2 * WINDOW), 1)
    dist = qi + WINDOW - kj
    in_window = (dist >= 0) & (dist < WINDOW)
    first_valid = in_window & (kj >= jnp.where(blk > 0, 0, WINDOW))
    for j in range(q_all.shape[0] // WINDOW):
        q = q_all[j * WINDOW:(j + 1) * WINDOW]
        k = k_all[j * WINDOW:(j + 2) * WINDOW]
        v = v_all[j * WINDOW:(j + 2) * WINDOW]
        valid = first_valid if j == 0 else in_window
        outs = []
        for g in range(SWA_KV_HEADS):
            lo = g * SWA_HEAD_DIM
            kg = k[:, lo:lo + SWA_HEAD_DIM].astype(BF16)
            vg = v[:, lo:lo + SWA_HEAD_DIM].astype(BF16)
            for h in range(g * grp, (g + 1) * grp):
                qh = q[:, h * SWA_HEAD_DIM:(h + 1) * SWA_HEAD_DIM].astype(BF16)
                s = lax.dot_general(qh, kg, NT_DIMS, preferred_element_type=F32) + bias_ref[h]
                s = jnp.where(valid, s, NEG_INF)
                sink = sink_ref[h]
                m = jnp.maximum(jnp.max(s, axis=-1, keepdims=True), sink)
                e = jnp.exp(s - m)
                denom = jnp.sum(e, axis=-1, keepdims=True) + jnp.exp(sink - m)
                outs.append(jnp.dot((e / denom).astype(BF16), vg, preferred_element_type=F32))
        o_ref[0, j * WINDOW:(j + 1) * WINDOW] = jnp.concatenate(outs, axis=-1).astype(o_ref.dtype)


def _swa(u_swa, rel_bias, gq, gk, sinks):
    b, s, _ = u_swa.shape
    rows = SWA_BLOCKS_PER_STEP * WINDOW
    kcol = SWA_WIDTH // LANES
    vcol = kcol + 1
    prev = lambda bi, n, *_: (bi, jnp.maximum(n * SWA_BLOCKS_PER_STEP - 1, 0))
    grp = SWA_HEADS // SWA_KV_HEADS

    def head_blocks(width):
        head = jnp.arange(width) // SWA_HEAD_DIM
        return (head[:, None] == head[None, :]).astype(BF16)

    return pl.pallas_call(
        _swa_kernel,
        grid_spec=pltpu.PrefetchScalarGridSpec(
            num_scalar_prefetch=1,
            grid=(b, s // rows),
            in_specs=[
                pl.BlockSpec((1, rows, SWA_WIDTH), lambda bi, n, *_: (bi, n, 0)),
                pl.BlockSpec((1, WINDOW, LANES), lambda bi, n, *_: prev(bi, n) + (kcol,)),
                pl.BlockSpec((1, rows, LANES), lambda bi, n, *_: (bi, n, kcol)),
                pl.BlockSpec((1, WINDOW, LANES), lambda bi, n, *_: prev(bi, n) + (vcol,)),
                pl.BlockSpec((1, rows, LANES), lambda bi, n, *_: (bi, n, vcol)),
                pl.BlockSpec((SWA_HEADS, WINDOW, 2 * WINDOW), lambda bi, n, *_: (0, 0, 0)),
                pl.BlockSpec((1, SWA_WIDTH), lambda bi, n, *_: (0, 0)),
                pl.BlockSpec((1, SWA_KV_WIDTH), lambda bi, n, *_: (0, 0)),
                pl.BlockSpec((SWA_WIDTH, SWA_WIDTH), lambda bi, n, *_: (0, 0)),
                pl.BlockSpec((SWA_KV_WIDTH, SWA_KV_WIDTH), lambda bi, n, *_: (0, 0)),
            ],
            out_specs=pl.BlockSpec((1, rows, SWA_WIDTH), lambda bi, n, *_: (bi, n, 0)),
        ),
        out_shape=jax.ShapeDtypeStruct((b, s, SWA_WIDTH), BF16),
        compiler_params=_params("parallel", "parallel"),
        name="swa_attention",
    )(sinks, u_swa, u_swa, u_swa, u_swa, u_swa, rel_bias,
      jnp.tile(gq * (SWA_HEAD_DIM ** -0.5), SWA_HEADS).reshape(1, SWA_WIDTH),
      jnp.tile(gk, SWA_KV_HEADS).reshape(1, SWA_KV_WIDTH), head_blocks(SWA_WIDTH), head_blocks(SWA_KV_WIDTH))


def _hgrn_level_matrix(c):
    t = jnp.arange(c)[:, None]
    r = jnp.arange(c)[None, :]
    mats = [(r <= t)]
    half = c // 2
    while half >= 1:
        mid = (t // (2 * half)) * (2 * half) + half
        is_q = (t & half) != 0
        if half < SUBLANES:
            mats.append(jnp.where(is_q, (r >= mid) & (r <= t), (r > t) & (r < mid)))
        half //= 2
    return jnp.concatenate(mats, axis=0).astype(BF16)


def _hgrn_pair_level(c):
    t = jnp.arange(c, dtype=I32)[:, None]
    s = jnp.arange(c, dtype=I32)[None, :]
    x = t ^ s
    lvl = jnp.zeros((c, c), I32)
    half = c // 2
    while half >= 1:
        lvl = jnp.where((x & (-half)) == half, half, lvl)
        half //= 2
    return jnp.where(t > s, lvl, 0)


def _hgrn_kernel(q_ref, f_ref, v_ref, gate_ref, par_ref, amat_ref, lvl_ref, o_ref, st_ref, *, chunk, n_sub):
    @pl.when(pl.program_id(2) == 0)
    def _():
        st_ref[...] = jnp.zeros_like(st_ref)

    c = chunk
    par = par_ref[0]
    log_lb, log1m_lb, one_m_lb, g_out = par[0:1], par[1:2], par[2:3], par[3:4]
    row = lax.broadcasted_iota(I32, (c, 1), 0)
    pair_level = lvl_ref[...]
    st = st_ref[...]
    for sub in range(n_sub):
        rows = pl.ds(sub * c, c)
        q = q_ref[0, rows].astype(F32)
        f = f_ref[0, rows].astype(F32)
        v = v_ref[0, rows].astype(F32)
        gate = gate_ref[0, rows].astype(F32)

        e = jnp.exp(-jnp.abs(f))
        log_sig = jnp.minimum(f, 0.0) - jnp.log(1.0 + e)
        bb = log1m_lb + log_sig
        log_f = jnp.maximum(log_lb, bb) + jnp.log(1.0 + jnp.exp(-jnp.abs(log_lb - bb)))
        log_f = log_f * LOG2_E
        kk = one_m_lb * jnp.where(f < 0.0, 1.0, e) / (1.0 + e)

        g_hi = log_f.astype(BF16)
        g_lo = (log_f - g_hi.astype(F32)).astype(BF16)
        e2 = jnp.dot(amat_ref[...], jnp.concatenate([g_hi, g_lo], axis=1), preferred_element_type=F32)
        expo = e2[:, :HG_DK] + e2[:, HG_DK:]
        bcum = expo[0:c]
        btot = bcum[c - 1:c]

        attn = jnp.zeros((c, c), F32)
        half = c // 2
        lvl = 1
        while half >= 1:
            is_q = (row & half) != 0
            seg = 2 * half
            if half >= SUBLANES:
                ref = jnp.concatenate([jnp.broadcast_to(bcum[a + half - 1:a + half], (seg, HG_DK))
                                       for a in range(0, c, seg)], axis=0)
                diff = bcum - ref
                w = jnp.exp2(jnp.where(is_q, diff, -diff))
            else:
                w = jnp.exp2(expo[lvl * c:(lvl + 1) * c])
                lvl += 1
            zf = jnp.where(is_q, q, kk) * w
            z = zf.astype(BF16)
            if half >= SUBLANES:
                zq = jnp.concatenate([zf[a + half:a + seg] for a in range(0, c, seg)], axis=0).astype(BF16)
                aq = lax.dot_general(zq, z, NT_DIMS, preferred_element_type=F32)
                blank = jnp.zeros((half, c), F32)
                a = jnp.concatenate([blk for j in range(c // seg)
                                     for blk in (blank, aq[j * half:(j + 1) * half])], axis=0)
            else:
                a = lax.dot_general(z, z, NT_DIMS, preferred_element_type=F32)
            attn = jnp.where(pair_level == half, a, attn)
            half //= 2

        vb = v.astype(BF16)
        diag = jnp.sum(q * kk, axis=-1, keepdims=True)
        intra = jnp.dot(attn.astype(BF16), vb, preferred_element_type=F32) + diag * v
        qe = (q * jnp.exp2(bcum)).astype(BF16)
        inter = lax.dot_general(qe, st.astype(BF16), NT_DIMS, preferred_element_type=F32)
        kd = (kk * jnp.exp2(btot - bcum)).astype(BF16)
        st = st * jnp.exp2(btot) +lax.dot_general(vb, kd, TN_DIMS, preferred_element_type=F32)

        o = _rms(inter + intra, g_out)
        o_ref[0, rows] = (o * gate / (1.0 + jnp.exp(-gate))).astype(o_ref.dtype)
    st_ref[...] = st


def _hgrn(u_hg, par, *, chunk, n_sub):
    b, s, _ = u_hg.shape
    amat = _hgrn_level_matrix(chunk)
    pair_level = _hgrn_pair_level(chunk)
    step = chunk * n_sub
    col = lambda off: (lambda bi, h, c: (bi, c, off + h))
    return pl.pallas_call(
        functools.partial(_hgrn_kernel, chunk=chunk, n_sub=n_sub),
        grid=(b, HG_HEADS, s // step),
        in_specs=[pl.BlockSpec((1, step, HG_DK), col(0)),
                  pl.BlockSpec((1, step, HG_DK), col(HG_HEADS)),
                  pl.BlockSpec((1, step, HG_DV), col(2 * HG_HEADS)),
                  pl.BlockSpec((1, step, HG_DV), col(3 * HG_HEADS)),
                  pl.BlockSpec((1, 8, HG_DK), lambda bi, h, c: (h, 0, 0)),
                  pl.BlockSpec(amat.shape, lambda bi, h, c: (0, 0)),
                  pl.BlockSpec(pair_level.shape, lambda bi, h, c: (0, 0))],
        out_specs=pl.BlockSpec((1, step, HG_DV), lambda bi, h, c: (bi, c, h)),
        out_shape=jax.ShapeDtypeStruct((b, s, HG_WIDTH), BF16),
        scratch_shapes=[pltpu.VMEM((HG_DV, HG_DK), F32)],
        compiler_params=_params("parallel", "parallel", "arbitrary"),
        name="hgrn2",
    )(u_hg, u_hg, u_hg, u_hg, par, amat, pair_level)


def _mla_prep_kernel(u_ref, tab_ref, wq_ref, wkv_ref, gcq_ref, gckv_ref, gqn_ref, gqr_ref, gkn_ref, gkr_ref,
                     q_ref, k_ref, v_ref):
    u = u_ref[0].astype(F32)
    cq = _rms(u[:, :MLA_Q_RANK], gcq_ref[...]).astype(BF16)
    ckv = _rms(u[:, MLA_Q_RANK:MLA_Q_RANK + MLA_KV_RANK], gckv_ref[...]).astype(BF16)
    kr = u[:, MLA_Q_RANK + MLA_KV_RANK:]
    qf = jnp.dot(cq, wq_ref[...], preferred_element_type=F32)
    kvf = jnp.dot(ckv, wkv_ref[...], preferred_element_type=F32)
    tab = tab_ref[...]
    low = lax.broadcasted_iota(I32, kr.shape, 1) < MLA_ROPE
    kr_sq = jnp.sum(jnp.where(low, kr * kr, 0.0), axis=-1, keepdims=True)
    scale = MLA_QK ** -0.5 * LOG2_E
    ones = jnp.ones((u.shape[0], MLA_V), F32)
    for h in range(MLA_HEADS):
        qn = qf[:, h * MLA_QK_PAD:h * MLA_QK_PAD + MLA_NOPE]
        qr = qf[:, h * MLA_QK_PAD + MLA_NOPE:(h + 1) * MLA_QK_PAD]
        ss = jnp.sum(qn * qn, axis=-1, keepdims=True) + jnp.sum(jnp.where(low, qr * qr, 0.0), axis=-1, keepdims=True)
        rstd = lax.rsqrt(ss / MLA_QK + EPS) * scale
        t = qr * rstd * tab * gqr_ref[...]
        rope = jnp.where(low, t + pltpu.roll(t, MLA_ROPE, 1), 0.0)
        q_ref[0, h] = jnp.concatenate([qn * rstd * gqn_ref[...], rope], axis=-1).astype(q_ref.dtype)

        kn = kvf[:, h * MLA_NOPE:(h + 1) * MLA_NOPE]
        ss = jnp.sum(kn * kn, axis=-1, keepdims=True) + kr_sq
        rstd = lax.rsqrt(ss / MLA_QK + EPS)
        t = kr * rstd * tab * gkr_ref[...]
        rope = t + pltpu.roll(t, MLA_ROPE, 1)
        k_ref[0, h] = jnp.concatenate([kn * rstd * gkn_ref[...], rope], axis=-1).astype(k_ref.dtype)
        vh = kvf[:, MLA_HEADS * MLA_NOPE + h * MLA_V:MLA_HEADS * MLA_NOPE + (h + 1) * MLA_V]
        v_ref[0, h] = jnp.concatenate([vh, ones], axis=-1).astype(v_ref.dtype)


def _swap_halves(a):
    half = a.shape[-1] // 2
    return jnp.concatenate([a[..., half:], a[..., :half]], axis=-1)


def _mla_prep(u_mla, w_uq, w_ukv, g_cq, g_ckv, gq, gk, *, tm):
    b, s, wu = u_mla.shape
    wq = w_uq.reshape(MLA_Q_RANK, MLA_HEADS, MLA_QK)
    wq = jnp.concatenate([wq, _swap_halves(wq[..., MLA_NOPE:])], axis=-1).reshape(MLA_Q_RANK, MLA_HEADS * MLA_QK_PAD)
    wkv = w_ukv.reshape(MLA_KV_RANK, MLA_HEADS, MLA_NOPE + MLA_V)
    wkv = jnp.concatenate([wkv[..., :MLA_NOPE].reshape(MLA_KV_RANK, -1), wkv[..., MLA_NOPE:].reshape(MLA_KV_RANK, -1)], axis=1)
    half = MLA_ROPE // 2
    inv_freq = ROPE_THETA ** (-jnp.arange(half, dtype=F32) / half)
    ang = jnp.arange(s, dtype=F32)[:, None] * inv_freq[None, :]
    cos, sin = jnp.cos(ang), jnp.sin(ang)
    tab = jnp.concatenate([cos, cos, -sin, sin], axis=-1)
    rope_gain = lambda g: jnp.concatenate([g[MLA_NOPE:], _swap_halves(g[MLA_NOPE:])]).reshape(1, 2 * MLA_ROPE)
    vec = lambda i, j: (0, 0)
    head_out = lambda width: pl.BlockSpec((1, MLA_HEADS, tm, width), lambda i, j: (i, 0, j, 0))
    return pl.pallas_call(
        _mla_prep_kernel,
        grid=(b, s // tm),
        in_specs=[pl.BlockSpec((1, tm, wu), lambda i, j: (i, j, 0)),
                  pl.BlockSpec((tm, 2 * MLA_ROPE), lambda i, j: (j, 0)),
                  pl.BlockSpec(wq.shape, vec),
                  pl.BlockSpec(wkv.shape, vec),
                  pl.BlockSpec((1, MLA_Q_RANK), vec),
                  pl.BlockSpec((1, MLA_KV_RANK), vec),
                  pl.BlockSpec((1, MLA_NOPE), vec),
                  pl.BlockSpec((1, 2 * MLA_ROPE), vec),
                  pl.BlockSpec((1, MLA_NOPE), vec),
                  pl.BlockSpec((1, 2 * MLA_ROPE), vec)],
        out_specs=[head_out(MLA_QK_PAD), head_out(MLA_QK_PAD), head_out(2 * MLA_V)],
        out_shape=[jax.ShapeDtypeStruct((b, MLA_HEADS, s, MLA_QK_PAD), BF16),
                   jax.ShapeDtypeStruct((b, MLA_HEADS, s, MLA_QK_PAD), BF16),
                   jax.ShapeDtypeStruct((b, MLA_HEADS, s, 2 * MLA_V), BF16)],
        compiler_params=_params("parallel", "parallel"),
        name="mla_prep",
    )(u_mla, tab, wq.astype(BF16), wkv.astype(BF16), g_cq.reshape(1, -1), g_ckv.reshape(1, -1),
      gq[:MLA_NOPE].reshape(1, -1), rope_gain(gq), gk[:MLA_NOPE].reshape(1, -1), rope_gain(gk))


def _mla_attn_kernel(qi_ref, ki_ref, q_ref, k_ref, v_ref, o_ref, m_ref, acc_ref, *, tq, tk):
    step = pl.program_id(2)
    qi = qi_ref[step]
    ki = ki_ref[step]

    @pl.when(ki == 0)
    def _():
        m_ref[...] = jnp.full_like(m_ref, NEG_INF)
        acc_ref[...] = jnp.zeros_like(acc_ref)

    def update(rel):
        for r0, k0 in [(r, k) for r in range(0, tq, MLA_ROW_GROUP) for k in range(0, tk, MLA_KEY_GROUP)]:
            first_key = None if rel is None else rel + k0
            if first_key is not None and first_key > r0 + MLA_ROW_GROUP - 1:
                continue
            rows = pl.ds(r0, MLA_ROW_GROUP)
            keys = pl.ds(k0, MLA_KEY_GROUP)
            s = lax.dot_general(q_ref[0, 0, rows], k_ref[0, 0, keys], NT_DIMS, preferred_element_type=F32)
            if first_key is not None and first_key + MLA_KEY_GROUP - 1 > r0:
                row = r0 + lax.broadcasted_iota(I32, s.shape, 0)
                col = first_key + lax.broadcasted_iota(I32, s.shape, 1)
                s = jnp.where(col <= row, s, NEG_INF)
            m_prev = m_ref[rows]
            m_next = jnp.maximum(m_prev, jnp.max(s, axis=-1, keepdims=True))
            alpha = jnp.exp2(m_prev - m_next)
            p = jnp.exp2(s - m_next[:, :1]).astype(BF16)
            pv = jnp.dot(p, v_ref[0, 0, keys], preferred_element_type=F32)
            acc_ref[rows, :MLA_V] = alpha * acc_ref[rows, :MLA_V] + pv[:, :MLA_V]
            acc_ref[rows, MLA_V:] = alpha * acc_ref[rows, MLA_V:] + pv[:, MLA_V:]
            m_ref[rows] = m_next

    key_offset = ki * tk - qi * tq
    pl.when(key_offset < 0)(lambda: update(None))
    for rel in range(0, tq, tk):
        pl.when(key_offset == rel)(functools.partial(update, rel))

    @pl.when((ki + 1) * tk == (qi + 1) * tq)
    def _():
        o_ref[0] = (acc_ref[:, :MLA_V] / acc_ref[:, MLA_V:]).astype(o_ref.dtype)


def _mla_attention(q, k, v, *, tq, tk):
    b, h, s, _ = q.shape
    pairs = [(qi, ki) for qi in range(s // tq) for ki in range((qi + 1) * tq // tk)]
    qi_of = jnp.array([p[0] for p in pairs], I32)
    ki_of = jnp.array([p[1] for p in pairs], I32)
    kv_idx = lambda bi, hi, t, qi_ref, ki_ref: (bi, hi, ki_ref[t], 0)
    return pl.pallas_call(
        functools.partial(_mla_attn_kernel, tq=tq, tk=tk),
        grid_spec=pltpu.PrefetchScalarGridSpec(
            num_scalar_prefetch=2,
            grid=(b, h, len(pairs)),
            in_specs=[pl.BlockSpec((1, 1, tq, MLA_QK_PAD), lambda bi, hi, t, qi_ref, ki_ref: (bi, hi, qi_ref[t], 0)),
                      pl.BlockSpec((1, 1, tk, MLA_QK_PAD), kv_idx),
                      pl.BlockSpec((1, 1, tk, 2 * MLA_V), kv_idx)],
            out_specs=pl.BlockSpec((1, tq, MLA_V), lambda bi, hi, t, qi_ref, ki_ref: (bi, qi_ref[t], hi)),
            scratch_shapes=[pltpu.VMEM((tq, MLA_V), F32), pltpu.VMEM((tq, 2 * MLA_V), F32)],
        ),
        out_shape=jax.ShapeDtypeStruct((b, s, h * MLA_V), BF16),
        compiler_params=_params("parallel", "parallel", "arbitrary"),
        name="mla_attention",
    )(qi_of, ki_of, q, k, v)


def _mem_attn_body(x, kv, g_ref, wq_ref, wo_ref, gq_ref, gk_ref):
    hn = _rms(x, g_ref[...]).astype(BF16)
    qf = jnp.dot(hn, wq_ref[...], preferred_element_type=F32)
    outs = []
    for h in range(MEM_HEADS):
        lo = h * MEM_HEAD_DIM
        qh = (_rms(qf[:, lo:lo + MEM_HEAD_DIM], gq_ref[...]) * (MEM_HEAD_DIM ** -0.5)).astype(BF16)
        kh = _rms(kv[:, lo:lo + MEM_HEAD_DIM], gk_ref[...]).astype(BF16)
        vh = kv[:, MEM_WIDTH + lo:MEM_WIDTH + lo + MEM_HEAD_DIM].astype(BF16)
        s = lax.dot_general(qh, kh, NT_DIMS, preferred_element_type=F32)
        e = jnp.exp(s - jnp.max(s, axis=-1, keepdims=True))
        p = (e / jnp.sum(e, axis=-1, keepdims=True)).astype(BF16)
        outs.append(jnp.dot(p, vh, preferred_element_type=F32))
    o = jnp.concatenate(outs, axis=-1).astype(BF16)
    return x + jnp.dot(o, wo_ref[...], preferred_element_type=F32)


def _split_bf16(a):
    hi = a.astype(BF16)
    return hi, (a - hi.astype(F32)).astype(BF16)


def _router_body(x, g_ref, whi_ref, wlo_ref, b_ref, ids_ref, gates_ref, cnt_ref, carry_ref):
    hn = _rms(x, g_ref[...])
    hi, lo = _split_bf16(hn)
    logits = (jnp.dot(hi, whi_ref[...], preferred_element_type=F32)
              + jnp.dot(hi, wlo_ref[...], preferred_element_type=F32)
              + jnp.dot(lo, whi_ref[...], preferred_element_type=F32)) + b_ref[...]
    lane = lax.broadcasted_iota(I32, logits.shape, 1)
    gl = jnp.where(lane < N_GROUPS, logits, NEG_INF)
    gmax = jnp.max(gl, axis=-1, keepdims=True)
    p_grp = 1.0 / jnp.sum(jnp.exp(gl - gmax), axis=-1, keepdims=True)
    grp = jnp.min(jnp.where(gl == gmax, lane, LANES), axis=-1, keepdims=True)
    in_grp = (lane >= N_GROUPS) & (lane < N_GROUPS + N_EXPERTS) & (((lane - N_GROUPS) // EXPERTS_PER_GROUP) == grp)
    el = jnp.where(in_grp, logits, NEG_INF)
    m1 = jnp.max(el, axis=-1, keepdims=True)
    i1 = jnp.min(jnp.where(el == m1, lane, LANES), axis=-1, keepdims=True)
    el2 = jnp.where(lane == i1, NEG_INF, el)
    m2 = jnp.max(el2, axis=-1, keepdims=True)
    i2 = jnp.min(jnp.where(el2 == m2, lane, LANES), axis=-1, keepdims=True)
    r = jnp.exp(m2 - m1)
    g1 = p_grp / (1.0 + r)
    gates_ref[...] = jnp.where(lane == 0, g1, jnp.where(lane == 1, g1 * r, 0.0))

    tm = logits.shape[0]
    used = jnp.where((lane == i1) | (lane == i2), 1.0, 0.0)
    earlier = lax.broadcasted_iota(I32, (tm, tm), 1) < lax.broadcasted_iota(I32, (tm, tm), 0)
    before = carry_ref[...] + jnp.dot(jnp.where(earlier, 1.0, 0.0).astype(BF16), used.astype(BF16),
                                      preferred_element_type=F32)
    r1 = jnp.sum(jnp.where(lane == i1, before, 0.0), axis=-1, keepdims=True).astype(I32)
    r2 = jnp.sum(jnp.where(lane == i2, before, 0.0), axis=-1, keepdims=True).astype(I32)
    carry_ref[...] += jnp.sum(used, axis=0, keepdims=True)
    cnt_ref[...] = carry_ref[...].astype(I32)
    ids_ref[...] = jnp.where(lane == 0, i1 - N_GROUPS, jnp.where(lane == 1, i2 - N_GROUPS,
                             jnp.where(lane == 2, r1, jnp.where(lane == 3, r2, 0))))


def _post_mixer_kernel(x_ref, a_ref, b_ref, c_ref, wa_ref, wb_ref, wc_ref,
                       kv_ref, gmq_ref, wmq_ref, wmo_ref, mgq_ref, mgk_ref,
                       gffn_ref, whi_ref, wlo_ref, bias_ref,
                       o_ref, ids_ref, gates_ref, cnt_ref, carry_ref):
    @pl.when((pl.program_id(0) == 0) & (pl.program_id(1) == 0))
    def _():
        carry_ref[...] = jnp.zeros_like(carry_ref)

    mix = jnp.dot(a_ref[0], wa_ref[...], preferred_element_type=F32)
    mix += jnp.dot(b_ref[0], wb_ref[...], preferred_element_type=F32)
    mix += jnp.dot(c_ref[0], wc_ref[...], preferred_element_type=F32)
    x = x_ref[0] + mix
    x = _mem_attn_body(x, kv_ref[0], gmq_ref, wmq_ref, wmo_ref, mgq_ref, mgk_ref)
    o_ref[0] = x
    _router_body(x, gffn_ref, whi_ref, wlo_ref, bias_ref, ids_ref, gates_ref, cnt_ref, carry_ref)


def _post_mixer(x, a, bmix, c, w_out, kv, g_mem_q, w_mq, w_mo, mem_gq, mem_gk,
                g_ffn, w_gr, b_gr, w_er, b_er, *, tm):
    b, s, d = x.shape
    m = kv.shape[1]
    nst = s // tm
    wa = w_out[:SWA_WIDTH].astype(BF16)
    wb = w_out[SWA_WIDTH:SWA_WIDTH + HG_WIDTH].astype(BF16)
    wc = w_out[SWA_WIDTH + HG_WIDTH:].astype(BF16)
    pad = LANES - N_GROUPS - N_EXPERTS
    wr = jnp.concatenate([w_gr, w_er, jnp.zeros((d, pad), F32)], axis=1)
    bias = jnp.concatenate([b_gr, b_er, jnp.zeros((pad,), F32)]).reshape(1, LANES)
    whi, wlo = _split_bf16(wr)
    tile = lambda width: pl.BlockSpec((1, tm, width), lambda i, j: (i, j, 0))
    const = lambda shape: pl.BlockSpec(shape, lambda i, j: (0,) * len(shape), pipeline_mode=pl.Buffered(1))
    flat = lambda width: pl.BlockSpec((tm, width), lambda i, j: (i * nst + j, 0))
    return pl.pallas_call(
        _post_mixer_kernel,
        grid=(b, nst),
        in_specs=[tile(d), tile(SWA_WIDTH), tile(HG_WIDTH), tile(MLA_WIDTH),
                  const(wa.shape), const(wb.shape), const(wc.shape),
                  pl.BlockSpec((1, m, 2 * MEM_WIDTH), lambda i, j: (i, 0, 0)),
                  const((1, d)), const((d, MEM_WIDTH)), const((MEM_WIDTH, d)),
                  const((1, MEM_HEAD_DIM)), const((1, MEM_HEAD_DIM)),
                  const((1, d)), const((d, LANES)), const((d, LANES)), const((1, LANES))],
        out_specs=[tile(d), flat(LANES), flat(LANES), pl.BlockSpec((1, LANES), lambda i, j: (0, 0))],
        out_shape=[jax.ShapeDtypeStruct((b, s, d), F32), jax.ShapeDtypeStruct((b * s, LANES), I32),
                   jax.ShapeDtypeStruct((b * s, LANES), F32), jax.ShapeDtypeStruct((1, LANES), I32)],
        scratch_shapes=[pltpu.VMEM((1, LANES), F32)],
        compiler_params=_params("arbitrary", "arbitrary"),
        name="post_mixer",
    )(x, a, bmix, c, wa, wb, wc, kv, g_mem_q.reshape(1, d), w_mq.astype(BF16), w_mo.astype(BF16),
      mem_gq.reshape(1, -1), mem_gk.reshape(1, -1), g_ffn.reshape(1, d), whi, wlo, bias)


HI_HALF_MASK = 0xFFFF0000


def _dispatch_kernel(dest_ref, x_ref, g_ref, xs_in_hbm, xs_hbm, buf, sem, *, ts):
    del xs_in_hbm
    i = pl.program_id(0)
    last = pl.num_programs(0) - 1
    slot = i % 2

    def row_copy(slot_, r, dst_row):
        return pltpu.make_async_copy(buf.at[slot_, pl.ds(r, 1)], xs_hbm.at[pl.ds(dst_row, 1)], sem.at[slot_])

    def wait_step(slot_):
        for _ in range(ts * TOP_K):
            row_copy(slot_, 0, 0).wait()

    @pl.when(i >= 2)
    def _():
        wait_step(slot)

    hn = _rms(x_ref[...], g_ref[...])
    half = hn.shape[1] // 2
    lo = pltpu.bitcast(hn[:, :half].astype(BF16).astype(F32), jnp.uint32) >> 16
    hi = pltpu.bitcast(hn[:, half:].astype(BF16).astype(F32), jnp.uint32) & jnp.uint32(HI_HALF_MASK)
    buf[slot] = lo | hi
    for r in range(ts):
        for kk in range(TOP_K):
            row_copy(slot, r, dest_ref[(i * ts + r) * TOP_K + kk]).start(priority=kk % DMA_PRIORITIES)

    @pl.when(i == last)
    def _():
        wait_step(slot)

    @pl.when((i == last) & (i >= 1))
    def _():
        wait_step(1 - slot)


def _dispatch(x, g_ffn, dest, n_rows, *, ts):
    n, d = x.shape
    xs0 = jnp.zeros((n_rows, d // 2), jnp.uint32)
    return pl.pallas_call(
        functools.partial(_dispatch_kernel, ts=ts),
        grid_spec=pltpu.PrefetchScalarGridSpec(
            num_scalar_prefetch=1,
            grid=(n // ts,),
            in_specs=[pl.BlockSpec((ts, d), lambda i, *_: (i, 0)),
                      pl.BlockSpec((1, d), lambda i, *_: (0, 0)),
                      pl.BlockSpec(memory_space=pl.ANY)],
            out_specs=pl.BlockSpec(memory_space=pl.ANY),
            scratch_shapes=[pltpu.VMEM((2, ts, d // 2), jnp.uint32), pltpu.SemaphoreType.DMA((2,))],
        ),
        out_shape=jax.ShapeDtypeStruct((n_rows, d // 2), jnp.uint32),
        input_output_aliases={3: 0},
        compiler_params=_params("arbitrary"),
        name="moe_dispatch",
    )(dest, x, g_ffn.reshape(1, d), xs0)


def _expert_kernel(bexp_ref, next_ref, nblk_ref, xs_ref, wg_hbm, wu_hbm, wd_hbm, y_ref,
                   wg_f32, wu_f32, wd_f32, sem, wg_bf, wu_bf, wd_bf, *, layer):
    i = pl.program_id(0)
    n_used = nblk_ref[0]
    expert = bexp_ref[i]
    first_of_run = (i == 0) | (expert != bexp_ref[jnp.maximum(i - 1, 0)])
    slot = next_ref[2 * i + 1]

    def weight_copies(e, slot_):
        return [pltpu.make_async_copy(hbm.at[layer, e], buf.at[slot_], sem.at[slot_, j])
                for j, (hbm, buf) in enumerate(((wg_hbm, wg_f32), (wu_hbm, wu_f32), (wd_hbm, wd_f32)))]

    @pl.when(i == 0)
    def _():
        for cp in weight_copies(expert, 0):
            cp.start()

    @pl.when(first_of_run & (i < n_used))
    def _():
        for cp in weight_copies(expert, slot):
            cp.wait()
        next_expert = next_ref[2 * i]

        @pl.when(next_expert >= 0)
        def _():
            for cp in weight_copies(next_expert, 1 - slot):
                cp.start()

        wg_bf[...] = wg_f32[slot].astype(BF16)
        wu_bf[...] = wu_f32[slot].astype(BF16)
        wd_bf[...] = wd_f32[slot].astype(BF16)

    @pl.when(i < n_used)
    def _():
        w = xs_ref[...]
        half = w.shape[1]
        lo = pltpu.bitcast(w << 16, F32).astype(BF16)
        hi = pltpu.bitcast(w & jnp.uint32(HI_HALF_MASK), F32).astype(BF16)
        gate = (jnp.dot(lo, wg_bf[:half], preferred_element_type=F32)
                + jnp.dot(hi, wg_bf[half:], preferred_element_type=F32))
        up = (jnp.dot(lo, wu_bf[:half], preferred_element_type=F32)
              + jnp.dot(hi, wu_bf[half:], preferred_element_type=F32))
        act = (gate / (1.0 + jnp.exp(-gate)) * up).astype(BF16)
        y_ref[...] = jnp.dot(act, wd_bf[...], preferred_element_type=F32)

    @pl.when(i >= nblk_ref[0])
    def _():
        y_ref[...] = jnp.zeros_like(y_ref)


def _experts(xs, block_expert, run_next, n_used, w_gate, w_up, w_down, layer, *, rows):
    d = 2 * xs.shape[1]
    n_blocks = block_expert.shape[0]
    hbm = pl.BlockSpec(memory_space=pl.ANY)
    return pl.pallas_call(
        functools.partial(_expert_kernel, layer=layer),
        grid_spec=pltpu.PrefetchScalarGridSpec(
            num_scalar_prefetch=3,
            grid=(n_blocks,),
            in_specs=[pl.BlockSpec((rows, d // 2), lambda i, *_: (i, 0)), hbm, hbm, hbm],
            out_specs=pl.BlockSpec((rows, d), lambda i, *_: (i, 0)),
            scratch_shapes=[pltpu.VMEM((2, d, D_EXPERT), F32),
                            pltpu.VMEM((2, d, D_EXPERT), F32),
                            pltpu.VMEM((2, D_EXPERT, d), F32),
                            pltpu.SemaphoreType.DMA((2, 3)),
                            pltpu.VMEM((d, D_EXPERT), BF16),
                            pltpu.VMEM((d, D_EXPERT), BF16),
                            pltpu.VMEM((D_EXPERT, d), BF16)],
        ),
        out_shape=jax.ShapeDtypeStruct((n_blocks * rows, d), F32),
        compiler_params=_params("arbitrary"),
        name="moe_experts",
    )(block_expert, run_next, n_used, xs, w_gate, w_up, w_down)


def _combine_kernel(pos_ref, x_ref, gates_ref, y_hbm, o_ref, ybuf, sem, *, tt):
    i = pl.program_id(0)
    nsteps = pl.num_programs(0)

    def row_copy(src_row, slot, dst_row):
        return pltpu.make_async_copy(y_hbm.at[pl.ds(src_row, 1)], ybuf.at[slot, pl.ds(dst_row, 1)], sem.at[slot])

    def start_gather(step, slot):
        for r in range(tt):
            for kk in range(TOP_K):
                row_copy(pos_ref[(step * tt + r) * TOP_K + kk], slot, kk * tt + r).start(priority=kk % DMA_PRIORITIES)

    def wait_gather(slot):
        for r in range(TOP_K * tt):
            row_copy(0, slot, r).wait()

    @pl.when(i == 0)
    def _():
        start_gather(0, 0)

    @pl.when(i + 1 < nsteps)
    def _():
        start_gather(i + 1, (i + 1) % 2)

    wait_gather(i % 2)
    yb = ybuf[i % 2]
    g = gates_ref[...]
    o_ref[...] = x_ref[...] + g[:, 0:1] * yb[:tt] + g[:, 1:2] * yb[tt:]


def _combine(x, gates, pos, y, *, tt):
    n, d = x.shape
    return pl.pallas_call(
        functools.partial(_combine_kernel, tt=tt),
        grid_spec=pltpu.PrefetchScalarGridSpec(
            num_scalar_prefetch=1,
            grid=(n // tt,),
            in_specs=[pl.BlockSpec((tt, d), lambda i, *_: (i, 0)),
                      pl.BlockSpec((tt, LANES), lambda i, *_: (i, 0)),
                      pl.BlockSpec(memory_space=pl.ANY)],
            out_specs=pl.BlockSpec((tt, d), lambda i, *_: (i, 0)),
            scratch_shapes=[pltpu.VMEM((2, TOP_K * tt, d), F32), pltpu.SemaphoreType.DMA((2,))],
        ),
        out_shape=jax.ShapeDtypeStruct((n, d), F32),
        compiler_params=_params("arbitrary"),
        name="moe_combine",
    )(pos, x, gates, y)


def _moe_plan(ids, cnt, *, rows):
    n = ids.shape[0]
    expert, rank = ids[:, :TOP_K], ids[:, TOP_K:2 * TOP_K]
    counts = cnt[0, N_GROUPS:N_GROUPS + N_EXPERTS]
    padded = (counts + rows - 1) // rows * rows
    padded_end = jnp.cumsum(padded)
    seg_start = padded_end - padded
    onehot = expert[..., None] == jnp.arange(N_EXPERTS, dtype=I32)
    dest = (jnp.sum(jnp.where(onehot, seg_start, 0), axis=-1) + rank).astype(I32).reshape(n * TOP_K)
    n_blocks = -(-(n * TOP_K) // rows) + N_EXPERTS
    block_start = jnp.arange(n_blocks, dtype=I32) * rows
    block_expert = jnp.minimum(jnp.sum(padded_end[None, :] <= block_start[:, None], axis=1), N_EXPERTS - 1).astype(I32)
    n_used = (padded_end[-1] // rows).astype(I32)
    block_onehot = block_expert[:, None] == jnp.arange(N_EXPERTS, dtype=I32)
    pick = lambda per_expert: jnp.sum(jnp.where(block_onehot, per_expert, 0), axis=-1)
    next_block = pick(padded_end // rows)
    next_onehot = jnp.minimum(next_block, n_blocks - 1)[:, None] == jnp.arange(n_blocks, dtype=I32)
    next_expert = jnp.where(next_block < n_used, jnp.sum(jnp.where(next_onehot, block_expert, 0), axis=-1), -1)
    nonempty = (counts > 0).astype(I32)
    run_parity = pick(jnp.cumsum(nonempty) - nonempty) % 2
    run_next = jnp.stack([next_expert, run_parity], axis=-1).astype(I32).reshape(2 * n_blocks)
    return dest, block_expert, run_next, n_used.reshape(1)


def _band_relative_bias(table):
    def bucket(nd):
        max_exact = REL_BUCKETS // 2
        nf = jnp.maximum(nd, 1).astype(F32)
        large = max_exact + (jnp.log(nf / max_exact) / math.log(REL_MAX_DIST / max_exact)
                             * (REL_BUCKETS - max_exact)).astype(I32)
        return jnp.where(nd < max_exact, nd, jnp.minimum(large, REL_BUCKETS - 1))

    qi = jnp.arange(WINDOW)[:, None]
    kj = jnp.arange(2 * WINDOW)[None, :]
    dist = jnp.maximum(qi + WINDOW - kj, 0)
    onehot = (bucket(dist)[..., None] == jnp.arange(REL_BUCKETS)).astype(F32)
    return jnp.einsum('qkb,bh->hqk', onehot, table.astype(F32), precision=lax.Precision.HIGHEST)


def _hgrn_params(lb, g_out):
    lb = lb.reshape(HG_HEADS, 1, HG_DK)
    gain = jnp.broadcast_to(g_out.reshape(1, 1, HG_DV), (HG_HEADS, 1, HG_DV))
    rows = [jnp.log(lb), jnp.log1p(-lb), 1.0 - lb, gain, jnp.zeros((HG_HEADS, 4, HG_DK), F32)]
    return jnp.concatenate(rows, axis=1).astype(F32)


def kernel(x, mem, rel_bias_table, hg_lb_logits, g_mix, w_in, swa_gq, swa_gk, swa_sinks, hg_g_out, mla_g_cq, mla_g_ckv, mla_w_uq, mla_w_ukv, mla_gq, mla_gk, w_out, g_mem_q, g_mem_kv, w_mq, w_mkv, mem_gq, mem_gk, w_mo, g_ffn, w_group_router, b_group_router, w_expert_router, b_expert_router, w_gate, w_up, w_down):
    b, s, d = x.shape
    n = b * s
    m = mem.shape[1]
    rel_bias = _band_relative_bias(rel_bias_table)
    lb_all = jnp.cumsum(jax.nn.softmax(hg_lb_logits.astype(F32), axis=0), axis=0)
    lb_all = lb_all - lb_all[:1]
    offs = [0]
    for width in IN_SIZES:
        offs.append(offs[-1] + width)
    o_hg, o_cq, o_kr = offs[3], offs[7], offs[9]

    xf = x.reshape(n, d)
    mem2 = mem.reshape(b * m, d)
    for l in range(DEPTH):
        w = w_in[l]
        kr_cols = w[:, o_kr:]
        w_swa = w[:, :o_hg].astype(BF16)
        w_hg = w[:, o_hg:o_cq].astype(BF16)
        w_mla = jnp.concatenate([w[:, o_cq:], _swap_halves(kr_cols)], axis=1).astype(BF16)
        u_swa = _rms_matmul(xf, g_mix[l], w_swa, tm=1024, tn=w_swa.shape[1], out_dtype=BF16).reshape(b, s, -1)
        u_hg = _rms_matmul(xf, g_mix[l], w_hg, tm=1024, tn=1024, out_dtype=BF16).reshape(b, s, -1)
        u_mla = _rms_matmul(xf, g_mix[l], w_mla, tm=1024, tn=w_mla.shape[1], out_dtype=BF16).reshape(b, s, -1)

        out_a = _swa(u_swa, rel_bias, swa_gq[l], swa_gk[l], swa_sinks[l])
        out_b = _hgrn(u_hg, _hgrn_params(lb_all[l], hg_g_out[l]), chunk=HG_CHUNK, n_sub=HG_SUB)
        qm, km, vm = _mla_prep(u_mla, mla_w_uq[l], mla_w_ukv[l], mla_g_cq[l], mla_g_ckv[l],
                               mla_gq[l], mla_gk[l], tm=512)
        out_c = _mla_attention(qm, km, vm, tq=MLA_TQ, tk=MLA_TK)

        kv = _rms_matmul(mem2, g_mem_kv[l], w_mkv[l].astype(BF16), tm=b * m, tn=2 * MEM_WIDTH)
        x3, ids, gates, cnt = _post_mixer(
            xf.reshape(b, s, d), out_a, out_b, out_c, w_out[l], kv.reshape(b, m, -1),
            g_mem_q[l], w_mq[l], w_mo[l], mem_gq[l], mem_gk[l],
            g_ffn[l], w_group_router[l], b_group_router[l], w_expert_router[l], b_expert_router[l], tm=512)
        xf = x3.reshape(n, d)
        dest, block_expert, run_next, n_used = _moe_plan(ids, cnt, rows=MOE_ROWS)
        xs = _dispatch(xf, g_ffn[l], dest, block_expert.shape[0] * MOE_ROWS, ts=64)
        y = _experts(xs, block_expert, run_next, n_used, w_gate, w_up, w_down, l, rows=MOE_ROWS)
        xf = _combine(xf, gates, dest, y, tt=64)
    return xf.reshape(b, s, d)
```

```python
import functools
import math

import jax
import jax.numpy as jnp
from jax import lax
from jax.experimental import pallas as pl
from jax.experimental.pallas import tpu as pltpu

F32 = jnp.float32
BF16 = jnp.bfloat16
I32 = jnp.int32

D_MODEL = 2048
DEPTH = 2
SWA_HEADS = 8
SWA_KV_HEADS = 2
SWA_HEAD_DIM = 64
WINDOW = 128
HG_HEADS = 8
HG_DK = 128
HG_DV = 128
MLA_HEADS = 4
MLA_Q_RANK = 512
MLA_KV_RANK = 256
MLA_NOPE = 128
MLA_ROPE = 64
MLA_QK = MLA_NOPE + MLA_ROPE
MLA_V = 128
ROPE_THETA = 10000.0
REL_BUCKETS = 32
REL_MAX_DIST = 128
MEM_HEADS = 4
MEM_HEAD_DIM = 128
MEM_WIDTH = MEM_HEADS * MEM_HEAD_DIM
N_GROUPS = 8
EXPERTS_PER_GROUP = 8
N_EXPERTS = N_GROUPS * EXPERTS_PER_GROUP
TOP_K = 2
D_EXPERT = 512
MOE_BURST_TOKENS = 256
MOE_ROWS = 256
EPS = 1e-6
NEG_INF = -1e30
LOG2_E = math.log2(math.e)

SWA_WIDTH = SWA_HEADS * SWA_HEAD_DIM
SWA_KV_WIDTH = SWA_KV_HEADS * SWA_HEAD_DIM
HG_WIDTH = HG_HEADS * HG_DV
MLA_WIDTH = MLA_HEADS * MLA_V
IN_SIZES = (SWA_WIDTH, SWA_KV_WIDTH, SWA_KV_WIDTH,
            HG_HEADS * HG_DK, HG_HEADS * HG_DK, HG_WIDTH, HG_WIDTH,
            MLA_Q_RANK, MLA_KV_RANK, MLA_ROPE)

LANES = 128
SUBLANES = 8
DMA_PRIORITIES = 2
MLA_QK_PAD = 2 * LANES
VMEM_LIMIT_BYTES = 56 * 1024 * 1024

HG_CHUNK = 128
SWA_BLOCKS_PER_STEP = 4
HG_SUB = 16
MLA_TQ = 2048
MLA_TK = 2048
MLA_KEY_GROUP = 512
MLA_ROW_GROUP = 256
NT_DIMS = (((1,), (1,)), ((), ()))
TN_DIMS = (((0,), (0,)), ((), ()))


def _params(*semantics):
    return pltpu.CompilerParams(dimension_semantics=semantics, vmem_limit_bytes=VMEM_LIMIT_BYTES)


def _rms(x, gain=None):
    y = x * lax.rsqrt(jnp.mean(x * x, axis=-1, keepdims=True) + EPS)
    return y if gain is None else y * gain


def _rms_matmul_kernel(x_ref, g_ref, w_ref, o_ref, hn_ref):
    @pl.when(pl.program_id(1) == 0)
    def _():
        hn_ref[...] = _rms(x_ref[...], g_ref[...]).astype(BF16)

    o_ref[...] = jnp.dot(hn_ref[...], w_ref[...], preferred_element_type=F32).astype(o_ref.dtype)


def _rms_matmul(x, gain, w, *, tm, tn, out_dtype=F32):
    n, d = x.shape
    nout = w.shape[1]
    return pl.pallas_call(
        _rms_matmul_kernel,
        grid=(n // tm, nout // tn),
        in_specs=[pl.BlockSpec((tm, d), lambda i, j: (i, 0)),
                  pl.BlockSpec((1, d), lambda i, j: (0, 0)),
                  pl.BlockSpec((d, tn), lambda i, j: (0, j))],
        out_specs=pl.BlockSpec((tm, tn), lambda i, j: (i, j)),
        out_shape=jax.ShapeDtypeStruct((n, nout), out_dtype),
        scratch_shapes=[pltpu.VMEM((tm, d), BF16)],
        compiler_params=_params("parallel", "arbitrary"),
        name="rms_matmul",
    )(x, gain.reshape(1, d), w)


def _head_rms(x, seg_ref, gain_ref):
    sq = x * x
    hi, lo = _split_bf16(sq)
    ss = (jnp.dot(hi, seg_ref[...], preferred_element_type=F32)
          + jnp.dot(lo, seg_ref[...], preferred_element_type=F32))
    return x * lax.rsqrt(ss * (1.0 / SWA_HEAD_DIM) + EPS) * gain_ref[...]


def _swa_kernel(sink_ref, q_ref, kp_ref, kc_ref, vp_ref, vc_ref, bias_ref, gq_ref, gk_ref, segq_ref, segk_ref, o_ref):
    blk = pl.program_id(1)
    grp = SWA_HEADS // SWA_KV_HEADS
    q_all = _head_rms(q_ref[0].astype(F32), segq_ref, gq_ref)
    k_all = _head_rms(jnp.concatenate([kp_ref[0], kc_ref[0]], axis=0).astype(F32), segk_ref, gk_ref)
    v_all = jnp.concatenate([vp_ref[0], vc_ref[0]], axis=0)
    qi = lax.broadcasted_iota(I32, (WINDOW, 2 * WINDOW), 0)
    kj = lax.broadcasted_iota(I32, (WINDOW, 2 * WINDOW), 1)
    dist = qi + WINDOW - kj
    in_window = (dist >= 0) & (dist < WINDOW)
    first_valid = in_window & (kj >= jnp.where(blk > 0, 0, WINDOW))
    for j in range(q_all.shape[0] // WINDOW):
        q = q_all[j * WINDOW:(j + 1) * WINDOW]
        k = k_all[j * WINDOW:(j + 2) * WINDOW]
        v = v_all[j * WINDOW:(j + 2) * WINDOW]
        valid = first_valid if j == 0 else in_window
        outs = []
        for g in range(SWA_KV_HEADS):
            lo = g * SWA_HEAD_DIM
            kg = k[:, lo:lo + SWA_HEAD_DIM].astype(BF16)
            vg = v[:, lo:lo + SWA_HEAD_DIM].astype(BF16)
            for h in range(g * grp, (g + 1) * grp):
                qh = q[:, h * SWA_HEAD_DIM:(h + 1) * SWA_HEAD_DIM].astype(BF16)
                s = lax.dot_general(qh, kg, NT_DIMS, preferred_element_type=F32) + bias_ref[h]
                s = jnp.where(valid, s, NEG_INF)
                sink = sink_ref[h]
                m = jnp.maximum(jnp.max(s, axis=-1, keepdims=True), sink)
                e = jnp.exp(s - m)
                denom = jnp.sum(e, axis=-1, keepdims=True) + jnp.exp(sink - m)
                outs.append(jnp.dot((e / denom).astype(BF16), vg, preferred_element_type=F32))
        o_ref[0, j * WINDOW:(j + 1) * WINDOW] = jnp.concatenate(outs, axis=-1).astype(o_ref.dtype)


def _swa(u_swa, rel_bias, gq, gk, sinks):
    b, s, _ = u_swa.shape
    rows = SWA_BLOCKS_PER_STEP * WINDOW
    kcol = SWA_WIDTH // LANES
    vcol = kcol + 1
    prev = lambda bi, n, *_: (bi, jnp.maximum(n * SWA_BLOCKS_PER_STEP - 1, 0))
    grp = SWA_HEADS // SWA_KV_HEADS

    def head_blocks(width):
        head = jnp.arange(width) // SWA_HEAD_DIM
        return (head[:, None] == head[None, :]).astype(BF16)

    return pl.pallas_call(
        _swa_kernel,
        grid_spec=pltpu.PrefetchScalarGridSpec(
            num_scalar_prefetch=1,
            grid=(b, s // rows),
            in_specs=[
                pl.BlockSpec((1, rows, SWA_WIDTH), lambda bi, n, *_: (bi, n, 0)),
                pl.BlockSpec((1, WINDOW, LANES), lambda bi, n, *_: prev(bi, n) + (kcol,)),
                pl.BlockSpec((1, rows, LANES), lambda bi, n, *_: (bi, n, kcol)),
                pl.BlockSpec((1, WINDOW, LANES), lambda bi, n, *_: prev(bi, n) + (vcol,)),
                pl.BlockSpec((1, rows, LANES), lambda bi, n, *_: (bi, n, vcol)),
                pl.BlockSpec((SWA_HEADS, WINDOW, 2 * WINDOW), lambda bi, n, *_: (0, 0, 0)),
                pl.BlockSpec((1, SWA_WIDTH), lambda bi, n, *_: (0, 0)),
                pl.BlockSpec((1, SWA_KV_WIDTH), lambda bi, n, *_: (0, 0)),
                pl.BlockSpec((SWA_WIDTH, SWA_WIDTH), lambda bi, n, *_: (0, 0)),
                pl.BlockSpec((SWA_KV_WIDTH, SWA_KV_WIDTH), lambda bi, n, *_: (0, 0)),
            ],
            out_specs=pl.BlockSpec((1, rows, SWA_WIDTH), lambda bi, n, *_: (bi, n, 0)),
        ),
        out_shape=jax.ShapeDtypeStruct((b, s, SWA_WIDTH), BF16),
        compiler_params=_params("parallel", "parallel"),
        name="swa_attention",
    )(sinks, u_swa, u_swa, u_swa, u_swa, u_swa, rel_bias,
      jnp.tile(gq * (SWA_HEAD_DIM ** -0.5), SWA_HEADS).reshape(1, SWA_WIDTH),
      jnp.tile(gk, SWA_KV_HEADS).reshape(1, SWA_KV_WIDTH), head_blocks(SWA_WIDTH), head_blocks(SWA_KV_WIDTH))


def _hgrn_level_matrix(c):
    t = jnp.arange(c)[:, None]
    r = jnp.arange(c)[None, :]
    mats = [(r <= t)]
    half = c // 2
    while half >= 1:
        mid = (t // (2 * half)) * (2 * half) + half
        is_q = (t & half) != 0
        if half < SUBLANES:
            mats.append(jnp.where(is_q, (r >= mid) & (r <= t), (r > t) & (r < mid)))
        half //= 2
    return jnp.concatenate(mats, axis=0).astype(BF16)


def _hgrn_pair_level(c):
    t = jnp.arange(c, dtype=I32)[:, None]
    s = jnp.arange(c, dtype=I32)[None, :]
    x = t ^ s
    lvl = jnp.zeros((c, c), I32)
    half = c // 2
    while half >= 1:
        lvl = jnp.where((x & (-half)) == half, half, lvl)
        half //= 2
    return jnp.where(t > s, lvl, 0)


def _hgrn_kernel(q_ref, f_ref, v_ref, gate_ref, par_ref, amat_ref, lvl_ref, o_ref, st_ref, *, chunk, n_sub):
    @pl.when(pl.program_id(2) == 0)
    def _():
        st_ref[...] = jnp.zeros_like(st_ref)

    c = chunk
    par = par_ref[0]
    log_lb, log1m_lb, one_m_lb, g_out = par[0:1], par[1:2], par[2:3], par[3:4]
    row = lax.broadcasted_iota(I32, (c, 1), 0)
    pair_level = lvl_ref[...]
    st = st_ref[...]
    for sub in range(n_sub):
        rows = pl.ds(sub * c, c)
        q = q_ref[0, rows].astype(F32)
        f = f_ref[0, rows].astype(F32)
        v = v_ref[0, rows].astype(F32)
        gate = gate_ref[0, rows].astype(F32)

        e = jnp.exp(-jnp.abs(f))
        log_sig = jnp.minimum(f, 0.0) - jnp.log(1.0 + e)
        bb = log1m_lb + log_sig
        log_f = jnp.maximum(log_lb, bb) + jnp.log(1.0 + jnp.exp(-jnp.abs(log_lb - bb)))
        log_f = log_f * LOG2_E
        kk = one_m_lb * jnp.where(f < 0.0, 1.0, e) / (1.0 + e)

        g_hi = log_f.astype(BF16)
        g_lo = (log_f - g_hi.astype(F32)).astype(BF16)
        e2 = jnp.dot(amat_ref[...], jnp.concatenate([g_hi, g_lo], axis=1), preferred_element_type=F32)
        expo = e2[:, :HG_DK] + e2[:, HG_DK:]
        bcum = expo[0:c]
        btot = bcum[c - 1:c]

        attn = jnp.zeros((c, c), F32)
        half = c // 2
        lvl = 1
        while half >= 1:
            is_q = (row & half) != 0
            seg = 2 * half
            if half >= SUBLANES:
                ref = jnp.concatenate([jnp.broadcast_to(bcum[a + half - 1:a + half], (seg, HG_DK))
                                       for a in range(0, c, seg)], axis=0)
                diff = bcum - ref
                w = jnp.exp2(jnp.where(is_q, diff, -diff))
            else:
                w = jnp.exp2(expo[lvl * c:(lvl + 1) * c])
                lvl += 1
            zf = jnp.where(is_q, q, kk) * w
            z = zf.astype(BF16)
            if half >= SUBLANES:
                zq = jnp.concatenate([zf[a + half:a + seg] for a in range(0, c, seg)], axis=0).astype(BF16)
                aq = lax.dot_general(zq, z, NT_DIMS, preferred_element_type=F32)
                blank = jnp.zeros((half, c), F32)
                a = jnp.concatenate([blk for j in range(c // seg)
                                     for blk in (blank, aq[j * half:(j + 1) * half])], axis=0)
            else:
                a = lax.dot_general(z, z, NT_DIMS, preferred_element_type=F32)
            attn = jnp.where(pair_level == half, a, attn)
            half //= 2

        vb = v.astype(BF16)
        diag = jnp.sum(q * kk, axis=-1, keepdims=True)
        intra = jnp.dot(attn.astype(BF16), vb, preferred_element_type=F32) + diag * v
        qe = (q * jnp.exp2(bcum)).astype(BF16)
        inter = lax.dot_general(qe, st.astype(BF16), NT_DIMS, preferred_element_type=F32)
        kd = (kk * jnp.exp2(btot - bcum)).astype(BF16)
        st = st * jnp.exp2(btot) +lax.dot_general(vb, kd, TN_DIMS, preferred_element_type=F32)

        o = _rms(inter + intra, g_out)
        o_ref[0, rows] = (o * gate / (1.0 + jnp.exp(-gate))).astype(o_ref.dtype)
    st_ref[...] = st


def _hgrn(u_hg, par, *, chunk, n_sub):
    b, s, _ = u_hg.shape
    amat = _hgrn_level_matrix(chunk)
    pair_level = _hgrn_pair_level(chunk)
    step = chunk * n_sub
    col = lambda off: (lambda bi, h, c: (bi, c, off + h))
    return pl.pallas_call(
        functools.partial(_hgrn_kernel, chunk=chunk, n_sub=n_sub),
        grid=(b, HG_HEADS, s // step),
        in_specs=[pl.BlockSpec((1, step, HG_DK), col(0)),
                  pl.BlockSpec((1, step, HG_DK), col(HG_HEADS)),
                  pl.BlockSpec((1, step, HG_DV), col(2 * HG_HEADS)),
                  pl.BlockSpec((1, step, HG_DV), col(3 * HG_HEADS)),
                  pl.BlockSpec((1, 8, HG_DK), lambda bi, h, c: (h, 0, 0)),
                  pl.BlockSpec(amat.shape, lambda bi, h, c: (0, 0)),
                  pl.BlockSpec(pair_level.shape, lambda bi, h, c: (0, 0))],
        out_specs=pl.BlockSpec((1, step, HG_DV), lambda bi, h, c: (bi, c, h)),
        out_shape=jax.ShapeDtypeStruct((b, s, HG_WIDTH), BF16),
        scratch_shapes=[pltpu.VMEM((HG_DV, HG_DK), F32)],
        compiler_params=_params("parallel", "parallel", "arbitrary"),
        name="hgrn2",
    )(u_hg, u_hg, u_hg, u_hg, par, amat, pair_level)


def _mla_prep_kernel(u_ref, tab_ref, wq_ref, wkv_ref, gcq_ref, gckv_ref, gqn_ref, gqr_ref, gkn_ref, gkr_ref,
                     q_ref, k_ref, v_ref):
    u = u_ref[0].astype(F32)
    cq = _rms(u[:, :MLA_Q_RANK], gcq_ref[...]).astype(BF16)
    ckv = _rms(u[:, MLA_Q_RANK:MLA_Q_RANK + MLA_KV_RANK], gckv_ref[...]).astype(BF16)
    kr = u[:, MLA_Q_RANK + MLA_KV_RANK:]
    qf = jnp.dot(cq, wq_ref[...], preferred_element_type=F32)
    kvf = jnp.dot(ckv, wkv_ref[...], preferred_element_type=F32)
    tab = tab_ref[...]
    low = lax.broadcasted_iota(I32, kr.shape, 1) < MLA_ROPE
    kr_sq = jnp.sum(jnp.where(low, kr * kr, 0.0), axis=-1, keepdims=True)
    scale = MLA_QK ** -0.5 * LOG2_E
    ones = jnp.ones((u.shape[0], MLA_V), F32)
    for h in range(MLA_HEADS):
        qn = qf[:, h * MLA_QK_PAD:h * MLA_QK_PAD + MLA_NOPE]
        qr = qf[:, h * MLA_QK_PAD + MLA_NOPE:(h + 1) * MLA_QK_PAD]
        ss = jnp.sum(qn * qn, axis=-1, keepdims=True) + jnp.sum(jnp.where(low, qr * qr, 0.0), axis=-1, keepdims=True)
        rstd = lax.rsqrt(ss / MLA_QK + EPS) * scale
        t = qr * rstd * tab * gqr_ref[...]
        rope = jnp.where(low, t + pltpu.roll(t, MLA_ROPE, 1), 0.0)
        q_ref[0, h] = jnp.concatenate([qn * rstd * gqn_ref[...], rope], axis=-1).astype(q_ref.dtype)

        kn = kvf[:, h * MLA_NOPE:(h + 1) * MLA_NOPE]
        ss = jnp.sum(kn * kn, axis=-1, keepdims=True) + kr_sq
        rstd = lax.rsqrt(ss / MLA_QK + EPS)
        t = kr * rstd * tab * gkr_ref[...]
        rope = t + pltpu.roll(t, MLA_ROPE, 1)
        k_ref[0, h] = jnp.concatenate([kn * rstd * gkn_ref[...], rope], axis=-1).astype(k_ref.dtype)
        vh = kvf[:, MLA_HEADS * MLA_NOPE + h * MLA_V:MLA_HEADS * MLA_NOPE + (h + 1) * MLA_V]
        v_ref[0, h] = jnp.concatenate([vh, ones], axis=-1).astype(v_ref.dtype)


def _swap_halves(a):
    half = a.shape[-1] // 2
    return jnp.concatenate([a[..., half:], a[..., :half]], axis=-1)


def _mla_prep(u_mla, w_uq, w_ukv, g_cq, g_ckv, gq, gk, *, tm):
    b, s, wu = u_mla.shape
    wq = w_uq.reshape(MLA_Q_RANK, MLA_HEADS, MLA_QK)
    wq = jnp.concatenate([wq, _swap_halves(wq[..., MLA_NOPE:])], axis=-1).reshape(MLA_Q_RANK, MLA_HEADS * MLA_QK_PAD)
    wkv = w_ukv.reshape(MLA_KV_RANK, MLA_HEADS, MLA_NOPE + MLA_V)
    wkv = jnp.concatenate([wkv[..., :MLA_NOPE].reshape(MLA_KV_RANK, -1), wkv[..., MLA_NOPE:].reshape(MLA_KV_RANK, -1)], axis=1)
    half = MLA_ROPE // 2
    inv_freq = ROPE_THETA ** (-jnp.arange(half, dtype=F32) / half)
    ang = jnp.arange(s, dtype=F32)[:, None] * inv_freq[None, :]
    cos, sin = jnp.cos(ang), jnp.sin(ang)
    tab = jnp.concatenate([cos, cos, -sin, sin], axis=-1)
    rope_gain = lambda g: jnp.concatenate([g[MLA_NOPE:], _swap_halves(g[MLA_NOPE:])]).reshape(1, 2 * MLA_ROPE)
    vec = lambda i, j: (0, 0)
    head_out = lambda width: pl.BlockSpec((1, MLA_HEADS, tm, width), lambda i, j: (i, 0, j, 0))
    return pl.pallas_call(
        _mla_prep_kernel,
        grid=(b, s // tm),
        in_specs=[pl.BlockSpec((1, tm, wu), lambda i, j: (i, j, 0)),
                  pl.BlockSpec((tm, 2 * MLA_ROPE), lambda i, j: (j, 0)),
                  pl.BlockSpec(wq.shape, vec),
                  pl.BlockSpec(wkv.shape, vec),
                  pl.BlockSpec((1, MLA_Q_RANK), vec),
                  pl.BlockSpec((1, MLA_KV_RANK), vec),
                  pl.BlockSpec((1, MLA_NOPE), vec),
                  pl.BlockSpec((1, 2 * MLA_ROPE), vec),
                  pl.BlockSpec((1, MLA_NOPE), vec),
                  pl.BlockSpec((1, 2 * MLA_ROPE), vec)],
        out_specs=[head_out(MLA_QK_PAD), head_out(MLA_QK_PAD), head_out(2 * MLA_V)],
        out_shape=[jax.ShapeDtypeStruct((b, MLA_HEADS, s, MLA_QK_PAD), BF16),
                   jax.ShapeDtypeStruct((b, MLA_HEADS, s, MLA_QK_PAD), BF16),
                   jax.ShapeDtypeStruct((b, MLA_HEADS, s, 2 * MLA_V), BF16)],
        compiler_params=_params("parallel", "parallel"),
        name="mla_prep",
    )(u_mla, tab, wq.astype(BF16), wkv.astype(BF16), g_cq.reshape(1, -1), g_ckv.reshape(1, -1),
      gq[:MLA_NOPE].reshape(1, -1), rope_gain(gq), gk[:MLA_NOPE].reshape(1, -1), rope_gain(gk))


def _mla_attn_kernel(qi_ref, ki_ref, q_ref, k_ref, v_ref, o_ref, m_ref, acc_ref, *, tq, tk):
    step = pl.program_id(2)
    qi = qi_ref[step]
    ki = ki_ref[step]

    @pl.when(ki == 0)
    def _():
        m_ref[...] = jnp.full_like(m_ref, NEG_INF)
        acc_ref[...] = jnp.zeros_like(acc_ref)

    def update(rel):
        for r0, k0 in [(r, k) for r in range(0, tq, MLA_ROW_GROUP) for k in range(0, tk, MLA_KEY_GROUP)]:
            first_key = None if rel is None else rel + k0
            if first_key is not None and first_key > r0 + MLA_ROW_GROUP - 1:
                continue
            rows = pl.ds(r0, MLA_ROW_GROUP)
            keys = pl.ds(k0, MLA_KEY_GROUP)
            s = lax.dot_general(q_ref[0, 0, rows], k_ref[0, 0, keys], NT_DIMS, preferred_element_type=F32)
            if first_key is not None and first_key + MLA_KEY_GROUP - 1 > r0:
                row = r0 + lax.broadcasted_iota(I32, s.shape, 0)
                col = first_key + lax.broadcasted_iota(I32, s.shape, 1)
                s = jnp.where(col <= row, s, NEG_INF)
            m_prev = m_ref[rows]
            m_next = jnp.maximum(m_prev, jnp.max(s, axis=-1, keepdims=True))
            alpha = jnp.exp2(m_prev - m_next)
            p = jnp.exp2(s - m_next[:, :1]).astype(BF16)
            pv = jnp.dot(p, v_ref[0, 0, keys], preferred_element_type=F32)
            acc_ref[rows, :MLA_V] = alpha * acc_ref[rows, :MLA_V] + pv[:, :MLA_V]
            acc_ref[rows, MLA_V:] = alpha * acc_ref[rows, MLA_V:] + pv[:, MLA_V:]
            m_ref[rows] = m_next

    key_offset = ki * tk - qi * tq
    pl.when(key_offset < 0)(lambda: update(None))
    for rel in range(0, tq, tk):
        pl.when(key_offset == rel)(functools.partial(update, rel))

    @pl.when((ki + 1) * tk == (qi + 1) * tq)
    def _():
        o_ref[0] = (acc_ref[:, :MLA_V] / acc_ref[:, MLA_V:]).astype(o_ref.dtype)


def _mla_attention(q, k, v, *, tq, tk):
    b, h, s, _ = q.shape
    pairs = [(qi, ki) for qi in range(s // tq) for ki in range((qi + 1) * tq // tk)]
    qi_of = jnp.array([p[0] for p in pairs], I32)
    ki_of = jnp.array([p[1] for p in pairs], I32)
    kv_idx = lambda bi, hi, t, qi_ref, ki_ref: (bi, hi, ki_ref[t], 0)
    return pl.pallas_call(
        functools.partial(_mla_attn_kernel, tq=tq, tk=tk),
        grid_spec=pltpu.PrefetchScalarGridSpec(
            num_scalar_prefetch=2,
            grid=(b, h, len(pairs)),
            in_specs=[pl.BlockSpec((1, 1, tq, MLA_QK_PAD), lambda bi, hi, t, qi_ref, ki_ref: (bi, hi, qi_ref[t], 0)),
                      pl.BlockSpec((1, 1, tk, MLA_QK_PAD), kv_idx),
                      pl.BlockSpec((1, 1, tk, 2 * MLA_V), kv_idx)],
            out_specs=pl.BlockSpec((1, tq, MLA_V), lambda bi, hi, t, qi_ref, ki_ref: (bi, qi_ref[t], hi)),
            scratch_shapes=[pltpu.VMEM((tq, MLA_V), F32), pltpu.VMEM((tq, 2 * MLA_V), F32)],
        ),
        out_shape=jax.ShapeDtypeStruct((b, s, h * MLA_V), BF16),
        compiler_params=_params("parallel", "parallel", "arbitrary"),
        name="mla_attention",
    )(qi_of, ki_of, q, k, v)


def _mem_attn_body(x, kv, g_ref, wq_ref, wo_ref, gq_ref, gk_ref):
    hn = _rms(x, g_ref[...]).astype(BF16)
    qf = jnp.dot(hn, wq_ref[...], preferred_element_type=F32)
    outs = []
    for h in range(MEM_HEADS):
        lo = h * MEM_HEAD_DIM
        qh = (_rms(qf[:, lo:lo + MEM_HEAD_DIM], gq_ref[...]) * (MEM_HEAD_DIM ** -0.5)).astype(BF16)
        kh = _rms(kv[:, lo:lo + MEM_HEAD_DIM], gk_ref[...]).astype(BF16)
        vh = kv[:, MEM_WIDTH + lo:MEM_WIDTH + lo + MEM_HEAD_DIM].astype(BF16)
        s = lax.dot_general(qh, kh, NT_DIMS, preferred_element_type=F32)
        e = jnp.exp(s - jnp.max(s, axis=-1, keepdims=True))
        p = (e / jnp.sum(e, axis=-1, keepdims=True)).astype(BF16)
        outs.append(jnp.dot(p, vh, preferred_element_type=F32))
    o = jnp.concatenate(outs, axis=-1).astype(BF16)
    return x + jnp.dot(o, wo_ref[...], preferred_element_type=F32)


def _split_bf16(a):
    hi = a.astype(BF16)
    return hi, (a - hi.astype(F32)).astype(BF16)


def _router_body(x, g_ref, whi_ref, wlo_ref, b_ref, ids_ref, gates_ref, cnt_ref, carry_ref):
    hn = _rms(x, g_ref[...])
    hi, lo = _split_bf16(hn)
    logits = (jnp.dot(hi, whi_ref[...], preferred_element_type=F32)
              + jnp.dot(hi, wlo_ref[...], preferred_element_type=F32)
              + jnp.dot(lo, whi_ref[...], preferred_element_type=F32)) + b_ref[...]
    lane = lax.broadcasted_iota(I32, logits.shape, 1)
    gl = jnp.where(lane < N_GROUPS, logits, NEG_INF)
    gmax = jnp.max(gl, axis=-1, keepdims=True)
    p_grp = 1.0 / jnp.sum(jnp.exp(gl - gmax), axis=-1, keepdims=True)
    grp = jnp.min(jnp.where(gl == gmax, lane, LANES), axis=-1, keepdims=True)
    in_grp = (lane >= N_GROUPS) & (lane < N_GROUPS + N_EXPERTS) & (((lane - N_GROUPS) // EXPERTS_PER_GROUP) == grp)
    el = jnp.where(in_grp, logits, NEG_INF)
    m1 = jnp.max(el, axis=-1, keepdims=True)
    i1 = jnp.min(jnp.where(el == m1, lane, LANES), axis=-1, keepdims=True)
    el2 = jnp.where(lane == i1, NEG_INF, el)
    m2 = jnp.max(el2, axis=-1, keepdims=True)
    i2 = jnp.min(jnp.where(el2 == m2, lane, LANES), axis=-1, keepdims=True)
    r = jnp.exp(m2 - m1)
    g1 = p_grp / (1.0 + r)
    gates_ref[...] = jnp.where(lane == 0, g1, jnp.where(lane == 1, g1 * r, 0.0))

    tm = logits.shape[0]
    used = jnp.where((lane == i1) | (lane == i2), 1.0, 0.0)
    earlier = lax.broadcasted_iota(I32, (tm, tm), 1) < lax.broadcasted_iota(I32, (tm, tm), 0)
    before = carry_ref[...] + jnp.dot(jnp.where(earlier, 1.0, 0.0).astype(BF16), used.astype(BF16),
                                      preferred_element_type=F32)
    r1 = jnp.sum(jnp.where(lane == i1, before, 0.0), axis=-1, keepdims=True).astype(I32)
    r2 = jnp.sum(jnp.where(lane == i2, before, 0.0), axis=-1, keepdims=True).astype(I32)
    carry_ref[...] += jnp.sum(used, axis=0, keepdims=True)
    cnt_ref[...] = carry_ref[...].astype(I32)
    ids_ref[...] = jnp.where(lane == 0, i1 - N_GROUPS, jnp.where(lane == 1, i2 - N_GROUPS,
                             jnp.where(lane == 2, r1, jnp.where(lane == 3, r2, 0))))


def _post_mixer_kernel(x_ref, a_ref, b_ref, c_ref, wa_ref, wb_ref, wc_ref,
                       kv_ref, gmq_ref, wmq_ref, wmo_ref, mgq_ref, mgk_ref,
                       gffn_ref, whi_ref, wlo_ref, bias_ref,
                       o_ref, ids_ref, gates_ref, cnt_ref, carry_ref):
    @pl.when((pl.program_id(0) == 0) & (pl.program_id(1) == 0))
    def _():
        carry_ref[...] = jnp.zeros_like(carry_ref)

    mix = jnp.dot(a_ref[0], wa_ref[...], preferred_element_type=F32)
    mix += jnp.dot(b_ref[0], wb_ref[...], preferred_element_type=F32)
    mix += jnp.dot(c_ref[0], wc_ref[...], preferred_element_type=F32)
    x = x_ref[0] + mix
    x = _mem_attn_body(x, kv_ref[0], gmq_ref, wmq_ref, wmo_ref, mgq_ref, mgk_ref)
    o_ref[0] = x
    _router_body(x, gffn_ref, whi_ref, wlo_ref, bias_ref, ids_ref, gates_ref, cnt_ref, carry_ref)


def _post_mixer(x, a, bmix, c, w_out, kv, g_mem_q, w_mq, w_mo, mem_gq, mem_gk,
                g_ffn, w_gr, b_gr, w_er, b_er, *, tm):
    b, s, d = x.shape
    m = kv.shape[1]
    nst = s // tm
    wa = w_out[:SWA_WIDTH].astype(BF16)
    wb = w_out[SWA_WIDTH:SWA_WIDTH + HG_WIDTH].astype(BF16)
    wc = w_out[SWA_WIDTH + HG_WIDTH:].astype(BF16)
    pad = LANES - N_GROUPS - N_EXPERTS
    wr = jnp.concatenate([w_gr, w_er, jnp.zeros((d, pad), F32)], axis=1)
    bias = jnp.concatenate([b_gr, b_er, jnp.zeros((pad,), F32)]).reshape(1, LANES)
    whi, wlo = _split_bf16(wr)
    tile = lambda width: pl.BlockSpec((1, tm, width), lambda i, j: (i, j, 0))
    const = lambda shape: pl.BlockSpec(shape, lambda i, j: (0,) * len(shape), pipeline_mode=pl.Buffered(1))
    flat = lambda width: pl.BlockSpec((tm, width), lambda i, j: (i * nst + j, 0))
    return pl.pallas_call(
        _post_mixer_kernel,
        grid=(b, nst),
        in_specs=[tile(d), tile(SWA_WIDTH), tile(HG_WIDTH), tile(MLA_WIDTH),
                  const(wa.shape), const(wb.shape), const(wc.shape),
                  pl.BlockSpec((1, m, 2 * MEM_WIDTH), lambda i, j: (i, 0, 0)),
                  const((1, d)), const((d, MEM_WIDTH)), const((MEM_WIDTH, d)),
                  const((1, MEM_HEAD_DIM)), const((1, MEM_HEAD_DIM)),
                  const((1, d)), const((d, LANES)), const((d, LANES)), const((1, LANES))],
        out_specs=[tile(d), flat(LANES), flat(LANES), pl.BlockSpec((1, LANES), lambda i, j: (0, 0))],
        out_shape=[jax.ShapeDtypeStruct((b, s, d), F32), jax.ShapeDtypeStruct((b * s, LANES), I32),
                   jax.ShapeDtypeStruct((b * s, LANES), F32), jax.ShapeDtypeStruct((1, LANES), I32)],
        scratch_shapes=[pltpu.VMEM((1, LANES), F32)],
        compiler_params=_params("arbitrary", "arbitrary"),
        name="post_mixer",
    )(x, a, bmix, c, wa, wb, wc, kv, g_mem_q.reshape(1, d), w_mq.astype(BF16), w_mo.astype(BF16),
      mem_gq.reshape(1, -1), mem_gk.reshape(1, -1), g_ffn.reshape(1, d), whi, wlo, bias)


HI_HALF_MASK = 0xFFFF0000


def _dispatch_kernel(dest_ref, x_ref, g_ref, xs_in_hbm, xs_hbm, buf, sem, *, ts):
    del xs_in_hbm
    i = pl.program_id(0)
    last = pl.num_programs(0) - 1
    slot = i % 2

    def row_copy(slot_, r, dst_row):
        return pltpu.make_async_copy(buf.at[slot_, pl.ds(r, 1)], xs_hbm.at[pl.ds(dst_row, 1)], sem.at[slot_])

    def wait_step(slot_):
        for _ in range(ts * TOP_K):
            row_copy(slot_, 0, 0).wait()

    @pl.when(i >= 2)
    def _():
        wait_step(slot)

    hn = _rms(x_ref[...], g_ref[...])
    half = hn.shape[1] // 2
    lo = pltpu.bitcast(hn[:, :half].astype(BF16).astype(F32), jnp.uint32) >> 16
    hi = pltpu.bitcast(hn[:, half:].astype(BF16).astype(F32), jnp.uint32) & jnp.uint32(HI_HALF_MASK)
    buf[slot] = lo | hi
    for r in range(ts):
        for kk in range(TOP_K):
            row_copy(slot, r, dest_ref[(i * ts + r) * TOP_K + kk]).start(priority=kk % DMA_PRIORITIES)

    @pl.when(i == last)
    def _():
        wait_step(slot)

    @pl.when((i == last) & (i >= 1))
    def _():
        wait_step(1 - slot)


def _dispatch(x, g_ffn, dest, n_rows, *, ts):
    n, d = x.shape
    xs0 = jnp.zeros((n_rows, d // 2), jnp.uint32)
    return pl.pallas_call(
        functools.partial(_dispatch_kernel, ts=ts),
        grid_spec=pltpu.PrefetchScalarGridSpec(
            num_scalar_prefetch=1,
            grid=(n // ts,),
            in_specs=[pl.BlockSpec((ts, d), lambda i, *_: (i, 0)),
                      pl.BlockSpec((1, d), lambda i, *_: (0, 0)),
                      pl.BlockSpec(memory_space=pl.ANY)],
            out_specs=pl.BlockSpec(memory_space=pl.ANY),
            scratch_shapes=[pltpu.VMEM((2, ts, d // 2), jnp.uint32), pltpu.SemaphoreType.DMA((2,))],
        ),
        out_shape=jax.ShapeDtypeStruct((n_rows, d // 2), jnp.uint32),
        input_output_aliases={3: 0},
        compiler_params=_params("arbitrary"),
        name="moe_dispatch",
    )(dest, x, g_ffn.reshape(1, d), xs0)


def _expert_kernel(bexp_ref, next_ref, nblk_ref, xs_ref, wg_hbm, wu_hbm, wd_hbm, y_ref,
                   wg_f32, wu_f32, wd_f32, sem, wg_bf, wu_bf, wd_bf, *, layer):
    i = pl.program_id(0)
    n_used = nblk_ref[0]
    expert = bexp_ref[i]
    first_of_run = (i == 0) | (expert != bexp_ref[jnp.maximum(i - 1, 0)])
    slot = next_ref[2 * i + 1]

    def weight_copies(e, slot_):
        return [pltpu.make_async_copy(hbm.at[layer, e], buf.at[slot_], sem.at[slot_, j])
                for j, (hbm, buf) in enumerate(((wg_hbm, wg_f32), (wu_hbm, wu_f32), (wd_hbm, wd_f32)))]

    @pl.when(i == 0)
    def _():
        for cp in weight_copies(expert, 0):
            cp.start()

    @pl.when(first_of_run & (i < n_used))
    def _():
        for cp in weight_copies(expert, slot):
            cp.wait()
        next_expert = next_ref[2 * i]

        @pl.when(next_expert >= 0)
        def _():
            for cp in weight_copies(next_expert, 1 - slot):
                cp.start()

        wg_bf[...] = wg_f32[slot].astype(BF16)
        wu_bf[...] = wu_f32[slot].astype(BF16)
        wd_bf[...] = wd_f32[slot].astype(BF16)

    @pl.when(i < n_used)
    def _():
        w = xs_ref[...]
        half = w.shape[1]
        lo = pltpu.bitcast(w << 16, F32).astype(BF16)
        hi = pltpu.bitcast(w & jnp.uint32(HI_HALF_MASK), F32).astype(BF16)
        gate = (jnp.dot(lo, wg_bf[:half], preferred_element_type=F32)
                + jnp.dot(hi, wg_bf[half:], preferred_element_type=F32))
        up = (jnp.dot(lo, wu_bf[:half], preferred_element_type=F32)
              + jnp.dot(hi, wu_bf[half:], preferred_element_type=F32))
        act = (gate / (1.0 + jnp.exp(-gate)) * up).astype(BF16)
        y_ref[...] = jnp.dot(act, wd_bf[...], preferred_element_type=F32)

    @pl.when(i >= nblk_ref[0])
    def _():
        y_ref[...] = jnp.zeros_like(y_ref)


def _experts(xs, block_expert, run_next, n_used, w_gate, w_up, w_down, layer, *, rows):
    d = 2 * xs.shape[1]
    n_blocks = block_expert.shape[0]
    hbm = pl.BlockSpec(memory_space=pl.ANY)
    return pl.pallas_call(
        functools.partial(_expert_kernel, layer=layer),
        grid_spec=pltpu.PrefetchScalarGridSpec(
            num_scalar_prefetch=3,
            grid=(n_blocks,),
            in_specs=[pl.BlockSpec((rows, d // 2), lambda i, bexp, nxt, nblk: (jnp.minimum(i, nblk[0] - 1), 0)),
                      hbm, hbm, hbm],
            out_specs=pl.BlockSpec((rows, d), lambda i, *_: (i, 0)),
            scratch_shapes=[pltpu.VMEM((2, d, D_EXPERT), F32),
                            pltpu.VMEM((2, d, D_EXPERT), F32),
                            pltpu.VMEM((2, D_EXPERT, d), F32),
                            pltpu.SemaphoreType.DMA((2, 3)),
                            pltpu.VMEM((d, D_EXPERT), BF16),
                            pltpu.VMEM((d, D_EXPERT), BF16),
                            pltpu.VMEM((D_EXPERT, d), BF16)],
        ),
        out_shape=jax.ShapeDtypeStruct((n_blocks * rows, d), F32),
        compiler_params=_params("arbitrary"),
        name="moe_experts",
    )(block_expert, run_next, n_used, xs, w_gate, w_up, w_down)


def _combine_kernel(pos_ref, x_ref, gates_ref, y_hbm, o_ref, ybuf, sem, *, tt):
    i = pl.program_id(0)
    nsteps = pl.num_programs(0)

    def row_copy(src_row, slot, dst_row):
        return pltpu.make_async_copy(y_hbm.at[pl.ds(src_row, 1)], ybuf.at[slot, pl.ds(dst_row, 1)], sem.at[slot])

    def start_gather(step, slot):
        for r in range(tt):
            for kk in range(TOP_K):
                row_copy(pos_ref[(step * tt + r) * TOP_K + kk], slot, kk * tt + r).start(priority=kk % DMA_PRIORITIES)

    def wait_gather(slot):
        for r in range(TOP_K * tt):
            row_copy(0, slot, r).wait()

    @pl.when(i == 0)
    def _():
        start_gather(0, 0)

    @pl.when(i + 1 < nsteps)
    def _():
        start_gather(i + 1, (i + 1) % 2)

    wait_gather(i % 2)
    yb = ybuf[i % 2]
    g = gates_ref[...]
    o_ref[...] = x_ref[...] + g[:, 0:1] * yb[:tt] + g[:, 1:2] * yb[tt:]


def _combine(x, gates, pos, y, *, tt):
    n, d = x.shape
    return pl.pallas_call(
        functools.partial(_combine_kernel, tt=tt),
        grid_spec=pltpu.PrefetchScalarGridSpec(
            num_scalar_prefetch=1,
            grid=(n // tt,),
            in_specs=[pl.BlockSpec((tt, d), lambda i, *_: (i, 0)),
                      pl.BlockSpec((tt, LANES), lambda i, *_: (i, 0)),
                      pl.BlockSpec(memory_space=pl.ANY)],
            out_specs=pl.BlockSpec((tt, d), lambda i, *_: (i, 0)),
            scratch_shapes=[pltpu.VMEM((2, TOP_K * tt, d), F32), pltpu.SemaphoreType.DMA((2,))],
        ),
        out_shape=jax.ShapeDtypeStruct((n, d), F32),
        compiler_params=_params("arbitrary"),
        name="moe_combine",
    )(pos, x, gates, y)


def _moe_plan(ids, cnt, *, rows):
    n = ids.shape[0]
    expert, rank = ids[:, :TOP_K], ids[:, TOP_K:2 * TOP_K]
    counts = cnt[0, N_GROUPS:N_GROUPS + N_EXPERTS]
    padded = (counts + rows - 1) // rows * rows
    padded_end = jnp.cumsum(padded)
    seg_start = padded_end - padded
    onehot = expert[..., None] == jnp.arange(N_EXPERTS, dtype=I32)
    dest = (jnp.sum(jnp.where(onehot, seg_start, 0), axis=-1) + rank).astype(I32).reshape(n * TOP_K)
    n_blocks = -(-(n * TOP_K) // rows) + N_EXPERTS
    block_start = jnp.arange(n_blocks, dtype=I32) * rows
    block_expert = jnp.minimum(jnp.sum(padded_end[None, :] <= block_start[:, None], axis=1), N_EXPERTS - 1).astype(I32)
    n_used = (padded_end[-1] // rows).astype(I32)
    block_onehot = block_expert[:, None] == jnp.arange(N_EXPERTS, dtype=I32)
    pick = lambda per_expert: jnp.sum(jnp.where(block_onehot, per_expert, 0), axis=-1)
    next_block = pick(padded_end // rows)
    next_onehot = jnp.minimum(next_block, n_blocks - 1)[:, None] == jnp.arange(n_blocks, dtype=I32)
    next_expert = jnp.where(next_block < n_used, jnp.sum(jnp.where(next_onehot, block_expert, 0), axis=-1), -1)
    nonempty = (counts > 0).astype(I32)
    run_parity = pick(jnp.cumsum(nonempty) - nonempty) % 2
    run_next = jnp.stack([next_expert, run_parity], axis=-1).astype(I32).reshape(2 * n_blocks)
    return dest, block_expert, run_next, n_used.reshape(1)


def _band_relative_bias(table):
    def bucket(nd):
        max_exact = REL_BUCKETS // 2
        nf = jnp.maximum(nd, 1).astype(F32)
        large = max_exact + (jnp.log(nf / max_exact) / math.log(REL_MAX_DIST / max_exact)
                             * (REL_BUCKETS - max_exact)).astype(I32)
        return jnp.where(nd < max_exact, nd, jnp.minimum(large, REL_BUCKETS - 1))

    qi = jnp.arange(WINDOW)[:, None]
    kj = jnp.arange(2 * WINDOW)[None, :]
    dist = jnp.maximum(qi + WINDOW - kj, 0)
    onehot = (bucket(dist)[..., None] == jnp.arange(REL_BUCKETS)).astype(F32)
    return jnp.einsum('qkb,bh->hqk', onehot, table.astype(F32), precision=lax.Precision.HIGHEST)


def _hgrn_params(lb, g_out):
    lb = lb.reshape(HG_HEADS, 1, HG_DK)
    gain = jnp.broadcast_to(g_out.reshape(1, 1, HG_DV), (HG_HEADS, 1, HG_DV))
    rows = [jnp.log(lb), jnp.log1p(-lb), 1.0 - lb, gain, jnp.zeros((HG_HEADS, 4, HG_DK), F32)]
    return jnp.concatenate(rows, axis=1).astype(F32)


def kernel(x, mem, rel_bias_table, hg_lb_logits, g_mix, w_in, swa_gq, swa_gk, swa_sinks, hg_g_out, mla_g_cq, mla_g_ckv, mla_w_uq, mla_w_ukv, mla_gq, mla_gk, w_out, g_mem_q, g_mem_kv, w_mq, w_mkv, mem_gq, mem_gk, w_mo, g_ffn, w_group_router, b_group_router, w_expert_router, b_expert_router, w_gate, w_up, w_down):
    b, s, d = x.shape
    n = b * s
    m = mem.shape[1]
    rel_bias = _band_relative_bias(rel_bias_table)
    lb_all = jnp.cumsum(jax.nn.softmax(hg_lb_logits.astype(F32), axis=0), axis=0)
    lb_all = lb_all - lb_all[:1]
    offs = [0]
    for width in IN_SIZES:
        offs.append(offs[-1] + width)
    o_hg, o_cq, o_kr = offs[3], offs[7], offs[9]

    xf = x.reshape(n, d)
    mem2 = mem.reshape(b * m, d)
    for l in range(DEPTH):
        w = w_in[l]
        kr_cols = w[:, o_kr:]
        w_swa = w[:, :o_hg].astype(BF16)
        w_hg = w[:, o_hg:o_cq].astype(BF16)
        w_mla = jnp.concatenate([w[:, o_cq:], _swap_halves(kr_cols)], axis=1).astype(BF16)
        u_swa = _rms_matmul(xf, g_mix[l], w_swa, tm=1024, tn=w_swa.shape[1], out_dtype=BF16).reshape(b, s, -1)
        u_hg = _rms_matmul(xf, g_mix[l], w_hg, tm=1024, tn=1024, out_dtype=BF16).reshape(b, s, -1)
        u_mla = _rms_matmul(xf, g_mix[l], w_mla, tm=1024, tn=w_mla.shape[1], out_dtype=BF16).reshape(b, s, -1)

        out_a = _swa(u_swa, rel_bias, swa_gq[l], swa_gk[l], swa_sinks[l])
        out_b = _hgrn(u_hg, _hgrn_params(lb_all[l], hg_g_out[l]), chunk=HG_CHUNK, n_sub=HG_SUB)
        qm, km, vm = _mla_prep(u_mla, mla_w_uq[l], mla_w_ukv[l], mla_g_cq[l], mla_g_ckv[l],
                               mla_gq[l], mla_gk[l], tm=512)
        out_c = _mla_attention(qm, km, vm, tq=MLA_TQ, tk=MLA_TK)

        kv = _rms_matmul(mem2, g_mem_kv[l], w_mkv[l].astype(BF16), tm=b * m, tn=2 * MEM_WIDTH)
        x3, ids, gates, cnt = _post_mixer(
            xf.reshape(b, s, d), out_a, out_b, out_c, w_out[l], kv.reshape(b, m, -1),
            g_mem_q[l], w_mq[l], w_mo[l], mem_gq[l], mem_gk[l],
            g_ffn[l], w_group_router[l], b_group_router[l], w_expert_router[l], b_expert_router[l], tm=512)
        xf = x3.reshape(n, d)
        dest, block_expert, run_next, n_used = _moe_plan(ids, cnt, rows=MOE_ROWS)
        xs = _dispatch(xf, g_ffn[l], dest, block_expert.shape[0] * MOE_ROWS, ts=MOE_BURST_TOKENS)
        y = _experts(xs, block_expert, run_next, n_used, w_gate, w_up, w_down, l, rows=MOE_ROWS)
        xf = _combine(xf, gates, dest, y, tt=MOE_BURST_TOKENS)
    return xf.reshape(b, s, d)
```

```python
import functools
import math

import jax
import jax.numpy as jnp
from jax import lax
from jax.experimental import pallas as pl
from jax.experimental.pallas import tpu as pltpu

F32 = jnp.float32
BF16 = jnp.bfloat16
I32 = jnp.int32

D_MODEL = 2048
DEPTH = 2
SWA_HEADS = 8
SWA_KV_HEADS = 2
SWA_HEAD_DIM = 64
WINDOW = 128
HG_HEADS = 8
HG_DK = 128
HG_DV = 128
MLA_HEADS = 4
MLA_Q_RANK = 512
MLA_KV_RANK = 256
MLA_NOPE = 128
MLA_ROPE = 64
MLA_QK = MLA_NOPE + MLA_ROPE
MLA_V = 128
ROPE_THETA = 10000.0
REL_BUCKETS = 32
REL_MAX_DIST = 128
MEM_HEADS = 4
MEM_HEAD_DIM = 128
MEM_WIDTH = MEM_HEADS * MEM_HEAD_DIM
N_GROUPS = 8
EXPERTS_PER_GROUP = 8
N_EXPERTS = N_GROUPS * EXPERTS_PER_GROUP
TOP_K = 2
D_EXPERT = 512
MOE_BURST_TOKENS = 512
MOE_ROWS = 256
EPS = 1e-6
NEG_INF = -1e30
LOG2_E = math.log2(math.e)

SWA_WIDTH = SWA_HEADS * SWA_HEAD_DIM
SWA_KV_WIDTH = SWA_KV_HEADS * SWA_HEAD_DIM
HG_WIDTH = HG_HEADS * HG_DV
MLA_WIDTH = MLA_HEADS * MLA_V
IN_SIZES = (SWA_WIDTH, SWA_KV_WIDTH, SWA_KV_WIDTH,
            HG_HEADS * HG_DK, HG_HEADS * HG_DK, HG_WIDTH, HG_WIDTH,
            MLA_Q_RANK, MLA_KV_RANK, MLA_ROPE)

LANES = 128
SUBLANES = 8
DMA_PRIORITIES = 2
MLA_QK_PAD = 2 * LANES
VMEM_LIMIT_BYTES = 56 * 1024 * 1024

HG_CHUNK = 128
SWA_BLOCKS_PER_STEP = 4
HG_SUB = 16
MLA_TQ = 2048
MLA_TK = 2048
MLA_KEY_GROUP = 512
MLA_ROW_GROUP = 256
NT_DIMS = (((1,), (1,)), ((), ()))
TN_DIMS = (((0,), (0,)), ((), ()))


def _params(*semantics):
    return pltpu.CompilerParams(dimension_semantics=semantics, vmem_limit_bytes=VMEM_LIMIT_BYTES)


def _rms(x, gain=None):
    y = x * lax.rsqrt(jnp.mean(x * x, axis=-1, keepdims=True) + EPS)
    return y if gain is None else y * gain


def _rms_matmul_kernel(x_ref, g_ref, w_ref, o_ref, hn_ref):
    @pl.when(pl.program_id(1) == 0)
    def _():
        hn_ref[...] = _rms(x_ref[...], g_ref[...]).astype(BF16)

    o_ref[...] = jnp.dot(hn_ref[...], w_ref[...], preferred_element_type=F32).astype(o_ref.dtype)


def _rms_matmul(x, gain, w, *, tm, tn, out_dtype=F32):
    n, d = x.shape
    nout = w.shape[1]
    return pl.pallas_call(
        _rms_matmul_kernel,
        grid=(n // tm, nout // tn),
        in_specs=[pl.BlockSpec((tm, d), lambda i, j: (i, 0)),
                  pl.BlockSpec((1, d), lambda i, j: (0, 0)),
                  pl.BlockSpec((d, tn), lambda i, j: (0, j))],
        out_specs=pl.BlockSpec((tm, tn), lambda i, j: (i, j)),
        out_shape=jax.ShapeDtypeStruct((n, nout), out_dtype),
        scratch_shapes=[pltpu.VMEM((tm, d), BF16)],
        compiler_params=_params("parallel", "arbitrary"),
        name="rms_matmul",
    )(x, gain.reshape(1, d), w)


def _head_rms(x, seg_ref, gain_ref):
    sq = x * x
    hi, lo = _split_bf16(sq)
    ss = (jnp.dot(hi, seg_ref[...], preferred_element_type=F32)
          + jnp.dot(lo, seg_ref[...], preferred_element_type=F32))
    return x * lax.rsqrt(ss * (1.0 / SWA_HEAD_DIM) + EPS) * gain_ref[...]


def _swa_kernel(sink_ref, q_ref, kp_ref, kc_ref, vp_ref, vc_ref, bias_ref, gq_ref, gk_ref, segq_ref, segk_ref, o_ref):
    blk = pl.program_id(1)
    grp = SWA_HEADS // SWA_KV_HEADS
    q_all = _head_rms(q_ref[0].astype(F32), segq_ref, gq_ref)
    k_all = _head_rms(jnp.concatenate([kp_ref[0], kc_ref[0]], axis=0).astype(F32), segk_ref, gk_ref)
    v_all = jnp.concatenate([vp_ref[0], vc_ref[0]], axis=0)
    qi = lax.broadcasted_iota(I32, (WINDOW, 2 * WINDOW), 0)
    kj = lax.broadcasted_iota(I32, (WINDOW, 2 * WINDOW), 1)
    dist = qi + WINDOW - kj
    in_window = (dist >= 0) & (dist < WINDOW)
    first_valid = in_window & (kj >= jnp.where(blk > 0, 0, WINDOW))
    for j in range(q_all.shape[0] // WINDOW):
        q = q_all[j * WINDOW:(j + 1) * WINDOW]
        k = k_all[j * WINDOW:(j + 2) * WINDOW]
        v = v_all[j * WINDOW:(j + 2) * WINDOW]
        valid = first_valid if j == 0 else in_window
        outs = []
        for g in range(SWA_KV_HEADS):
            lo = g * SWA_HEAD_DIM
            kg = k[:, lo:lo + SWA_HEAD_DIM].astype(BF16)
            vg = v[:, lo:lo + SWA_HEAD_DIM].astype(BF16)
            for h in range(g * grp, (g + 1) * grp):
                qh = q[:, h * SWA_HEAD_DIM:(h + 1) * SWA_HEAD_DIM].astype(BF16)
                s = lax.dot_general(qh, kg, NT_DIMS, preferred_element_type=F32) + bias_ref[h]
                s = jnp.where(valid, s, NEG_INF)
                sink = sink_ref[h]
                m = jnp.maximum(jnp.max(s, axis=-1, keepdims=True), sink)
                e = jnp.exp(s - m)
                denom = jnp.sum(e, axis=-1, keepdims=True) + jnp.exp(sink - m)
                outs.append(jnp.dot((e / denom).astype(BF16), vg, preferred_element_type=F32))
        o_ref[0, j * WINDOW:(j + 1) * WINDOW] = jnp.concatenate(outs, axis=-1).astype(o_ref.dtype)


def _swa(u_swa, rel_bias, gq, gk, sinks):
    b, s, _ = u_swa.shape
    rows = SWA_BLOCKS_PER_STEP * WINDOW
    kcol = SWA_WIDTH // LANES
    vcol = kcol + 1
    prev = lambda bi, n, *_: (bi, jnp.maximum(n * SWA_BLOCKS_PER_STEP - 1, 0))
    grp = SWA_HEADS // SWA_KV_HEADS

    def head_blocks(width):
        head = jnp.arange(width) // SWA_HEAD_DIM
        return (head[:, None] == head[None, :]).astype(BF16)

    return pl.pallas_call(
        _swa_kernel,
        grid_spec=pltpu.PrefetchScalarGridSpec(
            num_scalar_prefetch=1,
            grid=(b, s // rows),
            in_specs=[
                pl.BlockSpec((1, rows, SWA_WIDTH), lambda bi, n, *_: (bi, n, 0)),
                pl.BlockSpec((1, WINDOW, LANES), lambda bi, n, *_: prev(bi, n) + (kcol,)),
                pl.BlockSpec((1, rows, LANES), lambda bi, n, *_: (bi, n, kcol)),
                pl.BlockSpec((1, WINDOW, LANES), lambda bi, n, *_: prev(bi, n) + (vcol,)),
                pl.BlockSpec((1, rows, LANES), lambda bi, n, *_: (bi, n, vcol)),
                pl.BlockSpec((SWA_HEADS, WINDOW, 2 * WINDOW), lambda bi, n, *_: (0, 0, 0)),
                pl.BlockSpec((1, SWA_WIDTH), lambda bi, n, *_: (0, 0)),
                pl.BlockSpec((1, SWA_KV_WIDTH), lambda bi, n, *_: (0, 0)),
                pl.BlockSpec((SWA_WIDTH, SWA_WIDTH), lambda bi, n, *_: (0, 0)),
                pl.BlockSpec((SWA_KV_WIDTH, SWA_KV_WIDTH), lambda bi, n, *_: (0, 0)),
            ],
            out_specs=pl.BlockSpec((1, rows, SWA_WIDTH), lambda bi, n, *_: (bi, n, 0)),
        ),
        out_shape=jax.ShapeDtypeStruct((b, s, SWA_WIDTH), BF16),
        compiler_params=_params("parallel", "parallel"),
        name="swa_attention",
    )(sinks, u_swa, u_swa, u_swa, u_swa, u_swa, rel_bias,
      jnp.tile(gq * (SWA_HEAD_DIM ** -0.5), SWA_HEADS).reshape(1, SWA_WIDTH),
      jnp.tile(gk, SWA_KV_HEADS).reshape(1, SWA_KV_WIDTH), head_blocks(SWA_WIDTH), head_blocks(SWA_KV_WIDTH))


def _hgrn_level_matrix(c):
    t = jnp.arange(c)[:, None]
    r = jnp.arange(c)[None, :]
    mats = [(r <= t)]
    half = c // 2
    while half >= 1:
        mid = (t // (2 * half)) * (2 * half) + half
        is_q = (t & half) != 0
        if half < SUBLANES:
            mats.append(jnp.where(is_q, (r >= mid) & (r <= t), (r > t) & (r < mid)))
        half //= 2
    return jnp.concatenate(mats, axis=0).astype(BF16)


def _hgrn_pair_level(c):
    t = jnp.arange(c, dtype=I32)[:, None]
    s = jnp.arange(c, dtype=I32)[None, :]
    x = t ^ s
    lvl = jnp.zeros((c, c), I32)
    half = c // 2
    while half >= 1:
        lvl = jnp.where((x & (-half)) == half, half, lvl)
        half //= 2
    return jnp.where(t > s, lvl, 0)


def _hgrn_kernel(q_ref, f_ref, v_ref, gate_ref, par_ref, amat_ref, lvl_ref, o_ref, st_ref, *, chunk, n_sub):
    @pl.when(pl.program_id(2) == 0)
    def _():
        st_ref[...] = jnp.zeros_like(st_ref)

    c = chunk
    par = par_ref[0]
    log_lb, log1m_lb, one_m_lb, g_out = par[0:1], par[1:2], par[2:3], par[3:4]
    row = lax.broadcasted_iota(I32, (c, 1), 0)
    pair_level = lvl_ref[...]
    st = st_ref[...]
    for sub in range(n_sub):
        rows = pl.ds(sub * c, c)
        q = q_ref[0, rows].astype(F32)
        f = f_ref[0, rows].astype(F32)
        v = v_ref[0, rows].astype(F32)
        gate = gate_ref[0, rows].astype(F32)

        e = jnp.exp(-jnp.abs(f))
        log_sig = jnp.minimum(f, 0.0) - jnp.log(1.0 + e)
        bb = log1m_lb + log_sig
        log_f = jnp.maximum(log_lb, bb) + jnp.log(1.0 + jnp.exp(-jnp.abs(log_lb - bb)))
        log_f = log_f * LOG2_E
        kk = one_m_lb * jnp.where(f < 0.0, 1.0, e) / (1.0 + e)

        g_hi = log_f.astype(BF16)
        g_lo = (log_f - g_hi.astype(F32)).astype(BF16)
        e2 = jnp.dot(amat_ref[...], jnp.concatenate([g_hi, g_lo], axis=1), preferred_element_type=F32)
        expo = e2[:, :HG_DK] + e2[:, HG_DK:]
        bcum = expo[0:c]
        btot = bcum[c - 1:c]

        attn = jnp.zeros((c, c), F32)
        half = c // 2
        lvl = 1
        while half >= 1:
            is_q = (row & half) != 0
            seg = 2 * half
            if half >= SUBLANES:
                ref = jnp.concatenate([jnp.broadcast_to(bcum[a + half - 1:a + half], (seg, HG_DK))
                                       for a in range(0, c, seg)], axis=0)
                diff = bcum - ref
                w = jnp.exp2(jnp.where(is_q, diff, -diff))
            else:
                w = jnp.exp2(expo[lvl * c:(lvl + 1) * c])
                lvl += 1
            zf = jnp.where(is_q, q, kk) * w
            z = zf.astype(BF16)
            if half >= SUBLANES:
                zq = jnp.concatenate([zf[a + half:a + seg] for a in range(0, c, seg)], axis=0).astype(BF16)
                aq = lax.dot_general(zq, z, NT_DIMS, preferred_element_type=F32)
                blank = jnp.zeros((half, c), F32)
                a = jnp.concatenate([blk for j in range(c // seg)
                                     for blk in (blank, aq[j * half:(j + 1) * half])], axis=0)
            else:
                a = lax.dot_general(z, z, NT_DIMS, preferred_element_type=F32)
            attn = jnp.where(pair_level == half, a, attn)
            half //= 2

        vb = v.astype(BF16)
        diag = jnp.sum(q * kk, axis=-1, keepdims=True)
        intra = jnp.dot(attn.astype(BF16), vb, preferred_element_type=F32) + diag * v
        qe = (q * jnp.exp2(bcum)).astype(BF16)
        inter = lax.dot_general(qe, st.astype(BF16), NT_DIMS, preferred_element_type=F32)
        kd = (kk * jnp.exp2(btot - bcum)).astype(BF16)
        st = st * jnp.exp2(btot) +lax.dot_general(vb, kd, TN_DIMS, preferred_element_type=F32)

        o = _rms(inter + intra, g_out)
        o_ref[0, rows] = (o * gate / (1.0 + jnp.exp(-gate))).astype(o_ref.dtype)
    st_ref[...] = st


def _hgrn(u_hg, par, *, chunk, n_sub):
    b, s, _ = u_hg.shape
    amat = _hgrn_level_matrix(chunk)
    pair_level = _hgrn_pair_level(chunk)
    step = chunk * n_sub
    col = lambda off: (lambda bi, h, c: (bi, c, off + h))
    return pl.pallas_call(
        functools.partial(_hgrn_kernel, chunk=chunk, n_sub=n_sub),
        grid=(b, HG_HEADS, s // step),
        in_specs=[pl.BlockSpec((1, step, HG_DK), col(0)),
                  pl.BlockSpec((1, step, HG_DK), col(HG_HEADS)),
                  pl.BlockSpec((1, step, HG_DV), col(2 * HG_HEADS)),
                  pl.BlockSpec((1, step, HG_DV), col(3 * HG_HEADS)),
                  pl.BlockSpec((1, 8, HG_DK), lambda bi, h, c: (h, 0, 0)),
                  pl.BlockSpec(amat.shape, lambda bi, h, c: (0, 0)),
                  pl.BlockSpec(pair_level.shape, lambda bi, h, c: (0, 0))],
        out_specs=pl.BlockSpec((1, step, HG_DV), lambda bi, h, c: (bi, c, h)),
        out_shape=jax.ShapeDtypeStruct((b, s, HG_WIDTH), BF16),
        scratch_shapes=[pltpu.VMEM((HG_DV, HG_DK), F32)],
        compiler_params=_params("parallel", "parallel", "arbitrary"),
        name="hgrn2",
    )(u_hg, u_hg, u_hg, u_hg, par, amat, pair_level)


def _mla_prep_kernel(u_ref, tab_ref, wq_ref, wkv_ref, gcq_ref, gckv_ref, gqn_ref, gqr_ref, gkn_ref, gkr_ref,
                     q_ref, k_ref, v_ref):
    u = u_ref[0].astype(F32)
    cq = _rms(u[:, :MLA_Q_RANK], gcq_ref[...]).astype(BF16)
    ckv = _rms(u[:, MLA_Q_RANK:MLA_Q_RANK + MLA_KV_RANK], gckv_ref[...]).astype(BF16)
    kr = u[:, MLA_Q_RANK + MLA_KV_RANK:]
    qf = jnp.dot(cq, wq_ref[...], preferred_element_type=F32)
    kvf = jnp.dot(ckv, wkv_ref[...], preferred_element_type=F32)
    tab = tab_ref[...]
    low = lax.broadcasted_iota(I32, kr.shape, 1) < MLA_ROPE
    kr_sq = jnp.sum(jnp.where(low, kr * kr, 0.0), axis=-1, keepdims=True)
    scale = MLA_QK ** -0.5 * LOG2_E
    ones = jnp.ones((u.shape[0], MLA_V), F32)
    for h in range(MLA_HEADS):
        qn = qf[:, h * MLA_QK_PAD:h * MLA_QK_PAD + MLA_NOPE]
        qr = qf[:, h * MLA_QK_PAD + MLA_NOPE:(h + 1) * MLA_QK_PAD]
        ss = jnp.sum(qn * qn, axis=-1, keepdims=True) + jnp.sum(jnp.where(low, qr * qr, 0.0), axis=-1, keepdims=True)
        rstd = lax.rsqrt(ss / MLA_QK + EPS) * scale
        t = qr * rstd * tab * gqr_ref[...]
        rope = jnp.where(low, t + pltpu.roll(t, MLA_ROPE, 1), 0.0)
        q_ref[0, h] = jnp.concatenate([qn * rstd * gqn_ref[...], rope], axis=-1).astype(q_ref.dtype)

        kn = kvf[:, h * MLA_NOPE:(h + 1) * MLA_NOPE]
        ss = jnp.sum(kn * kn, axis=-1, keepdims=True) + kr_sq
        rstd = lax.rsqrt(ss / MLA_QK + EPS)
        t = kr * rstd * tab * gkr_ref[...]
        rope = t + pltpu.roll(t, MLA_ROPE, 1)
        k_ref[0, h] = jnp.concatenate([kn * rstd * gkn_ref[...], rope], axis=-1).astype(k_ref.dtype)
        vh = kvf[:, MLA_HEADS * MLA_NOPE + h * MLA_V:MLA_HEADS * MLA_NOPE + (h + 1) * MLA_V]
        v_ref[0, h] = jnp.concatenate([vh, ones], axis=-1).astype(v_ref.dtype)


def _swap_halves(a):
    half = a.shape[-1] // 2
    return jnp.concatenate([a[..., half:], a[..., :half]], axis=-1)


def _mla_prep(u_mla, w_uq, w_ukv, g_cq, g_ckv, gq, gk, *, tm):
    b, s, wu = u_mla.shape
    wq = w_uq.reshape(MLA_Q_RANK, MLA_HEADS, MLA_QK)
    wq = jnp.concatenate([wq, _swap_halves(wq[..., MLA_NOPE:])], axis=-1).reshape(MLA_Q_RANK, MLA_HEADS * MLA_QK_PAD)
    wkv = w_ukv.reshape(MLA_KV_RANK, MLA_HEADS, MLA_NOPE + MLA_V)
    wkv = jnp.concatenate([wkv[..., :MLA_NOPE].reshape(MLA_KV_RANK, -1), wkv[..., MLA_NOPE:].reshape(MLA_KV_RANK, -1)], axis=1)
    half = MLA_ROPE // 2
    inv_freq = ROPE_THETA ** (-jnp.arange(half, dtype=F32) / half)
    ang = jnp.arange(s, dtype=F32)[:, None] * inv_freq[None, :]
    cos, sin = jnp.cos(ang), jnp.sin(ang)
    tab = jnp.concatenate([cos, cos, -sin, sin], axis=-1)
    rope_gain = lambda g: jnp.concatenate([g[MLA_NOPE:], _swap_halves(g[MLA_NOPE:])]).reshape(1, 2 * MLA_ROPE)
    vec = lambda i, j: (0, 0)
    head_out = lambda width: pl.BlockSpec((1, MLA_HEADS, tm, width), lambda i, j: (i, 0, j, 0))
    return pl.pallas_call(
        _mla_prep_kernel,
        grid=(b, s // tm),
        in_specs=[pl.BlockSpec((1, tm, wu), lambda i, j: (i, j, 0)),
                  pl.BlockSpec((tm, 2 * MLA_ROPE), lambda i, j: (j, 0)),
                  pl.BlockSpec(wq.shape, vec),
                  pl.BlockSpec(wkv.shape, vec),
                  pl.BlockSpec((1, MLA_Q_RANK), vec),
                  pl.BlockSpec((1, MLA_KV_RANK), vec),
                  pl.BlockSpec((1, MLA_NOPE), vec),
                  pl.BlockSpec((1, 2 * MLA_ROPE), vec),
                  pl.BlockSpec((1, MLA_NOPE), vec),
                  pl.BlockSpec((1, 2 * MLA_ROPE), vec)],
        out_specs=[head_out(MLA_QK_PAD), head_out(MLA_QK_PAD), head_out(2 * MLA_V)],
        out_shape=[jax.ShapeDtypeStruct((b, MLA_HEADS, s, MLA_QK_PAD), BF16),
                   jax.ShapeDtypeStruct((b, MLA_HEADS, s, MLA_QK_PAD), BF16),
                   jax.ShapeDtypeStruct((b, MLA_HEADS, s, 2 * MLA_V), BF16)],
        compiler_params=_params("parallel", "parallel"),
        name="mla_prep",
    )(u_mla, tab, wq.astype(BF16), wkv.astype(BF16), g_cq.reshape(1, -1), g_ckv.reshape(1, -1),
      gq[:MLA_NOPE].reshape(1, -1), rope_gain(gq), gk[:MLA_NOPE].reshape(1, -1), rope_gain(gk))


def _mla_attn_kernel(qi_ref, ki_ref, q_ref, k_ref, v_ref, o_ref, m_ref, acc_ref, *, tq, tk):
    step = pl.program_id(2)
    qi = qi_ref[step]
    ki = ki_ref[step]

    @pl.when(ki == 0)
    def _():
        m_ref[...] = jnp.full_like(m_ref, NEG_INF)
        acc_ref[...] = jnp.zeros_like(acc_ref)

    def update(rel):
        for r0, k0 in [(r, k) for r in range(0, tq, MLA_ROW_GROUP) for k in range(0, tk, MLA_KEY_GROUP)]:
            first_key = None if rel is None else rel + k0
            if first_key is not None and first_key > r0 + MLA_ROW_GROUP - 1:
                continue
            rows = pl.ds(r0, MLA_ROW_GROUP)
            keys = pl.ds(k0, MLA_KEY_GROUP)
            s = lax.dot_general(q_ref[0, 0, rows], k_ref[0, 0, keys], NT_DIMS, preferred_element_type=F32)
            if first_key is not None and first_key + MLA_KEY_GROUP - 1 > r0:
                row = r0 + lax.broadcasted_iota(I32, s.shape, 0)
                col = first_key + lax.broadcasted_iota(I32, s.shape, 1)
                s = jnp.where(col <= row, s, NEG_INF)
            m_prev = m_ref[rows]
            m_next = jnp.maximum(m_prev, jnp.max(s, axis=-1, keepdims=True))
            alpha = jnp.exp2(m_prev - m_next)
            p = jnp.exp2(s - m_next[:, :1]).astype(BF16)
            pv = jnp.dot(p, v_ref[0, 0, keys], preferred_element_type=F32)
            acc_ref[rows, :MLA_V] = alpha * acc_ref[rows, :MLA_V] + pv[:, :MLA_V]
            acc_ref[rows, MLA_V:] = alpha * acc_ref[rows, MLA_V:] + pv[:, MLA_V:]
            m_ref[rows] = m_next

    key_offset = ki * tk - qi * tq
    pl.when(key_offset < 0)(lambda: update(None))
    for rel in range(0, tq, tk):
        pl.when(key_offset == rel)(functools.partial(update, rel))

    @pl.when((ki + 1) * tk == (qi + 1) * tq)
    def _():
        o_ref[0] = (acc_ref[:, :MLA_V] / acc_ref[:, MLA_V:]).astype(o_ref.dtype)


def _mla_attention(q, k, v, *, tq, tk):
    b, h, s, _ = q.shape
    pairs = [(qi, ki) for qi in range(s // tq) for ki in range((qi + 1) * tq // tk)]
    qi_of = jnp.array([p[0] for p in pairs], I32)
    ki_of = jnp.array([p[1] for p in pairs], I32)
    kv_idx = lambda bi, hi, t, qi_ref, ki_ref: (bi, hi, ki_ref[t], 0)
    return pl.pallas_call(
        functools.partial(_mla_attn_kernel, tq=tq, tk=tk),
        grid_spec=pltpu.PrefetchScalarGridSpec(
            num_scalar_prefetch=2,
            grid=(b, h, len(pairs)),
            in_specs=[pl.BlockSpec((1, 1, tq, MLA_QK_PAD), lambda bi, hi, t, qi_ref, ki_ref: (bi, hi, qi_ref[t], 0)),
                      pl.BlockSpec((1, 1, tk, MLA_QK_PAD), kv_idx),
                      pl.BlockSpec((1, 1, tk, 2 * MLA_V), kv_idx)],
            out_specs=pl.BlockSpec((1, tq, MLA_V), lambda bi, hi, t, qi_ref, ki_ref: (bi, qi_ref[t], hi)),
            scratch_shapes=[pltpu.VMEM((tq, MLA_V), F32), pltpu.VMEM((tq, 2 * MLA_V), F32)],
        ),
        out_shape=jax.ShapeDtypeStruct((b, s, h * MLA_V), BF16),
        compiler_params=_params("parallel", "parallel", "arbitrary"),
        name="mla_attention",
    )(qi_of, ki_of, q, k, v)


def _mem_attn_body(x, kv, g_ref, wq_ref, wo_ref, gq_ref, gk_ref):
    hn = _rms(x, g_ref[...]).astype(BF16)
    qf = jnp.dot(hn, wq_ref[...], preferred_element_type=F32)
    outs = []
    for h in range(MEM_HEADS):
        lo = h * MEM_HEAD_DIM
        qh = (_rms(qf[:, lo:lo + MEM_HEAD_DIM], gq_ref[...]) * (MEM_HEAD_DIM ** -0.5)).astype(BF16)
        kh = _rms(kv[:, lo:lo + MEM_HEAD_DIM], gk_ref[...]).astype(BF16)
        vh = kv[:, MEM_WIDTH + lo:MEM_WIDTH + lo + MEM_HEAD_DIM].astype(BF16)
        s = lax.dot_general(qh, kh, NT_DIMS, preferred_element_type=F32)
        e = jnp.exp(s - jnp.max(s, axis=-1, keepdims=True))
        p = (e / jnp.sum(e, axis=-1, keepdims=True)).astype(BF16)
        outs.append(jnp.dot(p, vh, preferred_element_type=F32))
    o = jnp.concatenate(outs, axis=-1).astype(BF16)
    return x + jnp.dot(o, wo_ref[...], preferred_element_type=F32)


def _split_bf16(a):
    hi = a.astype(BF16)
    return hi, (a - hi.astype(F32)).astype(BF16)


def _router_body(x, g_ref, whi_ref, wlo_ref, b_ref, ids_ref, gates_ref, cnt_ref, carry_ref):
    hn = _rms(x, g_ref[...])
    hi, lo = _split_bf16(hn)
    logits = (jnp.dot(hi, whi_ref[...], preferred_element_type=F32)
              + jnp.dot(hi, wlo_ref[...], preferred_element_type=F32)
              + jnp.dot(lo, whi_ref[...], preferred_element_type=F32)) + b_ref[...]
    lane = lax.broadcasted_iota(I32, logits.shape, 1)
    gl = jnp.where(lane < N_GROUPS, logits, NEG_INF)
    gmax = jnp.max(gl, axis=-1, keepdims=True)
    p_grp = 1.0 / jnp.sum(jnp.exp(gl - gmax), axis=-1, keepdims=True)
    grp = jnp.min(jnp.where(gl == gmax, lane, LANES), axis=-1, keepdims=True)
    in_grp = (lane >= N_GROUPS) & (lane < N_GROUPS + N_EXPERTS) & (((lane - N_GROUPS) // EXPERTS_PER_GROUP) == grp)
    el = jnp.where(in_grp, logits, NEG_INF)
    m1 = jnp.max(el, axis=-1, keepdims=True)
    i1 = jnp.min(jnp.where(el == m1, lane, LANES), axis=-1, keepdims=True)
    el2 = jnp.where(lane == i1, NEG_INF, el)
    m2 = jnp.max(el2, axis=-1, keepdims=True)
    i2 = jnp.min(jnp.where(el2 == m2, lane, LANES), axis=-1, keepdims=True)
    r = jnp.exp(m2 - m1)
    g1 = p_grp / (1.0 + r)
    gates_ref[...] = jnp.where(lane == 0, g1, jnp.where(lane == 1, g1 * r, 0.0))

    tm = logits.shape[0]
    used = jnp.where((lane == i1) | (lane == i2), 1.0, 0.0)
    earlier = lax.broadcasted_iota(I32, (tm, tm), 1) < lax.broadcasted_iota(I32, (tm, tm), 0)
    before = carry_ref[...] + jnp.dot(jnp.where(earlier, 1.0, 0.0).astype(BF16), used.astype(BF16),
                                      preferred_element_type=F32)
    r1 = jnp.sum(jnp.where(lane == i1, before, 0.0), axis=-1, keepdims=True).astype(I32)
    r2 = jnp.sum(jnp.where(lane == i2, before, 0.0), axis=-1, keepdims=True).astype(I32)
    carry_ref[...] += jnp.sum(used, axis=0, keepdims=True)
    cnt_ref[...] = carry_ref[...].astype(I32)
    ids_ref[...] = jnp.where(lane == 0, i1 - N_GROUPS, jnp.where(lane == 1, i2 - N_GROUPS,
                             jnp.where(lane == 2, r1, jnp.where(lane == 3, r2, 0))))


def _post_mixer_kernel(x_ref, a_ref, b_ref, c_ref, wa_ref, wb_ref, wc_ref,
                       kv_ref, gmq_ref, wmq_ref, wmo_ref, mgq_ref, mgk_ref,
                       gffn_ref, whi_ref, wlo_ref, bias_ref,
                       o_ref, ids_ref, gates_ref, cnt_ref, carry_ref):
    @pl.when((pl.program_id(0) == 0) & (pl.program_id(1) == 0))
    def _():
        carry_ref[...] = jnp.zeros_like(carry_ref)

    mix = jnp.dot(a_ref[0], wa_ref[...], preferred_element_type=F32)
    mix += jnp.dot(b_ref[0], wb_ref[...], preferred_element_type=F32)
    mix += jnp.dot(c_ref[0], wc_ref[...], preferred_element_type=F32)
    x = x_ref[0] + mix
    x = _mem_attn_body(x, kv_ref[0], gmq_ref, wmq_ref, wmo_ref, mgq_ref, mgk_ref)
    o_ref[0] = x
    _router_body(x, gffn_ref, whi_ref, wlo_ref, bias_ref, ids_ref, gates_ref, cnt_ref, carry_ref)


def _post_mixer(x, a, bmix, c, w_out, kv, g_mem_q, w_mq, w_mo, mem_gq, mem_gk,
                g_ffn, w_gr, b_gr, w_er, b_er, *, tm):
    b, s, d = x.shape
    m = kv.shape[1]
    nst = s // tm
    wa = w_out[:SWA_WIDTH].astype(BF16)
    wb = w_out[SWA_WIDTH:SWA_WIDTH + HG_WIDTH].astype(BF16)
    wc = w_out[SWA_WIDTH + HG_WIDTH:].astype(BF16)
    pad = LANES - N_GROUPS - N_EXPERTS
    wr = jnp.concatenate([w_gr, w_er, jnp.zeros((d, pad), F32)], axis=1)
    bias = jnp.concatenate([b_gr, b_er, jnp.zeros((pad,), F32)]).reshape(1, LANES)
    whi, wlo = _split_bf16(wr)
    tile = lambda width: pl.BlockSpec((1, tm, width), lambda i, j: (i, j, 0))
    const = lambda shape: pl.BlockSpec(shape, lambda i, j: (0,) * len(shape), pipeline_mode=pl.Buffered(1))
    flat = lambda width: pl.BlockSpec((tm, width), lambda i, j: (i * nst + j, 0))
    return pl.pallas_call(
        _post_mixer_kernel,
        grid=(b, nst),
        in_specs=[tile(d), tile(SWA_WIDTH), tile(HG_WIDTH), tile(MLA_WIDTH),
                  const(wa.shape), const(wb.shape), const(wc.shape),
                  pl.BlockSpec((1, m, 2 * MEM_WIDTH), lambda i, j: (i, 0, 0)),
                  const((1, d)), const((d, MEM_WIDTH)), const((MEM_WIDTH, d)),
                  const((1, MEM_HEAD_DIM)), const((1, MEM_HEAD_DIM)),
                  const((1, d)), const((d, LANES)), const((d, LANES)), const((1, LANES))],
        out_specs=[tile(d), flat(LANES), flat(LANES), pl.BlockSpec((1, LANES), lambda i, j: (0, 0))],
        out_shape=[jax.ShapeDtypeStruct((b, s, d), F32), jax.ShapeDtypeStruct((b * s, LANES), I32),
                   jax.ShapeDtypeStruct((b * s, LANES), F32), jax.ShapeDtypeStruct((1, LANES), I32)],
        scratch_shapes=[pltpu.VMEM((1, LANES), F32)],
        compiler_params=_params("arbitrary", "arbitrary"),
        name="post_mixer",
    )(x, a, bmix, c, wa, wb, wc, kv, g_mem_q.reshape(1, d), w_mq.astype(BF16), w_mo.astype(BF16),
      mem_gq.reshape(1, -1), mem_gk.reshape(1, -1), g_ffn.reshape(1, d), whi, wlo, bias)


HI_HALF_MASK = 0xFFFF0000


def _pack_bf16_pairs(a):
    half = a.shape[1] // 2
    lo = pltpu.bitcast(a[:, :half].astype(BF16).astype(F32), jnp.uint32) >> 16
    hi = pltpu.bitcast(a[:, half:].astype(BF16).astype(F32), jnp.uint32) & jnp.uint32(HI_HALF_MASK)
    return lo | hi


def _unpack_bf16_pairs(w):
    return pltpu.bitcast(w << 16, F32), pltpu.bitcast(w & jnp.uint32(HI_HALF_MASK), F32)


def _dispatch_kernel(dest_ref, x_ref, g_ref, xs_in_hbm, xs_hbm, buf, sem, *, ts):
    del xs_in_hbm
    i = pl.program_id(0)
    last = pl.num_programs(0) - 1
    slot = i % 2

    def row_copy(slot_, r, dst_row):
        return pltpu.make_async_copy(buf.at[slot_, pl.ds(r, 1)], xs_hbm.at[pl.ds(dst_row, 1)], sem.at[slot_])

    def wait_step(slot_):
        for _ in range(ts * TOP_K):
            row_copy(slot_, 0, 0).wait()

    @pl.when(i >= 2)
    def _():
        wait_step(slot)

    buf[slot] = _pack_bf16_pairs(_rms(x_ref[...], g_ref[...]))
    for r in range(ts):
        for kk in range(TOP_K):
            row_copy(slot, r, dest_ref[(i * ts + r) * TOP_K + kk]).start(priority=kk % DMA_PRIORITIES)

    @pl.when(i == last)
    def _():
        wait_step(slot)

    @pl.when((i == last) & (i >= 1))
    def _():
        wait_step(1 - slot)


def _dispatch(x, g_ffn, dest, n_rows, *, ts):
    n, d = x.shape
    xs0 = jnp.zeros((n_rows, d // 2), jnp.uint32)
    return pl.pallas_call(
        functools.partial(_dispatch_kernel, ts=ts),
        grid_spec=pltpu.PrefetchScalarGridSpec(
            num_scalar_prefetch=1,
            grid=(n // ts,),
            in_specs=[pl.BlockSpec((ts, d), lambda i, *_: (i, 0)),
                      pl.BlockSpec((1, d), lambda i, *_: (0, 0)),
                      pl.BlockSpec(memory_space=pl.ANY)],
            out_specs=pl.BlockSpec(memory_space=pl.ANY),
            scratch_shapes=[pltpu.VMEM((2, ts, d // 2), jnp.uint32), pltpu.SemaphoreType.DMA((2,))],
        ),
        out_shape=jax.ShapeDtypeStruct((n_rows, d // 2), jnp.uint32),
        input_output_aliases={3: 0},
        compiler_params=_params("arbitrary"),
        name="moe_dispatch",
    )(dest, x, g_ffn.reshape(1, d), xs0)


def _expert_kernel(bexp_ref, next_ref, nblk_ref, xs_ref, wg_hbm, wu_hbm, wd_hbm, y_ref,
                   wg_f32, wu_f32, wd_f32, sem, wg_bf, wu_bf, wd_bf, *, layer):
    i = pl.program_id(0)
    n_used = nblk_ref[0]
    expert = bexp_ref[i]
    first_of_run = (i == 0) | (expert != bexp_ref[jnp.maximum(i - 1, 0)])
    slot = next_ref[2 * i + 1]

    def weight_copies(e, slot_):
        return [pltpu.make_async_copy(hbm.at[layer, e], buf.at[slot_], sem.at[slot_, j])
                for j, (hbm, buf) in enumerate(((wg_hbm, wg_f32), (wu_hbm, wu_f32), (wd_hbm, wd_f32)))]

    @pl.when(i == 0)
    def _():
        for cp in weight_copies(expert, 0):
            cp.start()

    @pl.when(first_of_run & (i < n_used))
    def _():
        for cp in weight_copies(expert, slot):
            cp.wait()
        next_expert = next_ref[2 * i]

        @pl.when(next_expert >= 0)
        def _():
            for cp in weight_copies(next_expert, 1 - slot):
                cp.start()

        wg_bf[...] = wg_f32[slot].astype(BF16)
        wu_bf[...] = wu_f32[slot].astype(BF16)
        wd_bf[...] = wd_f32[slot].astype(BF16)

    @pl.when(i < n_used)
    def _():
        half = xs_ref.shape[1]
        lo, hi = (part.astype(BF16) for part in _unpack_bf16_pairs(xs_ref[...]))
        gate = (jnp.dot(lo, wg_bf[:half], preferred_element_type=F32)
                + jnp.dot(hi, wg_bf[half:], preferred_element_type=F32))
        up = (jnp.dot(lo, wu_bf[:half], preferred_element_type=F32)
              + jnp.dot(hi, wu_bf[half:], preferred_element_type=F32))
        act = (gate / (1.0 + jnp.exp(-gate)) * up).astype(BF16)
        y_ref[...] = _pack_bf16_pairs(jnp.dot(act, wd_bf[...], preferred_element_type=F32))

    @pl.when(i >= nblk_ref[0])
    def _():
        y_ref[...] = jnp.zeros_like(y_ref)


def _experts(xs, block_expert, run_next, n_used, w_gate, w_up, w_down, layer, *, rows):
    d = 2 * xs.shape[1]
    n_blocks = block_expert.shape[0]
    hbm = pl.BlockSpec(memory_space=pl.ANY)
    return pl.pallas_call(
        functools.partial(_expert_kernel, layer=layer),
        grid_spec=pltpu.PrefetchScalarGridSpec(
            num_scalar_prefetch=3,
            grid=(n_blocks,),
            in_specs=[pl.BlockSpec((rows, d // 2), lambda i, bexp, nxt, nblk: (jnp.minimum(i, nblk[0] - 1), 0)),
                      hbm, hbm, hbm],
            out_specs=pl.BlockSpec((rows, d // 2), lambda i, *_: (i, 0)),
            scratch_shapes=[pltpu.VMEM((2, d, D_EXPERT), F32),
                            pltpu.VMEM((2, d, D_EXPERT), F32),
                            pltpu.VMEM((2, D_EXPERT, d), F32),
                            pltpu.SemaphoreType.DMA((2, 3)),
                            pltpu.VMEM((d, D_EXPERT), BF16),
                            pltpu.VMEM((d, D_EXPERT), BF16),
                            pltpu.VMEM((D_EXPERT, d), BF16)],
        ),
        out_shape=jax.ShapeDtypeStruct((n_blocks * rows, d // 2), jnp.uint32),
        compiler_params=_params("arbitrary"),
        name="moe_experts",
    )(block_expert, run_next, n_used, xs, w_gate, w_up, w_down)


def _combine_kernel(pos_ref, x_ref, gates_ref, y_hbm, o_ref, ybuf, sem, *, tt):
    i = pl.program_id(0)
    nsteps = pl.num_programs(0)

    def row_copy(src_row, slot, dst_row):
        return pltpu.make_async_copy(y_hbm.at[pl.ds(src_row, 1)], ybuf.at[slot, pl.ds(dst_row, 1)], sem.at[slot])

    def start_gather(step, slot):
        for r in range(tt):
            for kk in range(TOP_K):
                row_copy(pos_ref[(step * tt + r) * TOP_K + kk], slot, kk * tt + r).start(priority=kk % DMA_PRIORITIES)

    def wait_gather(slot):
        for r in range(TOP_K * tt):
            row_copy(0, slot, r).wait()

    @pl.when(i == 0)
    def _():
        start_gather(0, 0)

    @pl.when(i + 1 < nsteps)
    def _():
        start_gather(i + 1, (i + 1) % 2)

    wait_gather(i % 2)
    g = gates_ref[...]
    half = ybuf.shape[2]
    for cols, y in zip((pl.ds(0, half), pl.ds(half, half)), _unpack_bf16_pairs(ybuf[i % 2])):
        o_ref[:, cols] = x_ref[:, cols] + g[:, 0:1] * y[:tt] + g[:, 1:2] * y[tt:]


def _combine(x, gates, pos, y, *, tt):
    n, d = x.shape
    return pl.pallas_call(
        functools.partial(_combine_kernel, tt=tt),
        grid_spec=pltpu.PrefetchScalarGridSpec(
            num_scalar_prefetch=1,
            grid=(n // tt,),
            in_specs=[pl.BlockSpec((tt, d), lambda i, *_: (i, 0)),
                      pl.BlockSpec((tt, LANES), lambda i, *_: (i, 0)),
                      pl.BlockSpec(memory_space=pl.ANY)],
            out_specs=pl.BlockSpec((tt, d), lambda i, *_: (i, 0)),
            scratch_shapes=[pltpu.VMEM((2, TOP_K * tt, d // 2), jnp.uint32), pltpu.SemaphoreType.DMA((2,))],
        ),
        out_shape=jax.ShapeDtypeStruct((n, d), F32),
        compiler_params=_params("arbitrary"),
        name="moe_combine",
    )(pos, x, gates, y)


def _moe_plan(ids, cnt, *, rows):
    n = ids.shape[0]
    expert, rank = ids[:, :TOP_K], ids[:, TOP_K:2 * TOP_K]
    counts = cnt[0, N_GROUPS:N_GROUPS + N_EXPERTS]
    padded = (counts + rows - 1) // rows * rows
    padded_end = jnp.cumsum(padded)
    seg_start = padded_end - padded
    onehot = expert[..., None] == jnp.arange(N_EXPERTS, dtype=I32)
    dest = (jnp.sum(jnp.where(onehot, seg_start, 0), axis=-1) + rank).astype(I32).reshape(n * TOP_K)
    n_blocks = -(-(n * TOP_K) // rows) + N_EXPERTS
    block_start = jnp.arange(n_blocks, dtype=I32) * rows
    block_expert = jnp.minimum(jnp.sum(padded_end[None, :] <= block_start[:, None], axis=1), N_EXPERTS - 1).astype(I32)
    n_used = (padded_end[-1] // rows).astype(I32)
    block_onehot = block_expert[:, None] == jnp.arange(N_EXPERTS, dtype=I32)
    pick = lambda per_expert: jnp.sum(jnp.where(block_onehot, per_expert, 0), axis=-1)
    next_block = pick(padded_end // rows)
    next_onehot = jnp.minimum(next_block, n_blocks - 1)[:, None] == jnp.arange(n_blocks, dtype=I32)
    next_expert = jnp.where(next_block < n_used, jnp.sum(jnp.where(next_onehot, block_expert, 0), axis=-1), -1)
    nonempty = (counts > 0).astype(I32)
    run_parity = pick(jnp.cumsum(nonempty) - nonempty) % 2
    run_next = jnp.stack([next_expert, run_parity], axis=-1).astype(I32).reshape(2 * n_blocks)
    return dest, block_expert, run_next, n_used.reshape(1)


def _band_relative_bias(table):
    def bucket(nd):
        max_exact = REL_BUCKETS // 2
        nf = jnp.maximum(nd, 1).astype(F32)
        large = max_exact + (jnp.log(nf / max_exact) / math.log(REL_MAX_DIST / max_exact)
                             * (REL_BUCKETS - max_exact)).astype(I32)
        return jnp.where(nd < max_exact, nd, jnp.minimum(large, REL_BUCKETS - 1))

    qi = jnp.arange(WINDOW)[:, None]
    kj = jnp.arange(2 * WINDOW)[None, :]
    dist = jnp.maximum(qi + WINDOW - kj, 0)
    onehot = (bucket(dist)[..., None] == jnp.arange(REL_BUCKETS)).astype(F32)
    return jnp.einsum('qkb,bh->hqk', onehot, table.astype(F32), precision=lax.Precision.HIGHEST)


def _hgrn_params(lb, g_out):
    lb = lb.reshape(HG_HEADS, 1, HG_DK)
    gain = jnp.broadcast_to(g_out.reshape(1, 1, HG_DV), (HG_HEADS, 1, HG_DV))
    rows = [jnp.log(lb), jnp.log1p(-lb), 1.0 - lb, gain, jnp.zeros((HG_HEADS, 4, HG_DK), F32)]
    return jnp.concatenate(rows, axis=1).astype(F32)


def kernel(x, mem, rel_bias_table, hg_lb_logits, g_mix, w_in, swa_gq, swa_gk, swa_sinks, hg_g_out, mla_g_cq, mla_g_ckv, mla_w_uq, mla_w_ukv, mla_gq, mla_gk, w_out, g_mem_q, g_mem_kv, w_mq, w_mkv, mem_gq, mem_gk, w_mo, g_ffn, w_group_router, b_group_router, w_expert_router, b_expert_router, w_gate, w_up, w_down):
    b, s, d = x.shape
    n = b * s
    m = mem.shape[1]
    rel_bias = _band_relative_bias(rel_bias_table)
    lb_all = jnp.cumsum(jax.nn.softmax(hg_lb_logits.astype(F32), axis=0), axis=0)
    lb_all = lb_all - lb_all[:1]
    offs = [0]
    for width in IN_SIZES:
        offs.append(offs[-1] + width)
    o_hg, o_cq, o_kr = offs[3], offs[7], offs[9]

    xf = x.reshape(n, d)
    mem2 = mem.reshape(b * m, d)
    for l in range(DEPTH):
        w = w_in[l]
        kr_cols = w[:, o_kr:]
        w_swa = w[:, :o_hg].astype(BF16)
        w_hg = w[:, o_hg:o_cq].astype(BF16)
        w_mla = jnp.concatenate([w[:, o_cq:], _swap_halves(kr_cols)], axis=1).astype(BF16)
        u_swa = _rms_matmul(xf, g_mix[l], w_swa, tm=1024, tn=w_swa.shape[1], out_dtype=BF16).reshape(b, s, -1)
        u_hg = _rms_matmul(xf, g_mix[l], w_hg, tm=1024, tn=1024, out_dtype=BF16).reshape(b, s, -1)
        u_mla = _rms_matmul(xf, g_mix[l], w_mla, tm=1024, tn=w_mla.shape[1], out_dtype=BF16).reshape(b, s, -1)

        out_a = _swa(u_swa, rel_bias, swa_gq[l], swa_gk[l], swa_sinks[l])
        out_b = _hgrn(u_hg, _hgrn_params(lb_all[l], hg_g_out[l]), chunk=HG_CHUNK, n_sub=HG_SUB)
        qm, km, vm = _mla_prep(u_mla, mla_w_uq[l], mla_w_ukv[l], mla_g_cq[l], mla_g_ckv[l],
                               mla_gq[l], mla_gk[l], tm=512)
        out_c = _mla_attention(qm, km, vm, tq=MLA_TQ, tk=MLA_TK)

        kv = _rms_matmul(mem2, g_mem_kv[l], w_mkv[l].astype(BF16), tm=b * m, tn=2 * MEM_WIDTH)
        x3, ids, gates, cnt = _post_mixer(
            xf.reshape(b, s, d), out_a, out_b, out_c, w_out[l], kv.reshape(b, m, -1),
            g_mem_q[l], w_mq[l], w_mo[l], mem_gq[l], mem_gk[l],
            g_ffn[l], w_group_router[l], b_group_router[l], w_expert_router[l], b_expert_router[l], tm=512)
        xf = x3.reshape(n, d)
        dest, block_expert, run_next, n_used = _moe_plan(ids, cnt, rows=MOE_ROWS)
        xs = _dispatch(xf, g_ffn[l], dest, block_expert.shape[0] * MOE_ROWS, ts=MOE_BURST_TOKENS)
        y = _experts(xs, block_expert, run_next, n_used, w_gate, w_up, w_down, l, rows=MOE_ROWS)
        xf = _combine(xf, gates, dest, y, tt=MOE_BURST_TOKENS)
    return xf.reshape(b, s, d)
```

```python
import functools
import math

import jax
import jax.numpy as jnp
from jax import lax
from jax.experimental import pallas as pl
from jax.experimental.pallas import tpu as pltpu

F32 = jnp.float32
BF16 = jnp.bfloat16
I32 = jnp.int32

D_MODEL = 2048
DEPTH = 2
SWA_HEADS = 8
SWA_KV_HEADS = 2
SWA_HEAD_DIM = 64
WINDOW = 128
HG_HEADS = 8
HG_DK = 128
HG_DV = 128
MLA_HEADS = 4
MLA_Q_RANK = 512
MLA_KV_RANK = 256
MLA_NOPE = 128
MLA_ROPE = 64
MLA_QK = MLA_NOPE + MLA_ROPE
MLA_V = 128
ROPE_THETA = 10000.0
REL_BUCKETS = 32
REL_MAX_DIST = 128
MEM_HEADS = 4
MEM_HEAD_DIM = 128
MEM_WIDTH = MEM_HEADS * MEM_HEAD_DIM
N_GROUPS = 8
EXPERTS_PER_GROUP = 8
N_EXPERTS = N_GROUPS * EXPERTS_PER_GROUP
TOP_K = 2
D_EXPERT = 512
MOE_BURST_TOKENS = 256
MOE_ROWS = 256
EPS = 1e-6
NEG_INF = -1e30
LOG2_E = math.log2(math.e)

SWA_WIDTH = SWA_HEADS * SWA_HEAD_DIM
SWA_KV_WIDTH = SWA_KV_HEADS * SWA_HEAD_DIM
HG_WIDTH = HG_HEADS * HG_DV
MLA_WIDTH = MLA_HEADS * MLA_V
IN_SIZES = (SWA_WIDTH, SWA_KV_WIDTH, SWA_KV_WIDTH,
            HG_HEADS * HG_DK, HG_HEADS * HG_DK, HG_WIDTH, HG_WIDTH,
            MLA_Q_RANK, MLA_KV_RANK, MLA_ROPE)

LANES = 128
SUBLANES = 8
DMA_PRIORITIES = 2
MLA_QK_PAD = 2 * LANES
VMEM_LIMIT_BYTES = 56 * 1024 * 1024

HG_CHUNK = 128
SWA_BLOCKS_PER_STEP = 4
HG_SUB = 16
MLA_TQ = 2048
MLA_TK = 2048
MLA_KEY_GROUP = 512
MLA_ROW_GROUP = 256
NT_DIMS = (((1,), (1,)), ((), ()))
TN_DIMS = (((0,), (0,)), ((), ()))


def _params(*semantics):
    return pltpu.CompilerParams(dimension_semantics=semantics, vmem_limit_bytes=VMEM_LIMIT_BYTES)


def _rms(x, gain=None):
    y = x * lax.rsqrt(jnp.mean(x * x, axis=-1, keepdims=True) + EPS)
    return y if gain is None else y * gain


def _rms_matmul_kernel(x_ref, g_ref, w_ref, o_ref, hn_ref):
    @pl.when(pl.program_id(1) == 0)
    def _():
        hn_ref[...] = _rms(x_ref[...], g_ref[...]).astype(BF16)

    o_ref[...] = jnp.dot(hn_ref[...], w_ref[...], preferred_element_type=F32).astype(o_ref.dtype)


def _rms_matmul(x, gain, w, *, tm, tn, out_dtype=F32):
    n, d = x.shape
    nout = w.shape[1]
    return pl.pallas_call(
        _rms_matmul_kernel,
        grid=(n // tm, nout // tn),
        in_specs=[pl.BlockSpec((tm, d), lambda i, j: (i, 0)),
                  pl.BlockSpec((1, d), lambda i, j: (0, 0)),
                  pl.BlockSpec((d, tn), lambda i, j: (0, j))],
        out_specs=pl.BlockSpec((tm, tn), lambda i, j: (i, j)),
        out_shape=jax.ShapeDtypeStruct((n, nout), out_dtype),
        scratch_shapes=[pltpu.VMEM((tm, d), BF16)],
        compiler_params=_params("parallel", "arbitrary"),
        name="rms_matmul",
    )(x, gain.reshape(1, d), w)


def _head_rms(x, seg_ref, gain_ref):
    sq = x * x
    hi, lo = _split_bf16(sq)
    ss = (jnp.dot(hi, seg_ref[...], preferred_element_type=F32)
          + jnp.dot(lo, seg_ref[...], preferred_element_type=F32))
    return x * lax.rsqrt(ss * (1.0 / SWA_HEAD_DIM) + EPS) * gain_ref[...]


def _swa_kernel(sink_ref, q_ref, kp_ref, kc_ref, vp_ref, vc_ref, bias_ref, gq_ref, gk_ref, segq_ref, segk_ref, o_ref):
    blk = pl.program_id(1)
    grp = SWA_HEADS // SWA_KV_HEADS
    q_all = _head_rms(q_ref[0].astype(F32), segq_ref, gq_ref)
    k_all = _head_rms(jnp.concatenate([kp_ref[0], kc_ref[0]], axis=0).astype(F32), segk_ref, gk_ref)
    v_all = jnp.concatenate([vp_ref[0], vc_ref[0]], axis=0)
    qi = lax.broadcasted_iota(I32, (WINDOW, 2 * WINDOW), 0)
    kj = lax.broadcasted_iota(I32, (WINDOW, 2 * WINDOW), 1)
    dist = qi + WINDOW - kj
    in_window = (dist >= 0) & (dist < WINDOW)
    first_valid = in_window & (kj >= jnp.where(blk > 0, 0, WINDOW))
    for j in range(q_all.shape[0] // WINDOW):
        q = q_all[j * WINDOW:(j + 1) * WINDOW]
        k = k_all[j * WINDOW:(j + 2) * WINDOW]
        v = v_all[j * WINDOW:(j + 2) * WINDOW]
        valid = first_valid if j == 0 else in_window
        outs = []
        for g in range(SWA_KV_HEADS):
            lo = g * SWA_HEAD_DIM
            kg = k[:, lo:lo + SWA_HEAD_DIM].astype(BF16)
            vg = v[:, lo:lo + SWA_HEAD_DIM].astype(BF16)
            for h in range(g * grp, (g + 1) * grp):
                qh = q[:, h * SWA_HEAD_DIM:(h + 1) * SWA_HEAD_DIM].astype(BF16)
                s = lax.dot_general(qh, kg, NT_DIMS, preferred_element_type=F32) + bias_ref[h]
                s = jnp.where(valid, s, NEG_INF)
                sink = sink_ref[h]
                m = jnp.maximum(jnp.max(s, axis=-1, keepdims=True), sink)
                e = jnp.exp(s - m)
                denom = jnp.sum(e, axis=-1, keepdims=True) + jnp.exp(sink - m)
                outs.append(jnp.dot((e / denom).astype(BF16), vg, preferred_element_type=F32))
        o_ref[0, j * WINDOW:(j + 1) * WINDOW] = jnp.concatenate(outs, axis=-1).astype(o_ref.dtype)


def _swa(u_swa, rel_bias, gq, gk, sinks):
    b, s, _ = u_swa.shape
    rows = SWA_BLOCKS_PER_STEP * WINDOW
    kcol = SWA_WIDTH // LANES
    vcol = kcol + 1
    prev = lambda bi, n, *_: (bi, jnp.maximum(n * SWA_BLOCKS_PER_STEP - 1, 0))
    grp = SWA_HEADS // SWA_KV_HEADS

    def head_blocks(width):
        head = jnp.arange(width) // SWA_HEAD_DIM
        return (head[:, None] == head[None, :]).astype(BF16)

    return pl.pallas_call(
        _swa_kernel,
        grid_spec=pltpu.PrefetchScalarGridSpec(
            num_scalar_prefetch=1,
            grid=(b, s // rows),
            in_specs=[
                pl.BlockSpec((1, rows, SWA_WIDTH), lambda bi, n, *_: (bi, n, 0)),
                pl.BlockSpec((1, WINDOW, LANES), lambda bi, n, *_: prev(bi, n) + (kcol,)),
                pl.BlockSpec((1, rows, LANES), lambda bi, n, *_: (bi, n, kcol)),
                pl.BlockSpec((1, WINDOW, LANES), lambda bi, n, *_: prev(bi, n) + (vcol,)),
                pl.BlockSpec((1, rows, LANES), lambda bi, n, *_: (bi, n, vcol)),
                pl.BlockSpec((SWA_HEADS, WINDOW, 2 * WINDOW), lambda bi, n, *_: (0, 0, 0)),
                pl.BlockSpec((1, SWA_WIDTH), lambda bi, n, *_: (0, 0)),
                pl.BlockSpec((1, SWA_KV_WIDTH), lambda bi, n, *_: (0, 0)),
                pl.BlockSpec((SWA_WIDTH, SWA_WIDTH), lambda bi, n, *_: (0, 0)),
                pl.BlockSpec((SWA_KV_WIDTH, SWA_KV_WIDTH), lambda bi, n, *_: (0, 0)),
            ],
            out_specs=pl.BlockSpec((1, rows, SWA_WIDTH), lambda bi, n, *_: (bi, n, 0)),
        ),
        out_shape=jax.ShapeDtypeStruct((b, s, SWA_WIDTH), BF16),
        compiler_params=_params("parallel", "parallel"),
        name="swa_attention",
    )(sinks, u_swa, u_swa, u_swa, u_swa, u_swa, rel_bias,
      jnp.tile(gq * (SWA_HEAD_DIM ** -0.5), SWA_HEADS).reshape(1, SWA_WIDTH),
      jnp.tile(gk, SWA_KV_HEADS).reshape(1, SWA_KV_WIDTH), head_blocks(SWA_WIDTH), head_blocks(SWA_KV_WIDTH))


def _hgrn_level_matrix(c):
    t = jnp.arange(c)[:, None]
    r = jnp.arange(c)[None, :]
    mats = [(r <= t)]
    half = c // 2
    while half >= 1:
        mid = (t // (2 * half)) * (2 * half) + half
        is_q = (t & half) != 0
        if half < SUBLANES:
            mats.append(jnp.where(is_q, (r >= mid) & (r <= t), (r > t) & (r < mid)))
        half //= 2
    return jnp.concatenate(mats, axis=0).astype(BF16)


def _hgrn_pair_level(c):
    t = jnp.arange(c, dtype=I32)[:, None]
    s = jnp.arange(c, dtype=I32)[None, :]
    x = t ^ s
    lvl = jnp.zeros((c, c), I32)
    half = c // 2
    while half >= 1:
        lvl = jnp.where((x & (-half)) == half, half, lvl)
        half //= 2
    return jnp.where(t > s, lvl, 0)


def _hgrn_kernel(q_ref, f_ref, v_ref, gate_ref, par_ref, amat_ref, lvl_ref, o_ref, st_ref, *, chunk, n_sub):
    @pl.when(pl.program_id(2) == 0)
    def _():
        st_ref[...] = jnp.zeros_like(st_ref)

    c = chunk
    par = par_ref[0]
    log_lb, log1m_lb, one_m_lb, g_out = par[0:1], par[1:2], par[2:3], par[3:4]
    row = lax.broadcasted_iota(I32, (c, 1), 0)
    pair_level = lvl_ref[...]
    st = st_ref[...]
    for sub in range(n_sub):
        rows = pl.ds(sub * c, c)
        q = q_ref[0, rows].astype(F32)
        f = f_ref[0, rows].astype(F32)
        v = v_ref[0, rows].astype(F32)
        gate = gate_ref[0, rows].astype(F32)

        e = jnp.exp(-jnp.abs(f))
        log_sig = jnp.minimum(f, 0.0) - jnp.log(1.0 + e)
        bb = log1m_lb + log_sig
        log_f = jnp.maximum(log_lb, bb) + jnp.log(1.0 + jnp.exp(-jnp.abs(log_lb - bb)))
        log_f = log_f * LOG2_E
        kk = one_m_lb * jnp.where(f < 0.0, 1.0, e) / (1.0 + e)

        g_hi = log_f.astype(BF16)
        g_lo = (log_f - g_hi.astype(F32)).astype(BF16)
        e2 = jnp.dot(amat_ref[...], jnp.concatenate([g_hi, g_lo], axis=1), preferred_element_type=F32)
        expo = e2[:, :HG_DK] + e2[:, HG_DK:]
        bcum = expo[0:c]
        btot = bcum[c - 1:c]

        attn = jnp.zeros((c, c), F32)
        half = c // 2
        lvl = 1
        while half >= 1:
            is_q = (row & half) != 0
            seg = 2 * half
            if half >= SUBLANES:
                ref = jnp.concatenate([jnp.broadcast_to(bcum[a + half - 1:a + half], (seg, HG_DK))
                                       for a in range(0, c, seg)], axis=0)
                diff = bcum - ref
                w = jnp.exp2(jnp.where(is_q, diff, -diff))
            else:
                w = jnp.exp2(expo[lvl * c:(lvl + 1) * c])
                lvl += 1
            zf = jnp.where(is_q, q, kk) * w
            z = zf.astype(BF16)
            if half >= SUBLANES:
                zq = jnp.concatenate([zf[a + half:a + seg] for a in range(0, c, seg)], axis=0).astype(BF16)
                aq = lax.dot_general(zq, z, NT_DIMS, preferred_element_type=F32)
                blank = jnp.zeros((half, c), F32)
                a = jnp.concatenate([blk for j in range(c // seg)
                                     for blk in (blank, aq[j * half:(j + 1) * half])], axis=0)
            else:
                a = lax.dot_general(z, z, NT_DIMS, preferred_element_type=F32)
            attn = jnp.where(pair_level == half, a, attn)
            half //= 2

        vb = v.astype(BF16)
        diag = jnp.sum(q * kk, axis=-1, keepdims=True)
        intra = jnp.dot(attn.astype(BF16), vb, preferred_element_type=F32) + diag * v
        qe = (q * jnp.exp2(bcum)).astype(BF16)
        inter = lax.dot_general(qe, st.astype(BF16), NT_DIMS, preferred_element_type=F32)
        kd = (kk * jnp.exp2(btot - bcum)).astype(BF16)
        st = st * jnp.exp2(btot) +lax.dot_general(vb, kd, TN_DIMS, preferred_element_type=F32)

        o = _rms(inter + intra, g_out)
        o_ref[0, rows] = (o * gate / (1.0 + jnp.exp(-gate))).astype(o_ref.dtype)
    st_ref[...] = st


def _hgrn(u_hg, par, *, chunk, n_sub):
    b, s, _ = u_hg.shape
    amat = _hgrn_level_matrix(chunk)
    pair_level = _hgrn_pair_level(chunk)
    step = chunk * n_sub
    col = lambda off: (lambda bi, h, c: (bi, c, off + h))
    return pl.pallas_call(
        functools.partial(_hgrn_kernel, chunk=chunk, n_sub=n_sub),
        grid=(b, HG_HEADS, s // step),
        in_specs=[pl.BlockSpec((1, step, HG_DK), col(0)),
                  pl.BlockSpec((1, step, HG_DK), col(HG_HEADS)),
                  pl.BlockSpec((1, step, HG_DV), col(2 * HG_HEADS)),
                  pl.BlockSpec((1, step, HG_DV), col(3 * HG_HEADS)),
                  pl.BlockSpec((1, 8, HG_DK), lambda bi, h, c: (h, 0, 0)),
                  pl.BlockSpec(amat.shape, lambda bi, h, c: (0, 0)),
                  pl.BlockSpec(pair_level.shape, lambda bi, h, c: (0, 0))],
        out_specs=pl.BlockSpec((1, step, HG_DV), lambda bi, h, c: (bi, c, h)),
        out_shape=jax.ShapeDtypeStruct((b, s, HG_WIDTH), BF16),
        scratch_shapes=[pltpu.VMEM((HG_DV, HG_DK), F32)],
        compiler_params=_params("parallel", "parallel", "arbitrary"),
        name="hgrn2",
    )(u_hg, u_hg, u_hg, u_hg, par, amat, pair_level)


def _mla_prep_kernel(u_ref, tab_ref, wq_ref, wkv_ref, gcq_ref, gckv_ref, gqn_ref, gqr_ref, gkn_ref, gkr_ref,
                     q_ref, k_ref, v_ref):
    u = u_ref[0].astype(F32)
    cq = _rms(u[:, :MLA_Q_RANK], gcq_ref[...]).astype(BF16)
    ckv = _rms(u[:, MLA_Q_RANK:MLA_Q_RANK + MLA_KV_RANK], gckv_ref[...]).astype(BF16)
    kr = u[:, MLA_Q_RANK + MLA_KV_RANK:]
    qf = jnp.dot(cq, wq_ref[...], preferred_element_type=F32)
    kvf = jnp.dot(ckv, wkv_ref[...], preferred_element_type=F32)
    tab = tab_ref[...]
    low = lax.broadcasted_iota(I32, kr.shape, 1) < MLA_ROPE
    kr_sq = jnp.sum(jnp.where(low, kr * kr, 0.0), axis=-1, keepdims=True)
    scale = MLA_QK ** -0.5 * LOG2_E
    ones = jnp.ones((u.shape[0], MLA_V), F32)
    for h in range(MLA_HEADS):
        qn = qf[:, h * MLA_QK_PAD:h * MLA_QK_PAD + MLA_NOPE]
        qr = qf[:, h * MLA_QK_PAD + MLA_NOPE:(h + 1) * MLA_QK_PAD]
        ss = jnp.sum(qn * qn, axis=-1, keepdims=True) + jnp.sum(jnp.where(low, qr * qr, 0.0), axis=-1, keepdims=True)
        rstd = lax.rsqrt(ss / MLA_QK + EPS) * scale
        t = qr * rstd * tab * gqr_ref[...]
        rope = jnp.where(low, t + pltpu.roll(t, MLA_ROPE, 1), 0.0)
        q_ref[0, h] = jnp.concatenate([qn * rstd * gqn_ref[...], rope], axis=-1).astype(q_ref.dtype)

        kn = kvf[:, h * MLA_NOPE:(h + 1) * MLA_NOPE]
        ss = jnp.sum(kn * kn, axis=-1, keepdims=True) + kr_sq
        rstd = lax.rsqrt(ss / MLA_QK + EPS)
        t = kr * rstd * tab * gkr_ref[...]
        rope = t + pltpu.roll(t, MLA_ROPE, 1)
        k_ref[0, h] = jnp.concatenate([kn * rstd * gkn_ref[...], rope], axis=-1).astype(k_ref.dtype)
        vh = kvf[:, MLA_HEADS * MLA_NOPE + h * MLA_V:MLA_HEADS * MLA_NOPE + (h + 1) * MLA_V]
        v_ref[0, h] = jnp.concatenate([vh, ones], axis=-1).astype(v_ref.dtype)


def _swap_halves(a):
    half = a.shape[-1] // 2
    return jnp.concatenate([a[..., half:], a[..., :half]], axis=-1)


def _mla_prep(u_mla, w_uq, w_ukv, g_cq, g_ckv, gq, gk, *, tm):
    b, s, wu = u_mla.shape
    wq = w_uq.reshape(MLA_Q_RANK, MLA_HEADS, MLA_QK)
    wq = jnp.concatenate([wq, _swap_halves(wq[..., MLA_NOPE:])], axis=-1).reshape(MLA_Q_RANK, MLA_HEADS * MLA_QK_PAD)
    wkv = w_ukv.reshape(MLA_KV_RANK, MLA_HEADS, MLA_NOPE + MLA_V)
    wkv = jnp.concatenate([wkv[..., :MLA_NOPE].reshape(MLA_KV_RANK, -1), wkv[..., MLA_NOPE:].reshape(MLA_KV_RANK, -1)], axis=1)
    half = MLA_ROPE // 2
    inv_freq = ROPE_THETA ** (-jnp.arange(half, dtype=F32) / half)
    ang = jnp.arange(s, dtype=F32)[:, None] * inv_freq[None, :]
    cos, sin = jnp.cos(ang), jnp.sin(ang)
    tab = jnp.concatenate([cos, cos, -sin, sin], axis=-1)
    rope_gain = lambda g: jnp.concatenate([g[MLA_NOPE:], _swap_halves(g[MLA_NOPE:])]).reshape(1, 2 * MLA_ROPE)
    vec = lambda i, j: (0, 0)
    head_out = lambda width: pl.BlockSpec((1, MLA_HEADS, tm, width), lambda i, j: (i, 0, j, 0))
    return pl.pallas_call(
        _mla_prep_kernel,
        grid=(b, s // tm),
        in_specs=[pl.BlockSpec((1, tm, wu), lambda i, j: (i, j, 0)),
                  pl.BlockSpec((tm, 2 * MLA_ROPE), lambda i, j: (j, 0)),
                  pl.BlockSpec(wq.shape, vec),
                  pl.BlockSpec(wkv.shape, vec),
                  pl.BlockSpec((1, MLA_Q_RANK), vec),
                  pl.BlockSpec((1, MLA_KV_RANK), vec),
                  pl.BlockSpec((1, MLA_NOPE), vec),
                  pl.BlockSpec((1, 2 * MLA_ROPE), vec),
                  pl.BlockSpec((1, MLA_NOPE), vec),
                  pl.BlockSpec((1, 2 * MLA_ROPE), vec)],
        out_specs=[head_out(MLA_QK_PAD), head_out(MLA_QK_PAD), head_out(2 * MLA_V)],
        out_shape=[jax.ShapeDtypeStruct((b, MLA_HEADS, s, MLA_QK_PAD), BF16),
                   jax.ShapeDtypeStruct((b, MLA_HEADS, s, MLA_QK_PAD), BF16),
                   jax.ShapeDtypeStruct((b, MLA_HEADS, s, 2 * MLA_V), BF16)],
        compiler_params=_params("parallel", "parallel"),
        name="mla_prep",
    )(u_mla, tab, wq.astype(BF16), wkv.astype(BF16), g_cq.reshape(1, -1), g_ckv.reshape(1, -1),
      gq[:MLA_NOPE].reshape(1, -1), rope_gain(gq), gk[:MLA_NOPE].reshape(1, -1), rope_gain(gk))


def _mla_attn_kernel(qi_ref, ki_ref, q_ref, k_ref, v_ref, o_ref, m_ref, acc_ref, *, tq, tk):
    step = pl.program_id(2)
    qi = qi_ref[step]
    ki = ki_ref[step]

    @pl.when(ki == 0)
    def _():
        m_ref[...] = jnp.full_like(m_ref, NEG_INF)
        acc_ref[...] = jnp.zeros_like(acc_ref)

    def update(rel):
        for r0, k0 in [(r, k) for r in range(0, tq, MLA_ROW_GROUP) for k in range(0, tk, MLA_KEY_GROUP)]:
            first_key = None if rel is None else rel + k0
            if first_key is not None and first_key > r0 + MLA_ROW_GROUP - 1:
                continue
            rows = pl.ds(r0, MLA_ROW_GROUP)
            keys = pl.ds(k0, MLA_KEY_GROUP)
            s = lax.dot_general(q_ref[0, 0, rows], k_ref[0, 0, keys], NT_DIMS, preferred_element_type=F32)
            if first_key is not None and first_key + MLA_KEY_GROUP - 1 > r0:
                row = r0 + lax.broadcasted_iota(I32, s.shape, 0)
                col = first_key + lax.broadcasted_iota(I32, s.shape, 1)
                s = jnp.where(col <= row, s, NEG_INF)
            m_prev = m_ref[rows]
            m_next = jnp.maximum(m_prev, jnp.max(s, axis=-1, keepdims=True))
            alpha = jnp.exp2(m_prev - m_next)
            p = jnp.exp2(s - m_next[:, :1]).astype(BF16)
            pv = jnp.dot(p, v_ref[0, 0, keys], preferred_element_type=F32)
            acc_ref[rows, :MLA_V] = alpha * acc_ref[rows, :MLA_V] + pv[:, :MLA_V]
            acc_ref[rows, MLA_V:] = alpha * acc_ref[rows, MLA_V:] + pv[:, MLA_V:]
            m_ref[rows] = m_next

    key_offset = ki * tk - qi * tq
    pl.when(key_offset < 0)(lambda: update(None))
    for rel in range(0, tq, tk):
        pl.when(key_offset == rel)(functools.partial(update, rel))

    @pl.when((ki + 1) * tk == (qi + 1) * tq)
    def _():
        o_ref[0] = (acc_ref[:, :MLA_V] / acc_ref[:, MLA_V:]).astype(o_ref.dtype)


def _mla_attention(q, k, v, *, tq, tk):
    b, h, s, _ = q.shape
    pairs = [(qi, ki) for qi in range(s // tq) for ki in range((qi + 1) * tq // tk)]
    qi_of = jnp.array([p[0] for p in pairs], I32)
    ki_of = jnp.array([p[1] for p in pairs], I32)
    kv_idx = lambda bi, hi, t, qi_ref, ki_ref: (bi, hi, ki_ref[t], 0)
    return pl.pallas_call(
        functools.partial(_mla_attn_kernel, tq=tq, tk=tk),
        grid_spec=pltpu.PrefetchScalarGridSpec(
            num_scalar_prefetch=2,
            grid=(b, h, len(pairs)),
            in_specs=[pl.BlockSpec((1, 1, tq, MLA_QK_PAD), lambda bi, hi, t, qi_ref, ki_ref: (bi, hi, qi_ref[t], 0)),
                      pl.BlockSpec((1, 1, tk, MLA_QK_PAD), kv_idx),
                      pl.BlockSpec((1, 1, tk, 2 * MLA_V), kv_idx)],
            out_specs=pl.BlockSpec((1, tq, MLA_V), lambda bi, hi, t, qi_ref, ki_ref: (bi, qi_ref[t], hi)),
            scratch_shapes=[pltpu.VMEM((tq, MLA_V), F32), pltpu.VMEM((tq, 2 * MLA_V), F32)],
        ),
        out_shape=jax.ShapeDtypeStruct((b, s, h * MLA_V), BF16),
        compiler_params=_params("parallel", "parallel", "arbitrary"),
        name="mla_attention",
    )(qi_of, ki_of, q, k, v)


def _mem_attn_body(x, kv, g_ref, wq_ref, wo_ref, gq_ref, gk_ref):
    hn = _rms(x, g_ref[...]).astype(BF16)
    qf = jnp.dot(hn, wq_ref[...], preferred_element_type=F32)
    outs = []
    for h in range(MEM_HEADS):
        lo = h * MEM_HEAD_DIM
        qh = (_rms(qf[:, lo:lo + MEM_HEAD_DIM], gq_ref[...]) * (MEM_HEAD_DIM ** -0.5)).astype(BF16)
        kh = _rms(kv[:, lo:lo + MEM_HEAD_DIM], gk_ref[...]).astype(BF16)
        vh = kv[:, MEM_WIDTH + lo:MEM_WIDTH + lo + MEM_HEAD_DIM].astype(BF16)
        s = lax.dot_general(qh, kh, NT_DIMS, preferred_element_type=F32)
        e = jnp.exp(s - jnp.max(s, axis=-1, keepdims=True))
        p = (e / jnp.sum(e, axis=-1, keepdims=True)).astype(BF16)
        outs.append(jnp.dot(p, vh, preferred_element_type=F32))
    o = jnp.concatenate(outs, axis=-1).astype(BF16)
    return x + jnp.dot(o, wo_ref[...], preferred_element_type=F32)


def _split_bf16(a):
    hi = a.astype(BF16)
    return hi, (a - hi.astype(F32)).astype(BF16)


def _router_body(x, g_ref, whi_ref, wlo_ref, b_ref, ids_ref, gates_ref, cnt_ref, carry_ref):
    hn = _rms(x, g_ref[...])
    hi, lo = _split_bf16(hn)
    logits = (jnp.dot(hi, whi_ref[...], preferred_element_type=F32)
              + jnp.dot(hi, wlo_ref[...], preferred_element_type=F32)
              + jnp.dot(lo, whi_ref[...], preferred_element_type=F32)) + b_ref[...]
    lane = lax.broadcasted_iota(I32, logits.shape, 1)
    gl = jnp.where(lane < N_GROUPS, logits, NEG_INF)
    gmax = jnp.max(gl, axis=-1, keepdims=True)
    p_grp = 1.0 / jnp.sum(jnp.exp(gl - gmax), axis=-1, keepdims=True)
    grp = jnp.min(jnp.where(gl == gmax, lane, LANES), axis=-1, keepdims=True)
    in_grp = (lane >= N_GROUPS) & (lane < N_GROUPS + N_EXPERTS) & (((lane - N_GROUPS) // EXPERTS_PER_GROUP) == grp)
    el = jnp.where(in_grp, logits, NEG_INF)
    m1 = jnp.max(el, axis=-1, keepdims=True)
    i1 = jnp.min(jnp.where(el == m1, lane, LANES), axis=-1, keepdims=True)
    el2 = jnp.where(lane == i1, NEG_INF, el)
    m2 = jnp.max(el2, axis=-1, keepdims=True)
    i2 = jnp.min(jnp.where(el2 == m2, lane, LANES), axis=-1, keepdims=True)
    r = jnp.exp(m2 - m1)
    g1 = p_grp / (1.0 + r)
    gates_ref[...] = jnp.where(lane == 0, g1, jnp.where(lane == 1, g1 * r, 0.0))

    tm = logits.shape[0]
    used = jnp.where((lane == i1) | (lane == i2), 1.0, 0.0)
    earlier = lax.broadcasted_iota(I32, (tm, tm), 1) < lax.broadcasted_iota(I32, (tm, tm), 0)
    before = carry_ref[...] + jnp.dot(jnp.where(earlier, 1.0, 0.0).astype(BF16), used.astype(BF16),
                                      preferred_element_type=F32)
    r1 = jnp.sum(jnp.where(lane == i1, before, 0.0), axis=-1, keepdims=True).astype(I32)
    r2 = jnp.sum(jnp.where(lane == i2, before, 0.0), axis=-1, keepdims=True).astype(I32)
    carry_ref[...] += jnp.sum(used, axis=0, keepdims=True)
    cnt_ref[...] = carry_ref[...].astype(I32)
    ids_ref[...] = jnp.where(lane == 0, i1 - N_GROUPS, jnp.where(lane == 1, i2 - N_GROUPS,
                             jnp.where(lane == 2, r1, jnp.where(lane == 3, r2, 0))))


def _post_mixer_kernel(x_ref, a_ref, b_ref, c_ref, wa_ref, wb_ref, wc_ref,
                       kv_ref, gmq_ref, wmq_ref, wmo_ref, mgq_ref, mgk_ref,
                       gffn_ref, whi_ref, wlo_ref, bias_ref,
                       o_ref, ids_ref, gates_ref, cnt_ref, carry_ref):
    @pl.when((pl.program_id(0) == 0) & (pl.program_id(1) == 0))
    def _():
        carry_ref[...] = jnp.zeros_like(carry_ref)

    mix = jnp.dot(a_ref[0], wa_ref[...], preferred_element_type=F32)
    mix += jnp.dot(b_ref[0], wb_ref[...], preferred_element_type=F32)
    mix += jnp.dot(c_ref[0], wc_ref[...], preferred_element_type=F32)
    x = x_ref[0] + mix
    x = _mem_attn_body(x, kv_ref[0], gmq_ref, wmq_ref, wmo_ref, mgq_ref, mgk_ref)
    o_ref[0] = x
    _router_body(x, gffn_ref, whi_ref, wlo_ref, bias_ref, ids_ref, gates_ref, cnt_ref, carry_ref)


def _post_mixer(x, a, bmix, c, w_out, kv, g_mem_q, w_mq, w_mo, mem_gq, mem_gk,
                g_ffn, w_gr, b_gr, w_er, b_er, *, tm):
    b, s, d = x.shape
    m = kv.shape[1]
    nst = s // tm
    wa = w_out[:SWA_WIDTH].astype(BF16)
    wb = w_out[SWA_WIDTH:SWA_WIDTH + HG_WIDTH].astype(BF16)
    wc = w_out[SWA_WIDTH + HG_WIDTH:].astype(BF16)
    pad = LANES - N_GROUPS - N_EXPERTS
    wr = jnp.concatenate([w_gr, w_er, jnp.zeros((d, pad), F32)], axis=1)
    bias = jnp.concatenate([b_gr, b_er, jnp.zeros((pad,), F32)]).reshape(1, LANES)
    whi, wlo = _split_bf16(wr)
    tile = lambda width: pl.BlockSpec((1, tm, width), lambda i, j: (i, j, 0))
    const = lambda shape: pl.BlockSpec(shape, lambda i, j: (0,) * len(shape), pipeline_mode=pl.Buffered(1))
    flat = lambda width: pl.BlockSpec((tm, width), lambda i, j: (i * nst + j, 0))
    return pl.pallas_call(
        _post_mixer_kernel,
        grid=(b, nst),
        in_specs=[tile(d), tile(SWA_WIDTH), tile(HG_WIDTH), tile(MLA_WIDTH),
                  const(wa.shape), const(wb.shape), const(wc.shape),
                  pl.BlockSpec((1, m, 2 * MEM_WIDTH), lambda i, j: (i, 0, 0)),
                  const((1, d)), const((d, MEM_WIDTH)), const((MEM_WIDTH, d)),
                  const((1, MEM_HEAD_DIM)), const((1, MEM_HEAD_DIM)),
                  const((1, d)), const((d, LANES)), const((d, LANES)), const((1, LANES))],
        out_specs=[tile(d), flat(LANES), flat(LANES), pl.BlockSpec((1, LANES), lambda i, j: (0, 0))],
        out_shape=[jax.ShapeDtypeStruct((b, s, d), F32), jax.ShapeDtypeStruct((b * s, LANES), I32),
                   jax.ShapeDtypeStruct((b * s, LANES), F32), jax.ShapeDtypeStruct((1, LANES), I32)],
        scratch_shapes=[pltpu.VMEM((1, LANES), F32)],
        compiler_params=_params("arbitrary", "arbitrary"),
        name="post_mixer",
    )(x, a, bmix, c, wa, wb, wc, kv, g_mem_q.reshape(1, d), w_mq.astype(BF16), w_mo.astype(BF16),
      mem_gq.reshape(1, -1), mem_gk.reshape(1, -1), g_ffn.reshape(1, d), whi, wlo, bias)


HI_HALF_MASK = 0xFFFF0000


def _pack_bf16_pairs(a):
    half = a.shape[1] // 2
    lo = pltpu.bitcast(a[:, :half].astype(BF16).astype(F32), jnp.uint32) >> 16
    hi = pltpu.bitcast(a[:, half:].astype(BF16).astype(F32), jnp.uint32) & jnp.uint32(HI_HALF_MASK)
    return lo | hi


def _unpack_bf16_pairs(w):
    return pltpu.bitcast(w << 16, F32), pltpu.bitcast(w & jnp.uint32(HI_HALF_MASK), F32)


def _dispatch_kernel(dest_ref, x_ref, g_ref, xs_in_hbm, xs_hbm, buf, sem, *, ts):
    del xs_in_hbm
    i = pl.program_id(0)
    last = pl.num_programs(0) - 1
    slot = i % 2

    def row_copy(slot_, r, dst_row):
        return pltpu.make_async_copy(buf.at[slot_, pl.ds(r, 1)], xs_hbm.at[pl.ds(dst_row, 1)], sem.at[slot_])

    def wait_step(slot_):
        for _ in range(ts * TOP_K):
            row_copy(slot_, 0, 0).wait()

    @pl.when(i >= 2)
    def _():
        wait_step(slot)

    buf[slot] = _pack_bf16_pairs(_rms(x_ref[...], g_ref[...]))
    for r in range(ts):
        for kk in range(TOP_K):
            row_copy(slot, r, dest_ref[(i * ts + r) * TOP_K + kk]).start(priority=kk % DMA_PRIORITIES)

    @pl.when(i == last)
    def _():
        wait_step(slot)

    @pl.when((i == last) & (i >= 1))
    def _():
        wait_step(1 - slot)


def _dispatch(x, g_ffn, dest, xs_init, *, ts):
    n, d = x.shape
    n_rows = xs_init.shape[0]
    return pl.pallas_call(
        functools.partial(_dispatch_kernel, ts=ts),
        grid_spec=pltpu.PrefetchScalarGridSpec(
            num_scalar_prefetch=1,
            grid=(n // ts,),
            in_specs=[pl.BlockSpec((ts, d), lambda i, *_: (i, 0)),
                      pl.BlockSpec((1, d), lambda i, *_: (0, 0)),
                      pl.BlockSpec(memory_space=pl.ANY)],
            out_specs=pl.BlockSpec(memory_space=pl.ANY),
            scratch_shapes=[pltpu.VMEM((2, ts, d // 2), jnp.uint32), pltpu.SemaphoreType.DMA((2,))],
        ),
        out_shape=jax.ShapeDtypeStruct((n_rows, d // 2), jnp.uint32),
        input_output_aliases={3: 0},
        compiler_params=_params("arbitrary"),
        name="moe_dispatch",
    )(dest, x, g_ffn.reshape(1, d), xs_init)


def _expert_kernel(bexp_ref, next_ref, nblk_ref, xs_ref, wg_hbm, wu_hbm, wd_hbm, y_ref,
                   wg_f32, wu_f32, wd_f32, sem, wg_bf, wu_bf, wd_bf, *, layer):
    i = pl.program_id(0)
    n_used = nblk_ref[0]
    expert = bexp_ref[i]
    first_of_run = (i == 0) | (expert != bexp_ref[jnp.maximum(i - 1, 0)])
    slot = next_ref[2 * i + 1]

    def weight_copies(e, slot_):
        return [pltpu.make_async_copy(hbm.at[layer, e], buf.at[slot_], sem.at[slot_, j])
                for j, (hbm, buf) in enumerate(((wg_hbm, wg_f32), (wu_hbm, wu_f32), (wd_hbm, wd_f32)))]

    @pl.when(i == 0)
    def _():
        for cp in weight_copies(expert, 0):
            cp.start()

    @pl.when(first_of_run & (i < n_used))
    def _():
        for cp in weight_copies(expert, slot):
            cp.wait()
        next_expert = next_ref[2 * i]

        @pl.when(next_expert >= 0)
        def _():
            for cp in weight_copies(next_expert, 1 - slot):
                cp.start()

        wg_bf[...] = wg_f32[slot].astype(BF16)
        wu_bf[...] = wu_f32[slot].astype(BF16)
        wd_bf[...] = wd_f32[slot].astype(BF16)

    @pl.when(i < n_used)
    def _():
        half = xs_ref.shape[1]
        lo, hi = (part.astype(BF16) for part in _unpack_bf16_pairs(xs_ref[...]))
        gate = (jnp.dot(lo, wg_bf[:half], preferred_element_type=F32)
                + jnp.dot(hi, wg_bf[half:], preferred_element_type=F32))
        up = (jnp.dot(lo, wu_bf[:half], preferred_element_type=F32)
              + jnp.dot(hi, wu_bf[half:], preferred_element_type=F32))
        act = (gate / (1.0 + jnp.exp(-gate)) * up).astype(BF16)
        y_ref[...] = _pack_bf16_pairs(jnp.dot(act, wd_bf[...], preferred_element_type=F32))

    @pl.when(i >= nblk_ref[0])
    def _():
        y_ref[...] = jnp.zeros_like(y_ref)


def _experts(xs, block_expert, run_next, n_used, w_gate, w_up, w_down, layer, *, rows):
    d = 2 * xs.shape[1]
    n_blocks = block_expert.shape[0]
    hbm = pl.BlockSpec(memory_space=pl.ANY)
    return pl.pallas_call(
        functools.partial(_expert_kernel, layer=layer),
        grid_spec=pltpu.PrefetchScalarGridSpec(
            num_scalar_prefetch=3,
            grid=(n_blocks,),
            in_specs=[pl.BlockSpec((rows, d // 2), lambda i, bexp, nxt, nblk: (jnp.minimum(i, nblk[0] - 1), 0)),
                      hbm, hbm, hbm],
            out_specs=pl.BlockSpec((rows, d // 2), lambda i, *_: (i, 0)),
            scratch_shapes=[pltpu.VMEM((2, d, D_EXPERT), F32),
                            pltpu.VMEM((2, d, D_EXPERT), F32),
                            pltpu.VMEM((2, D_EXPERT, d), F32),
                            pltpu.SemaphoreType.DMA((2, 3)),
                            pltpu.VMEM((d, D_EXPERT), BF16),
                            pltpu.VMEM((d, D_EXPERT), BF16),
                            pltpu.VMEM((D_EXPERT, d), BF16)],
        ),
        out_shape=jax.ShapeDtypeStruct((n_blocks * rows, d // 2), jnp.uint32),
        compiler_params=_params("arbitrary"),
        name="moe_experts",
    )(block_expert, run_next, n_used, xs, w_gate, w_up, w_down)


def _combine_kernel(pos_ref, x_ref, gates_ref, y_hbm, o_ref, ybuf, sem, *, tt):
    i = pl.program_id(0)
    nsteps = pl.num_programs(0)

    def row_copy(src_row, slot, dst_row):
        return pltpu.make_async_copy(y_hbm.at[pl.ds(src_row, 1)], ybuf.at[slot, pl.ds(dst_row, 1)], sem.at[slot])

    def start_gather(step, slot):
        for r in range(tt):
            for kk in range(TOP_K):
                row_copy(pos_ref[(step * tt + r) * TOP_K + kk], slot, kk * tt + r).start(priority=kk % DMA_PRIORITIES)

    def wait_gather(slot):
        for r in range(TOP_K * tt):
            row_copy(0, slot, r).wait()

    @pl.when(i == 0)
    def _():
        start_gather(0, 0)

    @pl.when(i + 1 < nsteps)
    def _():
        start_gather(i + 1, (i + 1) % 2)

    wait_gather(i % 2)
    g = gates_ref[...]
    half = ybuf.shape[2]
    for cols, y in zip((pl.ds(0, half), pl.ds(half, half)), _unpack_bf16_pairs(ybuf[i % 2])):
        o_ref[:, cols] = x_ref[:, cols] + g[:, 0:1] * y[:tt] + g[:, 1:2] * y[tt:]


def _combine(x, gates, pos, y, *, tt):
    n, d = x.shape
    return pl.pallas_call(
        functools.partial(_combine_kernel, tt=tt),
        grid_spec=pltpu.PrefetchScalarGridSpec(
            num_scalar_prefetch=1,
            grid=(n // tt,),
            in_specs=[pl.BlockSpec((tt, d), lambda i, *_: (i, 0)),
                      pl.BlockSpec((tt, LANES), lambda i, *_: (i, 0)),
                      pl.BlockSpec(memory_space=pl.ANY)],
            out_specs=pl.BlockSpec((tt, d), lambda i, *_: (i, 0)),
            scratch_shapes=[pltpu.VMEM((2, TOP_K * tt, d // 2), jnp.uint32), pltpu.SemaphoreType.DMA((2,))],
        ),
        out_shape=jax.ShapeDtypeStruct((n, d), F32),
        compiler_params=_params("arbitrary"),
        name="moe_combine",
    )(pos, x, gates, y)


def _moe_plan(ids, cnt, *, rows):
    n = ids.shape[0]
    expert, rank = ids[:, :TOP_K], ids[:, TOP_K:2 * TOP_K]
    counts = cnt[0, N_GROUPS:N_GROUPS + N_EXPERTS]
    padded = (counts + rows - 1) // rows * rows
    padded_end = jnp.cumsum(padded)
    seg_start = padded_end - padded
    onehot = expert[..., None] == jnp.arange(N_EXPERTS, dtype=I32)
    dest = (jnp.sum(jnp.where(onehot, seg_start, 0), axis=-1) + rank).astype(I32).reshape(n * TOP_K)
    n_blocks = -(-(n * TOP_K) // rows) + N_EXPERTS
    block_start = jnp.arange(n_blocks, dtype=I32) * rows
    block_expert = jnp.minimum(jnp.sum(padded_end[None, :] <= block_start[:, None], axis=1), N_EXPERTS - 1).astype(I32)
    n_used = (padded_end[-1] // rows).astype(I32)
    block_onehot = block_expert[:, None] == jnp.arange(N_EXPERTS, dtype=I32)
    pick = lambda per_expert: jnp.sum(jnp.where(block_onehot, per_expert, 0), axis=-1)
    next_block = pick(padded_end // rows)
    next_onehot = jnp.minimum(next_block, n_blocks - 1)[:, None] == jnp.arange(n_blocks, dtype=I32)
    next_expert = jnp.where(next_block < n_used, jnp.sum(jnp.where(next_onehot, block_expert, 0), axis=-1), -1)
    nonempty = (counts > 0).astype(I32)
    run_parity = pick(jnp.cumsum(nonempty) - nonempty) % 2
    run_next = jnp.stack([next_expert, run_parity], axis=-1).astype(I32).reshape(2 * n_blocks)
    return dest, block_expert, run_next, n_used.reshape(1)


def _band_relative_bias(table):
    def bucket(nd):
        max_exact = REL_BUCKETS // 2
        nf = jnp.maximum(nd, 1).astype(F32)
        large = max_exact + (jnp.log(nf / max_exact) / math.log(REL_MAX_DIST / max_exact)
                             * (REL_BUCKETS - max_exact)).astype(I32)
        return jnp.where(nd < max_exact, nd, jnp.minimum(large, REL_BUCKETS - 1))

    qi = jnp.arange(WINDOW)[:, None]
    kj = jnp.arange(2 * WINDOW)[None, :]
    dist = jnp.maximum(qi + WINDOW - kj, 0)
    onehot = (bucket(dist)[..., None] == jnp.arange(REL_BUCKETS)).astype(F32)
    return jnp.einsum('qkb,bh->hqk', onehot, table.astype(F32), precision=lax.Precision.HIGHEST)


def _hgrn_params(lb, g_out):
    lb = lb.reshape(HG_HEADS, 1, HG_DK)
    gain = jnp.broadcast_to(g_out.reshape(1, 1, HG_DV), (HG_HEADS, 1, HG_DV))
    rows = [jnp.log(lb), jnp.log1p(-lb), 1.0 - lb, gain, jnp.zeros((HG_HEADS, 4, HG_DK), F32)]
    return jnp.concatenate(rows, axis=1).astype(F32)


def kernel(x, mem, rel_bias_table, hg_lb_logits, g_mix, w_in, swa_gq, swa_gk, swa_sinks, hg_g_out, mla_g_cq, mla_g_ckv, mla_w_uq, mla_w_ukv, mla_gq, mla_gk, w_out, g_mem_q, g_mem_kv, w_mq, w_mkv, mem_gq, mem_gk, w_mo, g_ffn, w_group_router, b_group_router, w_expert_router, b_expert_router, w_gate, w_up, w_down):
    b, s, d = x.shape
    n = b * s
    m = mem.shape[1]
    rel_bias = _band_relative_bias(rel_bias_table)
    lb_all = jnp.cumsum(jax.nn.softmax(hg_lb_logits.astype(F32), axis=0), axis=0)
    lb_all = lb_all - lb_all[:1]
    offs = [0]
    for width in IN_SIZES:
        offs.append(offs[-1] + width)
    o_hg, o_cq, o_kr = offs[3], offs[7], offs[9]

    xf = x.reshape(n, d)
    mem2 = mem.reshape(b * m, d)
    xs = None
    for l in range(DEPTH):
        w = w_in[l].astype(BF16)
        w_swa = w[:, :o_hg]
        w_hg = w[:, o_hg:o_cq]
        w_mla = jnp.concatenate([w[:, o_cq:], _swap_halves(w[:, o_kr:])], axis=1)
        u_swa = _rms_matmul(xf, g_mix[l], w_swa, tm=1024, tn=w_swa.shape[1], out_dtype=BF16).reshape(b, s, -1)
        u_hg = _rms_matmul(xf, g_mix[l], w_hg, tm=1024, tn=1024, out_dtype=BF16).reshape(b, s, -1)
        u_mla = _rms_matmul(xf, g_mix[l], w_mla, tm=1024, tn=w_mla.shape[1], out_dtype=BF16).reshape(b, s, -1)

        out_a = _swa(u_swa, rel_bias, swa_gq[l], swa_gk[l], swa_sinks[l])
        out_b = _hgrn(u_hg, _hgrn_params(lb_all[l], hg_g_out[l]), chunk=HG_CHUNK, n_sub=HG_SUB)
        qm, km, vm = _mla_prep(u_mla, mla_w_uq[l], mla_w_ukv[l], mla_g_cq[l], mla_g_ckv[l],
                               mla_gq[l], mla_gk[l], tm=512)
        out_c = _mla_attention(qm, km, vm, tq=MLA_TQ, tk=MLA_TK)

        kv = _rms_matmul(mem2, g_mem_kv[l], w_mkv[l].astype(BF16), tm=b * m, tn=2 * MEM_WIDTH)
        x3, ids, gates, cnt = _post_mixer(
            xf.reshape(b, s, d), out_a, out_b, out_c, w_out[l], kv.reshape(b, m, -1),
            g_mem_q[l], w_mq[l], w_mo[l], mem_gq[l], mem_gk[l],
            g_ffn[l], w_group_router[l], b_group_router[l], w_expert_router[l], b_expert_router[l], tm=512)
        xf = x3.reshape(n, d)
        dest, block_expert, run_next, n_used = _moe_plan(ids, cnt, rows=MOE_ROWS)
        if xs is None:
            xs = jnp.zeros((block_expert.shape[0] * MOE_ROWS, d // 2), jnp.uint32)
        xs = _dispatch(xf, g_ffn[l], dest, xs, ts=MOE_BURST_TOKENS)
        y = _experts(xs, block_expert, run_next, n_used, w_gate, w_up, w_down, l, rows=MOE_ROWS)
        xf = _combine(xf, gates, dest, y, tt=MOE_BURST_TOKENS)
    return xf.reshape(b, s, d)
```

```python
import functools
import math

import jax
import jax.numpy as jnp
from jax import lax
from jax.experimental import pallas as pl
from jax.experimental.pallas import tpu as pltpu

F32 = jnp.float32
BF16 = jnp.bfloat16
I32 = jnp.int32

D_MODEL = 2048
DEPTH = 2
SWA_HEADS = 8
SWA_KV_HEADS = 2
SWA_HEAD_DIM = 64
WINDOW = 128
HG_HEADS = 8
HG_DK = 128
HG_DV = 128
MLA_HEADS = 4
MLA_Q_RANK = 512
MLA_KV_RANK = 256
MLA_NOPE = 128
MLA_ROPE = 64
MLA_QK = MLA_NOPE + MLA_ROPE
MLA_V = 128
ROPE_THETA = 10000.0
REL_BUCKETS = 32
REL_MAX_DIST = 128
MEM_HEADS = 4
MEM_HEAD_DIM = 128
MEM_WIDTH = MEM_HEADS * MEM_HEAD_DIM
N_GROUPS = 8
EXPERTS_PER_GROUP = 8
N_EXPERTS = N_GROUPS * EXPERTS_PER_GROUP
TOP_K = 2
D_EXPERT = 512
MOE_BURST_TOKENS = 256
MOE_ROWS = 256
EPS = 1e-6
NEG_INF = -1e30
LOG2_E = math.log2(math.e)

SWA_WIDTH = SWA_HEADS * SWA_HEAD_DIM
SWA_KV_WIDTH = SWA_KV_HEADS * SWA_HEAD_DIM
HG_WIDTH = HG_HEADS * HG_DV
MLA_WIDTH = MLA_HEADS * MLA_V
IN_SIZES = (SWA_WIDTH, SWA_KV_WIDTH, SWA_KV_WIDTH,
            HG_HEADS * HG_DK, HG_HEADS * HG_DK, HG_WIDTH, HG_WIDTH,
            MLA_Q_RANK, MLA_KV_RANK, MLA_ROPE)

LANES = 128
SUBLANES = 8
DMA_PRIORITIES = 2
MLA_QK_PAD = 2 * LANES
VMEM_LIMIT_BYTES = 56 * 1024 * 1024

HG_CHUNK = 128
SWA_BLOCKS_PER_STEP = 8
HG_SUB = 16
MLA_TQ = 2048
MLA_TK = 2048
MLA_KEY_GROUP = 1024
MLA_ROW_GROUP = 256
NT_DIMS = (((1,), (1,)), ((), ()))
TN_DIMS = (((0,), (0,)), ((), ()))


def _params(*semantics):
    return pltpu.CompilerParams(dimension_semantics=semantics, vmem_limit_bytes=VMEM_LIMIT_BYTES)


def _rms(x, gain=None):
    y = x * lax.rsqrt(jnp.mean(x * x, axis=-1, keepdims=True) + EPS)
    return y if gain is None else y * gain


def _rms_matmul_kernel(x_ref, g_ref, w_ref, o_ref, hn_ref):
    @pl.when(pl.program_id(1) == 0)
    def _():
        hn_ref[...] = _rms(x_ref[...], g_ref[...]).astype(BF16)

    o_ref[...] = jnp.dot(hn_ref[...], w_ref[...], preferred_element_type=F32).astype(o_ref.dtype)


def _rms_matmul(x, gain, w, *, tm, tn, out_dtype=F32):
    n, d = x.shape
    nout = w.shape[1]
    return pl.pallas_call(
        _rms_matmul_kernel,
        grid=(n // tm, nout // tn),
        in_specs=[pl.BlockSpec((tm, d), lambda i, j: (i, 0)),
                  pl.BlockSpec((1, d), lambda i, j: (0, 0)),
                  pl.BlockSpec((d, tn), lambda i, j: (0, j))],
        out_specs=pl.BlockSpec((tm, tn), lambda i, j: (i, j)),
        out_shape=jax.ShapeDtypeStruct((n, nout), out_dtype),
        scratch_shapes=[pltpu.VMEM((tm, d), BF16)],
        compiler_params=_params("parallel", "arbitrary"),
        name="rms_matmul",
    )(x, gain.reshape(1, d), w)


def _head_rms(x, seg_ref, gain_ref):
    sq = x * x
    hi, lo = _split_bf16(sq)
    ss = (jnp.dot(hi, seg_ref[...], preferred_element_type=F32)
          + jnp.dot(lo, seg_ref[...], preferred_element_type=F32))
    return x * lax.rsqrt(ss * (1.0 / SWA_HEAD_DIM) + EPS) * gain_ref[...]


def _swa_kernel(sink_ref, q_ref, kp_ref, kc_ref, vp_ref, vc_ref, bias_ref, gq_ref, gk_ref, segq_ref, segk_ref, o_ref):
    blk = pl.program_id(1)
    grp = SWA_HEADS // SWA_KV_HEADS
    q_all = _head_rms(q_ref[0].astype(F32), segq_ref, gq_ref)
    k_all = _head_rms(jnp.concatenate([kp_ref[0], kc_ref[0]], axis=0).astype(F32), segk_ref, gk_ref)
    v_all = jnp.concatenate([vp_ref[0], vc_ref[0]], axis=0)
    qi = lax.broadcasted_iota(I32, (WINDOW, 2 * WINDOW), 0)
    kj = lax.broadcasted_iota(I32, (WINDOW, 2 * WINDOW), 1)
    dist = qi + WINDOW - kj
    in_window = (dist >= 0) & (dist < WINDOW)
    first_valid = in_window & (kj >= jnp.where(blk > 0, 0, WINDOW))
    for j in range(q_all.shape[0] // WINDOW):
        q = q_all[j * WINDOW:(j + 1) * WINDOW]
        k = k_all[j * WINDOW:(j + 2) * WINDOW]
        v = v_all[j * WINDOW:(j + 2) * WINDOW]
        valid = first_valid if j == 0 else in_window
        outs = []
        for g in range(SWA_KV_HEADS):
            lo = g * SWA_HEAD_DIM
            kg = k[:, lo:lo + SWA_HEAD_DIM].astype(BF16)
            vg = v[:, lo:lo + SWA_HEAD_DIM].astype(BF16)
            for h in range(g * grp, (g + 1) * grp):
                qh = q[:, h * SWA_HEAD_DIM:(h + 1) * SWA_HEAD_DIM].astype(BF16)
                s = lax.dot_general(qh, kg, NT_DIMS, preferred_element_type=F32) + bias_ref[h]
                s = jnp.where(valid, s, NEG_INF)
                sink = sink_ref[h]
                m = jnp.maximum(jnp.max(s, axis=-1, keepdims=True), sink)
                e = jnp.exp(s - m)
                denom = jnp.sum(e, axis=-1, keepdims=True) + jnp.exp(sink - m)
                outs.append(jnp.dot((e / denom).astype(BF16), vg, preferred_element_type=F32))
        o_ref[0, j * WINDOW:(j + 1) * WINDOW] = jnp.concatenate(outs, axis=-1).astype(o_ref.dtype)


def _swa(u_swa, rel_bias, gq, gk, sinks):
    b, s, _ = u_swa.shape
    rows = SWA_BLOCKS_PER_STEP * WINDOW
    kcol = SWA_WIDTH // LANES
    vcol = kcol + 1
    prev = lambda bi, n, *_: (bi, jnp.maximum(n * SWA_BLOCKS_PER_STEP - 1, 0))
    grp = SWA_HEADS // SWA_KV_HEADS

    def head_blocks(width):
        head = jnp.arange(width) // SWA_HEAD_DIM
        return (head[:, None] == head[None, :]).astype(BF16)

    return pl.pallas_call(
        _swa_kernel,
        grid_spec=pltpu.PrefetchScalarGridSpec(
            num_scalar_prefetch=1,
            grid=(b, s // rows),
            in_specs=[
                pl.BlockSpec((1, rows, SWA_WIDTH), lambda bi, n, *_: (bi, n, 0)),
                pl.BlockSpec((1, WINDOW, LANES), lambda bi, n, *_: prev(bi, n) + (kcol,)),
                pl.BlockSpec((1, rows, LANES), lambda bi, n, *_: (bi, n, kcol)),
                pl.BlockSpec((1, WINDOW, LANES), lambda bi, n, *_: prev(bi, n) + (vcol,)),
                pl.BlockSpec((1, rows, LANES), lambda bi, n, *_: (bi, n, vcol)),
                pl.BlockSpec((SWA_HEADS, WINDOW, 2 * WINDOW), lambda bi, n, *_: (0, 0, 0)),
                pl.BlockSpec((1, SWA_WIDTH), lambda bi, n, *_: (0, 0)),
                pl.BlockSpec((1, SWA_KV_WIDTH), lambda bi, n, *_: (0, 0)),
                pl.BlockSpec((SWA_WIDTH, SWA_WIDTH), lambda bi, n, *_: (0, 0)),
                pl.BlockSpec((SWA_KV_WIDTH, SWA_KV_WIDTH), lambda bi, n, *_: (0, 0)),
            ],
            out_specs=pl.BlockSpec((1, rows, SWA_WIDTH), lambda bi, n, *_: (bi, n, 0)),
        ),
        out_shape=jax.ShapeDtypeStruct((b, s, SWA_WIDTH), BF16),
        compiler_params=_params("parallel", "parallel"),
        name="swa_attention",
    )(sinks, u_swa, u_swa, u_swa, u_swa, u_swa, rel_bias,
      jnp.tile(gq * (SWA_HEAD_DIM ** -0.5), SWA_HEADS).reshape(1, SWA_WIDTH),
      jnp.tile(gk, SWA_KV_HEADS).reshape(1, SWA_KV_WIDTH), head_blocks(SWA_WIDTH), head_blocks(SWA_KV_WIDTH))


def _hgrn_level_matrix(c):
    t = jnp.arange(c)[:, None]
    r = jnp.arange(c)[None, :]
    mats = [(r <= t)]
    half = c // 2
    while half >= 1:
        mid = (t // (2 * half)) * (2 * half) + half
        is_q = (t & half) != 0
        if half < SUBLANES:
            mats.append(jnp.where(is_q, (r >= mid) & (r <= t), (r > t) & (r < mid)))
        half //= 2
    return jnp.concatenate(mats, axis=0).astype(BF16)


def _hgrn_pair_level(c):
    t = jnp.arange(c, dtype=I32)[:, None]
    s = jnp.arange(c, dtype=I32)[None, :]
    x = t ^ s
    lvl = jnp.zeros((c, c), I32)
    half = c // 2
    while half >= 1:
        lvl = jnp.where((x & (-half)) == half, half, lvl)
        half //= 2
    return jnp.where(t > s, lvl, 0)


def _hgrn_kernel(q_ref, f_ref, v_ref, gate_ref, par_ref, amat_ref, lvl_ref, o_ref, st_ref, *, chunk, n_sub):
    @pl.when(pl.program_id(2) == 0)
    def _():
        st_ref[...] = jnp.zeros_like(st_ref)

    c = chunk
    par = par_ref[0]
    log_lb, log1m_lb, one_m_lb, g_out = par[0:1], par[1:2], par[2:3], par[3:4]
    row = lax.broadcasted_iota(I32, (c, 1), 0)
    pair_level = lvl_ref[...]
    st = st_ref[...]
    for sub in range(n_sub):
        rows = pl.ds(sub * c, c)
        q = q_ref[0, rows].astype(F32)
        f = f_ref[0, rows].astype(F32)
        v = v_ref[0, rows].astype(F32)
        gate = gate_ref[0, rows].astype(F32)

        e = jnp.exp(-jnp.abs(f))
        log_sig = jnp.minimum(f, 0.0) - jnp.log(1.0 + e)
        bb = log1m_lb + log_sig
        log_f = jnp.maximum(log_lb, bb) + jnp.log(1.0 + jnp.exp(-jnp.abs(log_lb - bb)))
        log_f = log_f * LOG2_E
        kk = one_m_lb * jnp.where(f < 0.0, 1.0, e) / (1.0 + e)

        g_hi = log_f.astype(BF16)
        g_lo = (log_f - g_hi.astype(F32)).astype(BF16)
        e2 = jnp.dot(amat_ref[...], jnp.concatenate([g_hi, g_lo], axis=1), preferred_element_type=F32)
        expo = e2[:, :HG_DK] + e2[:, HG_DK:]
        bcum = expo[0:c]
        btot = bcum[c - 1:c]

        attn = jnp.zeros((c, c), F32)
        half = c // 2
        lvl = 1
        while half >= 1:
            is_q = (row & half) != 0
            seg = 2 * half
            if half >= SUBLANES:
                ref = jnp.concatenate([jnp.broadcast_to(bcum[a + half - 1:a + half], (seg, HG_DK))
                                       for a in range(0, c, seg)], axis=0)
                diff = bcum - ref
                w = jnp.exp2(jnp.where(is_q, diff, -diff))
            else:
                w = jnp.exp2(expo[lvl * c:(lvl + 1) * c])
                lvl += 1
            zf = jnp.where(is_q, q, kk) * w
            z = zf.astype(BF16)
            if half >= SUBLANES:
                zq = jnp.concatenate([zf[a + half:a + seg] for a in range(0, c, seg)], axis=0).astype(BF16)
                aq = lax.dot_general(zq, z, NT_DIMS, preferred_element_type=F32)
                blank = jnp.zeros((half, c), F32)
                a = jnp.concatenate([blk for j in range(c // seg)
                                     for blk in (blank, aq[j * half:(j + 1) * half])], axis=0)
            else:
                a = lax.dot_general(z, z, NT_DIMS, preferred_element_type=F32)
            attn = jnp.where(pair_level == half, a, attn)
            half //= 2

        vb = v.astype(BF16)
        diag = jnp.sum(q * kk, axis=-1, keepdims=True)
        intra = jnp.dot(attn.astype(BF16), vb, preferred_element_type=F32) + diag * v
        qe = (q * jnp.exp2(bcum)).astype(BF16)
        inter = lax.dot_general(qe, st.astype(BF16), NT_DIMS, preferred_element_type=F32)
        kd = (kk * jnp.exp2(btot - bcum)).astype(BF16)
        st = st * jnp.exp2(btot) +lax.dot_general(vb, kd, TN_DIMS, preferred_element_type=F32)

        o = _rms(inter + intra, g_out)
        o_ref[0, rows] = (o * gate / (1.0 + jnp.exp(-gate))).astype(o_ref.dtype)
    st_ref[...] = st


def _hgrn(u_hg, par, *, chunk, n_sub):
    b, s, _ = u_hg.shape
    amat = _hgrn_level_matrix(chunk)
    pair_level = _hgrn_pair_level(chunk)
    step = chunk * n_sub
    col = lambda off: (lambda bi, h, c: (bi, c, off + h))
    return pl.pallas_call(
        functools.partial(_hgrn_kernel, chunk=chunk, n_sub=n_sub),
        grid=(b, HG_HEADS, s // step),
        in_specs=[pl.BlockSpec((1, step, HG_DK), col(0)),
                  pl.BlockSpec((1, step, HG_DK), col(HG_HEADS)),
                  pl.BlockSpec((1, step, HG_DV), col(2 * HG_HEADS)),
                  pl.BlockSpec((1, step, HG_DV), col(3 * HG_HEADS)),
                  pl.BlockSpec((1, 8, HG_DK), lambda bi, h, c: (h, 0, 0)),
                  pl.BlockSpec(amat.shape, lambda bi, h, c: (0, 0)),
                  pl.BlockSpec(pair_level.shape, lambda bi, h, c: (0, 0))],
        out_specs=pl.BlockSpec((1, step, HG_DV), lambda bi, h, c: (bi, c, h)),
        out_shape=jax.ShapeDtypeStruct((b, s, HG_WIDTH), BF16),
        scratch_shapes=[pltpu.VMEM((HG_DV, HG_DK), F32)],
        compiler_params=_params("parallel", "parallel", "arbitrary"),
        name="hgrn2",
    )(u_hg, u_hg, u_hg, u_hg, par, amat, pair_level)


def _mla_prep_kernel(u_ref, tab_ref, wq_ref, wkv_ref, gcq_ref, gckv_ref, gqn_ref, gqr_ref, gkn_ref, gkr_ref,
                     q_ref, k_ref, v_ref):
    u = u_ref[0].astype(F32)
    cq = _rms(u[:, :MLA_Q_RANK], gcq_ref[...]).astype(BF16)
    ckv = _rms(u[:, MLA_Q_RANK:MLA_Q_RANK + MLA_KV_RANK], gckv_ref[...]).astype(BF16)
    kr = u[:, MLA_Q_RANK + MLA_KV_RANK:]
    qf = jnp.dot(cq, wq_ref[...], preferred_element_type=F32)
    kvf = jnp.dot(ckv, wkv_ref[...], preferred_element_type=F32)
    tab = tab_ref[...]
    low = lax.broadcasted_iota(I32, kr.shape, 1) < MLA_ROPE
    kr_sq = jnp.sum(jnp.where(low, kr * kr, 0.0), axis=-1, keepdims=True)
    scale = MLA_QK ** -0.5 * LOG2_E
    ones = jnp.ones((u.shape[0], MLA_V), F32)
    for h in range(MLA_HEADS):
        qn = qf[:, h * MLA_QK_PAD:h * MLA_QK_PAD + MLA_NOPE]
        qr = qf[:, h * MLA_QK_PAD + MLA_NOPE:(h + 1) * MLA_QK_PAD]
        ss = jnp.sum(qn * qn, axis=-1, keepdims=True) + jnp.sum(jnp.where(low, qr * qr, 0.0), axis=-1, keepdims=True)
        rstd = lax.rsqrt(ss / MLA_QK + EPS) * scale
        t = qr * rstd * tab * gqr_ref[...]
        rope = jnp.where(low, t + pltpu.roll(t, MLA_ROPE, 1), 0.0)
        q_ref[0, h] = jnp.concatenate([qn * rstd * gqn_ref[...], rope], axis=-1).astype(q_ref.dtype)

        kn = kvf[:, h * MLA_NOPE:(h + 1) * MLA_NOPE]
        ss = jnp.sum(kn * kn, axis=-1, keepdims=True) + kr_sq
        rstd = lax.rsqrt(ss / MLA_QK + EPS)
        t = kr * rstd * tab * gkr_ref[...]
        rope = t + pltpu.roll(t, MLA_ROPE, 1)
        k_ref[0, h] = jnp.concatenate([kn * rstd * gkn_ref[...], rope], axis=-1).astype(k_ref.dtype)
        vh = kvf[:, MLA_HEADS * MLA_NOPE + h * MLA_V:MLA_HEADS * MLA_NOPE + (h + 1) * MLA_V]
        v_ref[0, h] = jnp.concatenate([vh, ones], axis=-1).astype(v_ref.dtype)


def _swap_halves(a):
    half = a.shape[-1] // 2
    return jnp.concatenate([a[..., half:], a[..., :half]], axis=-1)


def _mla_prep(u_mla, w_uq, w_ukv, g_cq, g_ckv, gq, gk, *, tm):
    b, s, wu = u_mla.shape
    wq = w_uq.reshape(MLA_Q_RANK, MLA_HEADS, MLA_QK)
    wq = jnp.concatenate([wq, _swap_halves(wq[..., MLA_NOPE:])], axis=-1).reshape(MLA_Q_RANK, MLA_HEADS * MLA_QK_PAD)
    wkv = w_ukv.reshape(MLA_KV_RANK, MLA_HEADS, MLA_NOPE + MLA_V)
    wkv = jnp.concatenate([wkv[..., :MLA_NOPE].reshape(MLA_KV_RANK, -1), wkv[..., MLA_NOPE:].reshape(MLA_KV_RANK, -1)], axis=1)
    half = MLA_ROPE // 2
    inv_freq = ROPE_THETA ** (-jnp.arange(half, dtype=F32) / half)
    ang = jnp.arange(s, dtype=F32)[:, None] * inv_freq[None, :]
    cos, sin = jnp.cos(ang), jnp.sin(ang)
    tab = jnp.concatenate([cos, cos, -sin, sin], axis=-1)
    rope_gain = lambda g: jnp.concatenate([g[MLA_NOPE:], _swap_halves(g[MLA_NOPE:])]).reshape(1, 2 * MLA_ROPE)
    vec = lambda i, j: (0, 0)
    head_out = lambda width: pl.BlockSpec((1, MLA_HEADS, tm, width), lambda i, j: (i, 0, j, 0))
    return pl.pallas_call(
        _mla_prep_kernel,
        grid=(b, s // tm),
        in_specs=[pl.BlockSpec((1, tm, wu), lambda i, j: (i, j, 0)),
                  pl.BlockSpec((tm, 2 * MLA_ROPE), lambda i, j: (j, 0)),
                  pl.BlockSpec(wq.shape, vec),
                  pl.BlockSpec(wkv.shape, vec),
                  pl.BlockSpec((1, MLA_Q_RANK), vec),
                  pl.BlockSpec((1, MLA_KV_RANK), vec),
                  pl.BlockSpec((1, MLA_NOPE), vec),
                  pl.BlockSpec((1, 2 * MLA_ROPE), vec),
                  pl.BlockSpec((1, MLA_NOPE), vec),
                  pl.BlockSpec((1, 2 * MLA_ROPE), vec)],
        out_specs=[head_out(MLA_QK_PAD), head_out(MLA_QK_PAD), head_out(2 * MLA_V)],
        out_shape=[jax.ShapeDtypeStruct((b, MLA_HEADS, s, MLA_QK_PAD), BF16),
                   jax.ShapeDtypeStruct((b, MLA_HEADS, s, MLA_QK_PAD), BF16),
                   jax.ShapeDtypeStruct((b, MLA_HEADS, s, 2 * MLA_V), BF16)],
        compiler_params=_params("parallel", "parallel"),
        name="mla_prep",
    )(u_mla, tab, wq.astype(BF16), wkv.astype(BF16), g_cq.reshape(1, -1), g_ckv.reshape(1, -1),
      gq[:MLA_NOPE].reshape(1, -1), rope_gain(gq), gk[:MLA_NOPE].reshape(1, -1), rope_gain(gk))


def _mla_attn_kernel(qi_ref, ki_ref, q_ref, k_ref, v_ref, o_ref, m_ref, acc_ref, *, tq, tk):
    step = pl.program_id(2)
    qi = qi_ref[step]
    ki = ki_ref[step]

    @pl.when(ki == 0)
    def _():
        m_ref[...] = jnp.full_like(m_ref, NEG_INF)
        acc_ref[...] = jnp.zeros_like(acc_ref)

    def update(rel):
        for r0, k0 in [(r, k) for r in range(0, tq, MLA_ROW_GROUP) for k in range(0, tk, MLA_KEY_GROUP)]:
            first_key = None if rel is None else rel + k0
            if first_key is not None and first_key > r0 + MLA_ROW_GROUP - 1:
                continue
            rows = pl.ds(r0, MLA_ROW_GROUP)
            keys = pl.ds(k0, MLA_KEY_GROUP)
            s = lax.dot_general(q_ref[0, 0, rows], k_ref[0, 0, keys], NT_DIMS, preferred_element_type=F32)
            if first_key is not None and first_key + MLA_KEY_GROUP - 1 > r0:
                row = r0 + lax.broadcasted_iota(I32, s.shape, 0)
                col = first_key + lax.broadcasted_iota(I32, s.shape, 1)
                s = jnp.where(col <= row, s, NEG_INF)
            m_prev = m_ref[rows]
            m_next = jnp.maximum(m_prev, jnp.max(s, axis=-1, keepdims=True))
            alpha = jnp.exp2(m_prev - m_next)
            p = jnp.exp2(s - m_next[:, :1]).astype(BF16)
            pv = jnp.dot(p, v_ref[0, 0, keys], preferred_element_type=F32)
            acc_ref[rows, :MLA_V] = alpha * acc_ref[rows, :MLA_V] + pv[:, :MLA_V]
            acc_ref[rows, MLA_V:] = alpha * acc_ref[rows, MLA_V:] + pv[:, MLA_V:]
            m_ref[rows] = m_next

    key_offset = ki * tk - qi * tq
    pl.when(key_offset < 0)(lambda: update(None))
    for rel in range(0, tq, tk):
        pl.when(key_offset == rel)(functools.partial(update, rel))

    @pl.when((ki + 1) * tk == (qi + 1) * tq)
    def _():
        o_ref[0] = (acc_ref[:, :MLA_V] / acc_ref[:, MLA_V:]).astype(o_ref.dtype)


def _mla_attention(q, k, v, *, tq, tk):
    b, h, s, _ = q.shape
    pairs = [(qi, ki) for qi in range(s // tq) for ki in range((qi + 1) * tq // tk)]
    qi_of = jnp.array([p[0] for p in pairs], I32)
    ki_of = jnp.array([p[1] for p in pairs], I32)
    kv_idx = lambda bi, hi, t, qi_ref, ki_ref: (bi, hi, ki_ref[t], 0)
    return pl.pallas_call(
        functools.partial(_mla_attn_kernel, tq=tq, tk=tk),
        grid_spec=pltpu.PrefetchScalarGridSpec(
            num_scalar_prefetch=2,
            grid=(b, h, len(pairs)),
            in_specs=[pl.BlockSpec((1, 1, tq, MLA_QK_PAD), lambda bi, hi, t, qi_ref, ki_ref: (bi, hi, qi_ref[t], 0)),
                      pl.BlockSpec((1, 1, tk, MLA_QK_PAD), kv_idx),
                      pl.BlockSpec((1, 1, tk, 2 * MLA_V), kv_idx)],
            out_specs=pl.BlockSpec((1, tq, MLA_V), lambda bi, hi, t, qi_ref, ki_ref: (bi, qi_ref[t], hi)),
            scratch_shapes=[pltpu.VMEM((tq, MLA_V), F32), pltpu.VMEM((tq, 2 * MLA_V), F32)],
        ),
        out_shape=jax.ShapeDtypeStruct((b, s, h * MLA_V), BF16),
        compiler_params=_params("parallel", "parallel", "arbitrary"),
        name="mla_attention",
    )(qi_of, ki_of, q, k, v)


def _mem_attn_body(x, kv, g_ref, wq_ref, wo_ref, gq_ref, gk_ref):
    hn = _rms(x, g_ref[...]).astype(BF16)
    qf = jnp.dot(hn, wq_ref[...], preferred_element_type=F32)
    outs = []
    for h in range(MEM_HEADS):
        lo = h * MEM_HEAD_DIM
        qh = (_rms(qf[:, lo:lo + MEM_HEAD_DIM], gq_ref[...]) * (MEM_HEAD_DIM ** -0.5)).astype(BF16)
        kh = _rms(kv[:, lo:lo + MEM_HEAD_DIM], gk_ref[...]).astype(BF16)
        vh = kv[:, MEM_WIDTH + lo:MEM_WIDTH + lo + MEM_HEAD_DIM].astype(BF16)
        s = lax.dot_general(qh, kh, NT_DIMS, preferred_element_type=F32)
        e = jnp.exp(s - jnp.max(s, axis=-1, keepdims=True))
        p = (e / jnp.sum(e, axis=-1, keepdims=True)).astype(BF16)
        outs.append(jnp.dot(p, vh, preferred_element_type=F32))
    o = jnp.concatenate(outs, axis=-1).astype(BF16)
    return x + jnp.dot(o, wo_ref[...], preferred_element_type=F32)


def _split_bf16(a):
    hi = a.astype(BF16)
    return hi, (a - hi.astype(F32)).astype(BF16)


def _router_body(x, g_ref, whi_ref, wlo_ref, b_ref, ids_ref, gates_ref, cnt_ref, carry_ref):
    hn = _rms(x, g_ref[...])
    hi, lo = _split_bf16(hn)
    logits = (jnp.dot(hi, whi_ref[...], preferred_element_type=F32)
              + jnp.dot(hi, wlo_ref[...], preferred_element_type=F32)
              + jnp.dot(lo, whi_ref[...], preferred_element_type=F32)) + b_ref[...]
    lane = lax.broadcasted_iota(I32, logits.shape, 1)
    gl = jnp.where(lane < N_GROUPS, logits, NEG_INF)
    gmax = jnp.max(gl, axis=-1, keepdims=True)
    p_grp = 1.0 / jnp.sum(jnp.exp(gl - gmax), axis=-1, keepdims=True)
    grp = jnp.min(jnp.where(gl == gmax, lane, LANES), axis=-1, keepdims=True)
    in_grp = (lane >= N_GROUPS) & (lane < N_GROUPS + N_EXPERTS) & (((lane - N_GROUPS) // EXPERTS_PER_GROUP) == grp)
    el = jnp.where(in_grp, logits, NEG_INF)
    m1 = jnp.max(el, axis=-1, keepdims=True)
    i1 = jnp.min(jnp.where(el == m1, lane, LANES), axis=-1, keepdims=True)
    el2 = jnp.where(lane == i1, NEG_INF, el)
    m2 = jnp.max(el2, axis=-1, keepdims=True)
    i2 = jnp.min(jnp.where(el2 == m2, lane, LANES), axis=-1, keepdims=True)
    r = jnp.exp(m2 - m1)
    g1 = p_grp / (1.0 + r)
    gates_ref[...] = jnp.where(lane == 0, g1, jnp.where(lane == 1, g1 * r, 0.0))

    tm = logits.shape[0]
    used = jnp.where((lane == i1) | (lane == i2), 1.0, 0.0)
    earlier = lax.broadcasted_iota(I32, (tm, tm), 1) < lax.broadcasted_iota(I32, (tm, tm), 0)
    before = carry_ref[...] + jnp.dot(jnp.where(earlier, 1.0, 0.0).astype(BF16), used.astype(BF16),
                                      preferred_element_type=F32)
    r1 = jnp.sum(jnp.where(lane == i1, before, 0.0), axis=-1, keepdims=True).astype(I32)
    r2 = jnp.sum(jnp.where(lane == i2, before, 0.0), axis=-1, keepdims=True).astype(I32)
    carry_ref[...] += jnp.sum(used, axis=0, keepdims=True)
    cnt_ref[...] = carry_ref[...].astype(I32)
    ids_ref[...] = jnp.where(lane == 0, i1 - N_GROUPS, jnp.where(lane == 1, i2 - N_GROUPS,
                             jnp.where(lane == 2, r1, jnp.where(lane == 3, r2, 0))))


def _post_mixer_kernel(x_ref, a_ref, b_ref, c_ref, wa_ref, wb_ref, wc_ref,
                       kv_ref, gmq_ref, wmq_ref, wmo_ref, mgq_ref, mgk_ref,
                       gffn_ref, whi_ref, wlo_ref, bias_ref,
                       o_ref, ids_ref, gates_ref, cnt_ref, carry_ref):
    @pl.when((pl.program_id(0) == 0) & (pl.program_id(1) == 0))
    def _():
        carry_ref[...] = jnp.zeros_like(carry_ref)

    mix = jnp.dot(a_ref[0], wa_ref[...], preferred_element_type=F32)
    mix += jnp.dot(b_ref[0], wb_ref[...], preferred_element_type=F32)
    mix += jnp.dot(c_ref[0], wc_ref[...], preferred_element_type=F32)
    x = x_ref[0] + mix
    x = _mem_attn_body(x, kv_ref[0], gmq_ref, wmq_ref, wmo_ref, mgq_ref, mgk_ref)
    o_ref[0] = x
    _router_body(x, gffn_ref, whi_ref, wlo_ref, bias_ref, ids_ref, gates_ref, cnt_ref, carry_ref)


def _post_mixer(x, a, bmix, c, w_out, kv, g_mem_q, w_mq, w_mo, mem_gq, mem_gk,
                g_ffn, w_gr, b_gr, w_er, b_er, *, tm):
    b, s, d = x.shape
    m = kv.shape[1]
    nst = s // tm
    wa = w_out[:SWA_WIDTH].astype(BF16)
    wb = w_out[SWA_WIDTH:SWA_WIDTH + HG_WIDTH].astype(BF16)
    wc = w_out[SWA_WIDTH + HG_WIDTH:].astype(BF16)
    pad = LANES - N_GROUPS - N_EXPERTS
    wr = jnp.concatenate([w_gr, w_er, jnp.zeros((d, pad), F32)], axis=1)
    bias = jnp.concatenate([b_gr, b_er, jnp.zeros((pad,), F32)]).reshape(1, LANES)
    whi, wlo = _split_bf16(wr)
    tile = lambda width: pl.BlockSpec((1, tm, width), lambda i, j: (i, j, 0))
    const = lambda shape: pl.BlockSpec(shape, lambda i, j: (0,) * len(shape), pipeline_mode=pl.Buffered(1))
    flat = lambda width: pl.BlockSpec((tm, width), lambda i, j: (i * nst + j, 0))
    return pl.pallas_call(
        _post_mixer_kernel,
        grid=(b, nst),
        in_specs=[tile(d), tile(SWA_WIDTH), tile(HG_WIDTH), tile(MLA_WIDTH),
                  const(wa.shape), const(wb.shape), const(wc.shape),
                  pl.BlockSpec((1, m, 2 * MEM_WIDTH), lambda i, j: (i, 0, 0)),
                  const((1, d)), const((d, MEM_WIDTH)), const((MEM_WIDTH, d)),
                  const((1, MEM_HEAD_DIM)), const((1, MEM_HEAD_DIM)),
                  const((1, d)), const((d, LANES)), const((d, LANES)), const((1, LANES))],
        out_specs=[tile(d), flat(LANES), flat(LANES), pl.BlockSpec((1, LANES), lambda i, j: (0, 0))],
        out_shape=[jax.ShapeDtypeStruct((b, s, d), F32), jax.ShapeDtypeStruct((b * s, LANES), I32),
                   jax.ShapeDtypeStruct((b * s, LANES), F32), jax.ShapeDtypeStruct((1, LANES), I32)],
        scratch_shapes=[pltpu.VMEM((1, LANES), F32)],
        compiler_params=_params("arbitrary", "arbitrary"),
        name="post_mixer",
    )(x, a, bmix, c, wa, wb, wc, kv, g_mem_q.reshape(1, d), w_mq.astype(BF16), w_mo.astype(BF16),
      mem_gq.reshape(1, -1), mem_gk.reshape(1, -1), g_ffn.reshape(1, d), whi, wlo, bias)


HI_HALF_MASK = 0xFFFF0000


def _pack_bf16_pairs(a):
    half = a.shape[1] // 2
    lo = pltpu.bitcast(a[:, :half].astype(BF16).astype(F32), jnp.uint32) >> 16
    hi = pltpu.bitcast(a[:, half:].astype(BF16).astype(F32), jnp.uint32) & jnp.uint32(HI_HALF_MASK)
    return lo | hi


def _unpack_bf16_pairs(w):
    return pltpu.bitcast(w << 16, F32), pltpu.bitcast(w & jnp.uint32(HI_HALF_MASK), F32)


def _dispatch_kernel(dest_ref, x_ref, g_ref, xs_in_hbm, xs_hbm, buf, sem, *, ts):
    del xs_in_hbm
    i = pl.program_id(0)
    last = pl.num_programs(0) - 1
    slot = i % 2

    def row_copy(slot_, r, dst_row):
        return pltpu.make_async_copy(buf.at[slot_, pl.ds(r, 1)], xs_hbm.at[pl.ds(dst_row, 1)], sem.at[slot_])

    def wait_step(slot_):
        for _ in range(ts * TOP_K):
            row_copy(slot_, 0, 0).wait()

    @pl.when(i >= 2)
    def _():
        wait_step(slot)

    buf[slot] = _pack_bf16_pairs(_rms(x_ref[...], g_ref[...]))
    for r in range(ts):
        for kk in range(TOP_K):
            row_copy(slot, r, dest_ref[(i * ts + r) * TOP_K + kk]).start(priority=kk % DMA_PRIORITIES)

    @pl.when(i == last)
    def _():
        wait_step(slot)

    @pl.when((i == last) & (i >= 1))
    def _():
        wait_step(1 - slot)


def _dispatch(x, g_ffn, dest, xs_init, *, ts):
    n, d = x.shape
    n_rows = xs_init.shape[0]
    return pl.pallas_call(
        functools.partial(_dispatch_kernel, ts=ts),
        grid_spec=pltpu.PrefetchScalarGridSpec(
            num_scalar_prefetch=1,
            grid=(n // ts,),
            in_specs=[pl.BlockSpec((ts, d), lambda i, *_: (i, 0)),
                      pl.BlockSpec((1, d), lambda i, *_: (0, 0)),
                      pl.BlockSpec(memory_space=pl.ANY)],
            out_specs=pl.BlockSpec(memory_space=pl.ANY),
            scratch_shapes=[pltpu.VMEM((2, ts, d // 2), jnp.uint32), pltpu.SemaphoreType.DMA((2,))],
        ),
        out_shape=jax.ShapeDtypeStruct((n_rows, d // 2), jnp.uint32),
        input_output_aliases={3: 0},
        compiler_params=_params("arbitrary"),
        name="moe_dispatch",
    )(dest, x, g_ffn.reshape(1, d), xs_init)


def _expert_kernel(bexp_ref, next_ref, nblk_ref, xs_ref, wg_hbm, wu_hbm, wd_hbm, y_ref,
                   wg_f32, wu_f32, wd_f32, sem, wg_bf, wu_bf, wd_bf, *, layer):
    i = pl.program_id(0)
    n_used = nblk_ref[0]
    expert = bexp_ref[i]
    first_of_run = (i == 0) | (expert != bexp_ref[jnp.maximum(i - 1, 0)])
    slot = next_ref[2 * i + 1]

    def weight_copies(e, slot_):
        return [pltpu.make_async_copy(hbm.at[layer, e], buf.at[slot_], sem.at[slot_, j])
                for j, (hbm, buf) in enumerate(((wg_hbm, wg_f32), (wu_hbm, wu_f32), (wd_hbm, wd_f32)))]

    @pl.when(i == 0)
    def _():
        for cp in weight_copies(expert, 0):
            cp.start()

    @pl.when(first_of_run & (i < n_used))
    def _():
        for cp in weight_copies(expert, slot):
            cp.wait()
        next_expert = next_ref[2 * i]

        @pl.when(next_expert >= 0)
        def _():
            for cp in weight_copies(next_expert, 1 - slot):
                cp.start()

        wg_bf[...] = wg_f32[slot].astype(BF16)
        wu_bf[...] = wu_f32[slot].astype(BF16)
        wd_bf[...] = wd_f32[slot].astype(BF16)

    @pl.when(i < n_used)
    def _():
        half = xs_ref.shape[1]
        lo, hi = (part.astype(BF16) for part in _unpack_bf16_pairs(xs_ref[...]))
        gate = (jnp.dot(lo, wg_bf[:half], preferred_element_type=F32)
                + jnp.dot(hi, wg_bf[half:], preferred_element_type=F32))
        up = (jnp.dot(lo, wu_bf[:half], preferred_element_type=F32)
              + jnp.dot(hi, wu_bf[half:], preferred_element_type=F32))
        act = (gate / (1.0 + jnp.exp(-gate)) * up).astype(BF16)
        y_ref[...] = _pack_bf16_pairs(jnp.dot(act, wd_bf[...], preferred_element_type=F32))

    @pl.when(i >= nblk_ref[0])
    def _():
        y_ref[...] = jnp.zeros_like(y_ref)


def _experts(xs, block_expert, run_next, n_used, w_gate, w_up, w_down, layer, *, rows):
    d = 2 * xs.shape[1]
    n_blocks = block_expert.shape[0]
    hbm = pl.BlockSpec(memory_space=pl.ANY)
    return pl.pallas_call(
        functools.partial(_expert_kernel, layer=layer),
        grid_spec=pltpu.PrefetchScalarGridSpec(
            num_scalar_prefetch=3,
            grid=(n_blocks,),
            in_specs=[pl.BlockSpec((rows, d // 2), lambda i, bexp, nxt, nblk: (jnp.minimum(i, nblk[0] - 1), 0)),
                      hbm, hbm, hbm],
            out_specs=pl.BlockSpec((rows, d // 2), lambda i, *_: (i, 0)),
            scratch_shapes=[pltpu.VMEM((2, d, D_EXPERT), F32),
                            pltpu.VMEM((2, d, D_EXPERT), F32),
                            pltpu.VMEM((2, D_EXPERT, d), F32),
                            pltpu.SemaphoreType.DMA((2, 3)),
                            pltpu.VMEM((d, D_EXPERT), BF16),
                            pltpu.VMEM((d, D_EXPERT), BF16),
                            pltpu.VMEM((D_EXPERT, d), BF16)],
        ),
        out_shape=jax.ShapeDtypeStruct((n_blocks * rows, d // 2), jnp.uint32),
        compiler_params=_params("arbitrary"),
        name="moe_experts",
    )(block_expert, run_next, n_used, xs, w_gate, w_up, w_down)


def _combine_kernel(pos_ref, x_ref, gates_ref, y_hbm, o_ref, ybuf, sem, *, tt):
    i = pl.program_id(0)
    nsteps = pl.num_programs(0)

    def row_copy(src_row, slot, dst_row):
        return pltpu.make_async_copy(y_hbm.at[pl.ds(src_row, 1)], ybuf.at[slot, pl.ds(dst_row, 1)], sem.at[slot])

    def start_gather(step, slot):
        for r in range(tt):
            for kk in range(TOP_K):
                row_copy(pos_ref[(step * tt + r) * TOP_K + kk], slot, kk * tt + r).start(priority=kk % DMA_PRIORITIES)

    def wait_gather(slot):
        for r in range(TOP_K * tt):
            row_copy(0, slot, r).wait()

    @pl.when(i == 0)
    def _():
        start_gather(0, 0)

    @pl.when(i + 1 < nsteps)
    def _():
        start_gather(i + 1, (i + 1) % 2)

    wait_gather(i % 2)
    g = gates_ref[...]
    half = ybuf.shape[2]
    for cols, y in zip((pl.ds(0, half), pl.ds(half, half)), _unpack_bf16_pairs(ybuf[i % 2])):
        o_ref[:, cols] = x_ref[:, cols] + g[:, 0:1] * y[:tt] + g[:, 1:2] * y[tt:]


def _combine(x, gates, pos, y, *, tt):
    n, d = x.shape
    return pl.pallas_call(
        functools.partial(_combine_kernel, tt=tt),
        grid_spec=pltpu.PrefetchScalarGridSpec(
            num_scalar_prefetch=1,
            grid=(n // tt,),
            in_specs=[pl.BlockSpec((tt, d), lambda i, *_: (i, 0)),
                      pl.BlockSpec((tt, LANES), lambda i, *_: (i, 0)),
                      pl.BlockSpec(memory_space=pl.ANY)],
            out_specs=pl.BlockSpec((tt, d), lambda i, *_: (i, 0)),
            scratch_shapes=[pltpu.VMEM((2, TOP_K * tt, d // 2), jnp.uint32), pltpu.SemaphoreType.DMA((2,))],
        ),
        out_shape=jax.ShapeDtypeStruct((n, d), F32),
        compiler_params=_params("arbitrary"),
        name="moe_combine",
    )(pos, x, gates, y)


def _moe_plan(ids, cnt, *, rows):
    n = ids.shape[0]
    expert, rank = ids[:, :TOP_K], ids[:, TOP_K:2 * TOP_K]
    counts = cnt[0, N_GROUPS:N_GROUPS + N_EXPERTS]
    padded = (counts + rows - 1) // rows * rows
    padded_end = jnp.cumsum(padded)
    seg_start = padded_end - padded
    onehot = expert[..., None] == jnp.arange(N_EXPERTS, dtype=I32)
    dest = (jnp.sum(jnp.where(onehot, seg_start, 0), axis=-1) + rank).astype(I32).reshape(n * TOP_K)
    n_blocks = -(-(n * TOP_K) // rows) + N_EXPERTS
    block_start = jnp.arange(n_blocks, dtype=I32) * rows
    block_expert = jnp.minimum(jnp.sum(padded_end[None, :] <= block_start[:, None], axis=1), N_EXPERTS - 1).astype(I32)
    n_used = (padded_end[-1] // rows).astype(I32)
    block_onehot = block_expert[:, None] == jnp.arange(N_EXPERTS, dtype=I32)
    pick = lambda per_expert: jnp.sum(jnp.where(block_onehot, per_expert, 0), axis=-1)
    next_block = pick(padded_end // rows)
    next_onehot = jnp.minimum(next_block, n_blocks - 1)[:, None] == jnp.arange(n_blocks, dtype=I32)
    next_expert = jnp.where(next_block < n_used, jnp.sum(jnp.where(next_onehot, block_expert, 0), axis=-1), -1)
    nonempty = (counts > 0).astype(I32)
    run_parity = pick(jnp.cumsum(nonempty) - nonempty) % 2
    run_next = jnp.stack([next_expert, run_parity], axis=-1).astype(I32).reshape(2 * n_blocks)
    return dest, block_expert, run_next, n_used.reshape(1)


def _band_relative_bias(table):
    def bucket(nd):
        max_exact = REL_BUCKETS // 2
        nf = jnp.maximum(nd, 1).astype(F32)
        large = max_exact + (jnp.log(nf / max_exact) / math.log(REL_MAX_DIST / max_exact)
                             * (REL_BUCKETS - max_exact)).astype(I32)
        return jnp.where(nd < max_exact, nd, jnp.minimum(large, REL_BUCKETS - 1))

    qi = jnp.arange(WINDOW)[:, None]
    kj = jnp.arange(2 * WINDOW)[None, :]
    dist = jnp.maximum(qi + WINDOW - kj, 0)
    onehot = (bucket(dist)[..., None] == jnp.arange(REL_BUCKETS)).astype(F32)
    return jnp.einsum('qkb,bh->hqk', onehot, table.astype(F32), precision=lax.Precision.HIGHEST)


def _hgrn_params(lb, g_out):
    lb = lb.reshape(HG_HEADS, 1, HG_DK)
    gain = jnp.broadcast_to(g_out.reshape(1, 1, HG_DV), (HG_HEADS, 1, HG_DV))
    rows = [jnp.log(lb), jnp.log1p(-lb), 1.0 - lb, gain, jnp.zeros((HG_HEADS, 4, HG_DK), F32)]
    return jnp.concatenate(rows, axis=1).astype(F32)


def kernel(x, mem, rel_bias_table, hg_lb_logits, g_mix, w_in, swa_gq, swa_gk, swa_sinks, hg_g_out, mla_g_cq, mla_g_ckv, mla_w_uq, mla_w_ukv, mla_gq, mla_gk, w_out, g_mem_q, g_mem_kv, w_mq, w_mkv, mem_gq, mem_gk, w_mo, g_ffn, w_group_router, b_group_router, w_expert_router, b_expert_router, w_gate, w_up, w_down):
    b, s, d = x.shape
    n = b * s
    m = mem.shape[1]
    rel_bias = _band_relative_bias(rel_bias_table)
    lb_all = jnp.cumsum(jax.nn.softmax(hg_lb_logits.astype(F32), axis=0), axis=0)
    lb_all = lb_all - lb_all[:1]
    offs = [0]
    for width in IN_SIZES:
        offs.append(offs[-1] + width)
    o_hg, o_cq, o_kr = offs[3], offs[7], offs[9]

    xf = x.reshape(n, d)
    mem2 = mem.reshape(b * m, d)
    for l in range(DEPTH):
        w = w_in[l].astype(BF16)
        w_swa = w[:, :o_hg]
        w_hg = w[:, o_hg:o_cq]
        w_mla = jnp.concatenate([w[:, o_cq:], _swap_halves(w[:, o_kr:])], axis=1)
        u_swa = _rms_matmul(xf, g_mix[l], w_swa, tm=1024, tn=w_swa.shape[1], out_dtype=BF16).reshape(b, s, -1)
        u_hg = _rms_matmul(xf, g_mix[l], w_hg, tm=1024, tn=2048, out_dtype=BF16).reshape(b, s, -1)
        u_mla = _rms_matmul(xf, g_mix[l], w_mla, tm=1024, tn=w_mla.shape[1], out_dtype=BF16).reshape(b, s, -1)

        out_a = _swa(u_swa, rel_bias, swa_gq[l], swa_gk[l], swa_sinks[l])
        out_b = _hgrn(u_hg, _hgrn_params(lb_all[l], hg_g_out[l]), chunk=HG_CHUNK, n_sub=HG_SUB)
        qm, km, vm = _mla_prep(u_mla, mla_w_uq[l], mla_w_ukv[l], mla_g_cq[l], mla_g_ckv[l],
                               mla_gq[l], mla_gk[l], tm=512)
        out_c = _mla_attention(qm, km, vm, tq=MLA_TQ, tk=MLA_TK)

        kv = _rms_matmul(mem2, g_mem_kv[l], w_mkv[l].astype(BF16), tm=b * m, tn=2 * MEM_WIDTH)
        x3, ids, gates, cnt = _post_mixer(
            xf.reshape(b, s, d), out_a, out_b, out_c, w_out[l], kv.reshape(b, m, -1),
            g_mem_q[l], w_mq[l], w_mo[l], mem_gq[l], mem_gk[l],
            g_ffn[l], w_group_router[l], b_group_router[l], w_expert_router[l], b_expert_router[l], tm=512)
        xf = x3.reshape(n, d)
        dest, block_expert, run_next, n_used = _moe_plan(ids, cnt, rows=MOE_ROWS)
        xs = _dispatch(xf, g_ffn[l], dest, jnp.zeros((block_expert.shape[0] * MOE_ROWS, d // 2), jnp.uint32),
                       ts=MOE_BURST_TOKENS)
        y = _experts(xs, block_expert, run_next, n_used, w_gate, w_up, w_down, l, rows=MOE_ROWS)
        xf = _combine(xf, gates, dest, y, tt=MOE_BURST_TOKENS)
    return xf.reshape(b, s, d)
```

```python
import functools
import math

import jax
import jax.numpy as jnp
from jax import lax
from jax.experimental import pallas as pl
from jax.experimental.pallas import tpu as pltpu

F32 = jnp.float32
BF16 = jnp.bfloat16
I32 = jnp.int32

D_MODEL = 2048
DEPTH = 2
SWA_HEADS = 8
SWA_KV_HEADS = 2
SWA_HEAD_DIM = 64
WINDOW = 128
HG_HEADS = 8
HG_DK = 128
HG_DV = 128
MLA_HEADS = 4
MLA_Q_RANK = 512
MLA_KV_RANK = 256
MLA_NOPE = 128
MLA_ROPE = 64
MLA_QK = MLA_NOPE + MLA_ROPE
MLA_V = 128
ROPE_THETA = 10000.0
REL_BUCKETS = 32
REL_MAX_DIST = 128
MEM_HEADS = 4
MEM_HEAD_DIM = 128
MEM_WIDTH = MEM_HEADS * MEM_HEAD_DIM
N_GROUPS = 8
EXPERTS_PER_GROUP = 8
N_EXPERTS = N_GROUPS * EXPERTS_PER_GROUP
TOP_K = 2
D_EXPERT = 512
MOE_BURST_TOKENS = 256
MOE_ROWS = 256
EPS = 1e-6
NEG_INF = -1e30
LOG2_E = math.log2(math.e)

SWA_WIDTH = SWA_HEADS * SWA_HEAD_DIM
SWA_KV_WIDTH = SWA_KV_HEADS * SWA_HEAD_DIM
HG_WIDTH = HG_HEADS * HG_DV
MLA_WIDTH = MLA_HEADS * MLA_V
IN_SIZES = (SWA_WIDTH, SWA_KV_WIDTH, SWA_KV_WIDTH,
            HG_HEADS * HG_DK, HG_HEADS * HG_DK, HG_WIDTH, HG_WIDTH,
            MLA_Q_RANK, MLA_KV_RANK, MLA_ROPE)

LANES = 128
SUBLANES = 8
DMA_PRIORITIES = 2
MLA_QK_PAD = 2 * LANES
VMEM_LIMIT_BYTES = 56 * 1024 * 1024

HG_CHUNK = 128
SWA_BLOCKS_PER_STEP = 8
HG_SUB = 16
MLA_TQ = 2048
MLA_TK = 2048
MLA_KEY_GROUP = 1024
MLA_ROW_GROUP = 256
NT_DIMS = (((1,), (1,)), ((), ()))
TN_DIMS = (((0,), (0,)), ((), ()))


def _params(*semantics):
    return pltpu.CompilerParams(dimension_semantics=semantics, vmem_limit_bytes=VMEM_LIMIT_BYTES)


def _rms(x, gain=None):
    y = x * lax.rsqrt(jnp.mean(x * x, axis=-1, keepdims=True) + EPS)
    return y if gain is None else y * gain


def _rms_matmul_kernel(x_ref, g_ref, w_ref, o_ref, hn_ref):
    @pl.when(pl.program_id(1) == 0)
    def _():
        hn_ref[...] = _rms(x_ref[...], g_ref[...]).astype(BF16)

    o_ref[...] = jnp.dot(hn_ref[...], w_ref[...], preferred_element_type=F32).astype(o_ref.dtype)


def _rms_matmul(x, gain, w, *, tm, tn, out_dtype=F32):
    n, d = x.shape
    nout = w.shape[1]
    return pl.pallas_call(
        _rms_matmul_kernel,
        grid=(n // tm, nout // tn),
        in_specs=[pl.BlockSpec((tm, d), lambda i, j: (i, 0)),
                  pl.BlockSpec((1, d), lambda i, j: (0, 0)),
                  pl.BlockSpec((d, tn), lambda i, j: (0, j))],
        out_specs=pl.BlockSpec((tm, tn), lambda i, j: (i, j)),
        out_shape=jax.ShapeDtypeStruct((n, nout), out_dtype),
        scratch_shapes=[pltpu.VMEM((tm, d), BF16)],
        compiler_params=_params("parallel", "arbitrary"),
        name="rms_matmul",
    )(x, gain.reshape(1, d), w)


def _head_rms(x, seg_ref, gain_ref):
    sq = x * x
    hi, lo = _split_bf16(sq)
    ss = (jnp.dot(hi, seg_ref[...], preferred_element_type=F32)
          + jnp.dot(lo, seg_ref[...], preferred_element_type=F32))
    return x * lax.rsqrt(ss * (1.0 / SWA_HEAD_DIM) + EPS) * gain_ref[...]


def _swa_kernel(sink_ref, q_ref, kp_ref, kc_ref, vp_ref, vc_ref, bias_ref, gq_ref, gk_ref, segq_ref, segk_ref, o_ref):
    blk = pl.program_id(1)
    grp = SWA_HEADS // SWA_KV_HEADS
    q_all = _head_rms(q_ref[0].astype(F32), segq_ref, gq_ref)
    k_all = _head_rms(jnp.concatenate([kp_ref[0], kc_ref[0]], axis=0).astype(F32), segk_ref, gk_ref)
    v_all = jnp.concatenate([vp_ref[0], vc_ref[0]], axis=0)
    qi = lax.broadcasted_iota(I32, (WINDOW, 2 * WINDOW), 0)
    kj = lax.broadcasted_iota(I32, (WINDOW, 2 * WINDOW), 1)
    dist = qi + WINDOW - kj
    in_window = (dist >= 0) & (dist < WINDOW)
    first_valid = in_window & (kj >= jnp.where(blk > 0, 0, WINDOW))
    for j in range(q_all.shape[0] // WINDOW):
        q = q_all[j * WINDOW:(j + 1) * WINDOW]
        k = k_all[j * WINDOW:(j + 2) * WINDOW]
        v = v_all[j * WINDOW:(j + 2) * WINDOW]
        valid = first_valid if j == 0 else in_window
        outs = []
        for g in range(SWA_KV_HEADS):
            lo = g * SWA_HEAD_DIM
            kg = k[:, lo:lo + SWA_HEAD_DIM].astype(BF16)
            vg = v[:, lo:lo + SWA_HEAD_DIM].astype(BF16)
            for h in range(g * grp, (g + 1) * grp):
                qh = q[:, h * SWA_HEAD_DIM:(h + 1) * SWA_HEAD_DIM].astype(BF16)
                s = lax.dot_general(qh, kg, NT_DIMS, preferred_element_type=F32) + bias_ref[h]
                s = jnp.where(valid, s, NEG_INF)
                sink = sink_ref[h]
                m = jnp.maximum(jnp.max(s, axis=-1, keepdims=True), sink)
                e = jnp.exp(s - m)
                denom = jnp.sum(e, axis=-1, keepdims=True) + jnp.exp(sink - m)
                outs.append(jnp.dot((e / denom).astype(BF16), vg, preferred_element_type=F32))
        o_ref[0, j * WINDOW:(j + 1) * WINDOW] = jnp.concatenate(outs, axis=-1).astype(o_ref.dtype)


def _swa(u_swa, rel_bias, gq, gk, sinks):
    b, s, _ = u_swa.shape
    rows = SWA_BLOCKS_PER_STEP * WINDOW
    kcol = SWA_WIDTH // LANES
    vcol = kcol + 1
    prev = lambda bi, n, *_: (bi, jnp.maximum(n * SWA_BLOCKS_PER_STEP - 1, 0))
    grp = SWA_HEADS // SWA_KV_HEADS

    def head_blocks(width):
        head = jnp.arange(width) // SWA_HEAD_DIM
        return (head[:, None] == head[None, :]).astype(BF16)

    return pl.pallas_call(
        _swa_kernel,
        grid_spec=pltpu.PrefetchScalarGridSpec(
            num_scalar_prefetch=1,
            grid=(b, s // rows),
            in_specs=[
                pl.BlockSpec((1, rows, SWA_WIDTH), lambda bi, n, *_: (bi, n, 0)),
                pl.BlockSpec((1, WINDOW, LANES), lambda bi, n, *_: prev(bi, n) + (kcol,)),
                pl.BlockSpec((1, rows, LANES), lambda bi, n, *_: (bi, n, kcol)),
                pl.BlockSpec((1, WINDOW, LANES), lambda bi, n, *_: prev(bi, n) + (vcol,)),
                pl.BlockSpec((1, rows, LANES), lambda bi, n, *_: (bi, n, vcol)),
                pl.BlockSpec((SWA_HEADS, WINDOW, 2 * WINDOW), lambda bi, n, *_: (0, 0, 0)),
                pl.BlockSpec((1, SWA_WIDTH), lambda bi, n, *_: (0, 0)),
                pl.BlockSpec((1, SWA_KV_WIDTH), lambda bi, n, *_: (0, 0)),
                pl.BlockSpec((SWA_WIDTH, SWA_WIDTH), lambda bi, n, *_: (0, 0)),
                pl.BlockSpec((SWA_KV_WIDTH, SWA_KV_WIDTH), lambda bi, n, *_: (0, 0)),
            ],
            out_specs=pl.BlockSpec((1, rows, SWA_WIDTH), lambda bi, n, *_: (bi, n, 0)),
        ),
        out_shape=jax.ShapeDtypeStruct((b, s, SWA_WIDTH), BF16),
        compiler_params=_params("parallel", "parallel"),
        name="swa_attention",
    )(sinks, u_swa, u_swa, u_swa, u_swa, u_swa, rel_bias,
      jnp.tile(gq * (SWA_HEAD_DIM ** -0.5), SWA_HEADS).reshape(1, SWA_WIDTH),
      jnp.tile(gk, SWA_KV_HEADS).reshape(1, SWA_KV_WIDTH), head_blocks(SWA_WIDTH), head_blocks(SWA_KV_WIDTH))


def _hgrn_level_matrix(c):
    t = jnp.arange(c)[:, None]
    r = jnp.arange(c)[None, :]
    mats = [(r <= t)]
    half = c // 2
    while half >= 1:
        mid = (t // (2 * half)) * (2 * half) + half
        is_q = (t & half) != 0
        if half < SUBLANES:
            mats.append(jnp.where(is_q, (r >= mid) & (r <= t), (r > t) & (r < mid)))
        half //= 2
    return jnp.concatenate(mats, axis=0).astype(BF16)


def _hgrn_pair_level(c):
    t = jnp.arange(c, dtype=I32)[:, None]
    s = jnp.arange(c, dtype=I32)[None, :]
    x = t ^ s
    lvl = jnp.zeros((c, c), I32)
    half = c // 2
    while half >= 1:
        lvl = jnp.where((x & (-half)) == half, half, lvl)
        half //= 2
    return jnp.where(t > s, lvl, 0)


def _hgrn_kernel(q_ref, f_ref, v_ref, gate_ref, par_ref, amat_ref, lvl_ref, o_ref, st_ref, *, chunk, n_sub):
    @pl.when(pl.program_id(2) == 0)
    def _():
        st_ref[...] = jnp.zeros_like(st_ref)

    c = chunk
    par = par_ref[0]
    log_lb, log1m_lb, one_m_lb, g_out = par[0:1], par[1:2], par[2:3], par[3:4]
    row = lax.broadcasted_iota(I32, (c, 1), 0)
    pair_level = lvl_ref[...]
    st = st_ref[...]
    for sub in range(n_sub):
        rows = pl.ds(sub * c, c)
        q = q_ref[0, rows].astype(F32)
        f = f_ref[0, rows].astype(F32)
        v = v_ref[0, rows].astype(F32)
        gate = gate_ref[0, rows].astype(F32)

        e = jnp.exp(-jnp.abs(f))
        log_sig = jnp.minimum(f, 0.0) - jnp.log(1.0 + e)
        bb = log1m_lb + log_sig
        log_f = jnp.maximum(log_lb, bb) + jnp.log(1.0 + jnp.exp(-jnp.abs(log_lb - bb)))
        log_f = log_f * LOG2_E
        kk = one_m_lb * jnp.where(f < 0.0, 1.0, e) / (1.0 + e)

        g_hi = log_f.astype(BF16)
        g_lo = (log_f - g_hi.astype(F32)).astype(BF16)
        e2 = jnp.dot(amat_ref[...], jnp.concatenate([g_hi, g_lo], axis=1), preferred_element_type=F32)
        expo = e2[:, :HG_DK] + e2[:, HG_DK:]
        bcum = expo[0:c]
        btot = bcum[c - 1:c]

        attn = jnp.zeros((c, c), F32)
        half = c // 2
        lvl = 1
        while half >= 1:
            is_q = (row & half) != 0
            seg = 2 * half
            if half >= SUBLANES:
                ref = jnp.concatenate([jnp.broadcast_to(bcum[a + half - 1:a + half], (seg, HG_DK))
                                       for a in range(0, c, seg)], axis=0)
                diff = bcum - ref
                w = jnp.exp2(jnp.where(is_q, diff, -diff))
            else:
                w = jnp.exp2(expo[lvl * c:(lvl + 1) * c])
                lvl += 1
            zf = jnp.where(is_q, q, kk) * w
            z = zf.astype(BF16)
            if half >= SUBLANES:
                zq = jnp.concatenate([zf[a + half:a + seg] for a in range(0, c, seg)], axis=0).astype(BF16)
                aq = lax.dot_general(zq, z, NT_DIMS, preferred_element_type=F32)
                blank = jnp.zeros((half, c), F32)
                a = jnp.concatenate([blk for j in range(c // seg)
                                     for blk in (blank, aq[j * half:(j + 1) * half])], axis=0)
            else:
                a = lax.dot_general(z, z, NT_DIMS, preferred_element_type=F32)
            attn = jnp.where(pair_level == half, a, attn)
            half //= 2

        vb = v.astype(BF16)
        diag = jnp.sum(q * kk, axis=-1, keepdims=True)
        intra = jnp.dot(attn.astype(BF16), vb, preferred_element_type=F32) + diag * v
        qe = (q * jnp.exp2(bcum)).astype(BF16)
        inter = lax.dot_general(qe, st.astype(BF16), NT_DIMS, preferred_element_type=F32)
        kd = (kk * jnp.exp2(btot - bcum)).astype(BF16)
        st = st * jnp.exp2(btot) +lax.dot_general(vb, kd, TN_DIMS, preferred_element_type=F32)

        o = _rms(inter + intra, g_out)
        o_ref[0, rows] = (o * gate / (1.0 + jnp.exp(-gate))).astype(o_ref.dtype)
    st_ref[...] = st


def _hgrn(u_hg, par, *, chunk, n_sub):
    b, s, _ = u_hg.shape
    amat = _hgrn_level_matrix(chunk)
    pair_level = _hgrn_pair_level(chunk)
    step = chunk * n_sub
    col = lambda off: (lambda bi, h, c: (bi, c, off + h))
    return pl.pallas_call(
        functools.partial(_hgrn_kernel, chunk=chunk, n_sub=n_sub),
        grid=(b, HG_HEADS, s // step),
        in_specs=[pl.BlockSpec((1, step, HG_DK), col(0)),
                  pl.BlockSpec((1, step, HG_DK), col(HG_HEADS)),
                  pl.BlockSpec((1, step, HG_DV), col(2 * HG_HEADS)),
                  pl.BlockSpec((1, step, HG_DV), col(3 * HG_HEADS)),
                  pl.BlockSpec((1, 8, HG_DK), lambda bi, h, c: (h, 0, 0)),
                  pl.BlockSpec(amat.shape, lambda bi, h, c: (0, 0)),
                  pl.BlockSpec(pair_level.shape, lambda bi, h, c: (0, 0))],
        out_specs=pl.BlockSpec((1, step, HG_DV), lambda bi, h, c: (bi, c, h)),
        out_shape=jax.ShapeDtypeStruct((b, s, HG_WIDTH), BF16),
        scratch_shapes=[pltpu.VMEM((HG_DV, HG_DK), F32)],
        compiler_params=_params("parallel", "parallel", "arbitrary"),
        name="hgrn2",
    )(u_hg, u_hg, u_hg, u_hg, par, amat, pair_level)


def _mla_prep_kernel(u_ref, tab_ref, wq_ref, wkv_ref, gcq_ref, gckv_ref, gqn_ref, gqr_ref, gkn_ref, gkr_ref,
                     q_ref, k_ref, v_ref):
    u = u_ref[0].astype(F32)
    cq = _rms(u[:, :MLA_Q_RANK], gcq_ref[...]).astype(BF16)
    ckv = _rms(u[:, MLA_Q_RANK:MLA_Q_RANK + MLA_KV_RANK], gckv_ref[...]).astype(BF16)
    kr = u[:, MLA_Q_RANK + MLA_KV_RANK:]
    qf = jnp.dot(cq, wq_ref[...], preferred_element_type=F32)
    kvf = jnp.dot(ckv, wkv_ref[...], preferred_element_type=F32)
    tab = tab_ref[...]
    low = lax.broadcasted_iota(I32, kr.shape, 1) < MLA_ROPE
    kr_sq = jnp.sum(jnp.where(low, kr * kr, 0.0), axis=-1, keepdims=True)
    scale = MLA_QK ** -0.5 * LOG2_E
    ones = jnp.ones((u.shape[0], MLA_V), F32)
    for h in range(MLA_HEADS):
        qn = qf[:, h * MLA_QK_PAD:h * MLA_QK_PAD + MLA_NOPE]
        qr = qf[:, h * MLA_QK_PAD + MLA_NOPE:(h + 1) * MLA_QK_PAD]
        ss = jnp.sum(qn * qn, axis=-1, keepdims=True) + jnp.sum(jnp.where(low, qr * qr, 0.0), axis=-1, keepdims=True)
        rstd = lax.rsqrt(ss / MLA_QK + EPS) * scale
        t = qr * rstd * tab * gqr_ref[...]
        rope = jnp.where(low, t + pltpu.roll(t, MLA_ROPE, 1), 0.0)
        q_ref[0, h] = jnp.concatenate([qn * rstd * gqn_ref[...], rope], axis=-1).astype(q_ref.dtype)

        kn = kvf[:, h * MLA_NOPE:(h + 1) * MLA_NOPE]
        ss = jnp.sum(kn * kn, axis=-1, keepdims=True) + kr_sq
        rstd = lax.rsqrt(ss / MLA_QK + EPS)
        t = kr * rstd * tab * gkr_ref[...]
        rope = t + pltpu.roll(t, MLA_ROPE, 1)
        k_ref[0, h] = jnp.concatenate([kn * rstd * gkn_ref[...], rope], axis=-1).astype(k_ref.dtype)
        vh = kvf[:, MLA_HEADS * MLA_NOPE + h * MLA_V:MLA_HEADS * MLA_NOPE + (h + 1) * MLA_V]
        v_ref[0, h] = jnp.concatenate([vh, ones], axis=-1).astype(v_ref.dtype)


def _swap_halves(a):
    half = a.shape[-1] // 2
    return jnp.concatenate([a[..., half:], a[..., :half]], axis=-1)


def _mla_prep(u_mla, w_uq, w_ukv, g_cq, g_ckv, gq, gk, *, tm):
    b, s, wu = u_mla.shape
    wq = w_uq.reshape(MLA_Q_RANK, MLA_HEADS, MLA_QK)
    wq = jnp.concatenate([wq, _swap_halves(wq[..., MLA_NOPE:])], axis=-1).reshape(MLA_Q_RANK, MLA_HEADS * MLA_QK_PAD)
    wkv = w_ukv.reshape(MLA_KV_RANK, MLA_HEADS, MLA_NOPE + MLA_V)
    wkv = jnp.concatenate([wkv[..., :MLA_NOPE].reshape(MLA_KV_RANK, -1), wkv[..., MLA_NOPE:].reshape(MLA_KV_RANK, -1)], axis=1)
    half = MLA_ROPE // 2
    inv_freq = ROPE_THETA ** (-jnp.arange(half, dtype=F32) / half)
    ang = jnp.arange(s, dtype=F32)[:, None] * inv_freq[None, :]
    cos, sin = jnp.cos(ang), jnp.sin(ang)
    tab = jnp.concatenate([cos, cos, -sin, sin], axis=-1)
    rope_gain = lambda g: jnp.concatenate([g[MLA_NOPE:], _swap_halves(g[MLA_NOPE:])]).reshape(1, 2 * MLA_ROPE)
    vec = lambda i, j: (0, 0)
    head_out = lambda width: pl.BlockSpec((1, MLA_HEADS, tm, width), lambda i, j: (i, 0, j, 0))
    return pl.pallas_call(
        _mla_prep_kernel,
        grid=(b, s // tm),
        in_specs=[pl.BlockSpec((1, tm, wu), lambda i, j: (i, j, 0)),
                  pl.BlockSpec((tm, 2 * MLA_ROPE), lambda i, j: (j, 0)),
                  pl.BlockSpec(wq.shape, vec),
                  pl.BlockSpec(wkv.shape, vec),
                  pl.BlockSpec((1, MLA_Q_RANK), vec),
                  pl.BlockSpec((1, MLA_KV_RANK), vec),
                  pl.BlockSpec((1, MLA_NOPE), vec),
                  pl.BlockSpec((1, 2 * MLA_ROPE), vec),
                  pl.BlockSpec((1, MLA_NOPE), vec),
                  pl.BlockSpec((1, 2 * MLA_ROPE), vec)],
        out_specs=[head_out(MLA_QK_PAD), head_out(MLA_QK_PAD), head_out(2 * MLA_V)],
        out_shape=[jax.ShapeDtypeStruct((b, MLA_HEADS, s, MLA_QK_PAD), BF16),
                   jax.ShapeDtypeStruct((b, MLA_HEADS, s, MLA_QK_PAD), BF16),
                   jax.ShapeDtypeStruct((b, MLA_HEADS, s, 2 * MLA_V), BF16)],
        compiler_params=_params("parallel", "parallel"),
        name="mla_prep",
    )(u_mla, tab, wq.astype(BF16), wkv.astype(BF16), g_cq.reshape(1, -1), g_ckv.reshape(1, -1),
      gq[:MLA_NOPE].reshape(1, -1), rope_gain(gq), gk[:MLA_NOPE].reshape(1, -1), rope_gain(gk))


def _mla_attn_kernel(qi_ref, ki_ref, q_ref, k_ref, v_ref, o_ref, m_ref, acc_ref, *, tq, tk):
    step = pl.program_id(2)
    qi = qi_ref[step]
    ki = ki_ref[step]

    @pl.when(ki == 0)
    def _():
        m_ref[...] = jnp.full_like(m_ref, NEG_INF)
        acc_ref[...] = jnp.zeros_like(acc_ref)

    def update(rel):
        for r0, k0 in [(r, k) for r in range(0, tq, MLA_ROW_GROUP) for k in range(0, tk, MLA_KEY_GROUP)]:
            first_key = None if rel is None else rel + k0
            if first_key is not None and first_key > r0 + MLA_ROW_GROUP - 1:
                continue
            rows = pl.ds(r0, MLA_ROW_GROUP)
            keys = pl.ds(k0, MLA_KEY_GROUP)
            s = lax.dot_general(q_ref[0, 0, rows], k_ref[0, 0, keys], NT_DIMS, preferred_element_type=F32)
            if first_key is not None and first_key + MLA_KEY_GROUP - 1 > r0:
                row = r0 + lax.broadcasted_iota(I32, s.shape, 0)
                col = first_key + lax.broadcasted_iota(I32, s.shape, 1)
                s = jnp.where(col <= row, s, NEG_INF)
            m_prev = m_ref[rows]
            m_next = jnp.maximum(m_prev, jnp.max(s, axis=-1, keepdims=True))
            alpha = jnp.exp2(m_prev - m_next)
            p = jnp.exp2(s - m_next[:, :1]).astype(BF16)
            pv = jnp.dot(p, v_ref[0, 0, keys], preferred_element_type=F32)
            acc_ref[rows, :MLA_V] = alpha * acc_ref[rows, :MLA_V] + pv[:, :MLA_V]
            acc_ref[rows, MLA_V:] = alpha * acc_ref[rows, MLA_V:] + pv[:, MLA_V:]
            m_ref[rows] = m_next

    key_offset = ki * tk - qi * tq
    pl.when(key_offset < 0)(lambda: update(None))
    for rel in range(0, tq, tk):
        pl.when(key_offset == rel)(functools.partial(update, rel))

    @pl.when((ki + 1) * tk == (qi + 1) * tq)
    def _():
        o_ref[0] = (acc_ref[:, :MLA_V] / acc_ref[:, MLA_V:]).astype(o_ref.dtype)


def _mla_attention(q, k, v, *, tq, tk):
    b, h, s, _ = q.shape
    pairs = [(qi, ki) for qi in range(s // tq) for ki in range((qi + 1) * tq // tk)]
    qi_of = jnp.array([p[0] for p in pairs], I32)
    ki_of = jnp.array([p[1] for p in pairs], I32)
    kv_idx = lambda bi, hi, t, qi_ref, ki_ref: (bi, hi, ki_ref[t], 0)
    return pl.pallas_call(
        functools.partial(_mla_attn_kernel, tq=tq, tk=tk),
        grid_spec=pltpu.PrefetchScalarGridSpec(
            num_scalar_prefetch=2,
            grid=(b, h, len(pairs)),
            in_specs=[pl.BlockSpec((1, 1, tq, MLA_QK_PAD), lambda bi, hi, t, qi_ref, ki_ref: (bi, hi, qi_ref[t], 0)),
                      pl.BlockSpec((1, 1, tk, MLA_QK_PAD), kv_idx),
                      pl.BlockSpec((1, 1, tk, 2 * MLA_V), kv_idx)],
            out_specs=pl.BlockSpec((1, tq, MLA_V), lambda bi, hi, t, qi_ref, ki_ref: (bi, qi_ref[t], hi)),
            scratch_shapes=[pltpu.VMEM((tq, MLA_V), F32), pltpu.VMEM((tq, 2 * MLA_V), F32)],
        ),
        out_shape=jax.ShapeDtypeStruct((b, s, h * MLA_V), BF16),
        compiler_params=_params("parallel", "parallel", "arbitrary"),
        name="mla_attention",
    )(qi_of, ki_of, q, k, v)


def _mem_attn_body(x, kv, g_ref, wq_ref, wo_ref, gq_ref, gk_ref):
    hn = _rms(x, g_ref[...]).astype(BF16)
    qf = jnp.dot(hn, wq_ref[...], preferred_element_type=F32)
    outs = []
    for h in range(MEM_HEADS):
        lo = h * MEM_HEAD_DIM
        qh = (_rms(qf[:, lo:lo + MEM_HEAD_DIM], gq_ref[...]) * (MEM_HEAD_DIM ** -0.5)).astype(BF16)
        kh = _rms(kv[:, lo:lo + MEM_HEAD_DIM], gk_ref[...]).astype(BF16)
        vh = kv[:, MEM_WIDTH + lo:MEM_WIDTH + lo + MEM_HEAD_DIM].astype(BF16)
        s = lax.dot_general(qh, kh, NT_DIMS, preferred_element_type=F32)
        e = jnp.exp(s - jnp.max(s, axis=-1, keepdims=True))
        p = (e / jnp.sum(e, axis=-1, keepdims=True)).astype(BF16)
        outs.append(jnp.dot(p, vh, preferred_element_type=F32))
    o = jnp.concatenate(outs, axis=-1).astype(BF16)
    return x + jnp.dot(o, wo_ref[...], preferred_element_type=F32)


def _split_bf16(a):
    hi = a.astype(BF16)
    return hi, (a - hi.astype(F32)).astype(BF16)


def _router_body(x, g_ref, whi_ref, wlo_ref, b_ref, ids_ref, gates_ref, cnt_ref, carry_ref):
    hn = _rms(x, g_ref[...])
    hi, lo = _split_bf16(hn)
    logits = (jnp.dot(hi, whi_ref[...], preferred_element_type=F32)
              + jnp.dot(hi, wlo_ref[...], preferred_element_type=F32)
              + jnp.dot(lo, whi_ref[...], preferred_element_type=F32)) + b_ref[...]
    lane = lax.broadcasted_iota(I32, logits.shape, 1)
    gl = jnp.where(lane < N_GROUPS, logits, NEG_INF)
    gmax = jnp.max(gl, axis=-1, keepdims=True)
    p_grp = 1.0 / jnp.sum(jnp.exp(gl - gmax), axis=-1, keepdims=True)
    grp = jnp.min(jnp.where(gl == gmax, lane, LANES), axis=-1, keepdims=True)
    in_grp = (lane >= N_GROUPS) & (lane < N_GROUPS + N_EXPERTS) & (((lane - N_GROUPS) // EXPERTS_PER_GROUP) == grp)
    el = jnp.where(in_grp, logits, NEG_INF)
    m1 = jnp.max(el, axis=-1, keepdims=True)
    i1 = jnp.min(jnp.where(el == m1, lane, LANES), axis=-1, keepdims=True)
    el2 = jnp.where(lane == i1, NEG_INF, el)
    m2 = jnp.max(el2, axis=-1, keepdims=True)
    i2 = jnp.min(jnp.where(el2 == m2, lane, LANES), axis=-1, keepdims=True)
    r = jnp.exp(m2 - m1)
    g1 = p_grp / (1.0 + r)
    gates_ref[...] = jnp.where(lane == 0, g1, jnp.where(lane == 1, g1 * r, 0.0))

    tm = logits.shape[0]
    used = jnp.where((lane == i1) | (lane == i2), 1.0, 0.0)
    earlier = lax.broadcasted_iota(I32, (tm, tm), 1) < lax.broadcasted_iota(I32, (tm, tm), 0)
    before = carry_ref[...] + jnp.dot(jnp.where(earlier, 1.0, 0.0).astype(BF16), used.astype(BF16),
                                      preferred_element_type=F32)
    r1 = jnp.sum(jnp.where(lane == i1, before, 0.0), axis=-1, keepdims=True).astype(I32)
    r2 = jnp.sum(jnp.where(lane == i2, before, 0.0), axis=-1, keepdims=True).astype(I32)
    carry_ref[...] += jnp.sum(used, axis=0, keepdims=True)
    cnt_ref[...] = carry_ref[...].astype(I32)
    ids_ref[...] = jnp.where(lane == 0, i1 - N_GROUPS, jnp.where(lane == 1, i2 - N_GROUPS,
                             jnp.where(lane == 2, r1, jnp.where(lane == 3, r2, 0))))


def _post_mixer_kernel(x_ref, a_ref, b_ref, c_ref, wa_ref, wb_ref, wc_ref,
                       kv_ref, gmq_ref, wmq_ref, wmo_ref, mgq_ref, mgk_ref,
                       gffn_ref, whi_ref, wlo_ref, bias_ref,
                       o_ref, ids_ref, gates_ref, cnt_ref, carry_ref):
    @pl.when((pl.program_id(0) == 0) & (pl.program_id(1) == 0))
    def _():
        carry_ref[...] = jnp.zeros_like(carry_ref)

    mix = jnp.dot(a_ref[0], wa_ref[...], preferred_element_type=F32)
    mix += jnp.dot(b_ref[0], wb_ref[...], preferred_element_type=F32)
    mix += jnp.dot(c_ref[0], wc_ref[...], preferred_element_type=F32)
    x = x_ref[0] + mix
    x = _mem_attn_body(x, kv_ref[0], gmq_ref, wmq_ref, wmo_ref, mgq_ref, mgk_ref)
    o_ref[0] = x
    _router_body(x, gffn_ref, whi_ref, wlo_ref, bias_ref, ids_ref, gates_ref, cnt_ref, carry_ref)


def _post_mixer(x, a, bmix, c, w_out, kv, g_mem_q, w_mq, w_mo, mem_gq, mem_gk,
                g_ffn, w_gr, b_gr, w_er, b_er, *, tm):
    b, s, d = x.shape
    m = kv.shape[1]
    nst = s // tm
    wa = w_out[:SWA_WIDTH].astype(BF16)
    wb = w_out[SWA_WIDTH:SWA_WIDTH + HG_WIDTH].astype(BF16)
    wc = w_out[SWA_WIDTH + HG_WIDTH:].astype(BF16)
    pad = LANES - N_GROUPS - N_EXPERTS
    wr = jnp.concatenate([w_gr, w_er, jnp.zeros((d, pad), F32)], axis=1)
    bias = jnp.concatenate([b_gr, b_er, jnp.zeros((pad,), F32)]).reshape(1, LANES)
    whi, wlo = _split_bf16(wr)
    tile = lambda width: pl.BlockSpec((1, tm, width), lambda i, j: (i, j, 0))
    const = lambda shape: pl.BlockSpec(shape, lambda i, j: (0,) * len(shape), pipeline_mode=pl.Buffered(1))
    flat = lambda width: pl.BlockSpec((tm, width), lambda i, j: (i * nst + j, 0))
    return pl.pallas_call(
        _post_mixer_kernel,
        grid=(b, nst),
        in_specs=[tile(d), tile(SWA_WIDTH), tile(HG_WIDTH), tile(MLA_WIDTH),
                  const(wa.shape), const(wb.shape), const(wc.shape),
                  pl.BlockSpec((1, m, 2 * MEM_WIDTH), lambda i, j: (i, 0, 0)),
                  const((1, d)), const((d, MEM_WIDTH)), const((MEM_WIDTH, d)),
                  const((1, MEM_HEAD_DIM)), const((1, MEM_HEAD_DIM)),
                  const((1, d)), const((d, LANES)), const((d, LANES)), const((1, LANES))],
        out_specs=[tile(d), flat(LANES), flat(LANES), pl.BlockSpec((1, LANES), lambda i, j: (0, 0))],
        out_shape=[jax.ShapeDtypeStruct((b, s, d), F32), jax.ShapeDtypeStruct((b * s, LANES), I32),
                   jax.ShapeDtypeStruct((b * s, LANES), F32), jax.ShapeDtypeStruct((1, LANES), I32)],
        scratch_shapes=[pltpu.VMEM((1, LANES), F32)],
        compiler_params=_params("arbitrary", "arbitrary"),
        name="post_mixer",
    )(x, a, bmix, c, wa, wb, wc, kv, g_mem_q.reshape(1, d), w_mq.astype(BF16), w_mo.astype(BF16),
      mem_gq.reshape(1, -1), mem_gk.reshape(1, -1), g_ffn.reshape(1, d), whi, wlo, bias)


HI_HALF_MASK = 0xFFFF0000


def _pack_bf16_pairs(a):
    half = a.shape[1] // 2
    lo = pltpu.bitcast(a[:, :half].astype(BF16).astype(F32), jnp.uint32) >> 16
    hi = pltpu.bitcast(a[:, half:].astype(BF16).astype(F32), jnp.uint32) & jnp.uint32(HI_HALF_MASK)
    return lo | hi


def _unpack_bf16_pairs(w):
    return pltpu.bitcast(w << 16, F32), pltpu.bitcast(w & jnp.uint32(HI_HALF_MASK), F32)


def _dispatch_kernel(dest_ref, x_ref, g_ref, xs_in_hbm, xs_hbm, buf, sem, *, ts):
    del xs_in_hbm
    i = pl.program_id(0)
    last = pl.num_programs(0) - 1
    slot = i % 2

    def row_copy(slot_, r, dst_row):
        return pltpu.make_async_copy(buf.at[slot_, pl.ds(r, 1)], xs_hbm.at[pl.ds(dst_row, 1)], sem.at[slot_])

    def wait_step(slot_):
        for _ in range(ts * TOP_K):
            row_copy(slot_, 0, 0).wait()

    @pl.when(i >= 2)
    def _():
        wait_step(slot)

    buf[slot] = _pack_bf16_pairs(_rms(x_ref[...], g_ref[...]))
    for r in range(ts):
        for kk in range(TOP_K):
            row_copy(slot, r, dest_ref[(i * ts + r) * TOP_K + kk]).start(priority=kk % DMA_PRIORITIES)

    @pl.when(i == last)
    def _():
        wait_step(slot)

    @pl.when((i == last) & (i >= 1))
    def _():
        wait_step(1 - slot)


def _dispatch(x, g_ffn, dest, xs_init, *, ts):
    n, d = x.shape
    n_rows = xs_init.shape[0]
    return pl.pallas_call(
        functools.partial(_dispatch_kernel, ts=ts),
        grid_spec=pltpu.PrefetchScalarGridSpec(
            num_scalar_prefetch=1,
            grid=(n // ts,),
            in_specs=[pl.BlockSpec((ts, d), lambda i, *_: (i, 0)),
                      pl.BlockSpec((1, d), lambda i, *_: (0, 0)),
                      pl.BlockSpec(memory_space=pl.ANY)],
            out_specs=pl.BlockSpec(memory_space=pl.ANY),
            scratch_shapes=[pltpu.VMEM((2, ts, d // 2), jnp.uint32), pltpu.SemaphoreType.DMA((2,))],
        ),
        out_shape=jax.ShapeDtypeStruct((n_rows, d // 2), jnp.uint32),
        input_output_aliases={3: 0},
        compiler_params=_params("arbitrary"),
        name="moe_dispatch",
    )(dest, x, g_ffn.reshape(1, d), xs_init)


def _expert_kernel(bexp_ref, next_ref, nblk_ref, xs_ref, wg_hbm, wu_hbm, wd_hbm, y_ref,
                   wg_f32, wu_f32, wd_f32, sem, wg_bf, wu_bf, wd_bf, *, layer):
    i = pl.program_id(0)
    n_used = nblk_ref[0]
    expert = bexp_ref[i]
    first_of_run = (i == 0) | (expert != bexp_ref[jnp.maximum(i - 1, 0)])
    slot = next_ref[2 * i + 1]

    def weight_copies(e, slot_):
        return [pltpu.make_async_copy(hbm.at[layer, e], buf.at[slot_], sem.at[slot_, j])
                for j, (hbm, buf) in enumerate(((wg_hbm, wg_f32), (wu_hbm, wu_f32), (wd_hbm, wd_f32)))]

    @pl.when(i == 0)
    def _():
        for j, cp in enumerate(weight_copies(expert, 0)):
            cp.start(priority=j % DMA_PRIORITIES)

    @pl.when(first_of_run & (i < n_used))
    def _():
        for cp in weight_copies(expert, slot):
            cp.wait()
        next_expert = next_ref[2 * i]

        @pl.when(next_expert >= 0)
        def _():
            for j, cp in enumerate(weight_copies(next_expert, 1 - slot)):
                cp.start(priority=j % DMA_PRIORITIES)

        wg_bf[...] = wg_f32[slot].astype(BF16)
        wu_bf[...] = wu_f32[slot].astype(BF16)
        wd_bf[...] = wd_f32[slot].astype(BF16)

    @pl.when(i < n_used)
    def _():
        half = xs_ref.shape[1]
        lo, hi = (part.astype(BF16) for part in _unpack_bf16_pairs(xs_ref[...]))
        gate = (jnp.dot(lo, wg_bf[:half], preferred_element_type=F32)
                + jnp.dot(hi, wg_bf[half:], preferred_element_type=F32))
        up = (jnp.dot(lo, wu_bf[:half], preferred_element_type=F32)
              + jnp.dot(hi, wu_bf[half:], preferred_element_type=F32))
        act = (gate / (1.0 + jnp.exp(-gate)) * up).astype(BF16)
        y_ref[...] = _pack_bf16_pairs(jnp.dot(act, wd_bf[...], preferred_element_type=F32))

    @pl.when(i >= nblk_ref[0])
    def _():
        y_ref[...] = jnp.zeros_like(y_ref)


def _experts(xs, block_expert, run_next, n_used, w_gate, w_up, w_down, layer, *, rows):
    d = 2 * xs.shape[1]
    n_blocks = block_expert.shape[0]
    hbm = pl.BlockSpec(memory_space=pl.ANY)
    return pl.pallas_call(
        functools.partial(_expert_kernel, layer=layer),
        grid_spec=pltpu.PrefetchScalarGridSpec(
            num_scalar_prefetch=3,
            grid=(n_blocks,),
            in_specs=[pl.BlockSpec((rows, d // 2), lambda i, bexp, nxt, nblk: (jnp.minimum(i, nblk[0] - 1), 0)),
                      hbm, hbm, hbm],
            out_specs=pl.BlockSpec((rows, d // 2), lambda i, *_: (i, 0)),
            scratch_shapes=[pltpu.VMEM((2, d, D_EXPERT), F32),
                            pltpu.VMEM((2, d, D_EXPERT), F32),
                            pltpu.VMEM((2, D_EXPERT, d), F32),
                            pltpu.SemaphoreType.DMA((2, 3)),
                            pltpu.VMEM((d, D_EXPERT), BF16),
                            pltpu.VMEM((d, D_EXPERT), BF16),
                            pltpu.VMEM((D_EXPERT, d), BF16)],
        ),
        out_shape=jax.ShapeDtypeStruct((n_blocks * rows, d // 2), jnp.uint32),
        compiler_params=_params("arbitrary"),
        name="moe_experts",
    )(block_expert, run_next, n_used, xs, w_gate, w_up, w_down)


def _combine_kernel(pos_ref, x_ref, gates_ref, y_hbm, o_ref, ybuf, sem, *, tt):
    i = pl.program_id(0)
    nsteps = pl.num_programs(0)

    def row_copy(src_row, slot, dst_row):
        return pltpu.make_async_copy(y_hbm.at[pl.ds(src_row, 1)], ybuf.at[slot, pl.ds(dst_row, 1)], sem.at[slot])

    def start_gather(step, slot):
        for r in range(tt):
            for kk in range(TOP_K):
                row_copy(pos_ref[(step * tt + r) * TOP_K + kk], slot, kk * tt + r).start(priority=kk % DMA_PRIORITIES)

    def wait_gather(slot):
        for r in range(TOP_K * tt):
            row_copy(0, slot, r).wait()

    @pl.when(i == 0)
    def _():
        start_gather(0, 0)

    @pl.when(i + 1 < nsteps)
    def _():
        start_gather(i + 1, (i + 1) % 2)

    wait_gather(i % 2)
    g = gates_ref[...]
    half = ybuf.shape[2]
    for cols, y in zip((pl.ds(0, half), pl.ds(half, half)), _unpack_bf16_pairs(ybuf[i % 2])):
        o_ref[:, cols] = x_ref[:, cols] + g[:, 0:1] * y[:tt] + g[:, 1:2] * y[tt:]


def _combine(x, gates, pos, y, *, tt):
    n, d = x.shape
    return pl.pallas_call(
        functools.partial(_combine_kernel, tt=tt),
        grid_spec=pltpu.PrefetchScalarGridSpec(
            num_scalar_prefetch=1,
            grid=(n // tt,),
            in_specs=[pl.BlockSpec((tt, d), lambda i, *_: (i, 0)),
                      pl.BlockSpec((tt, LANES), lambda i, *_: (i, 0)),
                      pl.BlockSpec(memory_space=pl.ANY)],
            out_specs=pl.BlockSpec((tt, d), lambda i, *_: (i, 0)),
            scratch_shapes=[pltpu.VMEM((2, TOP_K * tt, d // 2), jnp.uint32), pltpu.SemaphoreType.DMA((2,))],
        ),
        out_shape=jax.ShapeDtypeStruct((n, d), F32),
        compiler_params=_params("arbitrary"),
        name="moe_combine",
    )(pos, x, gates, y)


def _moe_plan(ids, cnt, *, rows):
    n = ids.shape[0]
    expert, rank = ids[:, :TOP_K], ids[:, TOP_K:2 * TOP_K]
    counts = cnt[0, N_GROUPS:N_GROUPS + N_EXPERTS]
    padded = (counts + rows - 1) // rows * rows
    padded_end = jnp.cumsum(padded)
    seg_start = padded_end - padded
    onehot = expert[..., None] == jnp.arange(N_EXPERTS, dtype=I32)
    dest = (jnp.sum(jnp.where(onehot, seg_start, 0), axis=-1) + rank).astype(I32).reshape(n * TOP_K)
    n_blocks = -(-(n * TOP_K) // rows) + N_EXPERTS
    block_start = jnp.arange(n_blocks, dtype=I32) * rows
    block_expert = jnp.minimum(jnp.sum(padded_end[None, :] <= block_start[:, None], axis=1), N_EXPERTS - 1).astype(I32)
    n_used = (padded_end[-1] // rows).astype(I32)
    block_onehot = block_expert[:, None] == jnp.arange(N_EXPERTS, dtype=I32)
    pick = lambda per_expert: jnp.sum(jnp.where(block_onehot, per_expert, 0), axis=-1)
    next_block = pick(padded_end // rows)
    next_onehot = jnp.minimum(next_block, n_blocks - 1)[:, None] == jnp.arange(n_blocks, dtype=I32)
    next_expert = jnp.where(next_block < n_used, jnp.sum(jnp.where(next_onehot, block_expert, 0), axis=-1), -1)
    nonempty = (counts > 0).astype(I32)
    run_parity = pick(jnp.cumsum(nonempty) - nonempty) % 2
    run_next = jnp.stack([next_expert, run_parity], axis=-1).astype(I32).reshape(2 * n_blocks)
    return dest, block_expert, run_next, n_used.reshape(1)


def _band_relative_bias(table):
    def bucket(nd):
        max_exact = REL_BUCKETS // 2
        nf = jnp.maximum(nd, 1).astype(F32)
        large = max_exact + (jnp.log(nf / max_exact) / math.log(REL_MAX_DIST / max_exact)
                             * (REL_BUCKETS - max_exact)).astype(I32)
        return jnp.where(nd < max_exact, nd, jnp.minimum(large, REL_BUCKETS - 1))

    qi = jnp.arange(WINDOW)[:, None]
    kj = jnp.arange(2 * WINDOW)[None, :]
    dist = jnp.maximum(qi + WINDOW - kj, 0)
    onehot = (bucket(dist)[..., None] == jnp.arange(REL_BUCKETS)).astype(F32)
    return jnp.einsum('qkb,bh->hqk', onehot, table.astype(F32), precision=lax.Precision.HIGHEST)


def _hgrn_params(lb, g_out):
    lb = lb.reshape(HG_HEADS, 1, HG_DK)
    gain = jnp.broadcast_to(g_out.reshape(1, 1, HG_DV), (HG_HEADS, 1, HG_DV))
    rows = [jnp.log(lb), jnp.log1p(-lb), 1.0 - lb, gain, jnp.zeros((HG_HEADS, 4, HG_DK), F32)]
    return jnp.concatenate(rows, axis=1).astype(F32)


def kernel(x, mem, rel_bias_table, hg_lb_logits, g_mix, w_in, swa_gq, swa_gk, swa_sinks, hg_g_out, mla_g_cq, mla_g_ckv, mla_w_uq, mla_w_ukv, mla_gq, mla_gk, w_out, g_mem_q, g_mem_kv, w_mq, w_mkv, mem_gq, mem_gk, w_mo, g_ffn, w_group_router, b_group_router, w_expert_router, b_expert_router, w_gate, w_up, w_down):
    b, s, d = x.shape
    n = b * s
    m = mem.shape[1]
    rel_bias = _band_relative_bias(rel_bias_table)
    lb_all = jnp.cumsum(jax.nn.softmax(hg_lb_logits.astype(F32), axis=0), axis=0)
    lb_all = lb_all - lb_all[:1]
    offs = [0]
    for width in IN_SIZES:
        offs.append(offs[-1] + width)
    o_hg, o_cq, o_kr = offs[3], offs[7], offs[9]

    xf = x.reshape(n, d)
    mem2 = mem.reshape(b * m, d)
    for l in range(DEPTH):
        w = w_in[l].astype(BF16)
        w_swa = w[:, :o_hg]
        w_hg = w[:, o_hg:o_cq]
        w_mla = jnp.concatenate([w[:, o_cq:], _swap_halves(w[:, o_kr:])], axis=1)
        u_swa = _rms_matmul(xf, g_mix[l], w_swa, tm=1024, tn=w_swa.shape[1], out_dtype=BF16).reshape(b, s, -1)
        u_hg = _rms_matmul(xf, g_mix[l], w_hg, tm=1024, tn=2048, out_dtype=BF16).reshape(b, s, -1)
        u_mla = _rms_matmul(xf, g_mix[l], w_mla, tm=1024, tn=w_mla.shape[1], out_dtype=BF16).reshape(b, s, -1)

        out_a = _swa(u_swa, rel_bias, swa_gq[l], swa_gk[l], swa_sinks[l])
        out_b = _hgrn(u_hg, _hgrn_params(lb_all[l], hg_g_out[l]), chunk=HG_CHUNK, n_sub=HG_SUB)
        qm, km, vm = _mla_prep(u_mla, mla_w_uq[l], mla_w_ukv[l], mla_g_cq[l], mla_g_ckv[l],
                               mla_gq[l], mla_gk[l], tm=512)
        out_c = _mla_attention(qm, km, vm, tq=MLA_TQ, tk=MLA_TK)

        kv = _rms_matmul(mem2, g_mem_kv[l], w_mkv[l].astype(BF16), tm=b * m, tn=2 * MEM_WIDTH)
        x3, ids, gates, cnt = _post_mixer(
            xf.reshape(b, s, d), out_a, out_b, out_c, w_out[l], kv.reshape(b, m, -1),
            g_mem_q[l], w_mq[l], w_mo[l], mem_gq[l], mem_gk[l],
            g_ffn[l], w_group_router[l], b_group_router[l], w_expert_router[l], b_expert_router[l], tm=512)
        xf = x3.reshape(n, d)
        dest, block_expert, run_next, n_used = _moe_plan(ids, cnt, rows=MOE_ROWS)
        xs = _dispatch(xf, g_ffn[l], dest, jnp.zeros((block_expert.shape[0] * MOE_ROWS, d // 2), jnp.uint32),
                       ts=MOE_BURST_TOKENS)
        y = _experts(xs, block_expert, run_next, n_used, w_gate, w_up, w_down, l, rows=MOE_ROWS)
        xf = _combine(xf, gates, dest, y, tt=MOE_BURST_TOKENS)
    return xf.reshape(b, s, d)
```

```python
import functools
import math

import jax
import jax.numpy as jnp
from jax import lax
from jax.experimental import pallas as pl
from jax.experimental.pallas import tpu as pltpu

F32 = jnp.float32
BF16 = jnp.bfloat16
I32 = jnp.int32

D_MODEL = 2048
DEPTH = 2
SWA_HEADS = 8
SWA_KV_HEADS = 2
SWA_HEAD_DIM = 64
WINDOW = 128
HG_HEADS = 8
HG_DK = 128
HG_DV = 128
MLA_HEADS = 4
MLA_Q_RANK = 512
MLA_KV_RANK = 256
MLA_NOPE = 128
MLA_ROPE = 64
MLA_QK = MLA_NOPE + MLA_ROPE
MLA_V = 128
ROPE_THETA = 10000.0
REL_BUCKETS = 32
REL_MAX_DIST = 128
MEM_HEADS = 4
MEM_HEAD_DIM = 128
MEM_WIDTH = MEM_HEADS * MEM_HEAD_DIM
N_GROUPS = 8
EXPERTS_PER_GROUP = 8
N_EXPERTS = N_GROUPS * EXPERTS_PER_GROUP
TOP_K = 2
D_EXPERT = 512
MOE_BURST_TOKENS = 256
MOE_ROWS = 256
EPS = 1e-6
NEG_INF = -1e30
LOG2_E = math.log2(math.e)

SWA_WIDTH = SWA_HEADS * SWA_HEAD_DIM
SWA_KV_WIDTH = SWA_KV_HEADS * SWA_HEAD_DIM
HG_WIDTH = HG_HEADS * HG_DV
MLA_WIDTH = MLA_HEADS * MLA_V
IN_SIZES = (SWA_WIDTH, SWA_KV_WIDTH, SWA_KV_WIDTH,
            HG_HEADS * HG_DK, HG_HEADS * HG_DK, HG_WIDTH, HG_WIDTH,
            MLA_Q_RANK, MLA_KV_RANK, MLA_ROPE)

LANES = 128
SUBLANES = 8
DMA_PRIORITIES = 2
MLA_QK_PAD = 2 * LANES
VMEM_LIMIT_BYTES = 56 * 1024 * 1024

IN_PROJ_ROWS = 1024
IN_PROJ_HG_COLS = 2048
SWA_BLOCKS_PER_STEP = 8
HG_CHUNK = 128
HG_SUB = 16
MLA_PREP_ROWS = 512
MLA_TQ = 2048
MLA_TK = 2048
MLA_KEY_GROUP = 1024
MLA_ROW_GROUP = 256
POST_MIXER_ROWS = 512
NT_DIMS = (((1,), (1,)), ((), ()))
TN_DIMS = (((0,), (0,)), ((), ()))


def _params(*semantics):
    return pltpu.CompilerParams(dimension_semantics=semantics, vmem_limit_bytes=VMEM_LIMIT_BYTES)


def _rms(x, gain=None):
    y = x * lax.rsqrt(jnp.mean(x * x, axis=-1, keepdims=True) + EPS)
    return y if gain is None else y * gain


def _rms_matmul_kernel(x_ref, g_ref, w_ref, o_ref, hn_ref):
    @pl.when(pl.program_id(1) == 0)
    def _():
        hn_ref[...] = _rms(x_ref[...], g_ref[...]).astype(BF16)

    o_ref[...] = jnp.dot(hn_ref[...], w_ref[...], preferred_element_type=F32).astype(o_ref.dtype)


def _rms_matmul(x, gain, w, *, tm, tn, out_dtype=F32):
    n, d = x.shape
    nout = w.shape[1]
    return pl.pallas_call(
        _rms_matmul_kernel,
        grid=(n // tm, nout // tn),
        in_specs=[pl.BlockSpec((tm, d), lambda i, j: (i, 0)),
                  pl.BlockSpec((1, d), lambda i, j: (0, 0)),
                  pl.BlockSpec((d, tn), lambda i, j: (0, j))],
        out_specs=pl.BlockSpec((tm, tn), lambda i, j: (i, j)),
        out_shape=jax.ShapeDtypeStruct((n, nout), out_dtype),
        scratch_shapes=[pltpu.VMEM((tm, d), BF16)],
        compiler_params=_params("parallel", "arbitrary"),
        name="rms_matmul",
    )(x, gain.reshape(1, d), w)


def _head_rms(x, seg_ref, gain_ref):
    sq = x * x
    hi, lo = _split_bf16(sq)
    ss = (jnp.dot(hi, seg_ref[...], preferred_element_type=F32)
          + jnp.dot(lo, seg_ref[...], preferred_element_type=F32))
    return x * lax.rsqrt(ss * (1.0 / SWA_HEAD_DIM) + EPS) * gain_ref[...]


def _swa_kernel(sink_ref, q_ref, kp_ref, kc_ref, vp_ref, vc_ref, bias_ref, gq_ref, gk_ref, segq_ref, segk_ref, o_ref):
    blk = pl.program_id(1)
    grp = SWA_HEADS // SWA_KV_HEADS
    q_all = _head_rms(q_ref[0].astype(F32), segq_ref, gq_ref)
    k_all = _head_rms(jnp.concatenate([kp_ref[0], kc_ref[0]], axis=0).astype(F32), segk_ref, gk_ref)
    v_all = jnp.concatenate([vp_ref[0], vc_ref[0]], axis=0)
    qi = lax.broadcasted_iota(I32, (WINDOW, 2 * WINDOW), 0)
    kj = lax.broadcasted_iota(I32, (WINDOW, 2 * WINDOW), 1)
    dist = qi + WINDOW - kj
    in_window = (dist >= 0) & (dist < WINDOW)
    first_valid = in_window & (kj >= jnp.where(blk > 0, 0, WINDOW))
    for j in range(q_all.shape[0] // WINDOW):
        q = q_all[j * WINDOW:(j + 1) * WINDOW]
        k = k_all[j * WINDOW:(j + 2) * WINDOW]
        v = v_all[j * WINDOW:(j + 2) * WINDOW]
        valid = first_valid if j == 0 else in_window
        outs = []
        for g in range(SWA_KV_HEADS):
            lo = g * SWA_HEAD_DIM
            kg = k[:, lo:lo + SWA_HEAD_DIM].astype(BF16)
            vg = v[:, lo:lo + SWA_HEAD_DIM].astype(BF16)
            for h in range(g * grp, (g + 1) * grp):
                qh = q[:, h * SWA_HEAD_DIM:(h + 1) * SWA_HEAD_DIM].astype(BF16)
                s = lax.dot_general(qh, kg, NT_DIMS, preferred_element_type=F32) + bias_ref[h]
                s = jnp.where(valid, s, NEG_INF)
                sink = sink_ref[h]
                m = jnp.maximum(jnp.max(s, axis=-1, keepdims=True), sink)
                e = jnp.exp(s - m)
                denom = jnp.sum(e, axis=-1, keepdims=True) + jnp.exp(sink - m)
                outs.append(jnp.dot((e / denom).astype(BF16), vg, preferred_element_type=F32))
        o_ref[0, j * WINDOW:(j + 1) * WINDOW] = jnp.concatenate(outs, axis=-1).astype(o_ref.dtype)


def _swa(u_swa, rel_bias, gq, gk, sinks):
    b, s, _ = u_swa.shape
    rows = SWA_BLOCKS_PER_STEP * WINDOW
    kcol = SWA_WIDTH // LANES
    vcol = kcol + 1
    prev = lambda bi, n, *_: (bi, jnp.maximum(n * SWA_BLOCKS_PER_STEP - 1, 0))
    grp = SWA_HEADS // SWA_KV_HEADS

    def head_blocks(width):
        head = jnp.arange(width) // SWA_HEAD_DIM
        return (head[:, None] == head[None, :]).astype(BF16)

    return pl.pallas_call(
        _swa_kernel,
        grid_spec=pltpu.PrefetchScalarGridSpec(
            num_scalar_prefetch=1,
            grid=(b, s // rows),
            in_specs=[
                pl.BlockSpec((1, rows, SWA_WIDTH), lambda bi, n, *_: (bi, n, 0)),
                pl.BlockSpec((1, WINDOW, LANES), lambda bi, n, *_: prev(bi, n) + (kcol,)),
                pl.BlockSpec((1, rows, LANES), lambda bi, n, *_: (bi, n, kcol)),
                pl.BlockSpec((1, WINDOW, LANES), lambda bi, n, *_: prev(bi, n) + (vcol,)),
                pl.BlockSpec((1, rows, LANES), lambda bi, n, *_: (bi, n, vcol)),
                pl.BlockSpec((SWA_HEADS, WINDOW, 2 * WINDOW), lambda bi, n, *_: (0, 0, 0)),
                pl.BlockSpec((1, SWA_WIDTH), lambda bi, n, *_: (0, 0)),
                pl.BlockSpec((1, SWA_KV_WIDTH), lambda bi, n, *_: (0, 0)),
                pl.BlockSpec((SWA_WIDTH, SWA_WIDTH), lambda bi, n, *_: (0, 0)),
                pl.BlockSpec((SWA_KV_WIDTH, SWA_KV_WIDTH), lambda bi, n, *_: (0, 0)),
            ],
            out_specs=pl.BlockSpec((1, rows, SWA_WIDTH), lambda bi, n, *_: (bi, n, 0)),
        ),
        out_shape=jax.ShapeDtypeStruct((b, s, SWA_WIDTH), BF16),
        compiler_params=_params("parallel", "parallel"),
        name="swa_attention",
    )(sinks, u_swa, u_swa, u_swa, u_swa, u_swa, rel_bias,
      jnp.tile(gq * (SWA_HEAD_DIM ** -0.5), SWA_HEADS).reshape(1, SWA_WIDTH),
      jnp.tile(gk, SWA_KV_HEADS).reshape(1, SWA_KV_WIDTH), head_blocks(SWA_WIDTH), head_blocks(SWA_KV_WIDTH))


def _hgrn_level_matrix(c):
    t = jnp.arange(c)[:, None]
    r = jnp.arange(c)[None, :]
    mats = [(r <= t)]
    half = c // 2
    while half >= 1:
        mid = (t // (2 * half)) * (2 * half) + half
        is_q = (t & half) != 0
        if half < SUBLANES:
            mats.append(jnp.where(is_q, (r >= mid) & (r <= t), (r > t) & (r < mid)))
        half //= 2
    return jnp.concatenate(mats, axis=0).astype(BF16)


def _hgrn_pair_level(c):
    t = jnp.arange(c, dtype=I32)[:, None]
    s = jnp.arange(c, dtype=I32)[None, :]
    x = t ^ s
    lvl = jnp.zeros((c, c), I32)
    half = c // 2
    while half >= 1:
        lvl = jnp.where((x & (-half)) == half, half, lvl)
        half //= 2
    return jnp.where(t > s, lvl, 0)


def _hgrn_kernel(q_ref, f_ref, v_ref, gate_ref, par_ref, amat_ref, lvl_ref, o_ref, st_ref, *, chunk, n_sub):
    @pl.when(pl.program_id(2) == 0)
    def _():
        st_ref[...] = jnp.zeros_like(st_ref)

    c = chunk
    par = par_ref[0]
    log_lb, log1m_lb, one_m_lb, g_out = par[0:1], par[1:2], par[2:3], par[3:4]
    row = lax.broadcasted_iota(I32, (c, 1), 0)
    pair_level = lvl_ref[...]
    st = st_ref[...]
    for sub in range(n_sub):
        rows = pl.ds(sub * c, c)
        q = q_ref[0, rows].astype(F32)
        f = f_ref[0, rows].astype(F32)
        v = v_ref[0, rows].astype(F32)
        gate = gate_ref[0, rows].astype(F32)

        e = jnp.exp(-jnp.abs(f))
        log_sig = jnp.minimum(f, 0.0) - jnp.log(1.0 + e)
        bb = log1m_lb + log_sig
        log_f = jnp.maximum(log_lb, bb) + jnp.log(1.0 + jnp.exp(-jnp.abs(log_lb - bb)))
        log_f = log_f * LOG2_E
        kk = one_m_lb * jnp.where(f < 0.0, 1.0, e) / (1.0 + e)

        g_hi = log_f.astype(BF16)
        g_lo = (log_f - g_hi.astype(F32)).astype(BF16)
        e2 = jnp.dot(amat_ref[...], jnp.concatenate([g_hi, g_lo], axis=1), preferred_element_type=F32)
        expo = e2[:, :HG_DK] + e2[:, HG_DK:]
        bcum = expo[0:c]
        btot = bcum[c - 1:c]

        attn = jnp.zeros((c, c), F32)
        half = c // 2
        lvl = 1
        while half >= 1:
            is_q = (row & half) != 0
            seg = 2 * half
            if half >= SUBLANES:
                ref = jnp.concatenate([jnp.broadcast_to(bcum[a + half - 1:a + half], (seg, HG_DK))
                                       for a in range(0, c, seg)], axis=0)
                diff = bcum - ref
                w = jnp.exp2(jnp.where(is_q, diff, -diff))
            else:
                w = jnp.exp2(expo[lvl * c:(lvl + 1) * c])
                lvl += 1
            zf = jnp.where(is_q, q, kk) * w
            z = zf.astype(BF16)
            if half >= SUBLANES:
                zq = jnp.concatenate([zf[a + half:a + seg] for a in range(0, c, seg)], axis=0).astype(BF16)
                aq = lax.dot_general(zq, z, NT_DIMS, preferred_element_type=F32)
                blank = jnp.zeros((half, c), F32)
                a = jnp.concatenate([blk for j in range(c // seg)
                                     for blk in (blank, aq[j * half:(j + 1) * half])], axis=0)
            else:
                a = lax.dot_general(z, z, NT_DIMS, preferred_element_type=F32)
            attn = jnp.where(pair_level == half, a, attn)
            half //= 2

        vb = v.astype(BF16)
        diag = jnp.sum(q * kk, axis=-1, keepdims=True)
        intra = jnp.dot(attn.astype(BF16), vb, preferred_element_type=F32) + diag * v
        qe = (q * jnp.exp2(bcum)).astype(BF16)
        inter = lax.dot_general(qe, st.astype(BF16), NT_DIMS, preferred_element_type=F32)
        kd = (kk * jnp.exp2(btot - bcum)).astype(BF16)
        st = st * jnp.exp2(btot) +lax.dot_general(vb, kd, TN_DIMS, preferred_element_type=F32)

        o = _rms(inter + intra, g_out)
        o_ref[0, rows] = (o * gate / (1.0 + jnp.exp(-gate))).astype(o_ref.dtype)
    st_ref[...] = st


def _hgrn(u_hg, par, *, chunk, n_sub):
    b, s, _ = u_hg.shape
    amat = _hgrn_level_matrix(chunk)
    pair_level = _hgrn_pair_level(chunk)
    step = chunk * n_sub
    col = lambda off: (lambda bi, h, c: (bi, c, off + h))
    return pl.pallas_call(
        functools.partial(_hgrn_kernel, chunk=chunk, n_sub=n_sub),
        grid=(b, HG_HEADS, s // step),
        in_specs=[pl.BlockSpec((1, step, HG_DK), col(0)),
                  pl.BlockSpec((1, step, HG_DK), col(HG_HEADS)),
                  pl.BlockSpec((1, step, HG_DV), col(2 * HG_HEADS)),
                  pl.BlockSpec((1, step, HG_DV), col(3 * HG_HEADS)),
                  pl.BlockSpec((1, 8, HG_DK), lambda bi, h, c: (h, 0, 0)),
                  pl.BlockSpec(amat.shape, lambda bi, h, c: (0, 0)),
                  pl.BlockSpec(pair_level.shape, lambda bi, h, c: (0, 0))],
        out_specs=pl.BlockSpec((1, step, HG_DV), lambda bi, h, c: (bi, c, h)),
        out_shape=jax.ShapeDtypeStruct((b, s, HG_WIDTH), BF16),
        scratch_shapes=[pltpu.VMEM((HG_DV, HG_DK), F32)],
        compiler_params=_params("parallel", "parallel", "arbitrary"),
        name="hgrn2",
    )(u_hg, u_hg, u_hg, u_hg, par, amat, pair_level)


def _mla_prep_kernel(u_ref, tab_ref, wq_ref, wkv_ref, gcq_ref, gckv_ref, gqn_ref, gqr_ref, gkn_ref, gkr_ref,
                     q_ref, k_ref, v_ref):
    u = u_ref[0].astype(F32)
    cq = _rms(u[:, :MLA_Q_RANK], gcq_ref[...]).astype(BF16)
    ckv = _rms(u[:, MLA_Q_RANK:MLA_Q_RANK + MLA_KV_RANK], gckv_ref[...]).astype(BF16)
    kr = u[:, MLA_Q_RANK + MLA_KV_RANK:]
    qf = jnp.dot(cq, wq_ref[...], preferred_element_type=F32)
    kvf = jnp.dot(ckv, wkv_ref[...], preferred_element_type=F32)
    tab = tab_ref[...]
    low = lax.broadcasted_iota(I32, kr.shape, 1) < MLA_ROPE
    kr_sq = jnp.sum(jnp.where(low, kr * kr, 0.0), axis=-1, keepdims=True)
    scale = MLA_QK ** -0.5 * LOG2_E
    ones = jnp.ones((u.shape[0], MLA_V), F32)
    for h in range(MLA_HEADS):
        qn = qf[:, h * MLA_QK_PAD:h * MLA_QK_PAD + MLA_NOPE]
        qr = qf[:, h * MLA_QK_PAD + MLA_NOPE:(h + 1) * MLA_QK_PAD]
        ss = jnp.sum(qn * qn, axis=-1, keepdims=True) + jnp.sum(jnp.where(low, qr * qr, 0.0), axis=-1, keepdims=True)
        rstd = lax.rsqrt(ss / MLA_QK + EPS) * scale
        t = qr * rstd * tab * gqr_ref[...]
        rope = jnp.where(low, t + pltpu.roll(t, MLA_ROPE, 1), 0.0)
        q_ref[0, h] = jnp.concatenate([qn * rstd * gqn_ref[...], rope], axis=-1).astype(q_ref.dtype)

        kn = kvf[:, h * MLA_NOPE:(h + 1) * MLA_NOPE]
        ss = jnp.sum(kn * kn, axis=-1, keepdims=True) + kr_sq
        rstd = lax.rsqrt(ss / MLA_QK + EPS)
        t = kr * rstd * tab * gkr_ref[...]
        rope = t + pltpu.roll(t, MLA_ROPE, 1)
        k_ref[0, h] = jnp.concatenate([kn * rstd * gkn_ref[...], rope], axis=-1).astype(k_ref.dtype)
        vh = kvf[:, MLA_HEADS * MLA_NOPE + h * MLA_V:MLA_HEADS * MLA_NOPE + (h + 1) * MLA_V]
        v_ref[0, h] = jnp.concatenate([vh, ones], axis=-1).astype(v_ref.dtype)


def _swap_halves(a):
    half = a.shape[-1] // 2
    return jnp.concatenate([a[..., half:], a[..., :half]], axis=-1)


def _mla_prep(u_mla, w_uq, w_ukv, g_cq, g_ckv, gq, gk, *, tm):
    b, s, wu = u_mla.shape
    wq = w_uq.reshape(MLA_Q_RANK, MLA_HEADS, MLA_QK)
    wq = jnp.concatenate([wq, _swap_halves(wq[..., MLA_NOPE:])], axis=-1).reshape(MLA_Q_RANK, MLA_HEADS * MLA_QK_PAD)
    wkv = w_ukv.reshape(MLA_KV_RANK, MLA_HEADS, MLA_NOPE + MLA_V)
    wkv = jnp.concatenate([wkv[..., :MLA_NOPE].reshape(MLA_KV_RANK, -1), wkv[..., MLA_NOPE:].reshape(MLA_KV_RANK, -1)], axis=1)
    half = MLA_ROPE // 2
    inv_freq = ROPE_THETA ** (-jnp.arange(half, dtype=F32) / half)
    ang = jnp.arange(s, dtype=F32)[:, None] * inv_freq[None, :]
    cos, sin = jnp.cos(ang), jnp.sin(ang)
    tab = jnp.concatenate([cos, cos, -sin, sin], axis=-1)
    rope_gain = lambda g: jnp.concatenate([g[MLA_NOPE:], _swap_halves(g[MLA_NOPE:])]).reshape(1, 2 * MLA_ROPE)
    vec = lambda i, j: (0, 0)
    head_out = lambda width: pl.BlockSpec((1, MLA_HEADS, tm, width), lambda i, j: (i, 0, j, 0))
    return pl.pallas_call(
        _mla_prep_kernel,
        grid=(b, s // tm),
        in_specs=[pl.BlockSpec((1, tm, wu), lambda i, j: (i, j, 0)),
                  pl.BlockSpec((tm, 2 * MLA_ROPE), lambda i, j: (j, 0)),
                  pl.BlockSpec(wq.shape, vec),
                  pl.BlockSpec(wkv.shape, vec),
                  pl.BlockSpec((1, MLA_Q_RANK), vec),
                  pl.BlockSpec((1, MLA_KV_RANK), vec),
                  pl.BlockSpec((1, MLA_NOPE), vec),
                  pl.BlockSpec((1, 2 * MLA_ROPE), vec),
                  pl.BlockSpec((1, MLA_NOPE), vec),
                  pl.BlockSpec((1, 2 * MLA_ROPE), vec)],
        out_specs=[head_out(MLA_QK_PAD), head_out(MLA_QK_PAD), head_out(2 * MLA_V)],
        out_shape=[jax.ShapeDtypeStruct((b, MLA_HEADS, s, MLA_QK_PAD), BF16),
                   jax.ShapeDtypeStruct((b, MLA_HEADS, s, MLA_QK_PAD), BF16),
                   jax.ShapeDtypeStruct((b, MLA_HEADS, s, 2 * MLA_V), BF16)],
        compiler_params=_params("parallel", "parallel"),
        name="mla_prep",
    )(u_mla, tab, wq.astype(BF16), wkv.astype(BF16), g_cq.reshape(1, -1), g_ckv.reshape(1, -1),
      gq[:MLA_NOPE].reshape(1, -1), rope_gain(gq), gk[:MLA_NOPE].reshape(1, -1), rope_gain(gk))


def _mla_attn_kernel(qi_ref, ki_ref, q_ref, k_ref, v_ref, o_ref, m_ref, acc_ref, *, tq, tk):
    step = pl.program_id(2)
    qi = qi_ref[step]
    ki = ki_ref[step]

    @pl.when(ki == 0)
    def _():
        m_ref[...] = jnp.full_like(m_ref, NEG_INF)
        acc_ref[...] = jnp.zeros_like(acc_ref)

    def update(rel):
        for r0, k0 in [(r, k) for r in range(0, tq, MLA_ROW_GROUP) for k in range(0, tk, MLA_KEY_GROUP)]:
            first_key = None if rel is None else rel + k0
            if first_key is not None and first_key > r0 + MLA_ROW_GROUP - 1:
                continue
            rows = pl.ds(r0, MLA_ROW_GROUP)
            keys = pl.ds(k0, MLA_KEY_GROUP)
            s = lax.dot_general(q_ref[0, 0, rows], k_ref[0, 0, keys], NT_DIMS, preferred_element_type=F32)
            if first_key is not None and first_key + MLA_KEY_GROUP - 1 > r0:
                row = r0 + lax.broadcasted_iota(I32, s.shape, 0)
                col = first_key + lax.broadcasted_iota(I32, s.shape, 1)
                s = jnp.where(col <= row, s, NEG_INF)
            m_prev = m_ref[rows]
            m_next = jnp.maximum(m_prev, jnp.max(s, axis=-1, keepdims=True))
            alpha = jnp.exp2(m_prev - m_next)
            p = jnp.exp2(s - m_next[:, :1]).astype(BF16)
            pv = jnp.dot(p, v_ref[0, 0, keys], preferred_element_type=F32)
            acc_ref[rows, :MLA_V] = alpha * acc_ref[rows, :MLA_V] + pv[:, :MLA_V]
            acc_ref[rows, MLA_V:] = alpha * acc_ref[rows, MLA_V:] + pv[:, MLA_V:]
            m_ref[rows] = m_next

    key_offset = ki * tk - qi * tq
    pl.when(key_offset < 0)(lambda: update(None))
    for rel in range(0, tq, tk):
        pl.when(key_offset == rel)(functools.partial(update, rel))

    @pl.when((ki + 1) * tk == (qi + 1) * tq)
    def _():
        o_ref[0] = (acc_ref[:, :MLA_V] / acc_ref[:, MLA_V:]).astype(o_ref.dtype)


def _mla_attention(q, k, v, *, tq, tk):
    b, h, s, _ = q.shape
    pairs = [(qi, ki) for qi in range(s // tq) for ki in range((qi + 1) * tq // tk)]
    qi_of = jnp.array([p[0] for p in pairs], I32)
    ki_of = jnp.array([p[1] for p in pairs], I32)
    kv_idx = lambda bi, hi, t, qi_ref, ki_ref: (bi, hi, ki_ref[t], 0)
    return pl.pallas_call(
        functools.partial(_mla_attn_kernel, tq=tq, tk=tk),
        grid_spec=pltpu.PrefetchScalarGridSpec(
            num_scalar_prefetch=2,
            grid=(b, h, len(pairs)),
            in_specs=[pl.BlockSpec((1, 1, tq, MLA_QK_PAD), lambda bi, hi, t, qi_ref, ki_ref: (bi, hi, qi_ref[t], 0)),
                      pl.BlockSpec((1, 1, tk, MLA_QK_PAD), kv_idx),
                      pl.BlockSpec((1, 1, tk, 2 * MLA_V), kv_idx)],
            out_specs=pl.BlockSpec((1, tq, MLA_V), lambda bi, hi, t, qi_ref, ki_ref: (bi, qi_ref[t], hi)),
            scratch_shapes=[pltpu.VMEM((tq, MLA_V), F32), pltpu.VMEM((tq, 2 * MLA_V), F32)],
        ),
        out_shape=jax.ShapeDtypeStruct((b, s, h * MLA_V), BF16),
        compiler_params=_params("parallel", "parallel", "arbitrary"),
        name="mla_attention",
    )(qi_of, ki_of, q, k, v)


def _mem_attn_body(x, kv, g_ref, wq_ref, wo_ref, gq_ref, gk_ref):
    hn = _rms(x, g_ref[...]).astype(BF16)
    qf = jnp.dot(hn, wq_ref[...], preferred_element_type=F32)
    outs = []
    for h in range(MEM_HEADS):
        lo = h * MEM_HEAD_DIM
        qh = (_rms(qf[:, lo:lo + MEM_HEAD_DIM], gq_ref[...]) * (MEM_HEAD_DIM ** -0.5)).astype(BF16)
        kh = _rms(kv[:, lo:lo + MEM_HEAD_DIM], gk_ref[...]).astype(BF16)
        vh = kv[:, MEM_WIDTH + lo:MEM_WIDTH + lo + MEM_HEAD_DIM].astype(BF16)
        s = lax.dot_general(qh, kh, NT_DIMS, preferred_element_type=F32)
        e = jnp.exp(s - jnp.max(s, axis=-1, keepdims=True))
        p = (e / jnp.sum(e, axis=-1, keepdims=True)).astype(BF16)
        outs.append(jnp.dot(p, vh, preferred_element_type=F32))
    o = jnp.concatenate(outs, axis=-1).astype(BF16)
    return x + jnp.dot(o, wo_ref[...], preferred_element_type=F32)


def _split_bf16(a):
    hi = a.astype(BF16)
    return hi, (a - hi.astype(F32)).astype(BF16)


def _router_body(x, g_ref, whi_ref, wlo_ref, b_ref, ids_ref, gates_ref, cnt_ref, carry_ref):
    hn = _rms(x, g_ref[...])
    hi, lo = _split_bf16(hn)
    logits = (jnp.dot(hi, whi_ref[...], preferred_element_type=F32)
              + jnp.dot(hi, wlo_ref[...], preferred_element_type=F32)
              + jnp.dot(lo, whi_ref[...], preferred_element_type=F32)) + b_ref[...]
    lane = lax.broadcasted_iota(I32, logits.shape, 1)
    gl = jnp.where(lane < N_GROUPS, logits, NEG_INF)
    gmax = jnp.max(gl, axis=-1, keepdims=True)
    p_grp = 1.0 / jnp.sum(jnp.exp(gl - gmax), axis=-1, keepdims=True)
    grp = jnp.min(jnp.where(gl == gmax, lane, LANES), axis=-1, keepdims=True)
    in_grp = (lane >= N_GROUPS) & (lane < N_GROUPS + N_EXPERTS) & (((lane - N_GROUPS) // EXPERTS_PER_GROUP) == grp)
    el = jnp.where(in_grp, logits, NEG_INF)
    m1 = jnp.max(el, axis=-1, keepdims=True)
    i1 = jnp.min(jnp.where(el == m1, lane, LANES), axis=-1, keepdims=True)
    el2 = jnp.where(lane == i1, NEG_INF, el)
    m2 = jnp.max(el2, axis=-1, keepdims=True)
    i2 = jnp.min(jnp.where(el2 == m2, lane, LANES), axis=-1, keepdims=True)
    r = jnp.exp(m2 - m1)
    g1 = p_grp / (1.0 + r)
    gates_ref[...] = jnp.where(lane == 0, g1, jnp.where(lane == 1, g1 * r, 0.0))

    tm = logits.shape[0]
    used = jnp.where((lane == i1) | (lane == i2), 1.0, 0.0)
    earlier = lax.broadcasted_iota(I32, (tm, tm), 1) < lax.broadcasted_iota(I32, (tm, tm), 0)
    before = carry_ref[...] + jnp.dot(jnp.where(earlier, 1.0, 0.0).astype(BF16), used.astype(BF16),
                                      preferred_element_type=F32)
    r1 = jnp.sum(jnp.where(lane == i1, before, 0.0), axis=-1, keepdims=True).astype(I32)
    r2 = jnp.sum(jnp.where(lane == i2, before, 0.0), axis=-1, keepdims=True).astype(I32)
    carry_ref[...] += jnp.sum(used, axis=0, keepdims=True)
    cnt_ref[...] = carry_ref[...].astype(I32)
    ids_ref[...] = jnp.where(lane == 0, i1 - N_GROUPS, jnp.where(lane == 1, i2 - N_GROUPS,
                             jnp.where(lane == 2, r1, jnp.where(lane == 3, r2, 0))))


def _post_mixer_kernel(x_ref, a_ref, b_ref, c_ref, wa_ref, wb_ref, wc_ref,
                       kv_ref, gmq_ref, wmq_ref, wmo_ref, mgq_ref, mgk_ref,
                       gffn_ref, whi_ref, wlo_ref, bias_ref,
                       o_ref, ids_ref, gates_ref, cnt_ref, carry_ref):
    @pl.when((pl.program_id(0) == 0) & (pl.program_id(1) == 0))
    def _():
        carry_ref[...] = jnp.zeros_like(carry_ref)

    mix = jnp.dot(a_ref[0], wa_ref[...], preferred_element_type=F32)
    mix += jnp.dot(b_ref[0], wb_ref[...], preferred_element_type=F32)
    mix += jnp.dot(c_ref[0], wc_ref[...], preferred_element_type=F32)
    x = x_ref[0] + mix
    x = _mem_attn_body(x, kv_ref[0], gmq_ref, wmq_ref, wmo_ref, mgq_ref, mgk_ref)
    o_ref[0] = x
    _router_body(x, gffn_ref, whi_ref, wlo_ref, bias_ref, ids_ref, gates_ref, cnt_ref, carry_ref)


def _post_mixer(x, a, bmix, c, w_out, kv, g_mem_q, w_mq, w_mo, mem_gq, mem_gk,
                g_ffn, w_gr, b_gr, w_er, b_er, *, tm):
    b, s, d = x.shape
    m = kv.shape[1]
    nst = s // tm
    wa = w_out[:SWA_WIDTH].astype(BF16)
    wb = w_out[SWA_WIDTH:SWA_WIDTH + HG_WIDTH].astype(BF16)
    wc = w_out[SWA_WIDTH + HG_WIDTH:].astype(BF16)
    pad = LANES - N_GROUPS - N_EXPERTS
    wr = jnp.concatenate([w_gr, w_er, jnp.zeros((d, pad), F32)], axis=1)
    bias = jnp.concatenate([b_gr, b_er, jnp.zeros((pad,), F32)]).reshape(1, LANES)
    whi, wlo = _split_bf16(wr)
    tile = lambda width: pl.BlockSpec((1, tm, width), lambda i, j: (i, j, 0))
    const = lambda shape: pl.BlockSpec(shape, lambda i, j: (0,) * len(shape), pipeline_mode=pl.Buffered(1))
    flat = lambda width: pl.BlockSpec((tm, width), lambda i, j: (i * nst + j, 0))
    return pl.pallas_call(
        _post_mixer_kernel,
        grid=(b, nst),
        in_specs=[tile(d), tile(SWA_WIDTH), tile(HG_WIDTH), tile(MLA_WIDTH),
                  const(wa.shape), const(wb.shape), const(wc.shape),
                  pl.BlockSpec((1, m, 2 * MEM_WIDTH), lambda i, j: (i, 0, 0)),
                  const((1, d)), const((d, MEM_WIDTH)), const((MEM_WIDTH, d)),
                  const((1, MEM_HEAD_DIM)), const((1, MEM_HEAD_DIM)),
                  const((1, d)), const((d, LANES)), const((d, LANES)), const((1, LANES))],
        out_specs=[tile(d), flat(LANES), flat(LANES), pl.BlockSpec((1, LANES), lambda i, j: (0, 0))],
        out_shape=[jax.ShapeDtypeStruct((b, s, d), F32), jax.ShapeDtypeStruct((b * s, LANES), I32),
                   jax.ShapeDtypeStruct((b * s, LANES), F32), jax.ShapeDtypeStruct((1, LANES), I32)],
        scratch_shapes=[pltpu.VMEM((1, LANES), F32)],
        compiler_params=_params("arbitrary", "arbitrary"),
        name="post_mixer",
    )(x, a, bmix, c, wa, wb, wc, kv, g_mem_q.reshape(1, d), w_mq.astype(BF16), w_mo.astype(BF16),
      mem_gq.reshape(1, -1), mem_gk.reshape(1, -1), g_ffn.reshape(1, d), whi, wlo, bias)


HI_HALF_MASK = 0xFFFF0000


def _pack_bf16_pairs(a):
    half = a.shape[1] // 2
    lo = pltpu.bitcast(a[:, :half].astype(BF16).astype(F32), jnp.uint32) >> 16
    hi = pltpu.bitcast(a[:, half:].astype(BF16).astype(F32), jnp.uint32) & jnp.uint32(HI_HALF_MASK)
    return lo | hi


def _unpack_bf16_pairs(w):
    return pltpu.bitcast(w << 16, F32), pltpu.bitcast(w & jnp.uint32(HI_HALF_MASK), F32)


def _dispatch_kernel(dest_ref, x_ref, g_ref, xs_in_hbm, xs_hbm, buf, sem, *, ts):
    del xs_in_hbm
    i = pl.program_id(0)
    last = pl.num_programs(0) - 1
    slot = i % 2

    def row_copy(slot_, r, dst_row):
        return pltpu.make_async_copy(buf.at[slot_, pl.ds(r, 1)], xs_hbm.at[pl.ds(dst_row, 1)], sem.at[slot_])

    def wait_step(slot_):
        for _ in range(ts * TOP_K):
            row_copy(slot_, 0, 0).wait()

    @pl.when(i >= 2)
    def _():
        wait_step(slot)

    buf[slot] = _pack_bf16_pairs(_rms(x_ref[...], g_ref[...]))
    for r in range(ts):
        for kk in range(TOP_K):
            row_copy(slot, r, dest_ref[(i * ts + r) * TOP_K + kk]).start(priority=kk % DMA_PRIORITIES)

    @pl.when(i == last)
    def _():
        wait_step(slot)

    @pl.when((i == last) & (i >= 1))
    def _():
        wait_step(1 - slot)


def _dispatch(x, g_ffn, dest, xs_init, *, ts):
    n, d = x.shape
    n_rows = xs_init.shape[0]
    return pl.pallas_call(
        functools.partial(_dispatch_kernel, ts=ts),
        grid_spec=pltpu.PrefetchScalarGridSpec(
            num_scalar_prefetch=1,
            grid=(n // ts,),
            in_specs=[pl.BlockSpec((ts, d), lambda i, *_: (i, 0)),
                      pl.BlockSpec((1, d), lambda i, *_: (0, 0)),
                      pl.BlockSpec(memory_space=pl.ANY)],
            out_specs=pl.BlockSpec(memory_space=pl.ANY),
            scratch_shapes=[pltpu.VMEM((2, ts, d // 2), jnp.uint32), pltpu.SemaphoreType.DMA((2,))],
        ),
        out_shape=jax.ShapeDtypeStruct((n_rows, d // 2), jnp.uint32),
        input_output_aliases={3: 0},
        compiler_params=_params("arbitrary"),
        name="moe_dispatch",
    )(dest, x, g_ffn.reshape(1, d), xs_init)


def _expert_kernel(bexp_ref, next_ref, nblk_ref, xs_ref, wg_hbm, wu_hbm, wd_hbm, y_ref,
                   wg_f32, wu_f32, wd_f32, sem, wg_bf, wu_bf, wd_bf, *, layer):
    i = pl.program_id(0)
    n_used = nblk_ref[0]
    expert = bexp_ref[i]
    first_of_run = (i == 0) | (expert != bexp_ref[jnp.maximum(i - 1, 0)])
    slot = next_ref[2 * i + 1]

    def weight_copies(e, slot_):
        return [pltpu.make_async_copy(hbm.at[layer, e], buf.at[slot_], sem.at[slot_, j])
                for j, (hbm, buf) in enumerate(((wg_hbm, wg_f32), (wu_hbm, wu_f32), (wd_hbm, wd_f32)))]

    @pl.when(i == 0)
    def _():
        for j, cp in enumerate(weight_copies(expert, 0)):
            cp.start(priority=j % DMA_PRIORITIES)

    @pl.when(first_of_run & (i < n_used))
    def _():
        for cp in weight_copies(expert, slot):
            cp.wait()
        next_expert = next_ref[2 * i]

        @pl.when(next_expert >= 0)
        def _():
            for j, cp in enumerate(weight_copies(next_expert, 1 - slot)):
                cp.start(priority=j % DMA_PRIORITIES)

        wg_bf[...] = wg_f32[slot].astype(BF16)
        wu_bf[...] = wu_f32[slot].astype(BF16)
        wd_bf[...] = wd_f32[slot].astype(BF16)

    @pl.when(i < n_used)
    def _():
        half = xs_ref.shape[1]
        lo, hi = (part.astype(BF16) for part in _unpack_bf16_pairs(xs_ref[...]))
        gate = (jnp.dot(lo, wg_bf[:half], preferred_element_type=F32)
                + jnp.dot(hi, wg_bf[half:], preferred_element_type=F32))
        up = (jnp.dot(lo, wu_bf[:half], preferred_element_type=F32)
              + jnp.dot(hi, wu_bf[half:], preferred_element_type=F32))
        act = (gate / (1.0 + jnp.exp(-gate)) * up).astype(BF16)
        y_ref[...] = _pack_bf16_pairs(jnp.dot(act, wd_bf[...], preferred_element_type=F32))

    @pl.when(i >= nblk_ref[0])
    def _():
        y_ref[...] = jnp.zeros_like(y_ref)


def _experts(xs, block_expert, run_next, n_used, w_gate, w_up, w_down, layer, *, rows):
    d = 2 * xs.shape[1]
    n_blocks = block_expert.shape[0]
    hbm = pl.BlockSpec(memory_space=pl.ANY)
    return pl.pallas_call(
        functools.partial(_expert_kernel, layer=layer),
        grid_spec=pltpu.PrefetchScalarGridSpec(
            num_scalar_prefetch=3,
            grid=(n_blocks,),
            in_specs=[pl.BlockSpec((rows, d // 2), lambda i, bexp, nxt, nblk: (jnp.minimum(i, nblk[0] - 1), 0)),
                      hbm, hbm, hbm],
            out_specs=pl.BlockSpec((rows, d // 2), lambda i, *_: (i, 0)),
            scratch_shapes=[pltpu.VMEM((2, d, D_EXPERT), F32),
                            pltpu.VMEM((2, d, D_EXPERT), F32),
                            pltpu.VMEM((2, D_EXPERT, d), F32),
                            pltpu.SemaphoreType.DMA((2, 3)),
                            pltpu.VMEM((d, D_EXPERT), BF16),
                            pltpu.VMEM((d, D_EXPERT), BF16),
                            pltpu.VMEM((D_EXPERT, d), BF16)],
        ),
        out_shape=jax.ShapeDtypeStruct((n_blocks * rows, d // 2), jnp.uint32),
        compiler_params=_params("arbitrary"),
        name="moe_experts",
    )(block_expert, run_next, n_used, xs, w_gate, w_up, w_down)


def _combine_kernel(pos_ref, x_ref, gates_ref, y_hbm, o_ref, ybuf, sem, *, tt):
    i = pl.program_id(0)
    nsteps = pl.num_programs(0)

    def row_copy(src_row, slot, dst_row):
        return pltpu.make_async_copy(y_hbm.at[pl.ds(src_row, 1)], ybuf.at[slot, pl.ds(dst_row, 1)], sem.at[slot])

    def start_gather(step, slot):
        for r in range(tt):
            for kk in range(TOP_K):
                row_copy(pos_ref[(step * tt + r) * TOP_K + kk], slot, kk * tt + r).start(priority=kk % DMA_PRIORITIES)

    def wait_gather(slot):
        for r in range(TOP_K * tt):
            row_copy(0, slot, r).wait()

    @pl.when(i == 0)
    def _():
        start_gather(0, 0)

    @pl.when(i + 1 < nsteps)
    def _():
        start_gather(i + 1, (i + 1) % 2)

    wait_gather(i % 2)
    g = gates_ref[...]
    half = ybuf.shape[2]
    for cols, y in zip((pl.ds(0, half), pl.ds(half, half)), _unpack_bf16_pairs(ybuf[i % 2])):
        o_ref[:, cols] = x_ref[:, cols] + g[:, 0:1] * y[:tt] + g[:, 1:2] * y[tt:]


def _combine(x, gates, pos, y, *, tt):
    n, d = x.shape
    return pl.pallas_call(
        functools.partial(_combine_kernel, tt=tt),
        grid_spec=pltpu.PrefetchScalarGridSpec(
            num_scalar_prefetch=1,
            grid=(n // tt,),
            in_specs=[pl.BlockSpec((tt, d), lambda i, *_: (i, 0)),
                      pl.BlockSpec((tt, LANES), lambda i, *_: (i, 0)),
                      pl.BlockSpec(memory_space=pl.ANY)],
            out_specs=pl.BlockSpec((tt, d), lambda i, *_: (i, 0)),
            scratch_shapes=[pltpu.VMEM((2, TOP_K * tt, d // 2), jnp.uint32), pltpu.SemaphoreType.DMA((2,))],
        ),
        out_shape=jax.ShapeDtypeStruct((n, d), F32),
        compiler_params=_params("arbitrary"),
        name="moe_combine",
    )(pos, x, gates, y)


def _moe_plan(ids, cnt, *, rows):
    n = ids.shape[0]
    expert, rank = ids[:, :TOP_K], ids[:, TOP_K:2 * TOP_K]
    counts = cnt[0, N_GROUPS:N_GROUPS + N_EXPERTS]
    padded = (counts + rows - 1) // rows * rows
    padded_end = jnp.cumsum(padded)
    seg_start = padded_end - padded
    onehot = expert[..., None] == jnp.arange(N_EXPERTS, dtype=I32)
    dest = (jnp.sum(jnp.where(onehot, seg_start, 0), axis=-1) + rank).astype(I32).reshape(n * TOP_K)
    n_blocks = -(-(n * TOP_K) // rows) + N_EXPERTS
    block_start = jnp.arange(n_blocks, dtype=I32) * rows
    block_expert = jnp.minimum(jnp.sum(padded_end[None, :] <= block_start[:, None], axis=1), N_EXPERTS - 1).astype(I32)
    n_used = (padded_end[-1] // rows).astype(I32)
    block_onehot = block_expert[:, None] == jnp.arange(N_EXPERTS, dtype=I32)
    pick = lambda per_expert: jnp.sum(jnp.where(block_onehot, per_expert, 0), axis=-1)
    next_block = pick(padded_end // rows)
    next_onehot = jnp.minimum(next_block, n_blocks - 1)[:, None] == jnp.arange(n_blocks, dtype=I32)
    next_expert = jnp.where(next_block < n_used, jnp.sum(jnp.where(next_onehot, block_expert, 0), axis=-1), -1)
    nonempty = (counts > 0).astype(I32)
    run_parity = pick(jnp.cumsum(nonempty) - nonempty) % 2
    run_next = jnp.stack([next_expert, run_parity], axis=-1).astype(I32).reshape(2 * n_blocks)
    return dest, block_expert, run_next, n_used.reshape(1)


def _band_relative_bias(table):
    def bucket(nd):
        max_exact = REL_BUCKETS // 2
        nf = jnp.maximum(nd, 1).astype(F32)
        large = max_exact + (jnp.log(nf / max_exact) / math.log(REL_MAX_DIST / max_exact)
                             * (REL_BUCKETS - max_exact)).astype(I32)
        return jnp.where(nd < max_exact, nd, jnp.minimum(large, REL_BUCKETS - 1))

    qi = jnp.arange(WINDOW)[:, None]
    kj = jnp.arange(2 * WINDOW)[None, :]
    dist = jnp.maximum(qi + WINDOW - kj, 0)
    onehot = (bucket(dist)[..., None] == jnp.arange(REL_BUCKETS)).astype(F32)
    return jnp.einsum('qkb,bh->hqk', onehot, table.astype(F32), precision=lax.Precision.HIGHEST)


def _hgrn_params(lb, g_out):
    lb = lb.reshape(HG_HEADS, 1, HG_DK)
    gain = jnp.broadcast_to(g_out.reshape(1, 1, HG_DV), (HG_HEADS, 1, HG_DV))
    rows = [jnp.log(lb), jnp.log1p(-lb), 1.0 - lb, gain, jnp.zeros((HG_HEADS, 4, HG_DK), F32)]
    return jnp.concatenate(rows, axis=1).astype(F32)


def kernel(x, mem, rel_bias_table, hg_lb_logits, g_mix, w_in, swa_gq, swa_gk, swa_sinks, hg_g_out, mla_g_cq, mla_g_ckv, mla_w_uq, mla_w_ukv, mla_gq, mla_gk, w_out, g_mem_q, g_mem_kv, w_mq, w_mkv, mem_gq, mem_gk, w_mo, g_ffn, w_group_router, b_group_router, w_expert_router, b_expert_router, w_gate, w_up, w_down):
    b, s, d = x.shape
    n = b * s
    m = mem.shape[1]
    rel_bias = _band_relative_bias(rel_bias_table)
    lb_all = jnp.cumsum(jax.nn.softmax(hg_lb_logits.astype(F32), axis=0), axis=0)
    lb_all = lb_all - lb_all[:1]
    offs = [0]
    for width in IN_SIZES:
        offs.append(offs[-1] + width)
    o_hg, o_cq, o_kr = offs[3], offs[7], offs[9]

    xf = x.reshape(n, d)
    mem2 = mem.reshape(b * m, d)
    for l in range(DEPTH):
        w = w_in[l].astype(BF16)
        w_swa = w[:, :o_hg]
        w_hg = w[:, o_hg:o_cq]
        w_mla = jnp.concatenate([w[:, o_cq:], _swap_halves(w[:, o_kr:])], axis=1)
        in_proj = functools.partial(_rms_matmul, xf, g_mix[l], tm=IN_PROJ_ROWS, out_dtype=BF16)
        u_swa = in_proj(w_swa, tn=w_swa.shape[1]).reshape(b, s, -1)
        u_hg = in_proj(w_hg, tn=IN_PROJ_HG_COLS).reshape(b, s, -1)
        u_mla = in_proj(w_mla, tn=w_mla.shape[1]).reshape(b, s, -1)

        out_a = _swa(u_swa, rel_bias, swa_gq[l], swa_gk[l], swa_sinks[l])
        out_b = _hgrn(u_hg, _hgrn_params(lb_all[l], hg_g_out[l]), chunk=HG_CHUNK, n_sub=HG_SUB)
        qm, km, vm = _mla_prep(u_mla, mla_w_uq[l], mla_w_ukv[l], mla_g_cq[l], mla_g_ckv[l],
                               mla_gq[l], mla_gk[l], tm=MLA_PREP_ROWS)
        out_c = _mla_attention(qm, km, vm, tq=MLA_TQ, tk=MLA_TK)

        kv = _rms_matmul(mem2, g_mem_kv[l], w_mkv[l].astype(BF16), tm=b * m, tn=2 * MEM_WIDTH)
        x3, ids, gates, cnt = _post_mixer(
            xf.reshape(b, s, d), out_a, out_b, out_c, w_out[l], kv.reshape(b, m, -1),
            g_mem_q[l], w_mq[l], w_mo[l], mem_gq[l], mem_gk[l],
            g_ffn[l], w_group_router[l], b_group_router[l], w_expert_router[l], b_expert_router[l],
            tm=POST_MIXER_ROWS)
        xf = x3.reshape(n, d)
        dest, block_expert, run_next, n_used = _moe_plan(ids, cnt, rows=MOE_ROWS)
        xs = _dispatch(xf, g_ffn[l], dest, jnp.zeros((block_expert.shape[0] * MOE_ROWS, d // 2), jnp.uint32),
                       ts=MOE_BURST_TOKENS)
        y = _experts(xs, block_expert, run_next, n_used, w_gate, w_up, w_down, l, rows=MOE_ROWS)
        xf = _combine(xf, gates, dest, y, tt=MOE_BURST_TOKENS)
    return xf.reshape(b, s, d)
```

```python
import functools
import math

import jax
import jax.numpy as jnp
from jax import lax
from jax.experimental import pallas as pl
from jax.experimental.pallas import tpu as pltpu

F32 = jnp.float32
BF16 = jnp.bfloat16
I32 = jnp.int32

D_MODEL = 2048
DEPTH = 2
SWA_HEADS = 8
SWA_KV_HEADS = 2
SWA_HEAD_DIM = 64
WINDOW = 128
HG_HEADS = 8
HG_DK = 128
HG_DV = 128
MLA_HEADS = 4
MLA_Q_RANK = 512
MLA_KV_RANK = 256
MLA_NOPE = 128
MLA_ROPE = 64
MLA_QK = MLA_NOPE + MLA_ROPE
MLA_V = 128
ROPE_THETA = 10000.0
REL_BUCKETS = 32
REL_MAX_DIST = 128
MEM_HEADS = 4
MEM_HEAD_DIM = 128
MEM_WIDTH = MEM_HEADS * MEM_HEAD_DIM
N_GROUPS = 8
EXPERTS_PER_GROUP = 8
N_EXPERTS = N_GROUPS * EXPERTS_PER_GROUP
TOP_K = 2
D_EXPERT = 512
MOE_BURST_TOKENS = 256
MOE_ROWS = 256
EPS = 1e-6
NEG_INF = -1e30
LOG2_E = math.log2(math.e)

SWA_WIDTH = SWA_HEADS * SWA_HEAD_DIM
SWA_KV_WIDTH = SWA_KV_HEADS * SWA_HEAD_DIM
HG_WIDTH = HG_HEADS * HG_DV
MLA_WIDTH = MLA_HEADS * MLA_V
IN_SIZES = (SWA_WIDTH, SWA_KV_WIDTH, SWA_KV_WIDTH,
            HG_HEADS * HG_DK, HG_HEADS * HG_DK, HG_WIDTH, HG_WIDTH,
            MLA_Q_RANK, MLA_KV_RANK, MLA_ROPE)

LANES = 128
SUBLANES = 8
DMA_PRIORITIES = 2
MLA_QK_PAD = 2 * LANES
VMEM_LIMIT_BYTES = 56 * 1024 * 1024

IN_PROJ_ROWS = 1024
IN_PROJ_HG_COLS = 2048
SWA_BLOCKS_PER_STEP = 8
HG_CHUNK = 128
HG_SUB = 16
MLA_PREP_ROWS = 512
MLA_TQ = 2048
MLA_TK = 2048
MLA_KEY_GROUP = 1024
MLA_ROW_GROUP = 256
POST_MIXER_ROWS = 512
NT_DIMS = (((1,), (1,)), ((), ()))
TN_DIMS = (((0,), (0,)), ((), ()))


def _params(*semantics):
    return pltpu.CompilerParams(dimension_semantics=semantics, vmem_limit_bytes=VMEM_LIMIT_BYTES)


def _rms(x, gain=None):
    y = x * lax.rsqrt(jnp.mean(x * x, axis=-1, keepdims=True) + EPS)
    return y if gain is None else y * gain


def _rms_matmul_kernel(x_ref, g_ref, w_ref, o_ref, hn_ref):
    @pl.when(pl.program_id(1) == 0)
    def _():
        hn_ref[...] = _rms(x_ref[...], g_ref[...]).astype(BF16)

    o_ref[...] = jnp.dot(hn_ref[...], w_ref[...], preferred_element_type=F32).astype(o_ref.dtype)


def _rms_matmul(x, gain, w, *, tm, tn, out_dtype=F32):
    n, d = x.shape
    nout = w.shape[1]
    return pl.pallas_call(
        _rms_matmul_kernel,
        grid=(n // tm, nout // tn),
        in_specs=[pl.BlockSpec((tm, d), lambda i, j: (i, 0)),
                  pl.BlockSpec((1, d), lambda i, j: (0, 0)),
                  pl.BlockSpec((d, tn), lambda i, j: (0, j))],
        out_specs=pl.BlockSpec((tm, tn), lambda i, j: (i, j)),
        out_shape=jax.ShapeDtypeStruct((n, nout), out_dtype),
        scratch_shapes=[pltpu.VMEM((tm, d), BF16)],
        compiler_params=_params("parallel", "arbitrary"),
        name="rms_matmul",
    )(x, gain.reshape(1, d), w)


def _head_rms(x, seg_ref, gain_ref):
    sq = x * x
    hi, lo = _split_bf16(sq)
    ss = (jnp.dot(hi, seg_ref[...], preferred_element_type=F32)
          + jnp.dot(lo, seg_ref[...], preferred_element_type=F32))
    return x * lax.rsqrt(ss * (1.0 / SWA_HEAD_DIM) + EPS) * gain_ref[...]


def _swa_kernel(sink_ref, q_ref, kp_ref, kc_ref, vp_ref, vc_ref, bias_ref, gq_ref, gk_ref, segq_ref, segk_ref, o_ref):
    blk = pl.program_id(1)
    grp = SWA_HEADS // SWA_KV_HEADS
    q_all = _head_rms(q_ref[0].astype(F32), segq_ref, gq_ref)
    k_all = _head_rms(jnp.concatenate([kp_ref[0], kc_ref[0]], axis=0).astype(F32), segk_ref, gk_ref)
    v_all = jnp.concatenate([vp_ref[0], vc_ref[0]], axis=0)
    qi = lax.broadcasted_iota(I32, (WINDOW, 2 * WINDOW), 0)
    kj = lax.broadcasted_iota(I32, (WINDOW, 2 * WINDOW), 1)
    dist = qi + WINDOW - kj
    in_window = (dist >= 0) & (dist < WINDOW)
    first_valid = in_window & (kj >= jnp.where(blk > 0, 0, WINDOW))
    for j in range(q_all.shape[0] // WINDOW):
        q = q_all[j * WINDOW:(j + 1) * WINDOW]
        k = k_all[j * WINDOW:(j + 2) * WINDOW]
        v = v_all[j * WINDOW:(j + 2) * WINDOW]
        valid = first_valid if j == 0 else in_window
        outs = []
        for g in range(SWA_KV_HEADS):
            lo = g * SWA_HEAD_DIM
            kg = k[:, lo:lo + SWA_HEAD_DIM].astype(BF16)
            vg = v[:, lo:lo + SWA_HEAD_DIM].astype(BF16)
            for h in range(g * grp, (g + 1) * grp):
                qh = q[:, h * SWA_HEAD_DIM:(h + 1) * SWA_HEAD_DIM].astype(BF16)
                s = lax.dot_general(qh, kg, NT_DIMS, preferred_element_type=F32) + bias_ref[h]
                s = jnp.where(valid, s, NEG_INF)
                sink = sink_ref[h]
                m = jnp.maximum(jnp.max(s, axis=-1, keepdims=True), sink)
                e = jnp.exp(s - m)
                denom = jnp.sum(e, axis=-1, keepdims=True) + jnp.exp(sink - m)
                outs.append(jnp.dot((e / denom).astype(BF16), vg, preferred_element_type=F32))
        o_ref[0, j * WINDOW:(j + 1) * WINDOW] = jnp.concatenate(outs, axis=-1).astype(o_ref.dtype)


def _swa(u_swa, rel_bias, gq, gk, sinks):
    b, s, _ = u_swa.shape
    rows = SWA_BLOCKS_PER_STEP * WINDOW
    kcol = SWA_WIDTH // LANES
    vcol = kcol + 1
    prev = lambda bi, n, *_: (bi, jnp.maximum(n * SWA_BLOCKS_PER_STEP - 1, 0))
    grp = SWA_HEADS // SWA_KV_HEADS

    def head_blocks(width):
        head = jnp.arange(width) // SWA_HEAD_DIM
        return (head[:, None] == head[None, :]).astype(BF16)

    return pl.pallas_call(
        _swa_kernel,
        grid_spec=pltpu.PrefetchScalarGridSpec(
            num_scalar_prefetch=1,
            grid=(b, s // rows),
            in_specs=[
                pl.BlockSpec((1, rows, SWA_WIDTH), lambda bi, n, *_: (bi, n, 0)),
                pl.BlockSpec((1, WINDOW, LANES), lambda bi, n, *_: prev(bi, n) + (kcol,)),
                pl.BlockSpec((1, rows, LANES), lambda bi, n, *_: (bi, n, kcol)),
                pl.BlockSpec((1, WINDOW, LANES), lambda bi, n, *_: prev(bi, n) + (vcol,)),
                pl.BlockSpec((1, rows, LANES), lambda bi, n, *_: (bi, n, vcol)),
                pl.BlockSpec((SWA_HEADS, WINDOW, 2 * WINDOW), lambda bi, n, *_: (0, 0, 0)),
                pl.BlockSpec((1, SWA_WIDTH), lambda bi, n, *_: (0, 0)),
                pl.BlockSpec((1, SWA_KV_WIDTH), lambda bi, n, *_: (0, 0)),
                pl.BlockSpec((SWA_WIDTH, SWA_WIDTH), lambda bi, n, *_: (0, 0)),
                pl.BlockSpec((SWA_KV_WIDTH, SWA_KV_WIDTH), lambda bi, n, *_: (0, 0)),
            ],
            out_specs=pl.BlockSpec((1, rows, SWA_WIDTH), lambda bi, n, *_: (bi, n, 0)),
        ),
        out_shape=jax.ShapeDtypeStruct((b, s, SWA_WIDTH), BF16),
        compiler_params=_params("parallel", "parallel"),
        name="swa_attention",
    )(sinks, u_swa, u_swa, u_swa, u_swa, u_swa, rel_bias,
      jnp.tile(gq * (SWA_HEAD_DIM ** -0.5), SWA_HEADS).reshape(1, SWA_WIDTH),
      jnp.tile(gk, SWA_KV_HEADS).reshape(1, SWA_KV_WIDTH), head_blocks(SWA_WIDTH), head_blocks(SWA_KV_WIDTH))


def _hgrn_level_matrix(c):
    t = jnp.arange(c)[:, None]
    r = jnp.arange(c)[None, :]
    mats = [(r <= t)]
    half = c // 2
    while half >= 1:
        mid = (t // (2 * half)) * (2 * half) + half
        is_q = (t & half) != 0
        if half < SUBLANES:
            mats.append(jnp.where(is_q, (r >= mid) & (r <= t), (r > t) & (r < mid)))
        half //= 2
    return jnp.concatenate(mats, axis=0).astype(BF16)


def _hgrn_pair_level(c):
    t = jnp.arange(c, dtype=I32)[:, None]
    s = jnp.arange(c, dtype=I32)[None, :]
    x = t ^ s
    lvl = jnp.zeros((c, c), I32)
    half = c // 2
    while half >= 1:
        lvl = jnp.where((x & (-half)) == half, half, lvl)
        half //= 2
    return jnp.where(t > s, lvl, 0)


def _hgrn_kernel(q_ref, f_ref, v_ref, gate_ref, par_ref, amat_ref, lvl_ref, o_ref, st_ref, *, chunk, n_sub):
    @pl.when(pl.program_id(2) == 0)
    def _():
        st_ref[...] = jnp.zeros_like(st_ref)

    c = chunk
    par = par_ref[0]
    log_lb, log1m_lb, one_m_lb, g_out = par[0:1], par[1:2], par[2:3], par[3:4]
    row = lax.broadcasted_iota(I32, (c, 1), 0)
    pair_level = lvl_ref[...]
    st = st_ref[...]
    for sub in range(n_sub):
        rows = pl.ds(sub * c, c)
        q = q_ref[0, rows].astype(F32)
        f = f_ref[0, rows].astype(F32)
        v = v_ref[0, rows].astype(F32)
        gate = gate_ref[0, rows].astype(F32)

        e = jnp.exp(-jnp.abs(f))
        log_sig = jnp.minimum(f, 0.0) - jnp.log(1.0 + e)
        bb = log1m_lb + log_sig
        log_f = jnp.maximum(log_lb, bb) + jnp.log(1.0 + jnp.exp(-jnp.abs(log_lb - bb)))
        log_f = log_f * LOG2_E
        kk = one_m_lb * jnp.where(f < 0.0, 1.0, e) / (1.0 + e)

        g_hi = log_f.astype(BF16)
        g_lo = (log_f - g_hi.astype(F32)).astype(BF16)
        e2 = jnp.dot(amat_ref[...], jnp.concatenate([g_hi, g_lo], axis=1), preferred_element_type=F32)
        expo = e2[:, :HG_DK] + e2[:, HG_DK:]
        bcum = expo[0:c]
        btot = bcum[c - 1:c]

        attn = jnp.zeros((c, c), F32)
        half = c // 2
        lvl = 1
        while half >= 1:
            is_q = (row & half) != 0
            seg = 2 * half
            if half >= SUBLANES:
                ref = jnp.concatenate([jnp.broadcast_to(bcum[a + half - 1:a + half], (seg, HG_DK))
                                       for a in range(0, c, seg)], axis=0)
                diff = bcum - ref
                w = jnp.exp2(jnp.where(is_q, diff, -diff))
            else:
                w = jnp.exp2(expo[lvl * c:(lvl + 1) * c])
                lvl += 1
            zf = jnp.where(is_q, q, kk) * w
            z = zf.astype(BF16)
            if half >= SUBLANES:
                zq = jnp.concatenate([zf[a + half:a + seg] for a in range(0, c, seg)], axis=0).astype(BF16)
                aq = lax.dot_general(zq, z, NT_DIMS, preferred_element_type=F32)
                blank = jnp.zeros((half, c), F32)
                a = jnp.concatenate([blk for j in range(c // seg)
                                     for blk in (blank, aq[j * half:(j + 1) * half])], axis=0)
            else:
                a = lax.dot_general(z, z, NT_DIMS, preferred_element_type=F32)
            attn = jnp.where(pair_level == half, a, attn)
            half //= 2

        vb = v.astype(BF16)
        diag = jnp.sum(q * kk, axis=-1, keepdims=True)
        intra = jnp.dot(attn.astype(BF16), vb, preferred_element_type=F32) + diag * v
        qe = (q * jnp.exp2(bcum)).astype(BF16)
        inter = lax.dot_general(qe, st.astype(BF16), NT_DIMS, preferred_element_type=F32)
        kd = (kk * jnp.exp2(btot - bcum)).astype(BF16)
        st = st * jnp.exp2(btot) +lax.dot_general(vb, kd, TN_DIMS, preferred_element_type=F32)

        o = _rms(inter + intra, g_out)
        o_ref[0, rows] = (o * gate / (1.0 + jnp.exp(-gate))).astype(o_ref.dtype)
    st_ref[...] = st


def _hgrn(u_hg, par, *, chunk, n_sub):
    b, s, _ = u_hg.shape
    amat = _hgrn_level_matrix(chunk)
    pair_level = _hgrn_pair_level(chunk)
    step = chunk * n_sub
    col = lambda off: (lambda bi, h, c: (bi, c, off + h))
    return pl.pallas_call(
        functools.partial(_hgrn_kernel, chunk=chunk, n_sub=n_sub),
        grid=(b, HG_HEADS, s // step),
        in_specs=[pl.BlockSpec((1, step, HG_DK), col(0)),
                  pl.BlockSpec((1, step, HG_DK), col(HG_HEADS)),
                  pl.BlockSpec((1, step, HG_DV), col(2 * HG_HEADS)),
                  pl.BlockSpec((1, step, HG_DV), col(3 * HG_HEADS)),
                  pl.BlockSpec((1, 8, HG_DK), lambda bi, h, c: (h, 0, 0)),
                  pl.BlockSpec(amat.shape, lambda bi, h, c: (0, 0)),
                  pl.BlockSpec(pair_level.shape, lambda bi, h, c: (0, 0))],
        out_specs=pl.BlockSpec((1, step, HG_DV), lambda bi, h, c: (bi, c, h)),
        out_shape=jax.ShapeDtypeStruct((b, s, HG_WIDTH), BF16),
        scratch_shapes=[pltpu.VMEM((HG_DV, HG_DK), F32)],
        compiler_params=_params("parallel", "parallel", "arbitrary"),
        name="hgrn2",
    )(u_hg, u_hg, u_hg, u_hg, par, amat, pair_level)


def _mla_prep_kernel(u_ref, tab_ref, wq_ref, wkv_ref, gcq_ref, gckv_ref, gqn_ref, gqr_ref, gkn_ref, gkr_ref,
                     q_ref, k_ref, v_ref):
    u = u_ref[0].astype(F32)
    cq = _rms(u[:, :MLA_Q_RANK], gcq_ref[...]).astype(BF16)
    ckv = _rms(u[:, MLA_Q_RANK:MLA_Q_RANK + MLA_KV_RANK], gckv_ref[...]).astype(BF16)
    kr = u[:, MLA_Q_RANK + MLA_KV_RANK:]
    qf = jnp.dot(cq, wq_ref[...], preferred_element_type=F32)
    kvf = jnp.dot(ckv, wkv_ref[...], preferred_element_type=F32)
    tab = tab_ref[...]
    low = lax.broadcasted_iota(I32, kr.shape, 1) < MLA_ROPE
    kr_sq = jnp.sum(jnp.where(low, kr * kr, 0.0), axis=-1, keepdims=True)
    scale = MLA_QK ** -0.5 * LOG2_E
    ones = jnp.ones((u.shape[0], MLA_V), F32)
    for h in range(MLA_HEADS):
        qn = qf[:, h * MLA_QK_PAD:h * MLA_QK_PAD + MLA_NOPE]
        qr = qf[:, h * MLA_QK_PAD + MLA_NOPE:(h + 1) * MLA_QK_PAD]
        ss = jnp.sum(qn * qn, axis=-1, keepdims=True) + jnp.sum(jnp.where(low, qr * qr, 0.0), axis=-1, keepdims=True)
        rstd = lax.rsqrt(ss / MLA_QK + EPS) * scale
        t = qr * rstd * tab * gqr_ref[...]
        rope = jnp.where(low, t + pltpu.roll(t, MLA_ROPE, 1), 0.0)
        q_ref[0, h] = jnp.concatenate([qn * rstd * gqn_ref[...], rope], axis=-1).astype(q_ref.dtype)

        kn = kvf[:, h * MLA_NOPE:(h + 1) * MLA_NOPE]
        ss = jnp.sum(kn * kn, axis=-1, keepdims=True) + kr_sq
        rstd = lax.rsqrt(ss / MLA_QK + EPS)
        t = kr * rstd * tab * gkr_ref[...]
        rope = t + pltpu.roll(t, MLA_ROPE, 1)
        k_ref[0, h] = jnp.concatenate([kn * rstd * gkn_ref[...], rope], axis=-1).astype(k_ref.dtype)
        vh = kvf[:, MLA_HEADS * MLA_NOPE + h * MLA_V:MLA_HEADS * MLA_NOPE + (h + 1) * MLA_V]
        v_ref[0, h] = jnp.concatenate([vh, ones], axis=-1).astype(v_ref.dtype)


def _swap_halves(a):
    half = a.shape[-1] // 2
    return jnp.concatenate([a[..., half:], a[..., :half]], axis=-1)


def _mla_prep(u_mla, w_uq, w_ukv, g_cq, g_ckv, gq, gk, *, tm):
    b, s, wu = u_mla.shape
    wq = w_uq.reshape(MLA_Q_RANK, MLA_HEADS, MLA_QK)
    wq = jnp.concatenate([wq, _swap_halves(wq[..., MLA_NOPE:])], axis=-1).reshape(MLA_Q_RANK, MLA_HEADS * MLA_QK_PAD)
    wkv = w_ukv.reshape(MLA_KV_RANK, MLA_HEADS, MLA_NOPE + MLA_V)
    wkv = jnp.concatenate([wkv[..., :MLA_NOPE].reshape(MLA_KV_RANK, -1), wkv[..., MLA_NOPE:].reshape(MLA_KV_RANK, -1)], axis=1)
    half = MLA_ROPE // 2
    inv_freq = ROPE_THETA ** (-jnp.arange(half, dtype=F32) / half)
    ang = jnp.arange(s, dtype=F32)[:, None] * inv_freq[None, :]
    cos, sin = jnp.cos(ang), jnp.sin(ang)
    tab = jnp.concatenate([cos, cos, -sin, sin], axis=-1)
    rope_gain = lambda g: jnp.concatenate([g[MLA_NOPE:], _swap_halves(g[MLA_NOPE:])]).reshape(1, 2 * MLA_ROPE)
    vec = lambda i, j: (0, 0)
    head_out = lambda width: pl.BlockSpec((1, MLA_HEADS, tm, width), lambda i, j: (i, 0, j, 0))
    return pl.pallas_call(
        _mla_prep_kernel,
        grid=(b, s // tm),
        in_specs=[pl.BlockSpec((1, tm, wu), lambda i, j: (i, j, 0)),
                  pl.BlockSpec((tm, 2 * MLA_ROPE), lambda i, j: (j, 0)),
                  pl.BlockSpec(wq.shape, vec),
                  pl.BlockSpec(wkv.shape, vec),
                  pl.BlockSpec((1, MLA_Q_RANK), vec),
                  pl.BlockSpec((1, MLA_KV_RANK), vec),
                  pl.BlockSpec((1, MLA_NOPE), vec),
                  pl.BlockSpec((1, 2 * MLA_ROPE), vec),
                  pl.BlockSpec((1, MLA_NOPE), vec),
                  pl.BlockSpec((1, 2 * MLA_ROPE), vec)],
        out_specs=[head_out(MLA_QK_PAD), head_out(MLA_QK_PAD), head_out(2 * MLA_V)],
        out_shape=[jax.ShapeDtypeStruct((b, MLA_HEADS, s, MLA_QK_PAD), BF16),
                   jax.ShapeDtypeStruct((b, MLA_HEADS, s, MLA_QK_PAD), BF16),
                   jax.ShapeDtypeStruct((b, MLA_HEADS, s, 2 * MLA_V), BF16)],
        compiler_params=_params("parallel", "parallel"),
        name="mla_prep",
    )(u_mla, tab, wq.astype(BF16), wkv.astype(BF16), g_cq.reshape(1, -1), g_ckv.reshape(1, -1),
      gq[:MLA_NOPE].reshape(1, -1), rope_gain(gq), gk[:MLA_NOPE].reshape(1, -1), rope_gain(gk))


def _mla_attn_kernel(qi_ref, ki_ref, q_ref, k_ref, v_ref, o_ref, m_ref, acc_ref, *, tq, tk):
    step = pl.program_id(2)
    qi = qi_ref[step]
    ki = ki_ref[step]

    @pl.when(ki == 0)
    def _():
        m_ref[...] = jnp.full_like(m_ref, NEG_INF)
        acc_ref[...] = jnp.zeros_like(acc_ref)

    def update(rel):
        for r0, k0 in [(r, k) for r in range(0, tq, MLA_ROW_GROUP) for k in range(0, tk, MLA_KEY_GROUP)]:
            first_key = None if rel is None else rel + k0
            if first_key is not None and first_key > r0 + MLA_ROW_GROUP - 1:
                continue
            rows = pl.ds(r0, MLA_ROW_GROUP)
            keys = pl.ds(k0, MLA_KEY_GROUP)
            s = lax.dot_general(q_ref[0, 0, rows], k_ref[0, 0, keys], NT_DIMS, preferred_element_type=F32)
            if first_key is not None and first_key + MLA_KEY_GROUP - 1 > r0:
                row = r0 + lax.broadcasted_iota(I32, s.shape, 0)
                col = first_key + lax.broadcasted_iota(I32, s.shape, 1)
                s = jnp.where(col <= row, s, NEG_INF)
            m_prev = m_ref[rows]
            m_next = jnp.maximum(m_prev, jnp.max(s, axis=-1, keepdims=True))
            alpha = jnp.exp2(m_prev - m_next)
            p = jnp.exp2(s - m_next[:, :1]).astype(BF16)
            pv = jnp.dot(p, v_ref[0, 0, keys], preferred_element_type=F32)
            acc_ref[rows, :MLA_V] = alpha * acc_ref[rows, :MLA_V] + pv[:, :MLA_V]
            acc_ref[rows, MLA_V:] = alpha * acc_ref[rows, MLA_V:] + pv[:, MLA_V:]
            m_ref[rows] = m_next

    key_offset = ki * tk - qi * tq
    pl.when(key_offset < 0)(lambda: update(None))
    for rel in range(0, tq, tk):
        pl.when(key_offset == rel)(functools.partial(update, rel))

    @pl.when((ki + 1) * tk == (qi + 1) * tq)
    def _():
        o_ref[0] = (acc_ref[:, :MLA_V] / acc_ref[:, MLA_V:]).astype(o_ref.dtype)


def _mla_attention(q, k, v, *, tq, tk):
    b, h, s, _ = q.shape
    pairs = [(qi, ki) for qi in range(s // tq) for ki in range((qi + 1) * tq // tk)]
    qi_of = jnp.array([p[0] for p in pairs], I32)
    ki_of = jnp.array([p[1] for p in pairs], I32)
    kv_idx = lambda bi, hi, t, qi_ref, ki_ref: (bi, hi, ki_ref[t], 0)
    return pl.pallas_call(
        functools.partial(_mla_attn_kernel, tq=tq, tk=tk),
        grid_spec=pltpu.PrefetchScalarGridSpec(
            num_scalar_prefetch=2,
            grid=(b, h, len(pairs)),
            in_specs=[pl.BlockSpec((1, 1, tq, MLA_QK_PAD), lambda bi, hi, t, qi_ref, ki_ref: (bi, hi, qi_ref[t], 0)),
                      pl.BlockSpec((1, 1, tk, MLA_QK_PAD), kv_idx),
                      pl.BlockSpec((1, 1, tk, 2 * MLA_V), kv_idx)],
            out_specs=pl.BlockSpec((1, tq, MLA_V), lambda bi, hi, t, qi_ref, ki_ref: (bi, qi_ref[t], hi)),
            scratch_shapes=[pltpu.VMEM((tq, MLA_V), F32), pltpu.VMEM((tq, 2 * MLA_V), F32)],
        ),
        out_shape=jax.ShapeDtypeStruct((b, s, h * MLA_V), BF16),
        compiler_params=_params("parallel", "parallel", "arbitrary"),
        name="mla_attention",
    )(qi_of, ki_of, q, k, v)


def _mem_attn_body(x, kv, g_ref, wq_ref, wo_ref, gq_ref, gk_ref):
    hn = _rms(x, g_ref[...]).astype(BF16)
    qf = jnp.dot(hn, wq_ref[...], preferred_element_type=F32)
    outs = []
    for h in range(MEM_HEADS):
        lo = h * MEM_HEAD_DIM
        qh = (_rms(qf[:, lo:lo + MEM_HEAD_DIM], gq_ref[...]) * (MEM_HEAD_DIM ** -0.5)).astype(BF16)
        kh = _rms(kv[:, lo:lo + MEM_HEAD_DIM], gk_ref[...]).astype(BF16)
        vh = kv[:, MEM_WIDTH + lo:MEM_WIDTH + lo + MEM_HEAD_DIM].astype(BF16)
        s = lax.dot_general(qh, kh, NT_DIMS, preferred_element_type=F32)
        e = jnp.exp(s - jnp.max(s, axis=-1, keepdims=True))
        p = (e / jnp.sum(e, axis=-1, keepdims=True)).astype(BF16)
        outs.append(jnp.dot(p, vh, preferred_element_type=F32))
    o = jnp.concatenate(outs, axis=-1).astype(BF16)
    return x + jnp.dot(o, wo_ref[...], preferred_element_type=F32)


def _split_bf16(a):
    hi = a.astype(BF16)
    return hi, (a - hi.astype(F32)).astype(BF16)


def _router_body(x, g_ref, whi_ref, wlo_ref, b_ref, ids_ref, gates_ref, cnt_ref, carry_ref):
    hn = _rms(x, g_ref[...])
    hi, lo = _split_bf16(hn)
    logits = (jnp.dot(hi, whi_ref[...], preferred_element_type=F32)
              + jnp.dot(hi, wlo_ref[...], preferred_element_type=F32)
              + jnp.dot(lo, whi_ref[...], preferred_element_type=F32)) + b_ref[...]
    lane = lax.broadcasted_iota(I32, logits.shape, 1)
    gl = jnp.where(lane < N_GROUPS, logits, NEG_INF)
    gmax = jnp.max(gl, axis=-1, keepdims=True)
    p_grp = 1.0 / jnp.sum(jnp.exp(gl - gmax), axis=-1, keepdims=True)
    grp = jnp.min(jnp.where(gl == gmax, lane, LANES), axis=-1, keepdims=True)
    in_grp = (lane >= N_GROUPS) & (lane < N_GROUPS + N_EXPERTS) & (((lane - N_GROUPS) // EXPERTS_PER_GROUP) == grp)
    el = jnp.where(in_grp, logits, NEG_INF)
    m1 = jnp.max(el, axis=-1, keepdims=True)
    i1 = jnp.min(jnp.where(el == m1, lane, LANES), axis=-1, keepdims=True)
    el2 = jnp.where(lane == i1, NEG_INF, el)
    m2 = jnp.max(el2, axis=-1, keepdims=True)
    i2 = jnp.min(jnp.where(el2 == m2, lane, LANES), axis=-1, keepdims=True)
    r = jnp.exp(m2 - m1)
    g1 = p_grp / (1.0 + r)
    gates_ref[...] = jnp.where(lane == 0, g1, jnp.where(lane == 1, g1 * r, 0.0))

    tm = logits.shape[0]
    used = jnp.where((lane == i1) | (lane == i2), 1.0, 0.0)
    earlier = lax.broadcasted_iota(I32, (tm, tm), 1) < lax.broadcasted_iota(I32, (tm, tm), 0)
    before = carry_ref[...] + jnp.dot(jnp.where(earlier, 1.0, 0.0).astype(BF16), used.astype(BF16),
                                      preferred_element_type=F32)
    r1 = jnp.sum(jnp.where(lane == i1, before, 0.0), axis=-1, keepdims=True).astype(I32)
    r2 = jnp.sum(jnp.where(lane == i2, before, 0.0), axis=-1, keepdims=True).astype(I32)
    carry_ref[...] += jnp.sum(used, axis=0, keepdims=True)
    cnt_ref[...] = carry_ref[...].astype(I32)
    ids_ref[...] = jnp.where(lane == 0, i1 - N_GROUPS, jnp.where(lane == 1, i2 - N_GROUPS,
                             jnp.where(lane == 2, r1, jnp.where(lane == 3, r2, 0))))


def _post_mixer_kernel(x_ref, a_ref, b_ref, c_ref, wa_ref, wb_ref, wc_ref,
                       kv_ref, gmq_ref, wmq_ref, wmo_ref, mgq_ref, mgk_ref,
                       gffn_ref, whi_ref, wlo_ref, bias_ref,
                       o_ref, ids_ref, gates_ref, cnt_ref, carry_ref):
    @pl.when((pl.program_id(0) == 0) & (pl.program_id(1) == 0))
    def _():
        carry_ref[...] = jnp.zeros_like(carry_ref)

    mix = jnp.dot(a_ref[0], wa_ref[...], preferred_element_type=F32)
    mix += jnp.dot(b_ref[0], wb_ref[...], preferred_element_type=F32)
    mix += jnp.dot(c_ref[0], wc_ref[...], preferred_element_type=F32)
    x = x_ref[0] + mix
    x = _mem_attn_body(x, kv_ref[0], gmq_ref, wmq_ref, wmo_ref, mgq_ref, mgk_ref)
    o_ref[0] = x
    _router_body(x, gffn_ref, whi_ref, wlo_ref, bias_ref, ids_ref, gates_ref, cnt_ref, carry_ref)


def _post_mixer(x, a, bmix, c, w_out, kv, g_mem_q, w_mq, w_mo, mem_gq, mem_gk,
                g_ffn, w_gr, b_gr, w_er, b_er, *, tm):
    b, s, d = x.shape
    m = kv.shape[1]
    nst = s // tm
    wa = w_out[:SWA_WIDTH].astype(BF16)
    wb = w_out[SWA_WIDTH:SWA_WIDTH + HG_WIDTH].astype(BF16)
    wc = w_out[SWA_WIDTH + HG_WIDTH:].astype(BF16)
    pad = LANES - N_GROUPS - N_EXPERTS
    wr = jnp.concatenate([w_gr, w_er, jnp.zeros((d, pad), F32)], axis=1)
    bias = jnp.concatenate([b_gr, b_er, jnp.zeros((pad,), F32)]).reshape(1, LANES)
    whi, wlo = _split_bf16(wr)
    tile = lambda width: pl.BlockSpec((1, tm, width), lambda i, j: (i, j, 0))
    const = lambda shape: pl.BlockSpec(shape, lambda i, j: (0,) * len(shape), pipeline_mode=pl.Buffered(1))
    flat = lambda width: pl.BlockSpec((tm, width), lambda i, j: (i * nst + j, 0))
    return pl.pallas_call(
        _post_mixer_kernel,
        grid=(b, nst),
        in_specs=[tile(d), tile(SWA_WIDTH), tile(HG_WIDTH), tile(MLA_WIDTH),
                  const(wa.shape), const(wb.shape), const(wc.shape),
                  pl.BlockSpec((1, m, 2 * MEM_WIDTH), lambda i, j: (i, 0, 0)),
                  const((1, d)), const((d, MEM_WIDTH)), const((MEM_WIDTH, d)),
                  const((1, MEM_HEAD_DIM)), const((1, MEM_HEAD_DIM)),
                  const((1, d)), const((d, LANES)), const((d, LANES)), const((1, LANES))],
        out_specs=[tile(d), flat(LANES), flat(LANES), pl.BlockSpec((1, LANES), lambda i, j: (0, 0))],
        out_shape=[jax.ShapeDtypeStruct((b, s, d), F32), jax.ShapeDtypeStruct((b * s, LANES), I32),
                   jax.ShapeDtypeStruct((b * s, LANES), F32), jax.ShapeDtypeStruct((1, LANES), I32)],
        scratch_shapes=[pltpu.VMEM((1, LANES), F32)],
        compiler_params=_params("arbitrary", "arbitrary"),
        name="post_mixer",
    )(x, a, bmix, c, wa, wb, wc, kv, g_mem_q.reshape(1, d), w_mq.astype(BF16), w_mo.astype(BF16),
      mem_gq.reshape(1, -1), mem_gk.reshape(1, -1), g_ffn.reshape(1, d), whi, wlo, bias)


HI_HALF_MASK = 0xFFFF0000


def _pack_bf16_pairs(a):
    half = a.shape[1] // 2
    lo = pltpu.bitcast(a[:, :half].astype(BF16).astype(F32), jnp.uint32) >> 16
    hi = pltpu.bitcast(a[:, half:].astype(BF16).astype(F32), jnp.uint32) & jnp.uint32(HI_HALF_MASK)
    return lo | hi


def _unpack_bf16_pairs(w):
    return pltpu.bitcast(w << 16, F32), pltpu.bitcast(w & jnp.uint32(HI_HALF_MASK), F32)


def _dispatch_kernel(dest_ref, x_ref, g_ref, xs_in_hbm, xs_hbm, buf, sem, *, ts):
    del xs_in_hbm
    i = pl.program_id(0)
    last = pl.num_programs(0) - 1
    slot = i % 2

    def row_copy(slot_, r, dst_row):
        return pltpu.make_async_copy(buf.at[slot_, pl.ds(r, 1)], xs_hbm.at[pl.ds(dst_row, 1)], sem.at[slot_])

    def wait_step(slot_):
        for _ in range(ts * TOP_K):
            row_copy(slot_, 0, 0).wait()

    @pl.when(i >= 2)
    def _():
        wait_step(slot)

    buf[slot] = _pack_bf16_pairs(_rms(x_ref[...], g_ref[...]))
    for r in range(ts):
        for kk in range(TOP_K):
            row_copy(slot, r, dest_ref[(i * ts + r) * TOP_K + kk]).start(priority=kk % DMA_PRIORITIES)

    @pl.when(i == last)
    def _():
        wait_step(slot)

    @pl.when((i == last) & (i >= 1))
    def _():
        wait_step(1 - slot)


def _dispatch(x, g_ffn, dest, xs_init, *, ts):
    n, d = x.shape
    n_rows = xs_init.shape[0]
    return pl.pallas_call(
        functools.partial(_dispatch_kernel, ts=ts),
        grid_spec=pltpu.PrefetchScalarGridSpec(
            num_scalar_prefetch=1,
            grid=(n // ts,),
            in_specs=[pl.BlockSpec((ts, d), lambda i, *_: (i, 0)),
                      pl.BlockSpec((1, d), lambda i, *_: (0, 0)),
                      pl.BlockSpec(memory_space=pl.ANY)],
            out_specs=pl.BlockSpec(memory_space=pl.ANY),
            scratch_shapes=[pltpu.VMEM((2, ts, d // 2), jnp.uint32), pltpu.SemaphoreType.DMA((2,))],
        ),
        out_shape=jax.ShapeDtypeStruct((n_rows, d // 2), jnp.uint32),
        input_output_aliases={3: 0},
        compiler_params=_params("arbitrary"),
        name="moe_dispatch",
    )(dest, x, g_ffn.reshape(1, d), xs_init)


def _expert_kernel(bexp_ref, next_ref, nblk_ref, xs_ref, wg_hbm, wu_hbm, wd_hbm, y_ref,
                   wg_f32, wu_f32, wd_f32, sem, wg_bf, wu_bf, wd_bf, *, layer):
    i = pl.program_id(0)
    n_used = nblk_ref[0]
    expert = bexp_ref[i]
    first_of_run = (i == 0) | (expert != bexp_ref[jnp.maximum(i - 1, 0)])
    slot = next_ref[2 * i + 1]

    def weight_copies(e, slot_):
        copies = []
        for hbm, buf in ((wg_hbm, wg_f32), (wu_hbm, wu_f32), (wd_hbm, wd_f32)):
            half = buf.shape[1] // 2
            for part in range(2):
                rows = pl.ds(part * half, half)
                copies.append(pltpu.make_async_copy(hbm.at[layer, e, rows], buf.at[slot_, rows],
                                                    sem.at[slot_, len(copies)]))
        return copies

    @pl.when(i == 0)
    def _():
        for j, cp in enumerate(weight_copies(expert, 0)):
            cp.start(priority=j % DMA_PRIORITIES)

    @pl.when(first_of_run & (i < n_used))
    def _():
        for cp in weight_copies(expert, slot):
            cp.wait()
        next_expert = next_ref[2 * i]

        @pl.when(next_expert >= 0)
        def _():
            for j, cp in enumerate(weight_copies(next_expert, 1 - slot)):
                cp.start(priority=j % DMA_PRIORITIES)

        wg_bf[...] = wg_f32[slot].astype(BF16)
        wu_bf[...] = wu_f32[slot].astype(BF16)
        wd_bf[...] = wd_f32[slot].astype(BF16)

    @pl.when(i < n_used)
    def _():
        half = xs_ref.shape[1]
        lo, hi = (part.astype(BF16) for part in _unpack_bf16_pairs(xs_ref[...]))
        gate = (jnp.dot(lo, wg_bf[:half], preferred_element_type=F32)
                + jnp.dot(hi, wg_bf[half:], preferred_element_type=F32))
        up = (jnp.dot(lo, wu_bf[:half], preferred_element_type=F32)
              + jnp.dot(hi, wu_bf[half:], preferred_element_type=F32))
        act = (gate / (1.0 + jnp.exp(-gate)) * up).astype(BF16)
        y_ref[...] = _pack_bf16_pairs(jnp.dot(act, wd_bf[...], preferred_element_type=F32))

    @pl.when(i >= nblk_ref[0])
    def _():
        y_ref[...] = jnp.zeros_like(y_ref)


def _experts(xs, block_expert, run_next, n_used, w_gate, w_up, w_down, layer, *, rows):
    d = 2 * xs.shape[1]
    n_blocks = block_expert.shape[0]
    hbm = pl.BlockSpec(memory_space=pl.ANY)
    return pl.pallas_call(
        functools.partial(_expert_kernel, layer=layer),
        grid_spec=pltpu.PrefetchScalarGridSpec(
            num_scalar_prefetch=3,
            grid=(n_blocks,),
            in_specs=[pl.BlockSpec((rows, d // 2), lambda i, bexp, nxt, nblk: (jnp.minimum(i, nblk[0] - 1), 0)),
                      hbm, hbm, hbm],
            out_specs=pl.BlockSpec((rows, d // 2), lambda i, *_: (i, 0)),
            scratch_shapes=[pltpu.VMEM((2, d, D_EXPERT), F32),
                            pltpu.VMEM((2, d, D_EXPERT), F32),
                            pltpu.VMEM((2, D_EXPERT, d), F32),
                            pltpu.SemaphoreType.DMA((2, 6)),
                            pltpu.VMEM((d, D_EXPERT), BF16),
                            pltpu.VMEM((d, D_EXPERT), BF16),
                            pltpu.VMEM((D_EXPERT, d), BF16)],
        ),
        out_shape=jax.ShapeDtypeStruct((n_blocks * rows, d // 2), jnp.uint32),
        compiler_params=_params("arbitrary"),
        name="moe_experts",
    )(block_expert, run_next, n_used, xs, w_gate, w_up, w_down)


def _combine_kernel(pos_ref, x_ref, gates_ref, y_hbm, o_ref, ybuf, sem, *, tt):
    i = pl.program_id(0)
    nsteps = pl.num_programs(0)

    def row_copy(src_row, slot, dst_row):
        return pltpu.make_async_copy(y_hbm.at[pl.ds(src_row, 1)], ybuf.at[slot, pl.ds(dst_row, 1)], sem.at[slot])

    def start_gather(step, slot):
        for r in range(tt):
            for kk in range(TOP_K):
                row_copy(pos_ref[(step * tt + r) * TOP_K + kk], slot, kk * tt + r).start(priority=kk % DMA_PRIORITIES)

    def wait_gather(slot):
        for r in range(TOP_K * tt):
            row_copy(0, slot, r).wait()

    @pl.when(i == 0)
    def _():
        start_gather(0, 0)

    @pl.when(i + 1 < nsteps)
    def _():
        start_gather(i + 1, (i + 1) % 2)

    wait_gather(i % 2)
    g = gates_ref[...]
    half = ybuf.shape[2]
    for cols, y in zip((pl.ds(0, half), pl.ds(half, half)), _unpack_bf16_pairs(ybuf[i % 2])):
        o_ref[:, cols] = x_ref[:, cols] + g[:, 0:1] * y[:tt] + g[:, 1:2] * y[tt:]


def _combine(x, gates, pos, y, *, tt):
    n, d = x.shape
    return pl.pallas_call(
        functools.partial(_combine_kernel, tt=tt),
        grid_spec=pltpu.PrefetchScalarGridSpec(
            num_scalar_prefetch=1,
            grid=(n // tt,),
            in_specs=[pl.BlockSpec((tt, d), lambda i, *_: (i, 0)),
                      pl.BlockSpec((tt, LANES), lambda i, *_: (i, 0)),
                      pl.BlockSpec(memory_space=pl.ANY)],
            out_specs=pl.BlockSpec((tt, d), lambda i, *_: (i, 0)),
            scratch_shapes=[pltpu.VMEM((2, TOP_K * tt, d // 2), jnp.uint32), pltpu.SemaphoreType.DMA((2,))],
        ),
        out_shape=jax.ShapeDtypeStruct((n, d), F32),
        compiler_params=_params("arbitrary"),
        name="moe_combine",
    )(pos, x, gates, y)


def _moe_plan(ids, cnt, *, rows):
    n = ids.shape[0]
    expert, rank = ids[:, :TOP_K], ids[:, TOP_K:2 * TOP_K]
    counts = cnt[0, N_GROUPS:N_GROUPS + N_EXPERTS]
    padded = (counts + rows - 1) // rows * rows
    padded_end = jnp.cumsum(padded)
    seg_start = padded_end - padded
    onehot = expert[..., None] == jnp.arange(N_EXPERTS, dtype=I32)
    dest = (jnp.sum(jnp.where(onehot, seg_start, 0), axis=-1) + rank).astype(I32).reshape(n * TOP_K)
    n_blocks = -(-(n * TOP_K) // rows) + N_EXPERTS
    block_start = jnp.arange(n_blocks, dtype=I32) * rows
    block_expert = jnp.minimum(jnp.sum(padded_end[None, :] <= block_start[:, None], axis=1), N_EXPERTS - 1).astype(I32)
    n_used = (padded_end[-1] // rows).astype(I32)
    block_onehot = block_expert[:, None] == jnp.arange(N_EXPERTS, dtype=I32)
    pick = lambda per_expert: jnp.sum(jnp.where(block_onehot, per_expert, 0), axis=-1)
    next_block = pick(padded_end // rows)
    next_onehot = jnp.minimum(next_block, n_blocks - 1)[:, None] == jnp.arange(n_blocks, dtype=I32)
    next_expert = jnp.where(next_block < n_used, jnp.sum(jnp.where(next_onehot, block_expert, 0), axis=-1), -1)
    nonempty = (counts > 0).astype(I32)
    run_parity = pick(jnp.cumsum(nonempty) - nonempty) % 2
    run_next = jnp.stack([next_expert, run_parity], axis=-1).astype(I32).reshape(2 * n_blocks)
    return dest, block_expert, run_next, n_used.reshape(1)


def _band_relative_bias(table):
    def bucket(nd):
        max_exact = REL_BUCKETS // 2
        nf = jnp.maximum(nd, 1).astype(F32)
        large = max_exact + (jnp.log(nf / max_exact) / math.log(REL_MAX_DIST / max_exact)
                             * (REL_BUCKETS - max_exact)).astype(I32)
        return jnp.where(nd < max_exact, nd, jnp.minimum(large, REL_BUCKETS - 1))

    qi = jnp.arange(WINDOW)[:, None]
    kj = jnp.arange(2 * WINDOW)[None, :]
    dist = jnp.maximum(qi + WINDOW - kj, 0)
    onehot = (bucket(dist)[..., None] == jnp.arange(REL_BUCKETS)).astype(F32)
    return jnp.einsum('qkb,bh->hqk', onehot, table.astype(F32), precision=lax.Precision.HIGHEST)


def _hgrn_params(lb, g_out):
    lb = lb.reshape(HG_HEADS, 1, HG_DK)
    gain = jnp.broadcast_to(g_out.reshape(1, 1, HG_DV), (HG_HEADS, 1, HG_DV))
    rows = [jnp.log(lb), jnp.log1p(-lb), 1.0 - lb, gain, jnp.zeros((HG_HEADS, 4, HG_DK), F32)]
    return jnp.concatenate(rows, axis=1).astype(F32)


def kernel(x, mem, rel_bias_table, hg_lb_logits, g_mix, w_in, swa_gq, swa_gk, swa_sinks, hg_g_out, mla_g_cq, mla_g_ckv, mla_w_uq, mla_w_ukv, mla_gq, mla_gk, w_out, g_mem_q, g_mem_kv, w_mq, w_mkv, mem_gq, mem_gk, w_mo, g_ffn, w_group_router, b_group_router, w_expert_router, b_expert_router, w_gate, w_up, w_down):
    b, s, d = x.shape
    n = b * s
    m = mem.shape[1]
    rel_bias = _band_relative_bias(rel_bias_table)
    lb_all = jnp.cumsum(jax.nn.softmax(hg_lb_logits.astype(F32), axis=0), axis=0)
    lb_all = lb_all - lb_all[:1]
    offs = [0]
    for width in IN_SIZES:
        offs.append(offs[-1] + width)
    o_hg, o_cq, o_kr = offs[3], offs[7], offs[9]

    xf = x.reshape(n, d)
    mem2 = mem.reshape(b * m, d)
    for l in range(DEPTH):
        w = w_in[l].astype(BF16)
        w_swa = w[:, :o_hg]
        w_hg = w[:, o_hg:o_cq]
        w_mla = jnp.concatenate([w[:, o_cq:], _swap_halves(w[:, o_kr:])], axis=1)
        in_proj = functools.partial(_rms_matmul, xf, g_mix[l], tm=IN_PROJ_ROWS, out_dtype=BF16)
        u_swa = in_proj(w_swa, tn=w_swa.shape[1]).reshape(b, s, -1)
        u_hg = in_proj(w_hg, tn=IN_PROJ_HG_COLS).reshape(b, s, -1)
        u_mla = in_proj(w_mla, tn=w_mla.shape[1]).reshape(b, s, -1)

        out_a = _swa(u_swa, rel_bias, swa_gq[l], swa_gk[l], swa_sinks[l])
        out_b = _hgrn(u_hg, _hgrn_params(lb_all[l], hg_g_out[l]), chunk=HG_CHUNK, n_sub=HG_SUB)
        qm, km, vm = _mla_prep(u_mla, mla_w_uq[l], mla_w_ukv[l], mla_g_cq[l], mla_g_ckv[l],
                               mla_gq[l], mla_gk[l], tm=MLA_PREP_ROWS)
        out_c = _mla_attention(qm, km, vm, tq=MLA_TQ, tk=MLA_TK)

        kv = _rms_matmul(mem2, g_mem_kv[l], w_mkv[l].astype(BF16), tm=b * m, tn=2 * MEM_WIDTH)
        x3, ids, gates, cnt = _post_mixer(
            xf.reshape(b, s, d), out_a, out_b, out_c, w_out[l], kv.reshape(b, m, -1),
            g_mem_q[l], w_mq[l], w_mo[l], mem_gq[l], mem_gk[l],
            g_ffn[l], w_group_router[l], b_group_router[l], w_expert_router[l], b_expert_router[l],
            tm=POST_MIXER_ROWS)
        xf = x3.reshape(n, d)
        dest, block_expert, run_next, n_used = _moe_plan(ids, cnt, rows=MOE_ROWS)
        xs = _dispatch(xf, g_ffn[l], dest, jnp.zeros((block_expert.shape[0] * MOE_ROWS, d // 2), jnp.uint32),
                       ts=MOE_BURST_TOKENS)
        y = _experts(xs, block_expert, run_next, n_used, w_gate, w_up, w_down, l, rows=MOE_ROWS)
        xf = _combine(xf, gates, dest, y, tt=MOE_BURST_TOKENS)
    return xf.reshape(b, s, d)
```

```python
import functools
import math

import jax
import jax.numpy as jnp
from jax import lax
from jax.experimental import pallas as pl
from jax.experimental.pallas import tpu as pltpu

F32 = jnp.float32
BF16 = jnp.bfloat16
I32 = jnp.int32

D_MODEL = 2048
DEPTH = 2
SWA_HEADS = 8
SWA_KV_HEADS = 2
SWA_HEAD_DIM = 64
WINDOW = 128
HG_HEADS = 8
HG_DK = 128
HG_DV = 128
MLA_HEADS = 4
MLA_Q_RANK = 512
MLA_KV_RANK = 256
MLA_NOPE = 128
MLA_ROPE = 64
MLA_QK = MLA_NOPE + MLA_ROPE
MLA_V = 128
ROPE_THETA = 10000.0
REL_BUCKETS = 32
REL_MAX_DIST = 128
MEM_HEADS = 4
MEM_HEAD_DIM = 128
MEM_WIDTH = MEM_HEADS * MEM_HEAD_DIM
N_GROUPS = 8
EXPERTS_PER_GROUP = 8
N_EXPERTS = N_GROUPS * EXPERTS_PER_GROUP
TOP_K = 2
D_EXPERT = 512
MOE_BURST_TOKENS = 256
MOE_ROWS = 256
EPS = 1e-6
NEG_INF = -1e30
LOG2_E = math.log2(math.e)

SWA_WIDTH = SWA_HEADS * SWA_HEAD_DIM
SWA_KV_WIDTH = SWA_KV_HEADS * SWA_HEAD_DIM
HG_WIDTH = HG_HEADS * HG_DV
MLA_WIDTH = MLA_HEADS * MLA_V
IN_SIZES = (SWA_WIDTH, SWA_KV_WIDTH, SWA_KV_WIDTH,
            HG_HEADS * HG_DK, HG_HEADS * HG_DK, HG_WIDTH, HG_WIDTH,
            MLA_Q_RANK, MLA_KV_RANK, MLA_ROPE)

LANES = 128
SUBLANES = 8
DMA_PRIORITIES = 2
MLA_QK_PAD = 2 * LANES
VMEM_LIMIT_BYTES = 56 * 1024 * 1024

IN_PROJ_ROWS = 1024
IN_PROJ_HG_COLS = 2048
SWA_BLOCKS_PER_STEP = 8
HG_CHUNK = 128
HG_SUB = 32
MLA_PREP_ROWS = 512
MLA_TQ = 2048
MLA_TK = 2048
MLA_KEY_GROUP = 1024
MLA_ROW_GROUP = 256
POST_MIXER_ROWS = 512
NT_DIMS = (((1,), (1,)), ((), ()))
TN_DIMS = (((0,), (0,)), ((), ()))


def _params(*semantics):
    return pltpu.CompilerParams(dimension_semantics=semantics, vmem_limit_bytes=VMEM_LIMIT_BYTES)


def _rms(x, gain=None):
    y = x * lax.rsqrt(jnp.mean(x * x, axis=-1, keepdims=True) + EPS)
    return y if gain is None else y * gain


def _rms_matmul_kernel(x_ref, g_ref, w_ref, o_ref, hn_ref):
    @pl.when(pl.program_id(1) == 0)
    def _():
        hn_ref[...] = _rms(x_ref[...], g_ref[...]).astype(BF16)

    o_ref[...] = jnp.dot(hn_ref[...], w_ref[...], preferred_element_type=F32).astype(o_ref.dtype)


def _rms_matmul(x, gain, w, *, tm, tn, out_dtype=F32):
    n, d = x.shape
    nout = w.shape[1]
    return pl.pallas_call(
        _rms_matmul_kernel,
        grid=(n // tm, nout // tn),
        in_specs=[pl.BlockSpec((tm, d), lambda i, j: (i, 0)),
                  pl.BlockSpec((1, d), lambda i, j: (0, 0)),
                  pl.BlockSpec((d, tn), lambda i, j: (0, j))],
        out_specs=pl.BlockSpec((tm, tn), lambda i, j: (i, j)),
        out_shape=jax.ShapeDtypeStruct((n, nout), out_dtype),
        scratch_shapes=[pltpu.VMEM((tm, d), BF16)],
        compiler_params=_params("parallel", "arbitrary"),
        name="rms_matmul",
    )(x, gain.reshape(1, d), w)


def _head_rms(x, seg_ref, gain_ref):
    sq = x * x
    hi, lo = _split_bf16(sq)
    ss = (jnp.dot(hi, seg_ref[...], preferred_element_type=F32)
          + jnp.dot(lo, seg_ref[...], preferred_element_type=F32))
    return x * lax.rsqrt(ss * (1.0 / SWA_HEAD_DIM) + EPS) * gain_ref[...]


def _swa_kernel(sink_ref, q_ref, kp_ref, kc_ref, vp_ref, vc_ref, bias_ref, gq_ref, gk_ref, segq_ref, segk_ref, o_ref):
    blk = pl.program_id(1)
    grp = SWA_HEADS // SWA_KV_HEADS
    q_all = _head_rms(q_ref[0].astype(F32), segq_ref, gq_ref)
    k_all = _head_rms(jnp.concatenate([kp_ref[0], kc_ref[0]], axis=0).astype(F32), segk_ref, gk_ref)
    v_all = jnp.concatenate([vp_ref[0], vc_ref[0]], axis=0)
    qi = lax.broadcasted_iota(I32, (WINDOW, 2 * WINDOW), 0)
    kj = lax.broadcasted_iota(I32, (WINDOW, 2 * WINDOW), 1)
    dist = qi + WINDOW - kj
    in_window = (dist >= 0) & (dist < WINDOW)
    first_valid = in_window & (kj >= jnp.where(blk > 0, 0, WINDOW))
    for j in range(q_all.shape[0] // WINDOW):
        q = q_all[j * WINDOW:(j + 1) * WINDOW]
        k = k_all[j * WINDOW:(j + 2) * WINDOW]
        v = v_all[j * WINDOW:(j + 2) * WINDOW]
        valid = first_valid if j == 0 else in_window
        outs = []
        for g in range(SWA_KV_HEADS):
            lo = g * SWA_HEAD_DIM
            kg = k[:, lo:lo + SWA_HEAD_DIM].astype(BF16)
            vg = v[:, lo:lo + SWA_HEAD_DIM].astype(BF16)
            for h in range(g * grp, (g + 1) * grp):
                qh = q[:, h * SWA_HEAD_DIM:(h + 1) * SWA_HEAD_DIM].astype(BF16)
                s = lax.dot_general(qh, kg, NT_DIMS, preferred_element_type=F32) + bias_ref[h]
                s = jnp.where(valid, s, NEG_INF)
                sink = sink_ref[h]
                m = jnp.maximum(jnp.max(s, axis=-1, keepdims=True), sink)
                e = jnp.exp(s - m)
                denom = jnp.sum(e, axis=-1, keepdims=True) + jnp.exp(sink - m)
                outs.append(jnp.dot((e / denom).astype(BF16), vg, preferred_element_type=F32))
        o_ref[0, j * WINDOW:(j + 1) * WINDOW] = jnp.concatenate(outs, axis=-1).astype(o_ref.dtype)


def _swa(u_swa, rel_bias, gq, gk, sinks):
    b, s, _ = u_swa.shape
    rows = SWA_BLOCKS_PER_STEP * WINDOW
    kcol = SWA_WIDTH // LANES
    vcol = kcol + 1
    prev = lambda bi, n, *_: (bi, jnp.maximum(n * SWA_BLOCKS_PER_STEP - 1, 0))
    grp = SWA_HEADS // SWA_KV_HEADS

    def head_blocks(width):
        head = jnp.arange(width) // SWA_HEAD_DIM
        return (head[:, None] == head[None, :]).astype(BF16)

    return pl.pallas_call(
        _swa_kernel,
        grid_spec=pltpu.PrefetchScalarGridSpec(
            num_scalar_prefetch=1,
            grid=(b, s // rows),
            in_specs=[
                pl.BlockSpec((1, rows, SWA_WIDTH), lambda bi, n, *_: (bi, n, 0)),
                pl.BlockSpec((1, WINDOW, LANES), lambda bi, n, *_: prev(bi, n) + (kcol,)),
                pl.BlockSpec((1, rows, LANES), lambda bi, n, *_: (bi, n, kcol)),
                pl.BlockSpec((1, WINDOW, LANES), lambda bi, n, *_: prev(bi, n) + (vcol,)),
                pl.BlockSpec((1, rows, LANES), lambda bi, n, *_: (bi, n, vcol)),
                pl.BlockSpec((SWA_HEADS, WINDOW, 2 * WINDOW), lambda bi, n, *_: (0, 0, 0)),
                pl.BlockSpec((1, SWA_WIDTH), lambda bi, n, *_: (0, 0)),
                pl.BlockSpec((1, SWA_KV_WIDTH), lambda bi, n, *_: (0, 0)),
                pl.BlockSpec((SWA_WIDTH, SWA_WIDTH), lambda bi, n, *_: (0, 0)),
                pl.BlockSpec((SWA_KV_WIDTH, SWA_KV_WIDTH), lambda bi, n, *_: (0, 0)),
            ],
            out_specs=pl.BlockSpec((1, rows, SWA_WIDTH), lambda bi, n, *_: (bi, n, 0)),
        ),
        out_shape=jax.ShapeDtypeStruct((b, s, SWA_WIDTH), BF16),
        compiler_params=_params("parallel", "parallel"),
        name="swa_attention",
    )(sinks, u_swa, u_swa, u_swa, u_swa, u_swa, rel_bias,
      jnp.tile(gq * (SWA_HEAD_DIM ** -0.5), SWA_HEADS).reshape(1, SWA_WIDTH),
      jnp.tile(gk, SWA_KV_HEADS).reshape(1, SWA_KV_WIDTH), head_blocks(SWA_WIDTH), head_blocks(SWA_KV_WIDTH))


def _hgrn_level_matrix(c):
    t = jnp.arange(c)[:, None]
    r = jnp.arange(c)[None, :]
    mats = [(r <= t)]
    half = c // 2
    while half >= 1:
        mid = (t // (2 * half)) * (2 * half) + half
        is_q = (t & half) != 0
        if half < SUBLANES:
            mats.append(jnp.where(is_q, (r >= mid) & (r <= t), (r > t) & (r < mid)))
        half //= 2
    return jnp.concatenate(mats, axis=0).astype(BF16)


def _hgrn_pair_level(c):
    t = jnp.arange(c, dtype=I32)[:, None]
    s = jnp.arange(c, dtype=I32)[None, :]
    x = t ^ s
    lvl = jnp.zeros((c, c), I32)
    half = c // 2
    while half >= 1:
        lvl = jnp.where((x & (-half)) == half, half, lvl)
        half //= 2
    return jnp.where(t > s, lvl, 0)


def _hgrn_kernel(q_ref, f_ref, v_ref, gate_ref, par_ref, amat_ref, lvl_ref, o_ref, st_ref, *, chunk, n_sub):
    @pl.when(pl.program_id(2) == 0)
    def _():
        st_ref[...] = jnp.zeros_like(st_ref)

    c = chunk
    par = par_ref[0]
    log_lb, log1m_lb, one_m_lb, g_out = par[0:1], par[1:2], par[2:3], par[3:4]
    row = lax.broadcasted_iota(I32, (c, 1), 0)
    pair_level = lvl_ref[...]
    st = st_ref[...]
    for sub in range(n_sub):
        rows = pl.ds(sub * c, c)
        q = q_ref[0, rows].astype(F32)
        f = f_ref[0, rows].astype(F32)
        v = v_ref[0, rows].astype(F32)
        gate = gate_ref[0, rows].astype(F32)

        e = jnp.exp(-jnp.abs(f))
        log_sig = jnp.minimum(f, 0.0) - jnp.log(1.0 + e)
        bb = log1m_lb + log_sig
        log_f = jnp.maximum(log_lb, bb) + jnp.log(1.0 + jnp.exp(-jnp.abs(log_lb - bb)))
        log_f = log_f * LOG2_E
        kk = one_m_lb * jnp.where(f < 0.0, 1.0, e) / (1.0 + e)

        g_hi = log_f.astype(BF16)
        g_lo = (log_f - g_hi.astype(F32)).astype(BF16)
        e2 = jnp.dot(amat_ref[...], jnp.concatenate([g_hi, g_lo], axis=1), preferred_element_type=F32)
        expo = e2[:, :HG_DK] + e2[:, HG_DK:]
        bcum = expo[0:c]
        btot = bcum[c - 1:c]

        attn = jnp.zeros((c, c), F32)
        half = c // 2
        lvl = 1
        while half >= 1:
            is_q = (row & half) != 0
            seg = 2 * half
            if half >= SUBLANES:
                ref = jnp.concatenate([jnp.broadcast_to(bcum[a + half - 1:a + half], (seg, HG_DK))
                                       for a in range(0, c, seg)], axis=0)
                diff = bcum - ref
                w = jnp.exp2(jnp.where(is_q, diff, -diff))
            else:
                w = jnp.exp2(expo[lvl * c:(lvl + 1) * c])
                lvl += 1
            zf = jnp.where(is_q, q, kk) * w
            z = zf.astype(BF16)
            if half >= SUBLANES:
                zq = jnp.concatenate([zf[a + half:a + seg] for a in range(0, c, seg)], axis=0).astype(BF16)
                aq = lax.dot_general(zq, z, NT_DIMS, preferred_element_type=F32)
                blank = jnp.zeros((half, c), F32)
                a = jnp.concatenate([blk for j in range(c // seg)
                                     for blk in (blank, aq[j * half:(j + 1) * half])], axis=0)
            else:
                a = lax.dot_general(z, z, NT_DIMS, preferred_element_type=F32)
            attn = jnp.where(pair_level == half, a, attn)
            half //= 2

        vb = v.astype(BF16)
        diag = jnp.sum(q * kk, axis=-1, keepdims=True)
        intra = jnp.dot(attn.astype(BF16), vb, preferred_element_type=F32) + diag * v
        qe = (q * jnp.exp2(bcum)).astype(BF16)
        inter = lax.dot_general(qe, st.astype(BF16), NT_DIMS, preferred_element_type=F32)
        kd = (kk * jnp.exp2(btot - bcum)).astype(BF16)
        st = st * jnp.exp2(btot) +lax.dot_general(vb, kd, TN_DIMS, preferred_element_type=F32)

        o = _rms(inter + intra, g_out)
        o_ref[0, rows] = (o * gate / (1.0 + jnp.exp(-gate))).astype(o_ref.dtype)
    st_ref[...] = st


def _hgrn(u_hg, par, *, chunk, n_sub):
    b, s, _ = u_hg.shape
    amat = _hgrn_level_matrix(chunk)
    pair_level = _hgrn_pair_level(chunk)
    step = chunk * n_sub
    col = lambda off: (lambda bi, h, c: (bi, c, off + h))
    return pl.pallas_call(
        functools.partial(_hgrn_kernel, chunk=chunk, n_sub=n_sub),
        grid=(b, HG_HEADS, s // step),
        in_specs=[pl.BlockSpec((1, step, HG_DK), col(0)),
                  pl.BlockSpec((1, step, HG_DK), col(HG_HEADS)),
                  pl.BlockSpec((1, step, HG_DV), col(2 * HG_HEADS)),
                  pl.BlockSpec((1, step, HG_DV), col(3 * HG_HEADS)),
                  pl.BlockSpec((1, 8, HG_DK), lambda bi, h, c: (h, 0, 0)),
                  pl.BlockSpec(amat.shape, lambda bi, h, c: (0, 0)),
                  pl.BlockSpec(pair_level.shape, lambda bi, h, c: (0, 0))],
        out_specs=pl.BlockSpec((1, step, HG_DV), lambda bi, h, c: (bi, c, h)),
        out_shape=jax.ShapeDtypeStruct((b, s, HG_WIDTH), BF16),
        scratch_shapes=[pltpu.VMEM((HG_DV, HG_DK), F32)],
        compiler_params=_params("parallel", "parallel", "arbitrary"),
        name="hgrn2",
    )(u_hg, u_hg, u_hg, u_hg, par, amat, pair_level)


def _mla_prep_kernel(u_ref, tab_ref, wq_ref, wkv_ref, gcq_ref, gckv_ref, gqn_ref, gqr_ref, gkn_ref, gkr_ref,
                     q_ref, k_ref, v_ref):
    u = u_ref[0].astype(F32)
    cq = _rms(u[:, :MLA_Q_RANK], gcq_ref[...]).astype(BF16)
    ckv = _rms(u[:, MLA_Q_RANK:MLA_Q_RANK + MLA_KV_RANK], gckv_ref[...]).astype(BF16)
    kr = u[:, MLA_Q_RANK + MLA_KV_RANK:]
    qf = jnp.dot(cq, wq_ref[...], preferred_element_type=F32)
    kvf = jnp.dot(ckv, wkv_ref[...], preferred_element_type=F32)
    tab = tab_ref[...]
    low = lax.broadcasted_iota(I32, kr.shape, 1) < MLA_ROPE
    kr_sq = jnp.sum(jnp.where(low, kr * kr, 0.0), axis=-1, keepdims=True)
    scale = MLA_QK ** -0.5 * LOG2_E
    ones = jnp.ones((u.shape[0], MLA_V), F32)
    for h in range(MLA_HEADS):
        qn = qf[:, h * MLA_QK_PAD:h * MLA_QK_PAD + MLA_NOPE]
        qr = qf[:, h * MLA_QK_PAD + MLA_NOPE:(h + 1) * MLA_QK_PAD]
        ss = jnp.sum(qn * qn, axis=-1, keepdims=True) + jnp.sum(jnp.where(low, qr * qr, 0.0), axis=-1, keepdims=True)
        rstd = lax.rsqrt(ss / MLA_QK + EPS) * scale
        t = qr * rstd * tab * gqr_ref[...]
        rope = jnp.where(low, t + pltpu.roll(t, MLA_ROPE, 1), 0.0)
        q_ref[0, h] = jnp.concatenate([qn * rstd * gqn_ref[...], rope], axis=-1).astype(q_ref.dtype)

        kn = kvf[:, h * MLA_NOPE:(h + 1) * MLA_NOPE]
        ss = jnp.sum(kn * kn, axis=-1, keepdims=True) + kr_sq
        rstd = lax.rsqrt(ss / MLA_QK + EPS)
        t = kr * rstd * tab * gkr_ref[...]
        rope = t + pltpu.roll(t, MLA_ROPE, 1)
        k_ref[0, h] = jnp.concatenate([kn * rstd * gkn_ref[...], rope], axis=-1).astype(k_ref.dtype)
        vh = kvf[:, MLA_HEADS * MLA_NOPE + h * MLA_V:MLA_HEADS * MLA_NOPE + (h + 1) * MLA_V]
        v_ref[0, h] = jnp.concatenate([vh, ones], axis=-1).astype(v_ref.dtype)


def _swap_halves(a):
    half = a.shape[-1] // 2
    return jnp.concatenate([a[..., half:], a[..., :half]], axis=-1)


def _mla_prep(u_mla, w_uq, w_ukv, g_cq, g_ckv, gq, gk, *, tm):
    b, s, wu = u_mla.shape
    wq = w_uq.reshape(MLA_Q_RANK, MLA_HEADS, MLA_QK)
    wq = jnp.concatenate([wq, _swap_halves(wq[..., MLA_NOPE:])], axis=-1).reshape(MLA_Q_RANK, MLA_HEADS * MLA_QK_PAD)
    wkv = w_ukv.reshape(MLA_KV_RANK, MLA_HEADS, MLA_NOPE + MLA_V)
    wkv = jnp.concatenate([wkv[..., :MLA_NOPE].reshape(MLA_KV_RANK, -1), wkv[..., MLA_NOPE:].reshape(MLA_KV_RANK, -1)], axis=1)
    half = MLA_ROPE // 2
    inv_freq = ROPE_THETA ** (-jnp.arange(half, dtype=F32) / half)
    ang = jnp.arange(s, dtype=F32)[:, None] * inv_freq[None, :]
    cos, sin = jnp.cos(ang), jnp.sin(ang)
    tab = jnp.concatenate([cos, cos, -sin, sin], axis=-1)
    rope_gain = lambda g: jnp.concatenate([g[MLA_NOPE:], _swap_halves(g[MLA_NOPE:])]).reshape(1, 2 * MLA_ROPE)
    vec = lambda i, j: (0, 0)
    head_out = lambda width: pl.BlockSpec((1, MLA_HEADS, tm, width), lambda i, j: (i, 0, j, 0))
    return pl.pallas_call(
        _mla_prep_kernel,
        grid=(b, s // tm),
        in_specs=[pl.BlockSpec((1, tm, wu), lambda i, j: (i, j, 0)),
                  pl.BlockSpec((tm, 2 * MLA_ROPE), lambda i, j: (j, 0)),
                  pl.BlockSpec(wq.shape, vec),
                  pl.BlockSpec(wkv.shape, vec),
                  pl.BlockSpec((1, MLA_Q_RANK), vec),
                  pl.BlockSpec((1, MLA_KV_RANK), vec),
                  pl.BlockSpec((1, MLA_NOPE), vec),
                  pl.BlockSpec((1, 2 * MLA_ROPE), vec),
                  pl.BlockSpec((1, MLA_NOPE), vec),
                  pl.BlockSpec((1, 2 * MLA_ROPE), vec)],
        out_specs=[head_out(MLA_QK_PAD), head_out(MLA_QK_PAD), head_out(2 * MLA_V)],
        out_shape=[jax.ShapeDtypeStruct((b, MLA_HEADS, s, MLA_QK_PAD), BF16),
                   jax.ShapeDtypeStruct((b, MLA_HEADS, s, MLA_QK_PAD), BF16),
                   jax.ShapeDtypeStruct((b, MLA_HEADS, s, 2 * MLA_V), BF16)],
        compiler_params=_params("parallel", "parallel"),
        name="mla_prep",
    )(u_mla, tab, wq.astype(BF16), wkv.astype(BF16), g_cq.reshape(1, -1), g_ckv.reshape(1, -1),
      gq[:MLA_NOPE].reshape(1, -1), rope_gain(gq), gk[:MLA_NOPE].reshape(1, -1), rope_gain(gk))


def _mla_attn_kernel(qi_ref, ki_ref, q_ref, k_ref, v_ref, o_ref, m_ref, acc_ref, *, tq, tk):
    step = pl.program_id(2)
    qi = qi_ref[step]
    ki = ki_ref[step]

    @pl.when(ki == 0)
    def _():
        m_ref[...] = jnp.full_like(m_ref, NEG_INF)
        acc_ref[...] = jnp.zeros_like(acc_ref)

    def update(rel):
        for r0, k0 in [(r, k) for r in range(0, tq, MLA_ROW_GROUP) for k in range(0, tk, MLA_KEY_GROUP)]:
            first_key = None if rel is None else rel + k0
            if first_key is not None and first_key > r0 + MLA_ROW_GROUP - 1:
                continue
            rows = pl.ds(r0, MLA_ROW_GROUP)
            keys = pl.ds(k0, MLA_KEY_GROUP)
            s = lax.dot_general(q_ref[0, 0, rows], k_ref[0, 0, keys], NT_DIMS, preferred_element_type=F32)
            if first_key is not None and first_key + MLA_KEY_GROUP - 1 > r0:
                row = r0 + lax.broadcasted_iota(I32, s.shape, 0)
                col = first_key + lax.broadcasted_iota(I32, s.shape, 1)
                s = jnp.where(col <= row, s, NEG_INF)
            m_prev = m_ref[rows]
            m_next = jnp.maximum(m_prev, jnp.max(s, axis=-1, keepdims=True))
            alpha = jnp.exp2(m_prev - m_next)
            p = jnp.exp2(s - m_next[:, :1]).astype(BF16)
            pv = jnp.dot(p, v_ref[0, 0, keys], preferred_element_type=F32)
            acc_ref[rows, :MLA_V] = alpha * acc_ref[rows, :MLA_V] + pv[:, :MLA_V]
            acc_ref[rows, MLA_V:] = alpha * acc_ref[rows, MLA_V:] + pv[:, MLA_V:]
            m_ref[rows] = m_next

    key_offset = ki * tk - qi * tq
    pl.when(key_offset < 0)(lambda: update(None))
    for rel in range(0, tq, tk):
        pl.when(key_offset == rel)(functools.partial(update, rel))

    @pl.when((ki + 1) * tk == (qi + 1) * tq)
    def _():
        o_ref[0] = (acc_ref[:, :MLA_V] / acc_ref[:, MLA_V:]).astype(o_ref.dtype)


def _mla_attention(q, k, v, *, tq, tk):
    b, h, s, _ = q.shape
    pairs = [(qi, ki) for qi in range(s // tq) for ki in range((qi + 1) * tq // tk)]
    qi_of = jnp.array([p[0] for p in pairs], I32)
    ki_of = jnp.array([p[1] for p in pairs], I32)
    kv_idx = lambda bi, hi, t, qi_ref, ki_ref: (bi, hi, ki_ref[t], 0)
    return pl.pallas_call(
        functools.partial(_mla_attn_kernel, tq=tq, tk=tk),
        grid_spec=pltpu.PrefetchScalarGridSpec(
            num_scalar_prefetch=2,
            grid=(b, h, len(pairs)),
            in_specs=[pl.BlockSpec((1, 1, tq, MLA_QK_PAD), lambda bi, hi, t, qi_ref, ki_ref: (bi, hi, qi_ref[t], 0)),
                      pl.BlockSpec((1, 1, tk, MLA_QK_PAD), kv_idx),
                      pl.BlockSpec((1, 1, tk, 2 * MLA_V), kv_idx)],
            out_specs=pl.BlockSpec((1, tq, MLA_V), lambda bi, hi, t, qi_ref, ki_ref: (bi, qi_ref[t], hi)),
            scratch_shapes=[pltpu.VMEM((tq, MLA_V), F32), pltpu.VMEM((tq, 2 * MLA_V), F32)],
        ),
        out_shape=jax.ShapeDtypeStruct((b, s, h * MLA_V), BF16),
        compiler_params=_params("parallel", "parallel", "arbitrary"),
        name="mla_attention",
    )(qi_of, ki_of, q, k, v)


def _mem_attn_body(x, kv, g_ref, wq_ref, wo_ref, gq_ref, gk_ref):
    hn = _rms(x, g_ref[...]).astype(BF16)
    qf = jnp.dot(hn, wq_ref[...], preferred_element_type=F32)
    outs = []
    for h in range(MEM_HEADS):
        lo = h * MEM_HEAD_DIM
        qh = (_rms(qf[:, lo:lo + MEM_HEAD_DIM], gq_ref[...]) * (MEM_HEAD_DIM ** -0.5)).astype(BF16)
        kh = _rms(kv[:, lo:lo + MEM_HEAD_DIM], gk_ref[...]).astype(BF16)
        vh = kv[:, MEM_WIDTH + lo:MEM_WIDTH + lo + MEM_HEAD_DIM].astype(BF16)
        s = lax.dot_general(qh, kh, NT_DIMS, preferred_element_type=F32)
        e = jnp.exp(s - jnp.max(s, axis=-1, keepdims=True))
        p = (e / jnp.sum(e, axis=-1, keepdims=True)).astype(BF16)
        outs.append(jnp.dot(p, vh, preferred_element_type=F32))
    o = jnp.concatenate(outs, axis=-1).astype(BF16)
    return x + jnp.dot(o, wo_ref[...], preferred_element_type=F32)


def _split_bf16(a):
    hi = a.astype(BF16)
    return hi, (a - hi.astype(F32)).astype(BF16)


def _router_body(x, g_ref, whi_ref, wlo_ref, b_ref, ids_ref, gates_ref, cnt_ref, carry_ref):
    hn = _rms(x, g_ref[...])
    hi, lo = _split_bf16(hn)
    logits = (jnp.dot(hi, whi_ref[...], preferred_element_type=F32)
              + jnp.dot(hi, wlo_ref[...], preferred_element_type=F32)
              + jnp.dot(lo, whi_ref[...], preferred_element_type=F32)) + b_ref[...]
    lane = lax.broadcasted_iota(I32, logits.shape, 1)
    gl = jnp.where(lane < N_GROUPS, logits, NEG_INF)
    gmax = jnp.max(gl, axis=-1, keepdims=True)
    p_grp = 1.0 / jnp.sum(jnp.exp(gl - gmax), axis=-1, keepdims=True)
    grp = jnp.min(jnp.where(gl == gmax, lane, LANES), axis=-1, keepdims=True)
    in_grp = (lane >= N_GROUPS) & (lane < N_GROUPS + N_EXPERTS) & (((lane - N_GROUPS) // EXPERTS_PER_GROUP) == grp)
    el = jnp.where(in_grp, logits, NEG_INF)
    m1 = jnp.max(el, axis=-1, keepdims=True)
    i1 = jnp.min(jnp.where(el == m1, lane, LANES), axis=-1, keepdims=True)
    el2 = jnp.where(lane == i1, NEG_INF, el)
    m2 = jnp.max(el2, axis=-1, keepdims=True)
    i2 = jnp.min(jnp.where(el2 == m2, lane, LANES), axis=-1, keepdims=True)
    r = jnp.exp(m2 - m1)
    g1 = p_grp / (1.0 + r)
    gates_ref[...] = jnp.where(lane == 0, g1, jnp.where(lane == 1, g1 * r, 0.0))

    tm = logits.shape[0]
    used = jnp.where((lane == i1) | (lane == i2), 1.0, 0.0)
    earlier = lax.broadcasted_iota(I32, (tm, tm), 1) < lax.broadcasted_iota(I32, (tm, tm), 0)
    before = carry_ref[...] + jnp.dot(jnp.where(earlier, 1.0, 0.0).astype(BF16), used.astype(BF16),
                                      preferred_element_type=F32)
    r1 = jnp.sum(jnp.where(lane == i1, before, 0.0), axis=-1, keepdims=True).astype(I32)
    r2 = jnp.sum(jnp.where(lane == i2, before, 0.0), axis=-1, keepdims=True).astype(I32)
    carry_ref[...] += jnp.sum(used, axis=0, keepdims=True)
    cnt_ref[...] = carry_ref[...].astype(I32)
    ids_ref[...] = jnp.where(lane == 0, i1 - N_GROUPS, jnp.where(lane == 1, i2 - N_GROUPS,
                             jnp.where(lane == 2, r1, jnp.where(lane == 3, r2, 0))))


def _post_mixer_kernel(x_ref, a_ref, b_ref, c_ref, wa_ref, wb_ref, wc_ref,
                       kv_ref, gmq_ref, wmq_ref, wmo_ref, mgq_ref, mgk_ref,
                       gffn_ref, whi_ref, wlo_ref, bias_ref,
                       o_ref, ids_ref, gates_ref, cnt_ref, carry_ref):
    @pl.when((pl.program_id(0) == 0) & (pl.program_id(1) == 0))
    def _():
        carry_ref[...] = jnp.zeros_like(carry_ref)

    mix = jnp.dot(a_ref[0], wa_ref[...], preferred_element_type=F32)
    mix += jnp.dot(b_ref[0], wb_ref[...], preferred_element_type=F32)
    mix += jnp.dot(c_ref[0], wc_ref[...], preferred_element_type=F32)
    x = x_ref[0] + mix
    x = _mem_attn_body(x, kv_ref[0], gmq_ref, wmq_ref, wmo_ref, mgq_ref, mgk_ref)
    o_ref[0] = x
    _router_body(x, gffn_ref, whi_ref, wlo_ref, bias_ref, ids_ref, gates_ref, cnt_ref, carry_ref)


def _post_mixer(x, a, bmix, c, w_out, kv, g_mem_q, w_mq, w_mo, mem_gq, mem_gk,
                g_ffn, w_gr, b_gr, w_er, b_er, *, tm):
    b, s, d = x.shape
    m = kv.shape[1]
    nst = s // tm
    wa = w_out[:SWA_WIDTH].astype(BF16)
    wb = w_out[SWA_WIDTH:SWA_WIDTH + HG_WIDTH].astype(BF16)
    wc = w_out[SWA_WIDTH + HG_WIDTH:].astype(BF16)
    pad = LANES - N_GROUPS - N_EXPERTS
    wr = jnp.concatenate([w_gr, w_er, jnp.zeros((d, pad), F32)], axis=1)
    bias = jnp.concatenate([b_gr, b_er, jnp.zeros((pad,), F32)]).reshape(1, LANES)
    whi, wlo = _split_bf16(wr)
    tile = lambda width: pl.BlockSpec((1, tm, width), lambda i, j: (i, j, 0))
    const = lambda shape: pl.BlockSpec(shape, lambda i, j: (0,) * len(shape), pipeline_mode=pl.Buffered(1))
    flat = lambda width: pl.BlockSpec((tm, width), lambda i, j: (i * nst + j, 0))
    return pl.pallas_call(
        _post_mixer_kernel,
        grid=(b, nst),
        in_specs=[tile(d), tile(SWA_WIDTH), tile(HG_WIDTH), tile(MLA_WIDTH),
                  const(wa.shape), const(wb.shape), const(wc.shape),
                  pl.BlockSpec((1, m, 2 * MEM_WIDTH), lambda i, j: (i, 0, 0)),
                  const((1, d)), const((d, MEM_WIDTH)), const((MEM_WIDTH, d)),
                  const((1, MEM_HEAD_DIM)), const((1, MEM_HEAD_DIM)),
                  const((1, d)), const((d, LANES)), const((d, LANES)), const((1, LANES))],
        out_specs=[tile(d), flat(LANES), flat(LANES), pl.BlockSpec((1, LANES), lambda i, j: (0, 0))],
        out_shape=[jax.ShapeDtypeStruct((b, s, d), F32), jax.ShapeDtypeStruct((b * s, LANES), I32),
                   jax.ShapeDtypeStruct((b * s, LANES), F32), jax.ShapeDtypeStruct((1, LANES), I32)],
        scratch_shapes=[pltpu.VMEM((1, LANES), F32)],
        compiler_params=_params("arbitrary", "arbitrary"),
        name="post_mixer",
    )(x, a, bmix, c, wa, wb, wc, kv, g_mem_q.reshape(1, d), w_mq.astype(BF16), w_mo.astype(BF16),
      mem_gq.reshape(1, -1), mem_gk.reshape(1, -1), g_ffn.reshape(1, d), whi, wlo, bias)


HI_HALF_MASK = 0xFFFF0000


def _pack_bf16_pairs(a):
    half = a.shape[1] // 2
    lo = pltpu.bitcast(a[:, :half].astype(BF16).astype(F32), jnp.uint32) >> 16
    hi = pltpu.bitcast(a[:, half:].astype(BF16).astype(F32), jnp.uint32) & jnp.uint32(HI_HALF_MASK)
    return lo | hi


def _unpack_bf16_pairs(w):
    return pltpu.bitcast(w << 16, F32), pltpu.bitcast(w & jnp.uint32(HI_HALF_MASK), F32)


def _dispatch_kernel(dest_ref, x_ref, g_ref, xs_in_hbm, xs_hbm, buf, sem, *, ts):
    del xs_in_hbm
    i = pl.program_id(0)
    last = pl.num_programs(0) - 1
    slot = i % 2

    def row_copy(slot_, r, dst_row):
        return pltpu.make_async_copy(buf.at[slot_, pl.ds(r, 1)], xs_hbm.at[pl.ds(dst_row, 1)], sem.at[slot_])

    def wait_step(slot_):
        for _ in range(ts * TOP_K):
            row_copy(slot_, 0, 0).wait()

    @pl.when(i >= 2)
    def _():
        wait_step(slot)

    buf[slot] = _pack_bf16_pairs(_rms(x_ref[...], g_ref[...]))
    for r in range(ts):
        for kk in range(TOP_K):
            row_copy(slot, r, dest_ref[(i * ts + r) * TOP_K + kk]).start(priority=kk % DMA_PRIORITIES)

    @pl.when(i == last)
    def _():
        wait_step(slot)

    @pl.when((i == last) & (i >= 1))
    def _():
        wait_step(1 - slot)


def _dispatch(x, g_ffn, dest, xs_init, *, ts):
    n, d = x.shape
    n_rows = xs_init.shape[0]
    return pl.pallas_call(
        functools.partial(_dispatch_kernel, ts=ts),
        grid_spec=pltpu.PrefetchScalarGridSpec(
            num_scalar_prefetch=1,
            grid=(n // ts,),
            in_specs=[pl.BlockSpec((ts, d), lambda i, *_: (i, 0)),
                      pl.BlockSpec((1, d), lambda i, *_: (0, 0)),
                      pl.BlockSpec(memory_space=pl.ANY)],
            out_specs=pl.BlockSpec(memory_space=pl.ANY),
            scratch_shapes=[pltpu.VMEM((2, ts, d // 2), jnp.uint32), pltpu.SemaphoreType.DMA((2,))],
        ),
        out_shape=jax.ShapeDtypeStruct((n_rows, d // 2), jnp.uint32),
        input_output_aliases={3: 0},
        compiler_params=_params("arbitrary"),
        name="moe_dispatch",
    )(dest, x, g_ffn.reshape(1, d), xs_init)


def _expert_kernel(bexp_ref, next_ref, nblk_ref, xs_ref, wg_hbm, wu_hbm, wd_hbm, y_ref,
                   wg_f32, wu_f32, wd_f32, sem, wg_bf, wu_bf, wd_bf, *, layer):
    i = pl.program_id(0)
    n_used = nblk_ref[0]
    expert = bexp_ref[i]
    first_of_run = (i == 0) | (expert != bexp_ref[jnp.maximum(i - 1, 0)])
    slot = next_ref[2 * i + 1]

    def weight_copies(e, slot_):
        return [pltpu.make_async_copy(hbm.at[layer, e], buf.at[slot_], sem.at[slot_, j])
                for j, (hbm, buf) in enumerate(((wg_hbm, wg_f32), (wu_hbm, wu_f32), (wd_hbm, wd_f32)))]

    @pl.when(i == 0)
    def _():
        for j, cp in enumerate(weight_copies(expert, 0)):
            cp.start(priority=j % DMA_PRIORITIES)

    @pl.when(first_of_run & (i < n_used))
    def _():
        for cp in weight_copies(expert, slot):
            cp.wait()
        next_expert = next_ref[2 * i]

        @pl.when(next_expert >= 0)
        def _():
            for j, cp in enumerate(weight_copies(next_expert, 1 - slot)):
                cp.start(priority=j % DMA_PRIORITIES)

        wg_bf[...] = wg_f32[slot].astype(BF16)
        wu_bf[...] = wu_f32[slot].astype(BF16)
        wd_bf[...] = wd_f32[slot].astype(BF16)

    @pl.when(i < n_used)
    def _():
        half = xs_ref.shape[1]
        lo, hi = (part.astype(BF16) for part in _unpack_bf16_pairs(xs_ref[...]))
        gate = (jnp.dot(lo, wg_bf[:half], preferred_element_type=F32)
                + jnp.dot(hi, wg_bf[half:], preferred_element_type=F32))
        up = (jnp.dot(lo, wu_bf[:half], preferred_element_type=F32)
              + jnp.dot(hi, wu_bf[half:], preferred_element_type=F32))
        act = (gate / (1.0 + jnp.exp(-gate)) * up).astype(BF16)
        y_ref[...] = _pack_bf16_pairs(jnp.dot(act, wd_bf[...], preferred_element_type=F32))

    @pl.when(i >= nblk_ref[0])
    def _():
        y_ref[...] = jnp.zeros_like(y_ref)


def _experts(xs, block_expert, run_next, n_used, w_gate, w_up, w_down, layer, *, rows):
    d = 2 * xs.shape[1]
    n_blocks = block_expert.shape[0]
    hbm = pl.BlockSpec(memory_space=pl.ANY)
    return pl.pallas_call(
        functools.partial(_expert_kernel, layer=layer),
        grid_spec=pltpu.PrefetchScalarGridSpec(
            num_scalar_prefetch=3,
            grid=(n_blocks,),
            in_specs=[pl.BlockSpec((rows, d // 2), lambda i, bexp, nxt, nblk: (jnp.minimum(i, nblk[0] - 1), 0)),
                      hbm, hbm, hbm],
            out_specs=pl.BlockSpec((rows, d // 2), lambda i, *_: (i, 0)),
            scratch_shapes=[pltpu.VMEM((2, d, D_EXPERT), F32),
                            pltpu.VMEM((2, d, D_EXPERT), F32),
                            pltpu.VMEM((2, D_EXPERT, d), F32),
                            pltpu.SemaphoreType.DMA((2, 3)),
                            pltpu.VMEM((d, D_EXPERT), BF16),
                            pltpu.VMEM((d, D_EXPERT), BF16),
                            pltpu.VMEM((D_EXPERT, d), BF16)],
        ),
        out_shape=jax.ShapeDtypeStruct((n_blocks * rows, d // 2), jnp.uint32),
        compiler_params=_params("arbitrary"),
        name="moe_experts",
    )(block_expert, run_next, n_used, xs, w_gate, w_up, w_down)


def _combine_kernel(pos_ref, x_ref, gates_ref, y_hbm, o_ref, ybuf, sem, *, tt):
    i = pl.program_id(0)
    nsteps = pl.num_programs(0)

    def row_copy(src_row, slot, dst_row):
        return pltpu.make_async_copy(y_hbm.at[pl.ds(src_row, 1)], ybuf.at[slot, pl.ds(dst_row, 1)], sem.at[slot])

    def start_gather(step, slot):
        for r in range(tt):
            for kk in range(TOP_K):
                row_copy(pos_ref[(step * tt + r) * TOP_K + kk], slot, kk * tt + r).start(priority=kk % DMA_PRIORITIES)

    def wait_gather(slot):
        for r in range(TOP_K * tt):
            row_copy(0, slot, r).wait()

    @pl.when(i == 0)
    def _():
        start_gather(0, 0)

    @pl.when(i + 1 < nsteps)
    def _():
        start_gather(i + 1, (i + 1) % 2)

    wait_gather(i % 2)
    g = gates_ref[...]
    half = ybuf.shape[2]
    for cols, y in zip((pl.ds(0, half), pl.ds(half, half)), _unpack_bf16_pairs(ybuf[i % 2])):
        o_ref[:, cols] = x_ref[:, cols] + g[:, 0:1] * y[:tt] + g[:, 1:2] * y[tt:]


def _combine(x, gates, pos, y, *, tt):
    n, d = x.shape
    return pl.pallas_call(
        functools.partial(_combine_kernel, tt=tt),
        grid_spec=pltpu.PrefetchScalarGridSpec(
            num_scalar_prefetch=1,
            grid=(n // tt,),
            in_specs=[pl.BlockSpec((tt, d), lambda i, *_: (i, 0)),
                      pl.BlockSpec((tt, LANES), lambda i, *_: (i, 0)),
                      pl.BlockSpec(memory_space=pl.ANY)],
            out_specs=pl.BlockSpec((tt, d), lambda i, *_: (i, 0)),
            scratch_shapes=[pltpu.VMEM((2, TOP_K * tt, d // 2), jnp.uint32), pltpu.SemaphoreType.DMA((2,))],
        ),
        out_shape=jax.ShapeDtypeStruct((n, d), F32),
        compiler_params=_params("arbitrary"),
        name="moe_combine",
    )(pos, x, gates, y)


def _moe_plan(ids, cnt, *, rows):
    n = ids.shape[0]
    expert, rank = ids[:, :TOP_K], ids[:, TOP_K:2 * TOP_K]
    counts = cnt[0, N_GROUPS:N_GROUPS + N_EXPERTS]
    padded = (counts + rows - 1) // rows * rows
    padded_end = jnp.cumsum(padded)
    seg_start = padded_end - padded
    onehot = expert[..., None] == jnp.arange(N_EXPERTS, dtype=I32)
    dest = (jnp.sum(jnp.where(onehot, seg_start, 0), axis=-1) + rank).astype(I32).reshape(n * TOP_K)
    n_blocks = -(-(n * TOP_K) // rows) + N_EXPERTS
    block_start = jnp.arange(n_blocks, dtype=I32) * rows
    block_expert = jnp.minimum(jnp.sum(padded_end[None, :] <= block_start[:, None], axis=1), N_EXPERTS - 1).astype(I32)
    n_used = (padded_end[-1] // rows).astype(I32)
    block_onehot = block_expert[:, None] == jnp.arange(N_EXPERTS, dtype=I32)
    pick = lambda per_expert: jnp.sum(jnp.where(block_onehot, per_expert, 0), axis=-1)
    next_block = pick(padded_end // rows)
    next_onehot = jnp.minimum(next_block, n_blocks - 1)[:, None] == jnp.arange(n_blocks, dtype=I32)
    next_expert = jnp.where(next_block < n_used, jnp.sum(jnp.where(next_onehot, block_expert, 0), axis=-1), -1)
    nonempty = (counts > 0).astype(I32)
    run_parity = pick(jnp.cumsum(nonempty) - nonempty) % 2
    run_next = jnp.stack([next_expert, run_parity], axis=-1).astype(I32).reshape(2 * n_blocks)
    return dest, block_expert, run_next, n_used.reshape(1)


def _band_relative_bias(table):
    def bucket(nd):
        max_exact = REL_BUCKETS // 2
        nf = jnp.maximum(nd, 1).astype(F32)
        large = max_exact + (jnp.log(nf / max_exact) / math.log(REL_MAX_DIST / max_exact)
                             * (REL_BUCKETS - max_exact)).astype(I32)
        return jnp.where(nd < max_exact, nd, jnp.minimum(large, REL_BUCKETS - 1))

    qi = jnp.arange(WINDOW)[:, None]
    kj = jnp.arange(2 * WINDOW)[None, :]
    dist = jnp.maximum(qi + WINDOW - kj, 0)
    onehot = (bucket(dist)[..., None] == jnp.arange(REL_BUCKETS)).astype(F32)
    return jnp.einsum('qkb,bh->hqk', onehot, table.astype(F32), precision=lax.Precision.HIGHEST)


def _hgrn_params(lb, g_out):
    lb = lb.reshape(HG_HEADS, 1, HG_DK)
    gain = jnp.broadcast_to(g_out.reshape(1, 1, HG_DV), (HG_HEADS, 1, HG_DV))
    rows = [jnp.log(lb), jnp.log1p(-lb), 1.0 - lb, gain, jnp.zeros((HG_HEADS, 4, HG_DK), F32)]
    return jnp.concatenate(rows, axis=1).astype(F32)


def kernel(x, mem, rel_bias_table, hg_lb_logits, g_mix, w_in, swa_gq, swa_gk, swa_sinks, hg_g_out, mla_g_cq, mla_g_ckv, mla_w_uq, mla_w_ukv, mla_gq, mla_gk, w_out, g_mem_q, g_mem_kv, w_mq, w_mkv, mem_gq, mem_gk, w_mo, g_ffn, w_group_router, b_group_router, w_expert_router, b_expert_router, w_gate, w_up, w_down):
    b, s, d = x.shape
    n = b * s
    m = mem.shape[1]
    rel_bias = _band_relative_bias(rel_bias_table)
    lb_all = jnp.cumsum(jax.nn.softmax(hg_lb_logits.astype(F32), axis=0), axis=0)
    lb_all = lb_all - lb_all[:1]
    offs = [0]
    for width in IN_SIZES:
        offs.append(offs[-1] + width)
    o_hg, o_cq, o_kr = offs[3], offs[7], offs[9]

    xf = x.reshape(n, d)
    mem2 = mem.reshape(b * m, d)
    for l in range(DEPTH):
        w = w_in[l].astype(BF16)
        w_swa = w[:, :o_hg]
        w_hg = w[:, o_hg:o_cq]
        w_mla = jnp.concatenate([w[:, o_cq:], _swap_halves(w[:, o_kr:])], axis=1)
        in_proj = functools.partial(_rms_matmul, xf, g_mix[l], tm=IN_PROJ_ROWS, out_dtype=BF16)
        u_swa = in_proj(w_swa, tn=w_swa.shape[1]).reshape(b, s, -1)
        u_hg = in_proj(w_hg, tn=IN_PROJ_HG_COLS).reshape(b, s, -1)
        u_mla = in_proj(w_mla, tn=w_mla.shape[1]).reshape(b, s, -1)

        out_a = _swa(u_swa, rel_bias, swa_gq[l], swa_gk[l], swa_sinks[l])
        out_b = _hgrn(u_hg, _hgrn_params(lb_all[l], hg_g_out[l]), chunk=HG_CHUNK, n_sub=HG_SUB)
        qm, km, vm = _mla_prep(u_mla, mla_w_uq[l], mla_w_ukv[l], mla_g_cq[l], mla_g_ckv[l],
                               mla_gq[l], mla_gk[l], tm=MLA_PREP_ROWS)
        out_c = _mla_attention(qm, km, vm, tq=MLA_TQ, tk=MLA_TK)

        kv = _rms_matmul(mem2, g_mem_kv[l], w_mkv[l].astype(BF16), tm=b * m, tn=2 * MEM_WIDTH)
        x3, ids, gates, cnt = _post_mixer(
            xf.reshape(b, s, d), out_a, out_b, out_c, w_out[l], kv.reshape(b, m, -1),
            g_mem_q[l], w_mq[l], w_mo[l], mem_gq[l], mem_gk[l],
            g_ffn[l], w_group_router[l], b_group_router[l], w_expert_router[l], b_expert_router[l],
            tm=POST_MIXER_ROWS)
        xf = x3.reshape(n, d)
        dest, block_expert, run_next, n_used = _moe_plan(ids, cnt, rows=MOE_ROWS)
        xs = _dispatch(xf, g_ffn[l], dest, jnp.zeros((block_expert.shape[0] * MOE_ROWS, d // 2), jnp.uint32),
                       ts=MOE_BURST_TOKENS)
        y = _experts(xs, block_expert, run_next, n_used, w_gate, w_up, w_down, l, rows=MOE_ROWS)
        xf = _combine(xf, gates, dest, y, tt=MOE_BURST_TOKENS)
    return xf.reshape(b, s, d)
```
